```python
import math
import jax, jax.numpy as jnp
from jax import lax
import numpy as np

D_MODEL = 1024
BATCH = 16
SEQ = 4096
DEPTH = 1
DEC_BATCH = 16
DEC_SEQ = 32
PAST_LEN = 2048

CHUNK = 64
H_A = 4
DK_A = D_MODEL // (2 * H_A)
DV_A = D_MODEL // H_A
GATE_RANK = 16
GATE_TAU = 16.0
H_B = 8
D_B = D_MODEL // (2 * H_B)
Q_BLOCK = 128
N_EXPERTS = 32
TOP_K = 4
D_FF = D_MODEL
SWIGLU_LIMIT = 7.0
SWIGLU_ALPHA = 1.702
ROW_BLOCK = 128
EPS = 1e-5
DEEPNORM_ALPHA = (2.0 * DEPTH) ** 0.25
DEEPNORM_BETA = (8.0 * DEPTH) ** -0.25
QA_W = H_A * DK_A
KA_W = H_A * DK_A
VA_W = H_A * DV_A
RA_W = H_A * DV_A
LR_W = GATE_RANK
QB_W = H_B * 2 * D_B
KB_W = H_B * 2 * D_B
VB_W = H_B * 2 * D_B
GT_W = 2 * D_MODEL
IN_SIZES = (QA_W, KA_W, VA_W, RA_W, LR_W, QB_W, KB_W, VB_W, GT_W)
IN_IS_VALUE = (False, False, True, False, False, False, False, True, False)
IN_WIDTH = QA_W + KA_W + VA_W + RA_W + LR_W + QB_W + KB_W + VB_W + GT_W

kernel_name = "gla_diffattn_gated_moe_stream_step"


def rms_norm(x, g):
    xf = x.astype(jnp.float32)
    y = xf * lax.rsqrt(jnp.mean(xf * xf, axis=-1, keepdims=True) + EPS)
    return (y * g.astype(jnp.float32)).astype(x.dtype)


def layer_norm(x, g, b):
    xf = x.astype(jnp.float32)
    mu = jnp.mean(xf, axis=-1, keepdims=True)
    var = jnp.mean(jnp.square(xf - mu), axis=-1, keepdims=True)
    y = (xf - mu) * lax.rsqrt(var + EPS)
    return (y * g.astype(jnp.float32) + b.astype(jnp.float32)).astype(x.dtype)


def alibi_slopes():
    return 2.0 ** (-8.0 * jnp.arange(1, H_B + 1, dtype=jnp.float32) / H_B)


def branch_inputs(x, w_in, b_in, w_alpha, b_alpha):
    B, T, _ = x.shape
    h = jnp.einsum('btd,df->btf', x, w_in) + b_in
    qa, ka, va, ra, lra, qb, kb, vb, gates = jnp.split(h, np.cumsum(IN_SIZES)[:-1].tolist(), axis=-1)
    qa = qa.reshape(B, T, H_A, DK_A) * (DK_A ** -0.5)
    ka = ka.reshape(B, T, H_A, DK_A)
    va = va.reshape(B, T, H_A, DV_A)
    log_a = jax.nn.log_sigmoid((lra @ w_alpha + b_alpha).astype(jnp.float32)) / GATE_TAU
    log_a = log_a.reshape(B, T, H_A, DK_A)
    qb = qb.reshape(B, T, H_B, 2, D_B)
    kb = kb.reshape(B, T, H_B, 2, D_B)
    vb = vb.reshape(B, T, H_B, 2 * D_B)
    return qa, ka, va, ra, log_a, qb, kb, vb, gates


def gla_chunk(S, q, k, v, log_a):
    L = q.shape[1]
    S = S.astype(jnp.float32)
    qf, kf, vf = q.astype(jnp.float32), k.astype(jnp.float32), v.astype(jnp.float32)
    b = jnp.cumsum(log_a.astype(jnp.float32), axis=1)
    causal = jnp.tril(jnp.ones((L, L), dtype=bool))[None, :, :, None, None]
    decay = jnp.exp(jnp.where(causal, b[:, :, None] - b[:, None, :], -jnp.inf))
    A = jnp.einsum('bthk,bshk,btshk->bhts', qf, kf, decay)
    o = jnp.einsum('bthk,bhkv->bthv', qf * jnp.exp(b), S) + jnp.einsum('bhts,bshv->bthv', A, vf)
    b_last = b[:, -1]
    k_dec = kf * jnp.exp(b_last[:, None] - b)
    S_new = jnp.exp(b_last)[..., None] * S + jnp.einsum('bshk,bshv->bhkv', k_dec, vf)
    return S_new, o


def gla_prompt(q, k, v, log_a):
    B, T = q.shape[0], q.shape[1]
    nc = T // CHUNK

    def chunks(a):
        return jnp.swapaxes(a.reshape((B, nc, CHUNK) + a.shape[2:]), 0, 1)

    S0 = jnp.zeros((B, H_A, DK_A, DV_A), jnp.float32)
    S_fin, o = lax.scan(lambda S, xs: gla_chunk(S, *xs), S0,
                        (chunks(q), chunks(k), chunks(v), chunks(log_a)))
    return S_fin, jnp.swapaxes(o, 0, 1).reshape(B, T, H_A, DV_A)


def diff_attend(q, k, v, q_pos, k_pos, lam):
    s = jnp.einsum('bqhjd,bkhjd->bhjqk', q.astype(jnp.float32), k.astype(jnp.float32)) * (D_B ** -0.5)
    dist = jnp.abs(q_pos[:, None] - k_pos[None, :]).astype(jnp.float32)
    bias = -alibi_slopes()[:, None, None] * dist
    visible = (k_pos[None, :] // CHUNK) <= (q_pos[:, None] // CHUNK)
    s = jnp.where(visible, s + bias[:, None], -jnp.inf)
    p = jax.nn.softmax(s, axis=-1)
    attn = p[:, :, 0] - lam * p[:, :, 1]
    return jnp.einsum('bhqk,bkhe->bqhe', attn, v.astype(jnp.float32))


def diff_prompt(q, k, v, lam):
    B, T = q.shape[0], q.shape[1]
    nb = T // Q_BLOCK
    qb = jnp.swapaxes(q.reshape(B, nb, Q_BLOCK, H_B, 2, D_B), 0, 1)
    k_pos = jnp.arange(T)

    def block(args):
        i, qi = args
        return diff_attend(qi, k, v, i * Q_BLOCK + jnp.arange(Q_BLOCK), k_pos, lam)

    o = lax.map(block, (jnp.arange(nb), qb))
    return jnp.swapaxes(o, 0, 1).reshape(B, T, H_B, 2 * D_B)


def moe(x, w_router, b_router, w_gu, b_gu, w_down, b_down):
    B, T, D = x.shape
    xt = x.reshape(-1, D)
    n_asg = xt.shape[0] * TOP_K
    logits = (xt @ w_router + b_router).astype(jnp.float32)
    top_val, top_idx = lax.top_k(logits, TOP_K)
    top_w = jax.nn.softmax(top_val, axis=-1)
    e_flat = top_idx.reshape(-1)
    tok_flat = jnp.arange(n_asg) // TOP_K
    w_flat = top_w.reshape(-1)
    order = jnp.argsort(e_flat)
    e_sorted = e_flat[order]
    tok_sorted = tok_flat[order]
    counts = jnp.bincount(e_flat, length=N_EXPERTS)
    start = jnp.cumsum(counts) - counts
    padded = (counts + ROW_BLOCK - 1) // ROW_BLOCK * ROW_BLOCK
    padded_end = jnp.cumsum(padded)
    padded_start = padded_end - padded
    dest = padded_start[e_sorted] + (jnp.arange(n_asg) - start[e_sorted])
    n_blocks = -(-n_asg // ROW_BLOCK) + N_EXPERTS
    rows = jnp.zeros((n_blocks * ROW_BLOCK, D), x.dtype).at[dest].set(xt[tok_sorted])
    block_e = jnp.minimum(jnp.searchsorted(padded_end, jnp.arange(n_blocks) * ROW_BLOCK, side='right'),
                          N_EXPERTS - 1)

    def expert_block(args):
        xb, e = args
        gu = xb @ w_gu[e] + b_gu[e]
        gate, up = jnp.split(gu, 2, axis=-1)
        gate = jnp.minimum(gate, SWIGLU_LIMIT)
        up = jnp.clip(up, -SWIGLU_LIMIT, SWIGLU_LIMIT)
        hidden = (up + 1.0) * (gate * jax.nn.sigmoid(SWIGLU_ALPHA * gate))
        return hidden @ w_down[e] + b_down[e]

    out = lax.map(expert_block, (rows.reshape(n_blocks, ROW_BLOCK, D), block_e)).reshape(-1, D)
    y = jnp.zeros_like(xt).at[tok_sorted].add(w_flat[order][:, None].astype(x.dtype) * out[dest])
    return y.reshape(B, T, D)


def finish_layer(x, oa, ra, ob, gates, lam_init, lp):
    B, T, _ = x.shape
    dt = x.dtype
    ya = (jax.nn.silu(ra) * rms_norm(oa, lp['gla_norm_g']).astype(dt).reshape(B, T, VA_W)) @ lp['w_pa']
    yb = (rms_norm(ob, lp['diff_norm_g']) * (1.0 - lam_init)).astype(dt).reshape(B, T, VB_W) @ lp['w_pb']
    g_a, g_b = jnp.split(jax.nn.sigmoid(gates), 2, axis=-1)
    mix = (g_a * ya + g_b * yb) @ lp['w_o']
    h = layer_norm(DEEPNORM_ALPHA * x + mix, lp['ln1_g'], lp['ln1_b'])
    f = moe(h, lp['w_router'], lp['b_router'], lp['w_gu'], lp['b_gu'], lp['w_down'], lp['b_down'])
    return layer_norm(DEEPNORM_ALPHA * h + f, lp['ln2_g'], lp['ln2_b'])


def setup_inputs(seed: int = 0) -> dict:
    key = jax.random.key(seed)
    ks = jax.random.split(key, 32)
    f32 = jnp.float32
    nrm = lambda k, shape, s: jax.random.normal(k, shape, f32) * s
    col_scale = jnp.concatenate([jnp.full((w,), DEEPNORM_BETA if is_v else 1.0, f32)
                                 for w, is_v in zip(IN_SIZES, IN_IS_VALUE)])
    return {
        'x_prompt': nrm(ks[0], (BATCH, SEQ, D_MODEL), 1.0),
        'x_sample': nrm(ks[1], (DEC_BATCH, DEC_SEQ, D_MODEL), 1.0),
        'cache_k': nrm(ks[2], (DEPTH, DEC_BATCH, PAST_LEN, H_B, 2, D_B), 1.0),
        'cache_v': nrm(ks[3], (DEPTH, DEC_BATCH, PAST_LEN, H_B, 2 * D_B), DEEPNORM_BETA),
        'state_gla': nrm(ks[4], (DEPTH, DEC_BATCH, H_A, DK_A, DV_A), 0.5),
        'w_in': nrm(ks[5], (DEPTH, D_MODEL, IN_WIDTH), D_MODEL ** -0.5) * col_scale,
        'b_in': nrm(ks[6], (DEPTH, IN_WIDTH), 0.02),
        'w_alpha': nrm(ks[7], (DEPTH, GATE_RANK, QA_W), GATE_RANK ** -0.5),
        'b_alpha': nrm(ks[8], (DEPTH, QA_W), 0.1),
        'gla_norm_g': 1.0 + nrm(ks[9], (DEPTH, DV_A), 0.02),
        'diff_lambda': nrm(ks[10], (DEPTH, 4, D_B), 0.1),
        'diff_norm_g': 1.0 + nrm(ks[11], (DEPTH, 2 * D_B), 0.02),
        'w_pa': nrm(ks[12], (DEPTH, VA_W, D_MODEL), VA_W ** -0.5 * DEEPNORM_BETA),
        'w_pb': nrm(ks[13], (DEPTH, VB_W, D_MODEL), VB_W ** -0.5 * DEEPNORM_BETA),
        'w_o': nrm(ks[14], (DEPTH, D_MODEL, D_MODEL), D_MODEL ** -0.5 * DEEPNORM_BETA),
        'ln1_g': 1.0 + nrm(ks[15], (DEPTH, D_MODEL), 0.02),
        'ln1_b': nrm(ks[16], (DEPTH, D_MODEL), 0.02),
        'w_router': nrm(ks[17], (DEPTH, D_MODEL, N_EXPERTS), D_MODEL ** -0.5),
        'b_router': nrm(ks[18], (DEPTH, N_EXPERTS), 0.01),
        'w_gu': nrm(ks[19], (DEPTH, N_EXPERTS, D_MODEL, 2 * D_FF), D_MODEL ** -0.5 * DEEPNORM_BETA),
        'b_gu': nrm(ks[20], (DEPTH, N_EXPERTS, 2 * D_FF), 0.01),
        'w_down': nrm(ks[21], (DEPTH, N_EXPERTS, D_FF, D_MODEL), D_FF ** -0.5 * DEEPNORM_BETA),
        'b_down': nrm(ks[22], (DEPTH, N_EXPERTS, D_MODEL), 0.01),
        'ln2_g': 1.0 + nrm(ks[23], (DEPTH, D_MODEL), 0.02),
        'ln2_b': nrm(ks[24], (DEPTH, D_MODEL), 0.02),
    }


def reference(x_prompt, x_sample, cache_k, cache_v, state_gla, w_in, b_in, w_alpha, b_alpha,
              gla_norm_g, diff_lambda, diff_norm_g, w_pa, w_pb, w_o, ln1_g, ln1_b,
              w_router, b_router, w_gu, b_gu, w_down, b_down, ln2_g, ln2_b):
    xp, xs = x_prompt, x_sample
    s_p_list, k_p_list, v_p_list, s_s_list, k_s_list, v_s_list = [], [], [], [], [], []
    for l in range(DEPTH):
        lp = {'gla_norm_g': gla_norm_g[l], 'diff_norm_g': diff_norm_g[l], 'w_pa': w_pa[l],
              'w_pb': w_pb[l], 'w_o': w_o[l], 'ln1_g': ln1_g[l], 'ln1_b': ln1_b[l],
              'w_router': w_router[l], 'b_router': b_router[l], 'w_gu': w_gu[l], 'b_gu': b_gu[l],
              'w_down': w_down[l], 'b_down': b_down[l], 'ln2_g': ln2_g[l], 'ln2_b': ln2_b[l]}
        lam_init = 0.8 - 0.6 * math.exp(-0.3 * l)
        dl = diff_lambda[l].astype(jnp.float32)
        lam = jnp.exp(jnp.sum(dl[0] * dl[1])) - jnp.exp(jnp.sum(dl[2] * dl[3])) + lam_init

        qa, ka, va, ra, log_a, qb, kb, vb, gates = branch_inputs(xp, w_in[l], b_in[l], w_alpha[l], b_alpha[l])
        S_p, oa = gla_prompt(qa, ka, va, log_a)
        ob = diff_prompt(qb, kb, vb, lam)
        xp = finish_layer(xp, oa, ra, ob, gates, lam_init, lp)
        s_p_list.append(S_p.astype(state_gla.dtype))
        k_p_list.append(kb)
        v_p_list.append(vb)

        qa, ka, va, ra, log_a, qb, kb, vb, gates = branch_inputs(xs, w_in[l], b_in[l], w_alpha[l], b_alpha[l])
        S_s, oa = gla_chunk(state_gla[l], qa, ka, va, log_a)
        past = cache_k.shape[2]
        n_new = xs.shape[1]
        k_all = jnp.concatenate([cache_k[l].astype(kb.dtype), kb], axis=1)
        v_all = jnp.concatenate([cache_v[l].astype(vb.dtype), vb], axis=1)
        ob = diff_attend(qb, k_all, v_all, past + jnp.arange(n_new), jnp.arange(past + n_new), lam)
        xs = finish_layer(xs, oa, ra, ob, gates, lam_init, lp)
        s_s_list.append(S_s.astype(state_gla.dtype))
        k_s_list.append(kb)
        v_s_list.append(vb)

    return (xp, xs, jnp.stack(s_p_list), jnp.stack(k_p_list), jnp.stack(v_p_list),
            jnp.stack(s_s_list), jnp.stack(k_s_list), jnp.stack(v_s_list))
```

```python
import functools
import math

import numpy as np
import jax
import jax.numpy as jnp
from jax import lax
from jax.experimental import pallas as pl
from jax.experimental.pallas import tpu as pltpu

F32 = jnp.float32
BF16 = jnp.bfloat16
U32 = jnp.uint32
I32 = jnp.int32

D_MODEL = 1024
CHUNK = 64
H_A = 4
DK_A = 128
DV_A = 256
GATE_RANK = 16
GATE_TAU = 16.0
H_B = 8
D_B = 64
N_EXPERTS = 32
TOP_K = 4
D_FF = D_MODEL
SWIGLU_LIMIT = 7.0
SWIGLU_ALPHA = 1.702
EPS = 1e-5
DEPTH = 1
DEEPNORM_ALPHA = (2.0 * DEPTH) ** 0.25

QA_W = H_A * DK_A
KA_W = H_A * DK_A
VA_W = H_A * DV_A
RA_W = H_A * DV_A
LR_W = GATE_RANK
QB_W = H_B * 2 * D_B
KB_W = H_B * 2 * D_B
VB_W = H_B * 2 * D_B
GT_W = 2 * D_MODEL
IN_SIZES = (QA_W, KA_W, VA_W, RA_W, LR_W, QB_W, KB_W, VB_W, GT_W)
IN_OFFS = tuple(int(v) for v in np.cumsum((0,) + IN_SIZES))

SLAB_QA, SLAB_KA, SLAB_VA, SLAB_RA, SLAB_QB, SLAB_GT = 0, 512, 1024, 2048, 3072, 4096
SLAB_W = 6144
LANES = 128
EXPERT_BM = 512
VMEM_LIMIT = 56 * 1024 * 1024

NEG_INF = float("-inf")


def _cparams(sem):
    return pltpu.CompilerParams(dimension_semantics=sem, vmem_limit_bytes=VMEM_LIMIT)


def _sigmoid(x):
    return 1.0 / (1.0 + jnp.exp(-x))


def _pack_bf16_pairs(x):
    n = x.shape[1] // 2
    xb = x.astype(BF16).astype(F32)
    lo = pltpu.bitcast(xb[:, :n], U32) >> 16
    hi = pltpu.bitcast(xb[:, n:], U32) & jnp.uint32(0xFFFF0000)
    return hi | lo


def _unpack_bf16_pairs(r):
    lo = pltpu.bitcast(r << 16, F32)
    hi = pltpu.bitcast(r & jnp.uint32(0xFFFF0000), F32)
    return jnp.concatenate([lo, hi], axis=1)


def _layer_norm(y, g, b):
    mu = jnp.mean(y, axis=-1, keepdims=True)
    d = y - mu
    var = jnp.mean(d * d, axis=-1, keepdims=True)
    return d * lax.rsqrt(var + EPS) * g + b


def _proj_slab_kernel(x_ref, w_ref, b_ref, s_ref, wlr_ref, blr_ref, wal_ref, bal_ref,
                      slab_ref, loga_ref, xb_ref):
    j = pl.program_id(1)

    @pl.when(j == 0)
    def _():
        xb_ref[...] = x_ref[...].astype(BF16)
        lra = jnp.dot(xb_ref[...], wlr_ref[...], preferred_element_type=F32) + blr_ref[...]
        z = jnp.dot(lra.astype(BF16), wal_ref[...], preferred_element_type=F32) + bal_ref[...]
        loga_ref[...] = (jnp.minimum(z, 0.0) - jnp.log(1.0 + jnp.exp(-jnp.abs(z)))) * (1.0 / GATE_TAU)

    acc = jnp.dot(xb_ref[...], w_ref[...], preferred_element_type=F32)
    slab_ref[...] = ((acc + b_ref[...]) * s_ref[...]).astype(BF16)


def _proj_slab(x2, p):
    n = x2.shape[0]
    tm = min(1024, n)
    tn = 512
    grid = (n // tm, SLAB_W // tn)
    return pl.pallas_call(
        _proj_slab_kernel,
        grid=grid,
        in_specs=[
            pl.BlockSpec((tm, D_MODEL), lambda i, j: (i, 0)),
            pl.BlockSpec((D_MODEL, tn), lambda i, j: (0, j)),
            pl.BlockSpec((1, tn), lambda i, j: (0, j)),
            pl.BlockSpec((1, tn), lambda i, j: (0, j)),
            pl.BlockSpec((D_MODEL, LANES), lambda i, j: (0, 0)),
            pl.BlockSpec((1, LANES), lambda i, j: (0, 0)),
            pl.BlockSpec((LANES, QA_W), lambda i, j: (0, 0)),
            pl.BlockSpec((1, QA_W), lambda i, j: (0, 0)),
        ],
        out_specs=[
            pl.BlockSpec((tm, tn), lambda i, j: (i, j)),
            pl.BlockSpec((tm, QA_W), lambda i, j: (i, 0)),
        ],
        out_shape=[
            jax.ShapeDtypeStruct((n, SLAB_W), BF16),
            jax.ShapeDtypeStruct((n, QA_W), F32),
        ],
        scratch_shapes=[pltpu.VMEM((tm, D_MODEL), BF16)],
        compiler_params=_cparams(("parallel", "arbitrary")),
        name="proj_slab",
    )(x2, p["w_slab"], p["b_slab"], p["s_slab"], p["w_lr"], p["b_lr"], p["w_alpha"], p["b_alpha"])


def _proj_kv_kernel(x_ref, wk_ref, wv_ref, bk_ref, bv_ref, k_ref, v_ref):
    xb = x_ref[...].astype(BF16)
    k_ref[...] = jnp.dot(xb, wk_ref[...], preferred_element_type=F32) + bk_ref[...]
    v_ref[...] = jnp.dot(xb, wv_ref[...], preferred_element_type=F32) + bv_ref[...]


def _proj_kv(x2, p):
    n = x2.shape[0]
    tm = min(512, n)
    const = lambda i: (0, 0)
    return pl.pallas_call(
        _proj_kv_kernel,
        grid=(n // tm,),
        in_specs=[
            pl.BlockSpec((tm, D_MODEL), lambda i: (i, 0)),
            pl.BlockSpec((D_MODEL, KB_W), const),
            pl.BlockSpec((D_MODEL, VB_W), const),
            pl.BlockSpec((1, KB_W), const),
            pl.BlockSpec((1, VB_W), const),
        ],
        out_specs=[
            pl.BlockSpec((tm, KB_W), lambda i: (i, 0)),
            pl.BlockSpec((tm, VB_W), lambda i: (i, 0)),
        ],
        out_shape=[
            jax.ShapeDtypeStruct((n, KB_W), F32),
            jax.ShapeDtypeStruct((n, VB_W), F32),
        ],
        compiler_params=_cparams(("parallel",)),
        name="proj_kv",
    )(x2, p["w_k"], p["w_v"], p["b_k"], p["b_v"])


def _gla_tables(L):
    nl = int(math.log2(L))
    t = np.arange(L)
    D = np.zeros(((nl + 2) * L, L), np.float32)
    masks = np.zeros((nl + 1, L, L), np.float32)
    for l in range(nl):
        m = L >> (l + 1)
        grp = t // (2 * m)
        mid = grp * 2 * m + m - 1
        upper = (t % (2 * m)) >= m
        for r in range(L):
            if upper[r]:
                D[l * L + r, mid[r] + 1:r + 1] = 1.0
            else:
                D[l * L + r, r + 1:mid[r] + 1] = 1.0
        masks[l] = (upper[:, None] & ~upper[None, :] & (grp[:, None] == grp[None, :])).astype(np.float32)
    D[nl * L:(nl + 1) * L] = np.tril(np.ones((L, L), np.float32))
    D[(nl + 1) * L:] = np.triu(np.ones((L, L), np.float32), 1)
    masks[nl] = np.eye(L, dtype=np.float32)
    return jnp.asarray(D, BF16), jnp.asarray(masks, F32)


def _gla_kernel(q_ref, k_ref, v_ref, la_ref, s0_ref, g_ref, d_ref, m_ref, o_ref, s_ref, *, L, n_chunks):
    c = pl.program_id(1)
    nl = int(math.log2(L))

    @pl.when(c == 0)
    def _():
        s_ref[...] = s0_ref[...]

    dmat = d_ref[...]
    ones_col = jnp.ones((L, LANES), BF16)
    g = g_ref[...]
    nt = (((1,), (1,)), ((), ()))
    tn = (((0,), (0,)), ((), ()))

    def chunk(ci, carry):
        r0 = pl.multiple_of(ci * L, L)
        rows = pl.ds(r0, L)
        for h in range(H_A):
            kc = slice(h * DK_A, (h + 1) * DK_A)
            vc = slice(h * DV_A, (h + 1) * DV_A)
            q = q_ref[rows, kc].astype(F32)
            k = k_ref[rows, kc].astype(F32)
            v = v_ref[rows, vc]
            la = la_ref[rows, kc]
            la_hi = la.astype(BF16)
            la_lo = (la - la_hi.astype(F32)).astype(BF16)
            e = (jnp.dot(dmat, la_hi, preferred_element_type=F32)
                 + jnp.dot(dmat, la_lo, preferred_element_type=F32))
            x = jnp.exp(e)
            a = m_ref[nl] * lax.dot_general(q.astype(BF16), k.astype(BF16), nt, preferred_element_type=F32)
            for l in range(nl):
                xl = x[l * L:(l + 1) * L]
                a = a + m_ref[l] * lax.dot_general((q * xl).astype(BF16), (k * xl).astype(BF16), nt,
                                                   preferred_element_type=F32)
            xb = x[nl * L:(nl + 1) * L]
            xs = x[(nl + 1) * L:]
            s_old = s_ref[0, h]
            o = (jnp.dot((q * xb).astype(BF16), s_old.astype(BF16), preferred_element_type=F32)
                 + jnp.dot(a.astype(BF16), v, preferred_element_type=F32))
            bl = (lax.dot_general(la_hi, ones_col, tn, preferred_element_type=F32)
                  + lax.dot_general(la_lo, ones_col, tn, preferred_element_type=F32))
            dec = jnp.exp(bl[:, 0:1])
            s_ref[0, h] = dec * s_old + lax.dot_general((k * xs).astype(BF16), v, tn, preferred_element_type=F32)
            ms = jnp.mean(o * o, axis=-1, keepdims=True)
            o_ref[rows, vc] = (o * lax.rsqrt(ms + EPS) * g).astype(BF16)
        return carry

    lax.fori_loop(0, n_chunks, chunk, 0)


def _gla(slab, loga, s0, g_norm, batch, t_len):
    L = min(CHUNK, t_len)
    tb = min(512, t_len)
    nb = t_len // tb
    dmat, masks = _gla_tables(L)
    kern = functools.partial(_gla_kernel, L=L, n_chunks=tb // L)
    row = lambda b, c: b * nb + c
    return pl.pallas_call(
        kern,
        grid=(batch, nb),
        in_specs=[
            pl.BlockSpec((tb, QA_W), lambda b, c: (row(b, c), SLAB_QA // QA_W)),
            pl.BlockSpec((tb, KA_W), lambda b, c: (row(b, c), SLAB_KA // KA_W)),
            pl.BlockSpec((tb, VA_W), lambda b, c: (row(b, c), SLAB_VA // VA_W)),
            pl.BlockSpec((tb, QA_W), lambda b, c: (row(b, c), 0)),
            pl.BlockSpec((1, H_A, DK_A, DV_A), lambda b, c: (b, 0, 0, 0)),
            pl.BlockSpec((1, DV_A), lambda b, c: (0, 0)),
            pl.BlockSpec(dmat.shape, lambda b, c: (0, 0)),
            pl.BlockSpec(masks.shape, lambda b, c: (0, 0, 0)),
        ],
        out_specs=[
            pl.BlockSpec((tb, VA_W), lambda b, c: (row(b, c), 0)),
            pl.BlockSpec((1, H_A, DK_A, DV_A), lambda b, c: (b, 0, 0, 0)),
        ],
        out_shape=[
            jax.ShapeDtypeStruct((batch * t_len, VA_W), BF16),
            jax.ShapeDtypeStruct((batch, H_A, DK_A, DV_A), F32),
        ],
        compiler_params=_cparams(("parallel", "arbitrary")),
        name="gla",
    )(slab, slab, slab, loga, s0, g_norm, dmat, masks)


def _lambda_from(dl_ref, lam_init):
    dl = dl_ref[...]
    a = jnp.sum(dl[0:1] * dl[1:2], axis=-1, keepdims=True)
    b = jnp.sum(dl[2:3] * dl[3:4], axis=-1, keepdims=True)
    return jnp.exp(a) - jnp.exp(b) + lam_init


def _split_maps(q):
    lane = lax.broadcasted_iota(I32, q.shape, 1)
    zero = jnp.zeros_like(q)
    return jnp.concatenate([jnp.where(lane < D_B, q, zero), jnp.where(lane >= D_B, q, zero)], axis=0)


def _scores(qq, kblk, q_pos, k_pos, slope):
    s = lax.dot_general(qq, kblk, (((1,), (1,)), ((), ())), preferred_element_type=F32)
    dist = jnp.abs(q_pos - k_pos).astype(F32)
    s = s - slope * dist
    visible = (k_pos // CHUNK) <= (q_pos // CHUNK)
    return jnp.where(visible, s, NEG_INF)


def _finish_heads(acc, l, lam, g, tq, lam_init):
    o = acc / l
    out = o[:tq] - lam * o[tq:]
    ms = jnp.mean(out * out, axis=-1, keepdims=True)
    return (out * lax.rsqrt(ms + EPS) * g * (1.0 - lam_init)).astype(BF16)


def _attn_prompt_kernel(q_ref, k_ref, v_ref, g_ref, dl_ref, o_ref, kb_ref, vb_ref, m_ref, l_ref, acc_ref,
                        *, tq, tk, lam_init):
    h = pl.program_id(1)
    qi = pl.program_id(2)

    @pl.when(qi == 0)
    def _():
        kb_ref[...] = k_ref[0].astype(BF16)
        vb_ref[...] = v_ref[0].astype(BF16)

    slope = jnp.exp2(-(h + 1).astype(F32))
    qq = _split_maps(q_ref[...])
    q_pos = qi * tq + lax.broadcasted_iota(I32, (2 * tq, 1), 0) % tq
    m_ref[...] = jnp.full(m_ref.shape, -1e30, F32)
    l_ref[...] = jnp.zeros(l_ref.shape, F32)
    acc_ref[...] = jnp.zeros(acc_ref.shape, F32)

    def body(kv, carry):
        k0 = pl.multiple_of(kv * tk, tk)
        k_pos = k0 + lax.broadcasted_iota(I32, (1, tk), 1)
        s = _scores(qq, kb_ref[pl.ds(k0, tk), :], q_pos, k_pos, slope)
        m_old = m_ref[...]
        m_new = jnp.maximum(m_old, jnp.max(s, axis=-1, keepdims=True))
        alpha = jnp.exp(m_old - m_new)
        p = jnp.exp(s - m_new)
        l_ref[...] = alpha * l_ref[...] + jnp.sum(p, axis=-1, keepdims=True)
        acc_ref[...] = alpha * acc_ref[...] + jnp.dot(p.astype(BF16), vb_ref[pl.ds(k0, tk), :],
                                                     preferred_element_type=F32)
        m_ref[...] = m_new
        return carry

    n_kv = ((qi + 1) * tq + tk - 1) // tk
    lax.fori_loop(0, n_kv, body, 0)
    lam = _lambda_from(dl_ref, lam_init)
    o_ref[...] = _finish_heads(acc_ref[...], l_ref[...], lam, g_ref[...], tq, lam_init)


def _attn_prompt(slab, kb, vb, g_norm, dl, batch, t_len, lam_init):
    tq = 256
    tk = 256
    nq = t_len // tq
    k3 = kb.reshape(batch, t_len, KB_W)
    v3 = vb.reshape(batch, t_len, VB_W)
    kern = functools.partial(_attn_prompt_kernel, tq=tq, tk=tk, lam_init=lam_init)
    hw = 2 * D_B
    return pl.pallas_call(
        kern,
        grid=(batch, H_B, nq),
        in_specs=[
            pl.BlockSpec((tq, hw), lambda b, h, q: (b * nq + q, SLAB_QB // hw + h)),
            pl.BlockSpec((1, t_len, hw), lambda b, h, q: (b, 0, h)),
            pl.BlockSpec((1, t_len, hw), lambda b, h, q: (b, 0, h)),
            pl.BlockSpec((1, hw), lambda b, h, q: (0, 0)),
            pl.BlockSpec((4, D_B), lambda b, h, q: (0, 0)),
        ],
        out_specs=pl.BlockSpec((tq, hw), lambda b, h, q: (b * nq + q, h)),
        out_shape=jax.ShapeDtypeStruct((batch * t_len, VB_W), BF16),
        scratch_shapes=[
            pltpu.VMEM((t_len, hw), BF16),
            pltpu.VMEM((t_len, hw), BF16),
            pltpu.VMEM((2 * tq, 1), F32),
            pltpu.VMEM((2 * tq, 1), F32),
            pltpu.VMEM((2 * tq, hw), F32),
        ],
        compiler_params=_cparams(("parallel", "parallel", "arbitrary")),
        name="attn_prompt",
    )(slab, k3, v3, g_norm, dl)


def _attn_sample_kernel(q_ref, kc_ref, vc_ref, kn_ref, vn_ref, g_ref, dl_ref, o_ref, *, tq, past, lam_init):
    h = pl.program_id(1)
    slope = jnp.exp2(-(h + 1).astype(F32))
    qq = _split_maps(q_ref[...])
    q_pos = past + lax.broadcasted_iota(I32, (2 * tq, 1), 0) % tq
    kc_pos = lax.broadcasted_iota(I32, (1, past), 1)
    kn_pos = past + lax.broadcasted_iota(I32, (1, tq), 1)
    sc = _scores(qq, kc_ref[0].astype(BF16), q_pos, kc_pos, slope)
    sn = _scores(qq, kn_ref[0].astype(BF16), q_pos, kn_pos, slope)
    m = jnp.maximum(jnp.max(sc, axis=-1, keepdims=True), jnp.max(sn, axis=-1, keepdims=True))
    pc = jnp.exp(sc - m)
    pn = jnp.exp(sn - m)
    l = jnp.sum(pc, axis=-1, keepdims=True) + jnp.sum(pn, axis=-1, keepdims=True)
    acc = (jnp.dot(pc.astype(BF16), vc_ref[0].astype(BF16), preferred_element_type=F32)
           + jnp.dot(pn.astype(BF16), vn_ref[0].astype(BF16), preferred_element_type=F32))
    lam = _lambda_from(dl_ref, lam_init)
    o_ref[...] = _finish_heads(acc, l, lam, g_ref[...], tq, lam_init)


def _attn_sample(slab, kb, vb, cache_k, cache_v, g_norm, dl, batch, t_len, lam_init):
    past = cache_k.shape[1]
    hw = 2 * D_B
    k3 = kb.reshape(batch, t_len, KB_W)
    v3 = vb.reshape(batch, t_len, VB_W)
    kern = functools.partial(_attn_sample_kernel, tq=t_len, past=past, lam_init=lam_init)
    return pl.pallas_call(
        kern,
        grid=(batch, H_B),
        in_specs=[
            pl.BlockSpec((t_len, hw), lambda b, h: (b, SLAB_QB // hw + h)),
            pl.BlockSpec((1, past, hw), lambda b, h: (b, 0, h)),
            pl.BlockSpec((1, past, hw), lambda b, h: (b, 0, h)),
            pl.BlockSpec((1, t_len, hw), lambda b, h: (b, 0, h)),
            pl.BlockSpec((1, t_len, hw), lambda b, h: (b, 0, h)),
            pl.BlockSpec((1, hw), lambda b, h: (0, 0)),
            pl.BlockSpec((4, D_B), lambda b, h: (0, 0)),
        ],
        out_specs=pl.BlockSpec((t_len, hw), lambda b, h: (b, h)),
        out_shape=jax.ShapeDtypeStruct((batch * t_len, VB_W), BF16),
        compiler_params=_cparams(("parallel", "parallel")),
        name="attn_sample",
    )(slab, cache_k, cache_v, k3, v3, g_norm, dl)


def _finish_kernel(x_ref, oa_ref, ra_ref, ob_ref, gt_ref, wpa_ref, wpb_ref, wo_ref, g_ref, b_ref,
                   wrh_ref, wrl_ref, br_ref, h_ref, hp_ref, idx_ref, tw_ref, cnt_ref):
    i = pl.program_id(0)
    ra = ra_ref[...].astype(F32)
    ua = ra * _sigmoid(ra) * oa_ref[...].astype(F32)
    ya = jnp.dot(ua.astype(BF16), wpa_ref[...], preferred_element_type=F32)
    yb = jnp.dot(ob_ref[...], wpb_ref[...], preferred_element_type=F32)
    gt = gt_ref[...].astype(F32)
    mixed = _sigmoid(gt[:, :D_MODEL]) * ya + _sigmoid(gt[:, D_MODEL:]) * yb
    mix = jnp.dot(mixed.astype(BF16), wo_ref[...], preferred_element_type=F32)
    hh = _layer_norm(DEEPNORM_ALPHA * x_ref[...] + mix, g_ref[...], b_ref[...])
    h_ref[...] = hh
    hp_ref[...] = _pack_bf16_pairs(hh)

    h_hi = hh.astype(BF16)
    h_lo = (hh - h_hi.astype(F32)).astype(BF16)
    lg = (jnp.dot(h_hi, wrh_ref[...], preferred_element_type=F32)
          + jnp.dot(h_lo, wrh_ref[...], preferred_element_type=F32)
          + jnp.dot(h_hi, wrl_ref[...], preferred_element_type=F32)) + br_ref[...]
    tm = lg.shape[0]
    lane = lax.broadcasted_iota(I32, (tm, LANES), 1)
    lane_f = lane.astype(F32)
    work = lg
    vals, idxs = [], []
    cnt = jnp.zeros((tm, LANES), F32)
    for _ in range(TOP_K):
        mx = jnp.max(work, axis=-1, keepdims=True)
        ix = jnp.min(jnp.where(work == mx, lane_f, float(LANES)), axis=-1, keepdims=True)
        hit = lane_f == ix
        cnt = cnt + hit.astype(F32)
        work = jnp.where(hit, NEG_INF, work)
        vals.append(mx)
        idxs.append(ix)
    es = [jnp.exp(v - vals[0]) for v in vals]
    den = es[0] + es[1] + es[2] + es[3]
    idx_full = jnp.zeros((tm, LANES), F32)
    tw_full = jnp.zeros((tm, LANES), F32)
    for r in range(TOP_K):
        idx_full = jnp.where(lane == r, idxs[r], idx_full)
        tw_full = jnp.where(lane == r, es[r] / den, tw_full)
    idx_ref[...] = idx_full[:, :TOP_K].astype(I32)
    tw_ref[...] = tw_full[:, :TOP_K]

    @pl.when(i == 0)
    def _():
        cnt_ref[...] = jnp.zeros(cnt_ref.shape, F32)

    cnt_ref[...] += jnp.sum(cnt, axis=0, keepdims=True)


def _finish(x2, oa, slab, ob, p):
    n = x2.shape[0]
    tm = min(256, n)
    const = lambda i: (0, 0)
    return pl.pallas_call(
        _finish_kernel,
        grid=(n // tm,),
        in_specs=[
            pl.BlockSpec((tm, D_MODEL), lambda i: (i, 0)),
            pl.BlockSpec((tm, VA_W), lambda i: (i, 0)),
            pl.BlockSpec((tm, RA_W), lambda i: (i, SLAB_RA // RA_W)),
            pl.BlockSpec((tm, VB_W), lambda i: (i, 0)),
            pl.BlockSpec((tm, GT_W), lambda i: (i, SLAB_GT // GT_W)),
            pl.BlockSpec((VA_W, D_MODEL), const),
            pl.BlockSpec((VB_W, D_MODEL), const),
            pl.BlockSpec((D_MODEL, D_MODEL), const),
            pl.BlockSpec((1, D_MODEL), const),
            pl.BlockSpec((1, D_MODEL), const),
            pl.BlockSpec((D_MODEL, LANES), const),
            pl.BlockSpec((D_MODEL, LANES), const),
            pl.BlockSpec((1, LANES), const),
        ],
        out_specs=[
            pl.BlockSpec((tm, D_MODEL), lambda i: (i, 0)),
            pl.BlockSpec((tm, D_MODEL // 2), lambda i: (i, 0)),
            pl.BlockSpec((tm, TOP_K), lambda i: (i, 0)),
            pl.BlockSpec((tm, TOP_K), lambda i: (i, 0)),
            pl.BlockSpec((1, LANES), const),
        ],
        out_shape=[
            jax.ShapeDtypeStruct((n, D_MODEL), F32),
            jax.ShapeDtypeStruct((n, D_MODEL // 2), U32),
            jax.ShapeDtypeStruct((n, TOP_K), I32),
            jax.ShapeDtypeStruct((n, TOP_K), F32),
            jax.ShapeDtypeStruct((1, LANES), F32),
        ],
        compiler_params=_cparams(("arbitrary",)),
        name="finish",
    )(x2, oa, slab, ob, slab, p["w_pa"], p["w_pb"], p["w_o"], p["ln1_g"], p["ln1_b"],
      p["w_r_hi"], p["w_r_lo"], p["b_r"])


def _dest_kernel(idx_ref, ps_ref, lt_ref, dest_ref, carry_ref):
    i = pl.program_id(0)

    @pl.when(i == 0)
    def _():
        carry_ref[...] = jnp.zeros(carry_ref.shape, F32)

    tm = idx_ref.shape[0]
    lane = lax.broadcasted_iota(I32, (tm, LANES), 1)
    idx = idx_ref[...]
    hits = [lane == idx[:, k:k + 1] for k in range(TOP_K)]
    cnt = jnp.zeros((tm, LANES), F32)
    for hit in hits:
        cnt = cnt + hit.astype(F32)
    before = jnp.dot(lt_ref[...], cnt.astype(BF16), preferred_element_type=F32)
    base = before + carry_ref[...] + ps_ref[...]
    dest_full = jnp.zeros((tm, LANES), F32)
    for k in range(TOP_K):
        d = jnp.sum(jnp.where(hits[k], base, 0.0), axis=-1, keepdims=True)
        dest_full = jnp.where(lane == k, d, dest_full)
    dest_ref[...] = dest_full[:, :TOP_K].astype(I32)
    carry_ref[...] += jnp.sum(cnt, axis=0, keepdims=True)


def _dest(top_idx, pstart):
    n = top_idx.shape[0]
    tm = min(512, n)
    ltri = jnp.asarray(np.tril(np.ones((tm, tm), np.float32), -1), BF16)
    return pl.pallas_call(
        _dest_kernel,
        grid=(n // tm,),
        in_specs=[
            pl.BlockSpec((tm, TOP_K), lambda i: (i, 0)),
            pl.BlockSpec((1, LANES), lambda i: (0, 0)),
            pl.BlockSpec((tm, tm), lambda i: (0, 0)),
        ],
        out_specs=pl.BlockSpec((tm, TOP_K), lambda i: (i, 0)),
        out_shape=jax.ShapeDtypeStruct((n, TOP_K), I32),
        scratch_shapes=[pltpu.VMEM((1, LANES), F32)],
        compiler_params=_cparams(("arbitrary",)),
        name="dest",
    )(top_idx, pstart, ltri)


def _scatter_kernel(dest_ref, hp_ref, rows_in_ref, rows_ref, sem):
    del rows_in_ref
    tm = hp_ref.shape[0]

    def row_copy(t, d):
        return pltpu.make_async_copy(hp_ref.at[pl.ds(t, 1)], rows_ref.at[pl.ds(d, 1)], sem)

    def issue(t, carry):
        for k in range(TOP_K):
            row_copy(t, dest_ref[t * TOP_K + k]).start()
        return carry

    def drain(t, carry):
        for k in range(TOP_K):
            row_copy(t, dest_ref[t * TOP_K + k]).wait()
        return carry

    lax.fori_loop(0, tm, issue, 0)
    lax.fori_loop(0, tm, drain, 0)


def _scatter(dest_flat, hp, n_rows):
    n = hp.shape[0]
    tm = min(512, n)
    rows0 = jnp.zeros((n_rows, D_MODEL // 2), U32)
    return pl.pallas_call(
        _scatter_kernel,
        grid=(n // tm,),
        in_specs=[
            pl.BlockSpec((tm * TOP_K,), lambda i: (i,), memory_space=pltpu.SMEM),
            pl.BlockSpec((tm, D_MODEL // 2), lambda i: (i, 0)),
            pl.BlockSpec(memory_space=pl.ANY),
        ],
        out_specs=pl.BlockSpec(memory_space=pl.ANY),
        out_shape=jax.ShapeDtypeStruct((n_rows, D_MODEL // 2), U32),
        scratch_shapes=[pltpu.SemaphoreType.DMA(())],
        input_output_aliases={2: 0},
        compiler_params=_cparams(("arbitrary",)),
        name="scatter",
    )(dest_flat, hp, rows0)


def _expert_kernel(be_ref, nu_ref, rows_ref, wgu_ref, bgu_ref, wd_ref, bd_ref, out_ref):
    del be_ref
    i = pl.program_id(0)

    @pl.when(i < nu_ref[0])
    def _():
        x = _unpack_bf16_pairs(rows_ref[...]).astype(BF16)
        gu = jnp.dot(x, wgu_ref[0], preferred_element_type=F32) + bgu_ref[0]
        gate = jnp.minimum(gu[:, :D_FF], SWIGLU_LIMIT)
        up = jnp.clip(gu[:, D_FF:], -SWIGLU_LIMIT, SWIGLU_LIMIT)
        hidden = (up + 1.0) * (gate * _sigmoid(SWIGLU_ALPHA * gate))
        o = jnp.dot(hidden.astype(BF16), wd_ref[0], preferred_element_type=F32) + bd_ref[0]
        out_ref[...] = _pack_bf16_pairs(o)

    @pl.when(i >= nu_ref[0])
    def _():
        out_ref[...] = jnp.zeros(out_ref.shape, U32)


def _experts(rows, block_e, n_used, p):
    n_rows = rows.shape[0]
    bm = EXPERT_BM
    n_blocks = n_rows // bm
    grid_spec = pltpu.PrefetchScalarGridSpec(
        num_scalar_prefetch=2,
        grid=(n_blocks,),
        in_specs=[
            pl.BlockSpec((bm, D_MODEL // 2), lambda i, be, nu: (i, 0)),
            pl.BlockSpec((1, D_MODEL, 2 * D_FF), lambda i, be, nu: (be[i], 0, 0)),
            pl.BlockSpec((1, 1, 2 * D_FF), lambda i, be, nu: (be[i], 0, 0)),
            pl.BlockSpec((1, D_FF, D_MODEL), lambda i, be, nu: (be[i], 0, 0)),
            pl.BlockSpec((1, 1, D_MODEL), lambda i, be, nu: (be[i], 0, 0)),
        ],
        out_specs=pl.BlockSpec((bm, D_MODEL // 2), lambda i, be, nu: (i, 0)),
    )
    return pl.pallas_call(
        _expert_kernel,
        grid_spec=grid_spec,
        out_shape=jax.ShapeDtypeStruct((n_rows, D_MODEL // 2), U32),
        compiler_params=_cparams(("arbitrary",)),
        name="experts",
    )(block_e, n_used, rows, p["w_gu"], p["b_gu"], p["w_down"], p["b_down"])


def _combine_kernel(dest_ref, tw_ref, h_ref, rows_ref, g_ref, b_ref, y_ref, buf_ref, sem):
    tm = h_ref.shape[0]

    def row_copy(t, k, d):
        return pltpu.make_async_copy(rows_ref.at[pl.ds(d, 1)], buf_ref.at[k, pl.ds(t, 1)], sem)

    def issue(t, carry):
        for k in range(TOP_K):
            row_copy(t, k, dest_ref[t * TOP_K + k]).start()
        return carry

    def drain(t, carry):
        for k in range(TOP_K):
            row_copy(t, k, dest_ref[t * TOP_K + k]).wait()
        return carry

    lax.fori_loop(0, tm, issue, 0)
    lax.fori_loop(0, tm, drain, 0)
    tw = tw_ref[...]
    f = jnp.zeros((tm, D_MODEL), F32)
    for k in range(TOP_K):
        f = f + tw[:, k:k + 1] * _unpack_bf16_pairs(buf_ref[k])
    y_ref[...] = _layer_norm(DEEPNORM_ALPHA * h_ref[...] + f, g_ref[...], b_ref[...])


def _combine(dest_flat, top_w, hh, out_rows, p):
    n = hh.shape[0]
    tm = min(256, n)
    const = lambda i: (0, 0)
    return pl.pallas_call(
        _combine_kernel,
        grid=(n // tm,),
        in_specs=[
            pl.BlockSpec((tm * TOP_K,), lambda i: (i,), memory_space=pltpu.SMEM),
            pl.BlockSpec((tm, TOP_K), lambda i: (i, 0)),
            pl.BlockSpec((tm, D_MODEL), lambda i: (i, 0)),
            pl.BlockSpec(memory_space=pl.ANY),
            pl.BlockSpec((1, D_MODEL), const),
            pl.BlockSpec((1, D_MODEL), const),
        ],
        out_specs=pl.BlockSpec((tm, D_MODEL), lambda i: (i, 0)),
        out_shape=jax.ShapeDtypeStruct((n, D_MODEL), F32),
        scratch_shapes=[pltpu.VMEM((TOP_K, tm, D_MODEL // 2), U32), pltpu.SemaphoreType.DMA(())],
        compiler_params=_cparams(("arbitrary",)),
        name="combine",
    )(dest_flat, top_w, hh, out_rows, p["ln2_g"], p["ln2_b"])


def _moe(hh, hp, top_idx, top_w, counts, p):
    n = hh.shape[0]
    bm = EXPERT_BM
    n_rows = n * TOP_K + N_EXPERTS * bm
    cnt = counts[0, :N_EXPERTS].astype(I32)
    padded = (cnt + bm - 1) // bm * bm
    pend = jnp.cumsum(padded)
    pstart = pend - padded
    ps_row = jnp.zeros((1, LANES), F32).at[0, :N_EXPERTS].set(pstart.astype(F32))
    block_e = jnp.minimum(jnp.searchsorted(pend, jnp.arange(n_rows // bm, dtype=I32) * bm, side="right"),
                          N_EXPERTS - 1).astype(I32)
    n_used = (pend[-1:] // bm).astype(I32)
    dest = _dest(top_idx, ps_row).reshape(-1)
    rows = _scatter(dest, hp, n_rows)
    out_rows = _experts(rows, block_e, n_used, p)
    return _combine(dest, top_w, hh, out_rows, p)


def _prep_params(w_in, b_in, w_alpha, b_alpha, gla_norm_g, diff_norm_g, w_pa, w_pb, w_o, ln1_g, ln1_b,
                 w_router, b_router, w_gu, b_gu, w_down, b_down, ln2_g, ln2_b):
    o = IN_OFFS
    seg = lambda a, i: a[..., o[i]:o[i + 1]]
    order = (0, 1, 2, 3, 5, 8)
    w_slab = jnp.concatenate([seg(w_in, i) for i in order], axis=1).astype(BF16)
    b_slab = jnp.concatenate([seg(b_in, i) for i in order])[None, :]
    scale = np.ones((1, SLAB_W), np.float32)
    scale[0, SLAB_QA:SLAB_QA + QA_W] = DK_A ** -0.5
    scale[0, SLAB_QB:SLAB_QB + QB_W] = D_B ** -0.5
    w_lr = jnp.zeros((D_MODEL, LANES), F32).at[:, :LR_W].set(seg(w_in, 4)).astype(BF16)
    b_lr = jnp.zeros((1, LANES), F32).at[0, :LR_W].set(seg(b_in, 4))
    w_al = jnp.zeros((LANES, QA_W), F32).at[:LR_W].set(w_alpha).astype(BF16)
    w_r = jnp.zeros((D_MODEL, LANES), F32).at[:, :N_EXPERTS].set(w_router)
    w_r_hi = w_r.astype(BF16)
    w_r_lo = (w_r - w_r_hi.astype(F32)).astype(BF16)
    b_r = jnp.full((1, LANES), NEG_INF, F32).at[0, :N_EXPERTS].set(b_router)
    return dict(
        w_slab=w_slab, b_slab=b_slab, s_slab=jnp.asarray(scale),
        w_lr=w_lr, b_lr=b_lr, w_alpha=w_al, b_alpha=b_alpha[None, :],
        w_k=seg(w_in, 6).astype(BF16), w_v=seg(w_in, 7).astype(BF16),
        b_k=seg(b_in, 6)[None, :], b_v=seg(b_in, 7)[None, :],
        gla_g=gla_norm_g[None, :], diff_g=diff_norm_g[None, :],
        w_pa=w_pa.astype(BF16), w_pb=w_pb.astype(BF16), w_o=w_o.astype(BF16),
        ln1_g=ln1_g[None, :], ln1_b=ln1_b[None, :],
        w_r_hi=w_r_hi, w_r_lo=w_r_lo, b_r=b_r,
        w_gu=w_gu.astype(BF16), b_gu=b_gu[:, None, :], w_down=w_down.astype(BF16), b_down=b_down[:, None, :],
        ln2_g=ln2_g[None, :], ln2_b=ln2_b[None, :],
    )


def _layer(x, p, dl, lam_init, s0, cache_k, cache_v):
    batch, t_len, _ = x.shape
    x2 = x.reshape(batch * t_len, D_MODEL)
    slab, loga = _proj_slab(x2, p)
    kb, vb = _proj_kv(x2, p)
    oa, s_new = _gla(slab, loga, s0, p["gla_g"], batch, t_len)
    if cache_k is None:
        ob = _attn_prompt(slab, kb, vb, p["diff_g"], dl, batch, t_len, lam_init)
    else:
        ob = _attn_sample(slab, kb, vb, cache_k, cache_v, p["diff_g"], dl, batch, t_len, lam_init)
    hh, hp, top_idx, top_w, counts = _finish(x2, oa, slab, ob, p)
    y = _moe(hh, hp, top_idx, top_w, counts, p)
    return (y.reshape(batch, t_len, D_MODEL), s_new,
            kb.reshape(batch, t_len, H_B, 2, D_B), vb.reshape(batch, t_len, H_B, 2 * D_B))


def kernel(x_prompt, x_sample, cache_k, cache_v, state_gla, w_in, b_in, w_alpha, b_alpha, gla_norm_g,
           diff_lambda, diff_norm_g, w_pa, w_pb, w_o, ln1_g, ln1_b, w_router, b_router, w_gu, b_gu,
           w_down, b_down, ln2_g, ln2_b):
    assert w_in.shape[0] == DEPTH == 1
    l = 0
    lam_init = 0.8 - 0.6 * math.exp(-0.3 * l)
    p = _prep_params(w_in[l], b_in[l], w_alpha[l], b_alpha[l], gla_norm_g[l], diff_norm_g[l], w_pa[l],
                     w_pb[l], w_o[l], ln1_g[l], ln1_b[l], w_router[l], b_router[l], w_gu[l], b_gu[l],
                     w_down[l], b_down[l], ln2_g[l], ln2_b[l])
    dl = diff_lambda[l]
    bp = x_prompt.shape[0]
    bs, ts = x_sample.shape[0], x_sample.shape[1]
    past = cache_k.shape[2]
    yp, s_p, k_p, v_p = _layer(x_prompt, p, dl, lam_init, jnp.zeros((bp, H_A, DK_A, DV_A), F32), None, None)
    ck = cache_k[l].reshape(bs, past, KB_W)
    cv = cache_v[l].reshape(bs, past, VB_W)
    ys, s_s, k_s, v_s = _layer(x_sample, p, dl, lam_init, state_gla[l], ck, cv)
    return (yp, ys, s_p[None], k_p[None], v_p[None], s_s[None], k_s[None], v_s[None])
```

```python
import functools
import math

import numpy as np
import jax
import jax.numpy as jnp
from jax import lax
from jax.experimental import pallas as pl
from jax.experimental.pallas import tpu as pltpu

F32 = jnp.float32
BF16 = jnp.bfloat16
U32 = jnp.uint32
I32 = jnp.int32

D_MODEL = 1024
CHUNK = 64
H_A = 4
DK_A = 128
DV_A = 256
GATE_RANK = 16
GATE_TAU = 16.0
H_B = 8
D_B = 64
N_EXPERTS = 32
TOP_K = 4
D_FF = D_MODEL
SWIGLU_LIMIT = 7.0
SWIGLU_ALPHA = 1.702
EPS = 1e-5
DEPTH = 1
DEEPNORM_ALPHA = (2.0 * DEPTH) ** 0.25

QA_W = H_A * DK_A
KA_W = H_A * DK_A
VA_W = H_A * DV_A
RA_W = H_A * DV_A
LR_W = GATE_RANK
QB_W = H_B * 2 * D_B
KB_W = H_B * 2 * D_B
VB_W = H_B * 2 * D_B
GT_W = 2 * D_MODEL
IN_SIZES = (QA_W, KA_W, VA_W, RA_W, LR_W, QB_W, KB_W, VB_W, GT_W)
IN_OFFS = tuple(int(v) for v in np.cumsum((0,) + IN_SIZES))

SLAB_QA, SLAB_KA, SLAB_VA, SLAB_RA, SLAB_QB, SLAB_GT = 0, 512, 1024, 2048, 3072, 4096
SLAB_W = 6144
LANES = 128
SUBLANES = 8
EXPERT_BM = 512
VMEM_LIMIT = 56 * 1024 * 1024

NEG_INF = float("-inf")
LOG2E = math.log2(math.e)


def _cparams(sem):
    return pltpu.CompilerParams(dimension_semantics=sem, vmem_limit_bytes=VMEM_LIMIT)


def _sigmoid(x):
    return 1.0 / (1.0 + jnp.exp(-x))


def _pack_bf16_pairs(x):
    n = x.shape[1] // 2
    xb = x.astype(BF16).astype(F32)
    lo = pltpu.bitcast(xb[:, :n], U32) >> 16
    hi = pltpu.bitcast(xb[:, n:], U32) & jnp.uint32(0xFFFF0000)
    return hi | lo


def _unpack_bf16_pairs(r):
    lo = pltpu.bitcast(r << 16, F32)
    hi = pltpu.bitcast(r & jnp.uint32(0xFFFF0000), F32)
    return jnp.concatenate([lo, hi], axis=1)


def _layer_norm(y, g, b):
    mu = jnp.mean(y, axis=-1, keepdims=True)
    d = y - mu
    var = jnp.mean(d * d, axis=-1, keepdims=True)
    return d * lax.rsqrt(var + EPS) * g + b


def _proj_slab_kernel(x_ref, w_ref, b_ref, s_ref, wlr_ref, blr_ref, wal_ref, bal_ref,
                      slab_ref, loga_ref, xb_ref):
    j = pl.program_id(1)

    @pl.when(j == 0)
    def _():
        xb_ref[...] = x_ref[...].astype(BF16)
        lra = jnp.dot(xb_ref[...], wlr_ref[...], preferred_element_type=F32) + blr_ref[...]
        z = jnp.dot(lra.astype(BF16), wal_ref[...], preferred_element_type=F32) + bal_ref[...]
        loga_ref[...] = (jnp.minimum(z, 0.0) - jnp.log(1.0 + jnp.exp(-jnp.abs(z)))) * (1.0 / GATE_TAU)

    acc = jnp.dot(xb_ref[...], w_ref[...], preferred_element_type=F32)
    slab_ref[...] = ((acc + b_ref[...]) * s_ref[...]).astype(BF16)


def _proj_slab(x2, p):
    n = x2.shape[0]
    tm = min(1024, n)
    tn = 512
    grid = (n // tm, SLAB_W // tn)
    return pl.pallas_call(
        _proj_slab_kernel,
        grid=grid,
        in_specs=[
            pl.BlockSpec((tm, D_MODEL), lambda i, j: (i, 0)),
            pl.BlockSpec((D_MODEL, tn), lambda i, j: (0, j)),
            pl.BlockSpec((1, tn), lambda i, j: (0, j)),
            pl.BlockSpec((1, tn), lambda i, j: (0, j)),
            pl.BlockSpec((D_MODEL, LANES), lambda i, j: (0, 0)),
            pl.BlockSpec((1, LANES), lambda i, j: (0, 0)),
            pl.BlockSpec((LANES, QA_W), lambda i, j: (0, 0)),
            pl.BlockSpec((1, QA_W), lambda i, j: (0, 0)),
        ],
        out_specs=[
            pl.BlockSpec((tm, tn), lambda i, j: (i, j)),
            pl.BlockSpec((tm, QA_W), lambda i, j: (i, 0)),
        ],
        out_shape=[
            jax.ShapeDtypeStruct((n, SLAB_W), BF16),
            jax.ShapeDtypeStruct((n, QA_W), F32),
        ],
        scratch_shapes=[pltpu.VMEM((tm, D_MODEL), BF16)],
        compiler_params=_cparams(("parallel", "arbitrary")),
        name="proj_slab",
    )(x2, p["w_slab"], p["b_slab"], p["s_slab"], p["w_lr"], p["b_lr"], p["w_alpha"], p["b_alpha"])


def _proj_kv_kernel(x_ref, wk_ref, wv_ref, bk_ref, bv_ref, k_ref, v_ref):
    xb = x_ref[...].astype(BF16)
    k_ref[...] = jnp.dot(xb, wk_ref[...], preferred_element_type=F32) + bk_ref[...]
    v_ref[...] = jnp.dot(xb, wv_ref[...], preferred_element_type=F32) + bv_ref[...]


def _proj_kv(x2, p):
    n = x2.shape[0]
    tm = min(512, n)
    const = lambda i: (0, 0)
    return pl.pallas_call(
        _proj_kv_kernel,
        grid=(n // tm,),
        in_specs=[
            pl.BlockSpec((tm, D_MODEL), lambda i: (i, 0)),
            pl.BlockSpec((D_MODEL, KB_W), const),
            pl.BlockSpec((D_MODEL, VB_W), const),
            pl.BlockSpec((1, KB_W), const),
            pl.BlockSpec((1, VB_W), const),
        ],
        out_specs=[
            pl.BlockSpec((tm, KB_W), lambda i: (i, 0)),
            pl.BlockSpec((tm, VB_W), lambda i: (i, 0)),
        ],
        out_shape=[
            jax.ShapeDtypeStruct((n, KB_W), F32),
            jax.ShapeDtypeStruct((n, VB_W), F32),
        ],
        compiler_params=_cparams(("parallel",)),
        name="proj_kv",
    )(x2, p["w_k"], p["w_v"], p["b_k"], p["b_v"])


def _gla_tables(L):
    nl = int(math.log2(L))
    t = np.arange(L)
    D = np.zeros(((nl + 2) * L, L), np.float32)
    masks = np.zeros((nl + 1, L, L), np.float32)
    for l in range(nl):
        m = L >> (l + 1)
        grp = t // (2 * m)
        mid = grp * 2 * m + m - 1
        upper = (t % (2 * m)) >= m
        for r in range(L):
            if upper[r]:
                D[l * L + r, mid[r] + 1:r + 1] = 1.0
            else:
                D[l * L + r, r + 1:mid[r] + 1] = 1.0
        masks[l] = (upper[:, None] & ~upper[None, :] & (grp[:, None] == grp[None, :])).astype(np.float32)
    D[nl * L:(nl + 1) * L] = np.tril(np.ones((L, L), np.float32))
    D[(nl + 1) * L:] = np.triu(np.ones((L, L), np.float32), 1)
    masks[nl] = np.eye(L, dtype=np.float32)
    return jnp.asarray(D, BF16), jnp.asarray(masks, F32)


def _gla_kernel(q_ref, k_ref, v_ref, la_ref, s0_ref, g_ref, d_ref, m_ref, o_ref, s_ref, *, L, n_chunks):
    c = pl.program_id(1)
    nl = int(math.log2(L))

    @pl.when(c == 0)
    def _():
        s_ref[...] = s0_ref[...]

    dmat = d_ref[...]
    ones_col = jnp.ones((L, LANES), BF16)
    g = g_ref[...]
    nt = (((1,), (1,)), ((), ()))
    tn = (((0,), (0,)), ((), ()))

    def chunk(ci, carry):
        r0 = pl.multiple_of(ci * L, L)
        rows = pl.ds(r0, L)
        for h in range(H_A):
            kc = slice(h * DK_A, (h + 1) * DK_A)
            vc = slice(h * DV_A, (h + 1) * DV_A)
            q = q_ref[rows, kc].astype(F32)
            k = k_ref[rows, kc].astype(F32)
            v = v_ref[rows, vc]
            la = la_ref[rows, kc]
            la_hi = la.astype(BF16)
            la_lo = (la - la_hi.astype(F32)).astype(BF16)
            e = (jnp.dot(dmat, la_hi, preferred_element_type=F32)
                 + jnp.dot(dmat, la_lo, preferred_element_type=F32))
            x = jnp.exp(e)
            a = m_ref[nl] * lax.dot_general(q.astype(BF16), k.astype(BF16), nt, preferred_element_type=F32)
            for l in range(nl):
                xl = x[l * L:(l + 1) * L]
                a = a + m_ref[l] * lax.dot_general((q * xl).astype(BF16), (k * xl).astype(BF16), nt,
                                                   preferred_element_type=F32)
            xb = x[nl * L:(nl + 1) * L]
            xs = x[(nl + 1) * L:]
            s_old = s_ref[0, h]
            o = (jnp.dot((q * xb).astype(BF16), s_old.astype(BF16), preferred_element_type=F32)
                 + jnp.dot(a.astype(BF16), v, preferred_element_type=F32))
            bl = (lax.dot_general(la_hi, ones_col, tn, preferred_element_type=F32)
                  + lax.dot_general(la_lo, ones_col, tn, preferred_element_type=F32))
            dec = jnp.exp(bl[:, 0:1])
            s_ref[0, h] = dec * s_old + lax.dot_general((k * xs).astype(BF16), v, tn, preferred_element_type=F32)
            ms = jnp.mean(o * o, axis=-1, keepdims=True)
            o_ref[rows, vc] = (o * lax.rsqrt(ms + EPS) * g).astype(BF16)
        return carry

    lax.fori_loop(0, n_chunks, chunk, 0)


def _gla(slab, loga, s0, g_norm, batch, t_len):
    L = min(CHUNK, t_len)
    tb = min(512, t_len)
    nb = t_len // tb
    dmat, masks = _gla_tables(L)
    kern = functools.partial(_gla_kernel, L=L, n_chunks=tb // L)
    row = lambda b, c: b * nb + c
    return pl.pallas_call(
        kern,
        grid=(batch, nb),
        in_specs=[
            pl.BlockSpec((tb, QA_W), lambda b, c: (row(b, c), SLAB_QA // QA_W)),
            pl.BlockSpec((tb, KA_W), lambda b, c: (row(b, c), SLAB_KA // KA_W)),
            pl.BlockSpec((tb, VA_W), lambda b, c: (row(b, c), SLAB_VA // VA_W)),
            pl.BlockSpec((tb, QA_W), lambda b, c: (row(b, c), 0)),
            pl.BlockSpec((1, H_A, DK_A, DV_A), lambda b, c: (b, 0, 0, 0)),
            pl.BlockSpec((1, DV_A), lambda b, c: (0, 0)),
            pl.BlockSpec(dmat.shape, lambda b, c: (0, 0)),
            pl.BlockSpec(masks.shape, lambda b, c: (0, 0, 0)),
        ],
        out_specs=[
            pl.BlockSpec((tb, VA_W), lambda b, c: (row(b, c), 0)),
            pl.BlockSpec((1, H_A, DK_A, DV_A), lambda b, c: (b, 0, 0, 0)),
        ],
        out_shape=[
            jax.ShapeDtypeStruct((batch * t_len, VA_W), BF16),
            jax.ShapeDtypeStruct((batch, H_A, DK_A, DV_A), F32),
        ],
        compiler_params=_cparams(("parallel", "arbitrary")),
        name="gla",
    )(slab, slab, slab, loga, s0, g_norm, dmat, masks)


def _lambda_from(dl_ref, lam_init):
    dl = dl_ref[...]
    a = jnp.sum(dl[0:1] * dl[1:2], axis=-1, keepdims=True)
    b = jnp.sum(dl[2:3] * dl[3:4], axis=-1, keepdims=True)
    return jnp.exp(a) - jnp.exp(b) + lam_init


def _split_maps(q):
    lane = lax.broadcasted_iota(I32, q.shape, 1)
    zero = jnp.zeros_like(q)
    return jnp.concatenate([jnp.where(lane < D_B, q, zero), jnp.where(lane >= D_B, q, zero)], axis=0)


def _alibi_coef(h):
    c = jnp.full((1, 1), LOG2E, F32) * jnp.exp2(-(h + 1).astype(F32))
    c_hi = c.astype(BF16).astype(F32)
    return c, c_hi, c - c_hi


def _finish_heads(acc, inv_l, lam, g, tq, lam_init):
    o = acc * inv_l
    out = o[:tq] - lam * o[tq:]
    ms = jnp.mean(out * out, axis=-1, keepdims=True)
    return (out * lax.rsqrt(ms + EPS) * g * (1.0 - lam_init)).astype(BF16)


def _attn_prompt_kernel(q_ref, k_ref, v_ref, g_ref, dl_ref, o_ref, ka_ref, vb_ref, qaug_ref, d_ref, m_ref, l_ref,
                        acc_ref,
                        *, tq, tk, lam_init):
    h = pl.program_id(1)
    qi = pl.program_id(2)
    c, c_hi, c_lo = _alibi_coef(h)
    hw = 2 * D_B

    def aug_lanes(shape, pos, sign):
        lane = lax.broadcasted_iota(I32, shape, 1)
        r = (pos & 255).astype(F32) * sign
        a = (pos >> 8).astype(F32) * sign
        coef = jnp.where(lane == 0, c_hi, jnp.where(lane == 1, c_lo,
                         jnp.where(lane == 2, 256.0 * c_hi, jnp.where(lane == 3, 256.0 * c_lo, 0.0))))
        ints = jnp.where(lane < 2, r, jnp.where(lane < 4, a, 0.0))
        return coef, ints, lane

    @pl.when(qi == 0)
    def _():
        t_len = ka_ref.shape[0]
        j_rel = lax.broadcasted_iota(I32, (t_len, hw), 0) & (tk - 1)
        coef, ints, lane = aug_lanes((t_len, hw), j_rel, 1.0)
        k_aug = jnp.where(lane < 4, ints, pltpu.roll(coef, 4, 1))
        ka_ref[:, :hw] = k_ref[0].astype(BF16)
        ka_ref[:, hw:] = k_aug.astype(BF16)
        vb_ref[...] = v_ref[0].astype(BF16)
        i_rel = lax.broadcasted_iota(I32, (2 * tq, hw), 0) & (tq - 1)
        coef, ints, lane = aug_lanes((2 * tq, hw), i_rel, -1.0)
        qaug_ref[...] = jnp.where(lane < 4, coef, pltpu.roll(ints, 4, 1)).astype(BF16)
        i = lax.broadcasted_iota(I32, (2 * tq, tk), 0) & (tq - 1)
        j = lax.broadcasted_iota(I32, (2 * tq, tk), 1)
        fwd = jnp.maximum(j - i, 0).astype(F32)
        d_ref[...] = jnp.where((j >> 6) <= (i >> 6), -2.0 * c * fwd, NEG_INF)

    qa = jnp.concatenate([_split_maps(q_ref[...]), qaug_ref[...]], axis=1)
    m_ref[...] = jnp.full(m_ref.shape, -1e30, F32)
    l_ref[...] = jnp.zeros(l_ref.shape, F32)
    acc_ref[...] = jnp.zeros(acc_ref.shape, F32)

    def block(kv, diag):
        k0 = pl.multiple_of(kv * tk, tk)
        s = lax.dot_general(qa, ka_ref[pl.ds(k0, tk), :], (((1,), (1,)), ((), ())), preferred_element_type=F32)
        if diag:
            s = s + d_ref[...]
            off = jnp.zeros((1, 1), F32)
        else:
            off = c * (qi * tq - kv * tk).astype(F32)
        m_old = m_ref[...]
        m_new = jnp.maximum(m_old, jnp.max(s, axis=-1, keepdims=True) - off)
        shift = m_new + off
        p = jnp.exp2(s - jnp.concatenate([shift] * (tk // hw), axis=1))
        alpha = jnp.exp2(m_old - m_new)
        l_ref[...] = alpha * l_ref[...] + jnp.sum(p, axis=-1, keepdims=True)
        acc_ref[...] = alpha * acc_ref[...] + jnp.dot(p.astype(BF16), vb_ref[pl.ds(k0, tk), :],
                                                     preferred_element_type=F32)
        m_ref[...] = m_new

    def body(kv, carry):
        block(kv, False)
        return carry

    lax.fori_loop(0, qi, body, 0)
    block(qi, True)
    lam = _lambda_from(dl_ref, lam_init)
    o_ref[...] = _finish_heads(acc_ref[...], 1.0 / l_ref[...], lam, g_ref[...], tq, lam_init)


def _attn_prompt(slab, kb, vb, g_norm, dl, batch, t_len, lam_init):
    tq = min(512, t_len)
    tk = tq
    assert tq % CHUNK == 0 and CHUNK == 64 and tk <= 512
    nq = t_len // tq
    k3 = kb.reshape(batch, t_len, KB_W)
    v3 = vb.reshape(batch, t_len, VB_W)
    kern = functools.partial(_attn_prompt_kernel, tq=tq, tk=tk, lam_init=lam_init)
    hw = 2 * D_B
    return pl.pallas_call(
        kern,
        grid=(batch, H_B, nq),
        in_specs=[
            pl.BlockSpec((tq, hw), lambda b, h, q: (b * nq + q, SLAB_QB // hw + h)),
            pl.BlockSpec((1, t_len, hw), lambda b, h, q: (b, 0, h)),
            pl.BlockSpec((1, t_len, hw), lambda b, h, q: (b, 0, h)),
            pl.BlockSpec((1, hw), lambda b, h, q: (0, 0)),
            pl.BlockSpec((4, D_B), lambda b, h, q: (0, 0)),
        ],
        out_specs=pl.BlockSpec((tq, hw), lambda b, h, q: (b * nq + q, h)),
        out_shape=jax.ShapeDtypeStruct((batch * t_len, VB_W), BF16),
        scratch_shapes=[
            pltpu.VMEM((t_len, 2 * hw), BF16),
            pltpu.VMEM((t_len, hw), BF16),
            pltpu.VMEM((2 * tq, hw), BF16),
            pltpu.VMEM((2 * tq, tk), F32),
            pltpu.VMEM((2 * tq, hw), F32),
            pltpu.VMEM((2 * tq, hw), F32),
            pltpu.VMEM((2 * tq, hw), F32),
        ],
        compiler_params=_cparams(("parallel", "parallel", "arbitrary")),
        name="attn_prompt",
    )(slab, k3, v3, g_norm, dl)


def _attn_sample_kernel(q_ref, kc_ref, vc_ref, kn_ref, vn_ref, g_ref, dl_ref, o_ref, *, tq, past, lam_init):
    h = pl.program_id(1)
    c, _, _ = _alibi_coef(h)
    qq = _split_maps(q_ref[...])
    q_pos = past + lax.broadcasted_iota(I32, (2 * tq, 1), 0) % tq

    def scores(k, k_pos):
        s = lax.dot_general(qq, k.astype(BF16), (((1,), (1,)), ((), ())), preferred_element_type=F32)
        s = s - c * jnp.abs(q_pos - k_pos).astype(F32)
        return jnp.where((k_pos // CHUNK) <= (q_pos // CHUNK), s, NEG_INF)

    sc = scores(kc_ref[0], lax.broadcasted_iota(I32, (1, past), 1))
    sn = scores(kn_ref[0], past + lax.broadcasted_iota(I32, (1, tq), 1))
    m = jnp.maximum(jnp.max(sc, axis=-1, keepdims=True), jnp.max(sn, axis=-1, keepdims=True))
    pc = jnp.exp2(sc - m)
    pn = jnp.exp2(sn - m)
    l = jnp.sum(pc, axis=-1, keepdims=True) + jnp.sum(pn, axis=-1, keepdims=True)
    acc = (jnp.dot(pc.astype(BF16), vc_ref[0].astype(BF16), preferred_element_type=F32)
           + jnp.dot(pn.astype(BF16), vn_ref[0].astype(BF16), preferred_element_type=F32))
    lam = _lambda_from(dl_ref, lam_init)
    o_ref[...] = _finish_heads(acc, 1.0 / l, lam, g_ref[...], tq, lam_init)


def _attn_sample(slab, kb, vb, cache_k, cache_v, g_norm, dl, batch, t_len, lam_init):
    past = cache_k.shape[1]
    hw = 2 * D_B
    k3 = kb.reshape(batch, t_len, KB_W)
    v3 = vb.reshape(batch, t_len, VB_W)
    kern = functools.partial(_attn_sample_kernel, tq=t_len, past=past, lam_init=lam_init)
    return pl.pallas_call(
        kern,
        grid=(batch, H_B),
        in_specs=[
            pl.BlockSpec((t_len, hw), lambda b, h: (b, SLAB_QB // hw + h)),
            pl.BlockSpec((1, past, hw), lambda b, h: (b, 0, h)),
            pl.BlockSpec((1, past, hw), lambda b, h: (b, 0, h)),
            pl.BlockSpec((1, t_len, hw), lambda b, h: (b, 0, h)),
            pl.BlockSpec((1, t_len, hw), lambda b, h: (b, 0, h)),
            pl.BlockSpec((1, hw), lambda b, h: (0, 0)),
            pl.BlockSpec((4, D_B), lambda b, h: (0, 0)),
        ],
        out_specs=pl.BlockSpec((t_len, hw), lambda b, h: (b, h)),
        out_shape=jax.ShapeDtypeStruct((batch * t_len, VB_W), BF16),
        compiler_params=_cparams(("parallel", "parallel")),
        name="attn_sample",
    )(slab, cache_k, cache_v, k3, v3, g_norm, dl)


def _finish_kernel(x_ref, oa_ref, ra_ref, ob_ref, gt_ref, wpa_ref, wpb_ref, wo_ref, g_ref, b_ref,
                   wrh_ref, wrl_ref, br_ref, h_ref, hp_ref, idx_ref, tw_ref, cnt_ref):
    i = pl.program_id(0)
    ra = ra_ref[...].astype(F32)
    ua = ra * _sigmoid(ra) * oa_ref[...].astype(F32)
    ya = jnp.dot(ua.astype(BF16), wpa_ref[...], preferred_element_type=F32)
    yb = jnp.dot(ob_ref[...], wpb_ref[...], preferred_element_type=F32)
    gt = gt_ref[...].astype(F32)
    mixed = _sigmoid(gt[:, :D_MODEL]) * ya + _sigmoid(gt[:, D_MODEL:]) * yb
    mix = jnp.dot(mixed.astype(BF16), wo_ref[...], preferred_element_type=F32)
    hh = _layer_norm(DEEPNORM_ALPHA * x_ref[...] + mix, g_ref[...], b_ref[...])
    h_ref[...] = hh
    hp_ref[...] = _pack_bf16_pairs(hh)

    h_hi = hh.astype(BF16)
    h_lo = (hh - h_hi.astype(F32)).astype(BF16)
    lg = (jnp.dot(h_hi, wrh_ref[...], preferred_element_type=F32)
          + jnp.dot(h_lo, wrh_ref[...], preferred_element_type=F32)
          + jnp.dot(h_hi, wrl_ref[...], preferred_element_type=F32)) + br_ref[...]
    tm = lg.shape[0]
    lane = lax.broadcasted_iota(I32, (tm, LANES), 1)
    lane_f = lane.astype(F32)
    work = lg
    vals, idxs = [], []
    cnt = jnp.zeros((tm, LANES), F32)
    for _ in range(TOP_K):
        mx = jnp.max(work, axis=-1, keepdims=True)
        ix = jnp.min(jnp.where(work == mx, lane_f, float(LANES)), axis=-1, keepdims=True)
        hit = lane_f == ix
        cnt = cnt + hit.astype(F32)
        work = jnp.where(hit, NEG_INF, work)
        vals.append(mx)
        idxs.append(ix)
    es = [jnp.exp(v - vals[0]) for v in vals]
    den = es[0] + es[1] + es[2] + es[3]
    idx_full = jnp.zeros((tm, LANES), F32)
    tw_full = jnp.zeros((tm, LANES), F32)
    for r in range(TOP_K):
        idx_full = jnp.where(lane == r, idxs[r], idx_full)
        tw_full = jnp.where(lane == r, es[r] / den, tw_full)
    idx_ref[...] = idx_full.T[:SUBLANES].astype(I32)
    tw_ref[...] = tw_full[:, :TOP_K]

    @pl.when(i == 0)
    def _():
        cnt_ref[...] = jnp.zeros(cnt_ref.shape, F32)

    cnt_ref[...] += jnp.sum(cnt, axis=0, keepdims=True)


def _finish(x2, oa, slab, ob, p):
    n = x2.shape[0]
    tm = min(256, n)
    const = lambda i: (0, 0)
    return pl.pallas_call(
        _finish_kernel,
        grid=(n // tm,),
        in_specs=[
            pl.BlockSpec((tm, D_MODEL), lambda i: (i, 0)),
            pl.BlockSpec((tm, VA_W), lambda i: (i, 0)),
            pl.BlockSpec((tm, RA_W), lambda i: (i, SLAB_RA // RA_W)),
            pl.BlockSpec((tm, VB_W), lambda i: (i, 0)),
            pl.BlockSpec((tm, GT_W), lambda i: (i, SLAB_GT // GT_W)),
            pl.BlockSpec((VA_W, D_MODEL), const),
            pl.BlockSpec((VB_W, D_MODEL), const),
            pl.BlockSpec((D_MODEL, D_MODEL), const),
            pl.BlockSpec((1, D_MODEL), const),
            pl.BlockSpec((1, D_MODEL), const),
            pl.BlockSpec((D_MODEL, LANES), const),
            pl.BlockSpec((D_MODEL, LANES), const),
            pl.BlockSpec((1, LANES), const),
        ],
        out_specs=[
            pl.BlockSpec((tm, D_MODEL), lambda i: (i, 0)),
            pl.BlockSpec((tm, D_MODEL // 2), lambda i: (i, 0)),
            pl.BlockSpec((SUBLANES, tm), lambda i: (0, i)),
            pl.BlockSpec((tm, TOP_K), lambda i: (i, 0)),
            pl.BlockSpec((1, LANES), const),
        ],
        out_shape=[
            jax.ShapeDtypeStruct((n, D_MODEL), F32),
            jax.ShapeDtypeStruct((n, D_MODEL // 2), U32),
            jax.ShapeDtypeStruct((SUBLANES, n), I32),
            jax.ShapeDtypeStruct((n, TOP_K), F32),
            jax.ShapeDtypeStruct((1, LANES), F32),
        ],
        compiler_params=_cparams(("arbitrary",)),
        name="finish",
    )(x2, oa, slab, ob, slab, p["w_pa"], p["w_pb"], p["w_o"], p["ln1_g"], p["ln1_b"],
      p["w_r_hi"], p["w_r_lo"], p["b_r"])


def _dest_kernel(idx_ref, ps_ref, ut_ref, dest_ref, carry_ref):
    i = pl.program_id(0)

    @pl.when(i == 0)
    def _():
        carry_ref[...] = jnp.zeros(carry_ref.shape, F32)

    tm = idx_ref.shape[1]
    expert = lax.broadcasted_iota(I32, (N_EXPERTS, tm), 0)
    idx = idx_ref[...]
    hits = [expert == idx[k:k + 1, :] for k in range(TOP_K)]
    cnt = jnp.zeros((N_EXPERTS, tm), F32)
    for hit in hits:
        cnt = cnt + hit.astype(F32)
    before = jnp.dot(cnt.astype(BF16), ut_ref[...], preferred_element_type=F32)
    base = before + carry_ref[:, 0:1] + ps_ref[:, 0:1]
    row = lax.broadcasted_iota(I32, (SUBLANES, tm), 0)
    dest = jnp.zeros((SUBLANES, tm), F32)
    for k in range(TOP_K):
        d = jnp.sum(jnp.where(hits[k], base, 0.0), axis=0, keepdims=True)
        dest = jnp.where(row == k, d, dest)
    dest_ref[...] = dest.astype(I32)
    carry_ref[...] += jnp.sum(cnt, axis=1, keepdims=True)


def _dest(idx_t, pstart):
    n = idx_t.shape[1]
    tm = min(512, n)
    utri = jnp.asarray(np.triu(np.ones((tm, tm), np.float32), 1), BF16)
    return pl.pallas_call(
        _dest_kernel,
        grid=(n // tm,),
        in_specs=[
            pl.BlockSpec((SUBLANES, tm), lambda i: (0, i)),
            pl.BlockSpec((N_EXPERTS, LANES), lambda i: (0, 0)),
            pl.BlockSpec((tm, tm), lambda i: (0, 0)),
        ],
        out_specs=pl.BlockSpec((SUBLANES, tm), lambda i: (0, i)),
        out_shape=jax.ShapeDtypeStruct((SUBLANES, n), I32),
        scratch_shapes=[pltpu.VMEM((N_EXPERTS, LANES), F32)],
        compiler_params=_cparams(("arbitrary",)),
        name="dest",
    )(idx_t, pstart, utri)


def _scatter_kernel(dest_ref, hp_ref, rows_in_ref, rows_ref, sem):
    del rows_in_ref
    tm = hp_ref.shape[0]

    def row_copy(t, d):
        return pltpu.make_async_copy(hp_ref.at[pl.ds(t, 1)], rows_ref.at[pl.ds(d, 1)], sem)

    def issue(t, carry):
        for k in range(TOP_K):
            row_copy(t, dest_ref[k, t]).start()
        return carry

    def drain(t, carry):
        for k in range(TOP_K):
            row_copy(t, dest_ref[k, t]).wait()
        return carry

    lax.fori_loop(0, tm, issue, 0)
    lax.fori_loop(0, tm, drain, 0)


def _scatter(dest_t, hp, n_rows):
    n = hp.shape[0]
    tm = min(512, n)
    rows0 = jnp.zeros((n_rows, D_MODEL // 2), U32)
    return pl.pallas_call(
        _scatter_kernel,
        grid=(n // tm,),
        in_specs=[
            pl.BlockSpec((SUBLANES, tm), lambda i: (0, i), memory_space=pltpu.SMEM),
            pl.BlockSpec((tm, D_MODEL // 2), lambda i: (i, 0)),
            pl.BlockSpec(memory_space=pl.ANY),
        ],
        out_specs=pl.BlockSpec(memory_space=pl.ANY),
        out_shape=jax.ShapeDtypeStruct((n_rows, D_MODEL // 2), U32),
        scratch_shapes=[pltpu.SemaphoreType.DMA(())],
        input_output_aliases={2: 0},
        compiler_params=_cparams(("arbitrary",)),
        name="scatter",
    )(dest_t, hp, rows0)


def _expert_kernel(be_ref, nu_ref, rows_ref, wgu_ref, bgu_ref, wd_ref, bd_ref, out_ref):
    del be_ref
    i = pl.program_id(0)

    @pl.when(i < nu_ref[0])
    def _():
        x = _unpack_bf16_pairs(rows_ref[...]).astype(BF16)
        gu = jnp.dot(x, wgu_ref[0], preferred_element_type=F32) + bgu_ref[0]
        gate = jnp.minimum(gu[:, :D_FF], SWIGLU_LIMIT)
        up = jnp.clip(gu[:, D_FF:], -SWIGLU_LIMIT, SWIGLU_LIMIT)
        hidden = (up + 1.0) * (gate * _sigmoid(SWIGLU_ALPHA * gate))
        o = jnp.dot(hidden.astype(BF16), wd_ref[0], preferred_element_type=F32) + bd_ref[0]
        out_ref[...] = _pack_bf16_pairs(o)

    @pl.when(i >= nu_ref[0])
    def _():
        out_ref[...] = jnp.zeros(out_ref.shape, U32)


def _experts(rows, block_e, n_used, p):
    n_rows = rows.shape[0]
    bm = EXPERT_BM
    n_blocks = n_rows // bm
    grid_spec = pltpu.PrefetchScalarGridSpec(
        num_scalar_prefetch=2,
        grid=(n_blocks,),
        in_specs=[
            pl.BlockSpec((bm, D_MODEL // 2), lambda i, be, nu: (i, 0)),
            pl.BlockSpec((1, D_MODEL, 2 * D_FF), lambda i, be, nu: (be[i], 0, 0)),
            pl.BlockSpec((1, 1, 2 * D_FF), lambda i, be, nu: (be[i], 0, 0)),
            pl.BlockSpec((1, D_FF, D_MODEL), lambda i, be, nu: (be[i], 0, 0)),
            pl.BlockSpec((1, 1, D_MODEL), lambda i, be, nu: (be[i], 0, 0)),
        ],
        out_specs=pl.BlockSpec((bm, D_MODEL // 2), lambda i, be, nu: (i, 0)),
    )
    return pl.pallas_call(
        _expert_kernel,
        grid_spec=grid_spec,
        out_shape=jax.ShapeDtypeStruct((n_rows, D_MODEL // 2), U32),
        compiler_params=_cparams(("arbitrary",)),
        name="experts",
    )(block_e, n_used, rows, p["w_gu"], p["b_gu"], p["w_down"], p["b_down"])


def _combine_kernel(dest_ref, tw_ref, h_ref, rows_ref, g_ref, b_ref, y_ref, buf_ref, sem):
    tm = h_ref.shape[0]

    def row_copy(t, k, d):
        return pltpu.make_async_copy(rows_ref.at[pl.ds(d, 1)], buf_ref.at[k, pl.ds(t, 1)], sem)

    def issue(t, carry):
        for k in range(TOP_K):
            row_copy(t, k, dest_ref[k, t]).start()
        return carry

    def drain(t, carry):
        for k in range(TOP_K):
            row_copy(t, k, dest_ref[k, t]).wait()
        return carry

    lax.fori_loop(0, tm, issue, 0)
    lax.fori_loop(0, tm, drain, 0)
    tw = tw_ref[...]
    f = jnp.zeros((tm, D_MODEL), F32)
    for k in range(TOP_K):
        f = f + tw[:, k:k + 1] * _unpack_bf16_pairs(buf_ref[k])
    y_ref[...] = _layer_norm(DEEPNORM_ALPHA * h_ref[...] + f, g_ref[...], b_ref[...])


def _combine(dest_t, top_w, hh, out_rows, p):
    n = hh.shape[0]
    tm = min(256, n)
    const = lambda i: (0, 0)
    return pl.pallas_call(
        _combine_kernel,
        grid=(n // tm,),
        in_specs=[
            pl.BlockSpec((SUBLANES, tm), lambda i: (0, i), memory_space=pltpu.SMEM),
            pl.BlockSpec((tm, TOP_K), lambda i: (i, 0)),
            pl.BlockSpec((tm, D_MODEL), lambda i: (i, 0)),
            pl.BlockSpec(memory_space=pl.ANY),
            pl.BlockSpec((1, D_MODEL), const),
            pl.BlockSpec((1, D_MODEL), const),
        ],
        out_specs=pl.BlockSpec((tm, D_MODEL), lambda i: (i, 0)),
        out_shape=jax.ShapeDtypeStruct((n, D_MODEL), F32),
        scratch_shapes=[pltpu.VMEM((TOP_K, tm, D_MODEL // 2), U32), pltpu.SemaphoreType.DMA(())],
        compiler_params=_cparams(("arbitrary",)),
        name="combine",
    )(dest_t, top_w, hh, out_rows, p["ln2_g"], p["ln2_b"])


def _moe(hh, hp, top_idx, top_w, counts, p):
    n = hh.shape[0]
    bm = EXPERT_BM
    n_rows = n * TOP_K + N_EXPERTS * bm
    cnt = counts[0, :N_EXPERTS].astype(I32)
    padded = (cnt + bm - 1) // bm * bm
    pend = jnp.cumsum(padded)
    pstart = pend - padded
    ps_col = jnp.broadcast_to(pstart.astype(F32)[:, None], (N_EXPERTS, LANES))
    block_row0 = jnp.arange(n_rows // bm, dtype=I32) * bm
    block_e = jnp.minimum(jnp.sum((pend[None, :] <= block_row0[:, None]).astype(I32), axis=1), N_EXPERTS - 1)
    n_used = (pend[-1:] // bm).astype(I32)
    dest = _dest(top_idx, ps_col)
    rows = _scatter(dest, hp, n_rows)
    out_rows = _experts(rows, block_e, n_used, p)
    return _combine(dest, top_w, hh, out_rows, p)


def _prep_params(w_in, b_in, w_alpha, b_alpha, gla_norm_g, diff_norm_g, w_pa, w_pb, w_o, ln1_g, ln1_b,
                 w_router, b_router, w_gu, b_gu, w_down, b_down, ln2_g, ln2_b):
    o = IN_OFFS
    seg = lambda a, i: a[..., o[i]:o[i + 1]]
    order = (0, 1, 2, 3, 5, 8)
    w_slab = jnp.concatenate([seg(w_in, i) for i in order], axis=1).astype(BF16)
    b_slab = jnp.concatenate([seg(b_in, i) for i in order])[None, :]
    scale = np.ones((1, SLAB_W), np.float32)
    scale[0, SLAB_QA:SLAB_QA + QA_W] = DK_A ** -0.5
    scale[0, SLAB_QB:SLAB_QB + QB_W] = D_B ** -0.5 * LOG2E
    w_lr = jnp.zeros((D_MODEL, LANES), F32).at[:, :LR_W].set(seg(w_in, 4)).astype(BF16)
    b_lr = jnp.zeros((1, LANES), F32).at[0, :LR_W].set(seg(b_in, 4))
    w_al = jnp.zeros((LANES, QA_W), F32).at[:LR_W].set(w_alpha).astype(BF16)
    w_r = jnp.zeros((D_MODEL, LANES), F32).at[:, :N_EXPERTS].set(w_router)
    w_r_hi = w_r.astype(BF16)
    w_r_lo = (w_r - w_r_hi.astype(F32)).astype(BF16)
    b_r = jnp.full((1, LANES), NEG_INF, F32).at[0, :N_EXPERTS].set(b_router)
    return dict(
        w_slab=w_slab, b_slab=b_slab, s_slab=jnp.asarray(scale),
        w_lr=w_lr, b_lr=b_lr, w_alpha=w_al, b_alpha=b_alpha[None, :],
        w_k=seg(w_in, 6).astype(BF16), w_v=seg(w_in, 7).astype(BF16),
        b_k=seg(b_in, 6)[None, :], b_v=seg(b_in, 7)[None, :],
        gla_g=gla_norm_g[None, :], diff_g=diff_norm_g[None, :],
        w_pa=w_pa.astype(BF16), w_pb=w_pb.astype(BF16), w_o=w_o.astype(BF16),
        ln1_g=ln1_g[None, :], ln1_b=ln1_b[None, :],
        w_r_hi=w_r_hi, w_r_lo=w_r_lo, b_r=b_r,
        w_gu=w_gu.astype(BF16), b_gu=b_gu[:, None, :], w_down=w_down.astype(BF16), b_down=b_down[:, None, :],
        ln2_g=ln2_g[None, :], ln2_b=ln2_b[None, :],
    )


def _layer(x, p, dl, lam_init, s0, cache_k, cache_v):
    batch, t_len, _ = x.shape
    x2 = x.reshape(batch * t_len, D_MODEL)
    slab, loga = _proj_slab(x2, p)
    kb, vb = _proj_kv(x2, p)
    oa, s_new = _gla(slab, loga, s0, p["gla_g"], batch, t_len)
    if cache_k is None:
        ob = _attn_prompt(slab, kb, vb, p["diff_g"], dl, batch, t_len, lam_init)
    else:
        ob = _attn_sample(slab, kb, vb, cache_k, cache_v, p["diff_g"], dl, batch, t_len, lam_init)
    hh, hp, top_idx, top_w, counts = _finish(x2, oa, slab, ob, p)
    y = _moe(hh, hp, top_idx, top_w, counts, p)
    return (y.reshape(batch, t_len, D_MODEL), s_new,
            kb.reshape(batch, t_len, H_B, 2, D_B), vb.reshape(batch, t_len, H_B, 2 * D_B))


def kernel(x_prompt, x_sample, cache_k, cache_v, state_gla, w_in, b_in, w_alpha, b_alpha, gla_norm_g,
           diff_lambda, diff_norm_g, w_pa, w_pb, w_o, ln1_g, ln1_b, w_router, b_router, w_gu, b_gu,
           w_down, b_down, ln2_g, ln2_b):
    assert w_in.shape[0] == DEPTH == 1
    l = 0
    lam_init = 0.8 - 0.6 * math.exp(-0.3 * l)
    p = _prep_params(w_in[l], b_in[l], w_alpha[l], b_alpha[l], gla_norm_g[l], diff_norm_g[l], w_pa[l],
                     w_pb[l], w_o[l], ln1_g[l], ln1_b[l], w_router[l], b_router[l], w_gu[l], b_gu[l],
                     w_down[l], b_down[l], ln2_g[l], ln2_b[l])
    dl = diff_lambda[l]
    bp = x_prompt.shape[0]
    bs, ts = x_sample.shape[0], x_sample.shape[1]
    past = cache_k.shape[2]
    yp, s_p, k_p, v_p = _layer(x_prompt, p, dl, lam_init, jnp.zeros((bp, H_A, DK_A, DV_A), F32), None, None)
    ck = cache_k[l].reshape(bs, past, KB_W)
    cv = cache_v[l].reshape(bs, past, VB_W)
    ys, s_s, k_s, v_s = _layer(x_sample, p, dl, lam_init, state_gla[l], ck, cv)
    return (yp, ys, s_p[None], k_p[None], v_p[None], s_s[None], k_s[None], v_s[None])
```

```python
import functools
import math

import numpy as np
import jax
import jax.numpy as jnp
from jax import lax
from jax.experimental import pallas as pl
from jax.experimental.pallas import tpu as pltpu

F32 = jnp.float32
BF16 = jnp.bfloat16
U32 = jnp.uint32
I32 = jnp.int32

D_MODEL = 1024
CHUNK = 64
H_A = 4
DK_A = 128
DV_A = 256
GATE_RANK = 16
GATE_TAU = 16.0
H_B = 8
D_B = 64
N_EXPERTS = 32
TOP_K = 4
D_FF = D_MODEL
SWIGLU_LIMIT = 7.0
SWIGLU_ALPHA = 1.702
EPS = 1e-5
DEPTH = 1
DEEPNORM_ALPHA = (2.0 * DEPTH) ** 0.25

QA_W = H_A * DK_A
KA_W = H_A * DK_A
VA_W = H_A * DV_A
RA_W = H_A * DV_A
LR_W = GATE_RANK
QB_W = H_B * 2 * D_B
KB_W = H_B * 2 * D_B
VB_W = H_B * 2 * D_B
GT_W = 2 * D_MODEL
IN_SIZES = (QA_W, KA_W, VA_W, RA_W, LR_W, QB_W, KB_W, VB_W, GT_W)
IN_OFFS = tuple(int(v) for v in np.cumsum((0,) + IN_SIZES))

SLAB_QA, SLAB_KA, SLAB_VA, SLAB_RA, SLAB_QB, SLAB_GT = 0, 512, 1024, 2048, 3072, 4096
SLAB_W = 6144
LANES = 128
SUBLANES = 8
EXPERT_BM = 512
VMEM_LIMIT = 56 * 1024 * 1024

NEG_INF = float("-inf")
LOG2E = math.log2(math.e)


def _cparams(sem):
    return pltpu.CompilerParams(dimension_semantics=sem, vmem_limit_bytes=VMEM_LIMIT)


def _sigmoid(x):
    return 1.0 / (1.0 + jnp.exp(-x))


def _pack_bf16_pairs(x):
    n = x.shape[1] // 2
    xb = x.astype(BF16).astype(F32)
    lo = pltpu.bitcast(xb[:, :n], U32) >> 16
    hi = pltpu.bitcast(xb[:, n:], U32) & jnp.uint32(0xFFFF0000)
    return hi | lo


def _unpack_bf16_pairs(r):
    lo = pltpu.bitcast(r << 16, F32)
    hi = pltpu.bitcast(r & jnp.uint32(0xFFFF0000), F32)
    return jnp.concatenate([lo, hi], axis=1)


def _layer_norm(y, g, b):
    mu = jnp.mean(y, axis=-1, keepdims=True)
    d = y - mu
    var = jnp.mean(d * d, axis=-1, keepdims=True)
    return d * lax.rsqrt(var + EPS) * g + b


def _proj_slab_kernel(x_ref, w_ref, b_ref, s_ref, wlr_ref, blr_ref, wal_ref, bal_ref,
                      slab_ref, loga_ref, xb_ref):
    j = pl.program_id(1)

    @pl.when(j == 0)
    def _():
        xb_ref[...] = x_ref[...].astype(BF16)
        lra = jnp.dot(xb_ref[...], wlr_ref[...], preferred_element_type=F32) + blr_ref[...]
        z = jnp.dot(lra.astype(BF16), wal_ref[...], preferred_element_type=F32) + bal_ref[...]
        loga_ref[...] = (jnp.minimum(z, 0.0) - jnp.log(1.0 + jnp.exp(-jnp.abs(z)))) * (1.0 / GATE_TAU)

    acc = jnp.dot(xb_ref[...], w_ref[...], preferred_element_type=F32)
    slab_ref[...] = ((acc + b_ref[...]) * s_ref[...]).astype(BF16)


def _proj_slab(x2, p):
    n = x2.shape[0]
    tm = min(1024, n)
    tn = 1024
    grid = (n // tm, SLAB_W // tn)
    return pl.pallas_call(
        _proj_slab_kernel,
        grid=grid,
        in_specs=[
            pl.BlockSpec((tm, D_MODEL), lambda i, j: (i, 0)),
            pl.BlockSpec((D_MODEL, tn), lambda i, j: (0, j)),
            pl.BlockSpec((1, tn), lambda i, j: (0, j)),
            pl.BlockSpec((1, tn), lambda i, j: (0, j)),
            pl.BlockSpec((D_MODEL, LANES), lambda i, j: (0, 0)),
            pl.BlockSpec((1, LANES), lambda i, j: (0, 0)),
            pl.BlockSpec((LANES, QA_W), lambda i, j: (0, 0)),
            pl.BlockSpec((1, QA_W), lambda i, j: (0, 0)),
        ],
        out_specs=[
            pl.BlockSpec((tm, tn), lambda i, j: (i, j)),
            pl.BlockSpec((tm, QA_W), lambda i, j: (i, 0)),
        ],
        out_shape=[
            jax.ShapeDtypeStruct((n, SLAB_W), BF16),
            jax.ShapeDtypeStruct((n, QA_W), F32),
        ],
        scratch_shapes=[pltpu.VMEM((tm, D_MODEL), BF16)],
        compiler_params=_cparams(("parallel", "arbitrary")),
        name="proj_slab",
    )(x2, p["w_slab"], p["b_slab"], p["s_slab"], p["w_lr"], p["b_lr"], p["w_alpha"], p["b_alpha"])


def _proj_kv_kernel(x_ref, wk_ref, wv_ref, bk_ref, bv_ref, k_ref, v_ref):
    xb = x_ref[...].astype(BF16)
    k_ref[...] = jnp.dot(xb, wk_ref[...], preferred_element_type=F32) + bk_ref[...]
    v_ref[...] = jnp.dot(xb, wv_ref[...], preferred_element_type=F32) + bv_ref[...]


def _proj_kv(x2, p):
    n = x2.shape[0]
    tm = min(512, n)
    const = lambda i: (0, 0)
    return pl.pallas_call(
        _proj_kv_kernel,
        grid=(n // tm,),
        in_specs=[
            pl.BlockSpec((tm, D_MODEL), lambda i: (i, 0)),
            pl.BlockSpec((D_MODEL, KB_W), const),
            pl.BlockSpec((D_MODEL, VB_W), const),
            pl.BlockSpec((1, KB_W), const),
            pl.BlockSpec((1, VB_W), const),
        ],
        out_specs=[
            pl.BlockSpec((tm, KB_W), lambda i: (i, 0)),
            pl.BlockSpec((tm, VB_W), lambda i: (i, 0)),
        ],
        out_shape=[
            jax.ShapeDtypeStruct((n, KB_W), F32),
            jax.ShapeDtypeStruct((n, VB_W), F32),
        ],
        compiler_params=_cparams(("parallel",)),
        name="proj_kv",
    )(x2, p["w_k"], p["w_v"], p["b_k"], p["b_v"])


def _gla_tables(L):
    nl = int(math.log2(L))
    t = np.arange(L)
    D = np.zeros(((nl + 2) * L, L), np.float32)
    masks = np.zeros((nl + 1, L, L), np.float32)
    for l in range(nl):
        m = L >> (l + 1)
        grp = t // (2 * m)
        mid = grp * 2 * m + m - 1
        upper = (t % (2 * m)) >= m
        for r in range(L):
            if upper[r]:
                D[l * L + r, mid[r] + 1:r + 1] = 1.0
            else:
                D[l * L + r, r + 1:mid[r] + 1] = 1.0
        masks[l] = (upper[:, None] & ~upper[None, :] & (grp[:, None] == grp[None, :])).astype(np.float32)
    D[nl * L:(nl + 1) * L] = np.tril(np.ones((L, L), np.float32))
    D[(nl + 1) * L:] = np.triu(np.ones((L, L), np.float32), 1)
    masks[nl] = np.eye(L, dtype=np.float32)
    return jnp.asarray(D, BF16), jnp.asarray(masks, F32)


def _gla_kernel(q_ref, k_ref, v_ref, la_ref, s0_ref, g_ref, d_ref, m_ref, o_ref, s_ref, *, L, n_chunks):
    c = pl.program_id(1)
    nl = int(math.log2(L))

    @pl.when(c == 0)
    def _():
        s_ref[...] = s0_ref[...]

    dmat = d_ref[...]
    ones_col = jnp.ones((L, LANES), BF16)
    g = g_ref[...]
    nt = (((1,), (1,)), ((), ()))
    tn = (((0,), (0,)), ((), ()))

    def chunk(ci, carry):
        r0 = pl.multiple_of(ci * L, L)
        rows = pl.ds(r0, L)
        for h in range(H_A):
            kc = slice(h * DK_A, (h + 1) * DK_A)
            vc = slice(h * DV_A, (h + 1) * DV_A)
            q = q_ref[rows, kc].astype(F32)
            k = k_ref[rows, kc].astype(F32)
            v = v_ref[rows, vc]
            la = la_ref[rows, kc]
            la_hi = la.astype(BF16)
            la_lo = (la - la_hi.astype(F32)).astype(BF16)
            e = (jnp.dot(dmat, la_hi, preferred_element_type=F32)
                 + jnp.dot(dmat, la_lo, preferred_element_type=F32))
            x = jnp.exp(e)
            a = m_ref[nl] * lax.dot_general(q.astype(BF16), k.astype(BF16), nt, preferred_element_type=F32)
            for l in range(nl):
                xl = x[l * L:(l + 1) * L]
                a = a + m_ref[l] * lax.dot_general((q * xl).astype(BF16), (k * xl).astype(BF16), nt,
                                                   preferred_element_type=F32)
            xb = x[nl * L:(nl + 1) * L]
            xs = x[(nl + 1) * L:]
            s_old = s_ref[0, h]
            o = (jnp.dot((q * xb).astype(BF16), s_old.astype(BF16), preferred_element_type=F32)
                 + jnp.dot(a.astype(BF16), v, preferred_element_type=F32))
            bl = (lax.dot_general(la_hi, ones_col, tn, preferred_element_type=F32)
                  + lax.dot_general(la_lo, ones_col, tn, preferred_element_type=F32))
            dec = jnp.exp(bl[:, 0:1])
            s_ref[0, h] = dec * s_old + lax.dot_general((k * xs).astype(BF16), v, tn, preferred_element_type=F32)
            ms = jnp.mean(o * o, axis=-1, keepdims=True)
            o_ref[rows, vc] = (o * lax.rsqrt(ms + EPS) * g).astype(BF16)
        return carry

    lax.fori_loop(0, n_chunks, chunk, 0)


def _gla(slab, loga, s0, g_norm, batch, t_len):
    L = min(CHUNK, t_len)
    tb = min(512, t_len)
    nb = t_len // tb
    dmat, masks = _gla_tables(L)
    kern = functools.partial(_gla_kernel, L=L, n_chunks=tb // L)
    row = lambda b, c: b * nb + c
    return pl.pallas_call(
        kern,
        grid=(batch, nb),
        in_specs=[
            pl.BlockSpec((tb, QA_W), lambda b, c: (row(b, c), SLAB_QA // QA_W)),
            pl.BlockSpec((tb, KA_W), lambda b, c: (row(b, c), SLAB_KA // KA_W)),
            pl.BlockSpec((tb, VA_W), lambda b, c: (row(b, c), SLAB_VA // VA_W)),
            pl.BlockSpec((tb, QA_W), lambda b, c: (row(b, c), 0)),
            pl.BlockSpec((1, H_A, DK_A, DV_A), lambda b, c: (b, 0, 0, 0)),
            pl.BlockSpec((1, DV_A), lambda b, c: (0, 0)),
            pl.BlockSpec(dmat.shape, lambda b, c: (0, 0)),
            pl.BlockSpec(masks.shape, lambda b, c: (0, 0, 0)),
        ],
        out_specs=[
            pl.BlockSpec((tb, VA_W), lambda b, c: (row(b, c), 0)),
            pl.BlockSpec((1, H_A, DK_A, DV_A), lambda b, c: (b, 0, 0, 0)),
        ],
        out_shape=[
            jax.ShapeDtypeStruct((batch * t_len, VA_W), BF16),
            jax.ShapeDtypeStruct((batch, H_A, DK_A, DV_A), F32),
        ],
        compiler_params=_cparams(("parallel", "arbitrary")),
        name="gla",
    )(slab, slab, slab, loga, s0, g_norm, dmat, masks)


def _lambda_from(dl_ref, lam_init):
    dl = dl_ref[...]
    a = jnp.sum(dl[0:1] * dl[1:2], axis=-1, keepdims=True)
    b = jnp.sum(dl[2:3] * dl[3:4], axis=-1, keepdims=True)
    return jnp.exp(a) - jnp.exp(b) + lam_init


def _split_maps(q):
    lane = lax.broadcasted_iota(I32, q.shape, 1)
    zero = jnp.zeros_like(q)
    return jnp.concatenate([jnp.where(lane < D_B, q, zero), jnp.where(lane >= D_B, q, zero)], axis=0)


def _alibi_coef(h):
    c = jnp.full((1, 1), LOG2E, F32) * jnp.exp2(-(h + 1).astype(F32))
    c_hi = c.astype(BF16).astype(F32)
    return c, c_hi, c - c_hi


def _finish_heads(acc, inv_l, lam, g, tq, lam_init):
    o = acc * inv_l
    out = o[:tq] - lam * o[tq:]
    ms = jnp.mean(out * out, axis=-1, keepdims=True)
    return (out * lax.rsqrt(ms + EPS) * g * (1.0 - lam_init)).astype(BF16)


def _attn_prompt_kernel(q_ref, k_ref, v_ref, g_ref, dl_ref, o_ref, ka_ref, vb_ref, qaug_ref, d_ref, m_ref, l_ref,
                        acc_ref, s0_ref, s1_ref, p0_ref, p1_ref,
                        *, tq, tk, lam_init):
    h = pl.program_id(1)
    qi = pl.program_id(2)
    c, c_hi, c_lo = _alibi_coef(h)
    hw = 2 * D_B

    def aug_lanes(shape, pos, sign):
        lane = lax.broadcasted_iota(I32, shape, 1)
        r = (pos & 255).astype(F32) * sign
        a = (pos >> 8).astype(F32) * sign
        coef = jnp.where(lane == 0, c_hi, jnp.where(lane == 1, c_lo,
                         jnp.where(lane == 2, 256.0 * c_hi, jnp.where(lane == 3, 256.0 * c_lo, 0.0))))
        ints = jnp.where(lane < 2, r, jnp.where(lane < 4, a, 0.0))
        return coef, ints, lane

    @pl.when(qi == 0)
    def _():
        t_len = ka_ref.shape[0]
        j_rel = lax.broadcasted_iota(I32, (t_len, hw), 0) & (tk - 1)
        coef, ints, lane = aug_lanes((t_len, hw), j_rel, 1.0)
        k_aug = jnp.where(lane < 4, ints, pltpu.roll(coef, 4, 1))
        ka_ref[:, :hw] = k_ref[0].astype(BF16)
        ka_ref[:, hw:] = k_aug.astype(BF16)
        vb_ref[...] = v_ref[0].astype(BF16)
        i_rel = lax.broadcasted_iota(I32, (2 * tq, hw), 0) & (tq - 1)
        coef, ints, lane = aug_lanes((2 * tq, hw), i_rel, -1.0)
        qaug_ref[...] = jnp.where(lane < 4, coef, pltpu.roll(ints, 4, 1)).astype(BF16)
        i = lax.broadcasted_iota(I32, (2 * tq, tk), 0) & (tq - 1)
        j = lax.broadcasted_iota(I32, (2 * tq, tk), 1)
        fwd = jnp.maximum(j - i, 0).astype(F32)
        d_ref[...] = jnp.where((j >> 6) <= (i >> 6), -2.0 * c * fwd, NEG_INF)

    qa = jnp.concatenate([_split_maps(q_ref[...]), qaug_ref[...]], axis=1)
    m_ref[...] = jnp.full(m_ref.shape, -1e30, F32)
    l_ref[...] = jnp.zeros(l_ref.shape, F32)

    def scores(kv):
        k0 = pl.multiple_of(kv * tk, tk)
        return lax.dot_general(qa, ka_ref[pl.ds(k0, tk), :], (((1,), (1,)), ((), ())), preferred_element_type=F32)

    def weighted_values(p_ref, kv):
        k0 = pl.multiple_of(kv * tk, tk)
        return jnp.dot(p_ref[...], vb_ref[pl.ds(k0, tk), :], preferred_element_type=F32)

    def softmax_step(s, off):
        m_old = m_ref[...]
        m_new = jnp.maximum(m_old, jnp.max(s, axis=-1, keepdims=True) - off)
        shift = m_new + off
        p = jnp.exp2(s - jnp.concatenate([shift] * (tk // hw), axis=1))
        alpha = jnp.exp2(m_old - m_new)
        l_ref[...] = alpha * l_ref[...] + jnp.sum(p, axis=-1, keepdims=True)
        m_ref[...] = m_new
        return p.astype(BF16), alpha

    p_diag, _ = softmax_step(scores(qi) + d_ref[...], jnp.zeros((1, 1), F32))
    p1_ref[...] = p_diag
    acc_ref[...] = jnp.zeros(acc_ref.shape, F32)

    @pl.when(qi > 0)
    def _():
        s0_ref[...] = scores(0)

    def step(kv, s_cur, s_nxt, p_prev, p_cur):
        pv_prev = weighted_values(p_prev, jnp.where(kv == 0, qi, kv - 1))
        s_nxt[...] = scores(jnp.minimum(kv + 1, qi - 1))
        p, alpha = softmax_step(s_cur[...], c * (qi * tq - kv * tk).astype(F32))
        acc_ref[...] = alpha * (acc_ref[...] + pv_prev)
        p_cur[...] = p

    def pair(j, carry):
        step(2 * j, s0_ref, s1_ref, p1_ref, p0_ref)
        step(2 * j + 1, s1_ref, s0_ref, p0_ref, p1_ref)
        return carry

    lax.fori_loop(0, qi // 2, pair, 0)

    @pl.when(qi % 2 == 1)
    def _():
        step(qi - 1, s0_ref, s1_ref, p1_ref, p0_ref)
        acc_ref[...] += weighted_values(p0_ref, qi - 1)

    @pl.when(qi % 2 == 0)
    def _():
        acc_ref[...] += weighted_values(p1_ref, jnp.maximum(qi - 1, 0))

    lam = _lambda_from(dl_ref, lam_init)
    o_ref[...] = _finish_heads(acc_ref[...], 1.0 / l_ref[...], lam, g_ref[...], tq, lam_init)


def _attn_prompt(slab, kb, vb, g_norm, dl, batch, t_len, lam_init):
    tq = min(512, t_len)
    tk = tq
    assert tq % CHUNK == 0 and CHUNK == 64 and tk <= 512
    nq = t_len // tq
    k3 = kb.reshape(batch, t_len, KB_W)
    v3 = vb.reshape(batch, t_len, VB_W)
    kern = functools.partial(_attn_prompt_kernel, tq=tq, tk=tk, lam_init=lam_init)
    hw = 2 * D_B
    return pl.pallas_call(
        kern,
        grid=(batch, H_B, nq),
        in_specs=[
            pl.BlockSpec((tq, hw), lambda b, h, q: (b * nq + q, SLAB_QB // hw + h)),
            pl.BlockSpec((1, t_len, hw), lambda b, h, q: (b, 0, h)),
            pl.BlockSpec((1, t_len, hw), lambda b, h, q: (b, 0, h)),
            pl.BlockSpec((1, hw), lambda b, h, q: (0, 0)),
            pl.BlockSpec((4, D_B), lambda b, h, q: (0, 0)),
        ],
        out_specs=pl.BlockSpec((tq, hw), lambda b, h, q: (b * nq + q, h)),
        out_shape=jax.ShapeDtypeStruct((batch * t_len, VB_W), BF16),
        scratch_shapes=[
            pltpu.VMEM((t_len, 2 * hw), BF16),
            pltpu.VMEM((t_len, hw), BF16),
            pltpu.VMEM((2 * tq, hw), BF16),
            pltpu.VMEM((2 * tq, tk), F32),
            pltpu.VMEM((2 * tq, hw), F32),
            pltpu.VMEM((2 * tq, hw), F32),
            pltpu.VMEM((2 * tq, hw), F32),
            pltpu.VMEM((2 * tq, tk), F32),
            pltpu.VMEM((2 * tq, tk), F32),
            pltpu.VMEM((2 * tq, tk), BF16),
            pltpu.VMEM((2 * tq, tk), BF16),
        ],
        compiler_params=_cparams(("parallel", "parallel", "arbitrary")),
        name="attn_prompt",
    )(slab, k3, v3, g_norm, dl)


def _attn_sample_kernel(q_ref, kc_ref, vc_ref, kn_ref, vn_ref, g_ref, dl_ref, o_ref, *, tq, past, lam_init):
    h = pl.program_id(1)
    c, _, _ = _alibi_coef(h)
    qq = _split_maps(q_ref[...])
    q_pos = past + lax.broadcasted_iota(I32, (2 * tq, 1), 0) % tq

    def scores(k, k_pos):
        s = lax.dot_general(qq, k.astype(BF16), (((1,), (1,)), ((), ())), preferred_element_type=F32)
        s = s - c * jnp.abs(q_pos - k_pos).astype(F32)
        return jnp.where((k_pos // CHUNK) <= (q_pos // CHUNK), s, NEG_INF)

    sc = scores(kc_ref[0], lax.broadcasted_iota(I32, (1, past), 1))
    sn = scores(kn_ref[0], past + lax.broadcasted_iota(I32, (1, tq), 1))
    m = jnp.maximum(jnp.max(sc, axis=-1, keepdims=True), jnp.max(sn, axis=-1, keepdims=True))
    pc = jnp.exp2(sc - m)
    pn = jnp.exp2(sn - m)
    l = jnp.sum(pc, axis=-1, keepdims=True) + jnp.sum(pn, axis=-1, keepdims=True)
    acc = (jnp.dot(pc.astype(BF16), vc_ref[0].astype(BF16), preferred_element_type=F32)
           + jnp.dot(pn.astype(BF16), vn_ref[0].astype(BF16), preferred_element_type=F32))
    lam = _lambda_from(dl_ref, lam_init)
    o_ref[...] = _finish_heads(acc, 1.0 / l, lam, g_ref[...], tq, lam_init)


def _attn_sample(slab, kb, vb, cache_k, cache_v, g_norm, dl, batch, t_len, lam_init):
    past = cache_k.shape[1]
    hw = 2 * D_B
    k3 = kb.reshape(batch, t_len, KB_W)
    v3 = vb.reshape(batch, t_len, VB_W)
    kern = functools.partial(_attn_sample_kernel, tq=t_len, past=past, lam_init=lam_init)
    return pl.pallas_call(
        kern,
        grid=(batch, H_B),
        in_specs=[
            pl.BlockSpec((t_len, hw), lambda b, h: (b, SLAB_QB // hw + h)),
            pl.BlockSpec((1, past, hw), lambda b, h: (b, 0, h)),
            pl.BlockSpec((1, past, hw), lambda b, h: (b, 0, h)),
            pl.BlockSpec((1, t_len, hw), lambda b, h: (b, 0, h)),
            pl.BlockSpec((1, t_len, hw), lambda b, h: (b, 0, h)),
            pl.BlockSpec((1, hw), lambda b, h: (0, 0)),
            pl.BlockSpec((4, D_B), lambda b, h: (0, 0)),
        ],
        out_specs=pl.BlockSpec((t_len, hw), lambda b, h: (b, h)),
        out_shape=jax.ShapeDtypeStruct((batch * t_len, VB_W), BF16),
        compiler_params=_cparams(("parallel", "parallel")),
        name="attn_sample",
    )(slab, cache_k, cache_v, k3, v3, g_norm, dl)


def _finish_kernel(x_ref, oa_ref, ra_ref, ob_ref, gt_ref, wpa_ref, wpb_ref, wo_ref, g_ref, b_ref,
                   wrh_ref, wrl_ref, br_ref, h_ref, hp_ref, idx_ref, tw_ref, cnt_ref):
    i = pl.program_id(0)
    ra = ra_ref[...].astype(F32)
    ua = ra * _sigmoid(ra) * oa_ref[...].astype(F32)
    ya = jnp.dot(ua.astype(BF16), wpa_ref[...], preferred_element_type=F32)
    yb = jnp.dot(ob_ref[...], wpb_ref[...], preferred_element_type=F32)
    gt = gt_ref[...].astype(F32)
    mixed = _sigmoid(gt[:, :D_MODEL]) * ya + _sigmoid(gt[:, D_MODEL:]) * yb
    mix = jnp.dot(mixed.astype(BF16), wo_ref[...], preferred_element_type=F32)
    hh = _layer_norm(DEEPNORM_ALPHA * x_ref[...] + mix, g_ref[...], b_ref[...])
    h_ref[...] = hh
    hp_ref[...] = _pack_bf16_pairs(hh)

    h_hi = hh.astype(BF16)
    h_lo = (hh - h_hi.astype(F32)).astype(BF16)
    lg = (jnp.dot(h_hi, wrh_ref[...], preferred_element_type=F32)
          + jnp.dot(h_lo, wrh_ref[...], preferred_element_type=F32)
          + jnp.dot(h_hi, wrl_ref[...], preferred_element_type=F32)) + br_ref[...]
    tm = lg.shape[0]
    lane = lax.broadcasted_iota(I32, (tm, LANES), 1)
    lane_f = lane.astype(F32)
    work = lg
    vals, idxs = [], []
    cnt = jnp.zeros((tm, LANES), F32)
    for _ in range(TOP_K):
        mx = jnp.max(work, axis=-1, keepdims=True)
        ix = jnp.min(jnp.where(work == mx, lane_f, float(LANES)), axis=-1, keepdims=True)
        hit = lane_f == ix
        cnt = cnt + hit.astype(F32)
        work = jnp.where(hit, NEG_INF, work)
        vals.append(mx)
        idxs.append(ix)
    es = [jnp.exp(v - vals[0]) for v in vals]
    den = es[0] + es[1] + es[2] + es[3]
    idx_full = jnp.zeros((tm, LANES), F32)
    tw_full = jnp.zeros((tm, LANES), F32)
    for r in range(TOP_K):
        idx_full = jnp.where(lane == r, idxs[r], idx_full)
        tw_full = jnp.where(lane == r, es[r] / den, tw_full)
    idx_ref[...] = idx_full.T[:SUBLANES].astype(I32)
    tw_ref[...] = tw_full[:, :TOP_K]

    @pl.when(i == 0)
    def _():
        cnt_ref[...] = jnp.zeros(cnt_ref.shape, F32)

    cnt_ref[...] += jnp.sum(cnt, axis=0, keepdims=True)


def _finish(x2, oa, slab, ob, p):
    n = x2.shape[0]
    tm = min(512, n)
    const = lambda i: (0, 0)
    return pl.pallas_call(
        _finish_kernel,
        grid=(n // tm,),
        in_specs=[
            pl.BlockSpec((tm, D_MODEL), lambda i: (i, 0)),
            pl.BlockSpec((tm, VA_W), lambda i: (i, 0)),
            pl.BlockSpec((tm, RA_W), lambda i: (i, SLAB_RA // RA_W)),
            pl.BlockSpec((tm, VB_W), lambda i: (i, 0)),
            pl.BlockSpec((tm, GT_W), lambda i: (i, SLAB_GT // GT_W)),
            pl.BlockSpec((VA_W, D_MODEL), const),
            pl.BlockSpec((VB_W, D_MODEL), const),
            pl.BlockSpec((D_MODEL, D_MODEL), const),
            pl.BlockSpec((1, D_MODEL), const),
            pl.BlockSpec((1, D_MODEL), const),
            pl.BlockSpec((D_MODEL, LANES), const),
            pl.BlockSpec((D_MODEL, LANES), const),
            pl.BlockSpec((1, LANES), const),
        ],
        out_specs=[
            pl.BlockSpec((tm, D_MODEL), lambda i: (i, 0)),
            pl.BlockSpec((tm, D_MODEL // 2), lambda i: (i, 0)),
            pl.BlockSpec((SUBLANES, tm), lambda i: (0, i)),
            pl.BlockSpec((tm, TOP_K), lambda i: (i, 0)),
            pl.BlockSpec((1, LANES), const),
        ],
        out_shape=[
            jax.ShapeDtypeStruct((n, D_MODEL), F32),
            jax.ShapeDtypeStruct((n, D_MODEL // 2), U32),
            jax.ShapeDtypeStruct((SUBLANES, n), I32),
            jax.ShapeDtypeStruct((n, TOP_K), F32),
            jax.ShapeDtypeStruct((1, LANES), F32),
        ],
        compiler_params=_cparams(("arbitrary",)),
        name="finish",
    )(x2, oa, slab, ob, slab, p["w_pa"], p["w_pb"], p["w_o"], p["ln1_g"], p["ln1_b"],
      p["w_r_hi"], p["w_r_lo"], p["b_r"])


def _dest_kernel(idx_ref, ps_ref, ut_ref, dest_ref, carry_ref):
    i = pl.program_id(0)

    @pl.when(i == 0)
    def _():
        carry_ref[...] = jnp.zeros(carry_ref.shape, F32)

    tm = idx_ref.shape[1]
    expert = lax.broadcasted_iota(I32, (N_EXPERTS, tm), 0)
    idx = idx_ref[...]
    hits = [expert == idx[k:k + 1, :] for k in range(TOP_K)]
    cnt = jnp.zeros((N_EXPERTS, tm), F32)
    for hit in hits:
        cnt = cnt + hit.astype(F32)
    before = jnp.dot(cnt.astype(BF16), ut_ref[...], preferred_element_type=F32)
    base = before + carry_ref[:, 0:1] + ps_ref[:, 0:1]
    row = lax.broadcasted_iota(I32, (SUBLANES, tm), 0)
    dest = jnp.zeros((SUBLANES, tm), F32)
    for k in range(TOP_K):
        d = jnp.sum(jnp.where(hits[k], base, 0.0), axis=0, keepdims=True)
        dest = jnp.where(row == k, d, dest)
    dest_ref[...] = dest.astype(I32)
    carry_ref[...] += jnp.sum(cnt, axis=1, keepdims=True)


def _dest(idx_t, pstart):
    n = idx_t.shape[1]
    tm = min(512, n)
    utri = jnp.asarray(np.triu(np.ones((tm, tm), np.float32), 1), BF16)
    return pl.pallas_call(
        _dest_kernel,
        grid=(n // tm,),
        in_specs=[
            pl.BlockSpec((SUBLANES, tm), lambda i: (0, i)),
            pl.BlockSpec((N_EXPERTS, LANES), lambda i: (0, 0)),
            pl.BlockSpec((tm, tm), lambda i: (0, 0)),
        ],
        out_specs=pl.BlockSpec((SUBLANES, tm), lambda i: (0, i)),
        out_shape=jax.ShapeDtypeStruct((SUBLANES, n), I32),
        scratch_shapes=[pltpu.VMEM((N_EXPERTS, LANES), F32)],
        compiler_params=_cparams(("arbitrary",)),
        name="dest",
    )(idx_t, pstart, utri)


def _scatter_kernel(dest_ref, hp_ref, rows_in_ref, rows_ref, sem):
    del rows_in_ref
    tm = hp_ref.shape[0]

    def row_copy(t, d):
        return pltpu.make_async_copy(hp_ref.at[pl.ds(t, 1)], rows_ref.at[pl.ds(d, 1)], sem)

    def issue(t, carry):
        for k in range(TOP_K):
            row_copy(t, dest_ref[k, t]).start(priority=k % 2)
        return carry

    lax.fori_loop(0, tm, issue, 0, unroll=4)
    for k in range(TOP_K):
        pltpu.make_async_copy(hp_ref, rows_ref.at[pl.ds(0, tm)], sem).wait()


def _scatter(dest_t, hp, n_rows):
    n = hp.shape[0]
    tm = min(512, n)
    rows0 = jnp.zeros((n_rows, D_MODEL // 2), U32)
    return pl.pallas_call(
        _scatter_kernel,
        grid=(n // tm,),
        in_specs=[
            pl.BlockSpec((SUBLANES, tm), lambda i: (0, i), memory_space=pltpu.SMEM),
            pl.BlockSpec((tm, D_MODEL // 2), lambda i: (i, 0)),
            pl.BlockSpec(memory_space=pl.ANY),
        ],
        out_specs=pl.BlockSpec(memory_space=pl.ANY),
        out_shape=jax.ShapeDtypeStruct((n_rows, D_MODEL // 2), U32),
        scratch_shapes=[pltpu.SemaphoreType.DMA(())],
        input_output_aliases={2: 0},
        compiler_params=_cparams(("arbitrary",)),
        name="scatter",
    )(dest_t, hp, rows0)


def _expert_kernel(be_ref, nu_ref, rows_ref, wgu_ref, bgu_ref, wd_ref, bd_ref, out_ref):
    del be_ref
    i = pl.program_id(0)

    @pl.when(i < nu_ref[0])
    def _():
        x = _unpack_bf16_pairs(rows_ref[...]).astype(BF16)
        gu = jnp.dot(x, wgu_ref[0], preferred_element_type=F32) + bgu_ref[0]
        gate = jnp.minimum(gu[:, :D_FF], SWIGLU_LIMIT)
        up = jnp.clip(gu[:, D_FF:], -SWIGLU_LIMIT, SWIGLU_LIMIT)
        hidden = (up + 1.0) * (gate * _sigmoid(SWIGLU_ALPHA * gate))
        o = jnp.dot(hidden.astype(BF16), wd_ref[0], preferred_element_type=F32) + bd_ref[0]
        out_ref[...] = _pack_bf16_pairs(o)

    @pl.when(i >= nu_ref[0])
    def _():
        out_ref[...] = jnp.zeros(out_ref.shape, U32)


def _experts(rows, block_e, n_used, p):
    n_rows = rows.shape[0]
    bm = EXPERT_BM
    n_blocks = n_rows // bm
    grid_spec = pltpu.PrefetchScalarGridSpec(
        num_scalar_prefetch=2,
        grid=(n_blocks,),
        in_specs=[
            pl.BlockSpec((bm, D_MODEL // 2), lambda i, be, nu: (i, 0)),
            pl.BlockSpec((1, D_MODEL, 2 * D_FF), lambda i, be, nu: (be[i], 0, 0)),
            pl.BlockSpec((1, 1, 2 * D_FF), lambda i, be, nu: (be[i], 0, 0)),
            pl.BlockSpec((1, D_FF, D_MODEL), lambda i, be, nu: (be[i], 0, 0)),
            pl.BlockSpec((1, 1, D_MODEL), lambda i, be, nu: (be[i], 0, 0)),
        ],
        out_specs=pl.BlockSpec((bm, D_MODEL // 2), lambda i, be, nu: (i, 0)),
    )
    return pl.pallas_call(
        _expert_kernel,
        grid_spec=grid_spec,
        out_shape=jax.ShapeDtypeStruct((n_rows, D_MODEL // 2), U32),
        compiler_params=_cparams(("arbitrary",)),
        name="experts",
    )(block_e, n_used, rows, p["w_gu"], p["b_gu"], p["w_down"], p["b_down"])


def _combine_kernel(dest_ref, tw_ref, h_ref, rows_ref, g_ref, b_ref, y_ref, buf_ref, sem):
    tm = h_ref.shape[0]

    def row_copy(t, k, d):
        return pltpu.make_async_copy(rows_ref.at[pl.ds(d, 1)], buf_ref.at[k, pl.ds(t, 1)], sem)

    def issue(t, carry):
        for k in range(TOP_K):
            row_copy(t, k, dest_ref[k, t]).start(priority=k % 2)
        return carry

    lax.fori_loop(0, tm, issue, 0, unroll=4)
    for k in range(TOP_K):
        pltpu.make_async_copy(rows_ref.at[pl.ds(0, tm)], buf_ref.at[k], sem).wait()
    tw = tw_ref[...]
    f = jnp.zeros((tm, D_MODEL), F32)
    for k in range(TOP_K):
        f = f + tw[:, k:k + 1] * _unpack_bf16_pairs(buf_ref[k])
    y_ref[...] = _layer_norm(DEEPNORM_ALPHA * h_ref[...] + f, g_ref[...], b_ref[...])


def _combine(dest_t, top_w, hh, out_rows, p):
    n = hh.shape[0]
    tm = min(512, n)
    const = lambda i: (0, 0)
    return pl.pallas_call(
        _combine_kernel,
        grid=(n // tm,),
        in_specs=[
            pl.BlockSpec((SUBLANES, tm), lambda i: (0, i), memory_space=pltpu.SMEM),
            pl.BlockSpec((tm, TOP_K), lambda i: (i, 0)),
            pl.BlockSpec((tm, D_MODEL), lambda i: (i, 0)),
            pl.BlockSpec(memory_space=pl.ANY),
            pl.BlockSpec((1, D_MODEL), const),
            pl.BlockSpec((1, D_MODEL), const),
        ],
        out_specs=pl.BlockSpec((tm, D_MODEL), lambda i: (i, 0)),
        out_shape=jax.ShapeDtypeStruct((n, D_MODEL), F32),
        scratch_shapes=[pltpu.VMEM((TOP_K, tm, D_MODEL // 2), U32), pltpu.SemaphoreType.DMA(())],
        compiler_params=_cparams(("arbitrary",)),
        name="combine",
    )(dest_t, top_w, hh, out_rows, p["ln2_g"], p["ln2_b"])


def _moe(hh, hp, top_idx, top_w, counts, p):
    n = hh.shape[0]
    bm = EXPERT_BM
    n_rows = n * TOP_K + N_EXPERTS * bm
    cnt = counts[0, :N_EXPERTS].astype(I32)
    padded = (cnt + bm - 1) // bm * bm
    pend = jnp.cumsum(padded)
    pstart = pend - padded
    ps_col = jnp.broadcast_to(pstart.astype(F32)[:, None], (N_EXPERTS, LANES))
    block_row0 = jnp.arange(n_rows // bm, dtype=I32) * bm
    block_e = jnp.minimum(jnp.sum((pend[None, :] <= block_row0[:, None]).astype(I32), axis=1), N_EXPERTS - 1)
    n_used = (pend[-1:] // bm).astype(I32)
    dest = _dest(top_idx, ps_col)
    rows = _scatter(dest, hp, n_rows)
    out_rows = _experts(rows, block_e, n_used, p)
    return _combine(dest, top_w, hh, out_rows, p)


def _prep_params(w_in, b_in, w_alpha, b_alpha, gla_norm_g, diff_norm_g, w_pa, w_pb, w_o, ln1_g, ln1_b,
                 w_router, b_router, w_gu, b_gu, w_down, b_down, ln2_g, ln2_b):
    o = IN_OFFS
    seg = lambda a, i: a[..., o[i]:o[i + 1]]
    order = (0, 1, 2, 3, 5, 8)
    w_slab = jnp.concatenate([seg(w_in, i) for i in order], axis=1).astype(BF16)
    b_slab = jnp.concatenate([seg(b_in, i) for i in order])[None, :]
    scale = np.ones((1, SLAB_W), np.float32)
    scale[0, SLAB_QA:SLAB_QA + QA_W] = DK_A ** -0.5
    scale[0, SLAB_QB:SLAB_QB + QB_W] = D_B ** -0.5 * LOG2E
    w_lr = jnp.zeros((D_MODEL, LANES), F32).at[:, :LR_W].set(seg(w_in, 4)).astype(BF16)
    b_lr = jnp.zeros((1, LANES), F32).at[0, :LR_W].set(seg(b_in, 4))
    w_al = jnp.zeros((LANES, QA_W), F32).at[:LR_W].set(w_alpha).astype(BF16)
    w_r = jnp.zeros((D_MODEL, LANES), F32).at[:, :N_EXPERTS].set(w_router)
    w_r_hi = w_r.astype(BF16)
    w_r_lo = (w_r - w_r_hi.astype(F32)).astype(BF16)
    b_r = jnp.full((1, LANES), NEG_INF, F32).at[0, :N_EXPERTS].set(b_router)
    return dict(
        w_slab=w_slab, b_slab=b_slab, s_slab=jnp.asarray(scale),
        w_lr=w_lr, b_lr=b_lr, w_alpha=w_al, b_alpha=b_alpha[None, :],
        w_k=seg(w_in, 6).astype(BF16), w_v=seg(w_in, 7).astype(BF16),
        b_k=seg(b_in, 6)[None, :], b_v=seg(b_in, 7)[None, :],
        gla_g=gla_norm_g[None, :], diff_g=diff_norm_g[None, :],
        w_pa=w_pa.astype(BF16), w_pb=w_pb.astype(BF16), w_o=w_o.astype(BF16),
        ln1_g=ln1_g[None, :], ln1_b=ln1_b[None, :],
        w_r_hi=w_r_hi, w_r_lo=w_r_lo, b_r=b_r,
        w_gu=w_gu.astype(BF16), b_gu=b_gu[:, None, :], w_down=w_down.astype(BF16), b_down=b_down[:, None, :],
        ln2_g=ln2_g[None, :], ln2_b=ln2_b[None, :],
    )


def _layer(x, p, dl, lam_init, s0, cache_k, cache_v):
    batch, t_len, _ = x.shape
    x2 = x.reshape(batch * t_len, D_MODEL)
    slab, loga = _proj_slab(x2, p)
    kb, vb = _proj_kv(x2, p)
    oa, s_new = _gla(slab, loga, s0, p["gla_g"], batch, t_len)
    if cache_k is None:
        ob = _attn_prompt(slab, kb, vb, p["diff_g"], dl, batch, t_len, lam_init)
    else:
        ob = _attn_sample(slab, kb, vb, cache_k, cache_v, p["diff_g"], dl, batch, t_len, lam_init)
    hh, hp, top_idx, top_w, counts = _finish(x2, oa, slab, ob, p)
    y = _moe(hh, hp, top_idx, top_w, counts, p)
    return (y.reshape(batch, t_len, D_MODEL), s_new,
            kb.reshape(batch, t_len, H_B, 2, D_B), vb.reshape(batch, t_len, H_B, 2 * D_B))


def kernel(x_prompt, x_sample, cache_k, cache_v, state_gla, w_in, b_in, w_alpha, b_alpha, gla_norm_g,
           diff_lambda, diff_norm_g, w_pa, w_pb, w_o, ln1_g, ln1_b, w_router, b_router, w_gu, b_gu,
           w_down, b_down, ln2_g, ln2_b):
    assert w_in.shape[0] == DEPTH == 1
    l = 0
    lam_init = 0.8 - 0.6 * math.exp(-0.3 * l)
    p = _prep_params(w_in[l], b_in[l], w_alpha[l], b_alpha[l], gla_norm_g[l], diff_norm_g[l], w_pa[l],
                     w_pb[l], w_o[l], ln1_g[l], ln1_b[l], w_router[l], b_router[l], w_gu[l], b_gu[l],
                     w_down[l], b_down[l], ln2_g[l], ln2_b[l])
    dl = diff_lambda[l]
    bp = x_prompt.shape[0]
    bs, ts = x_sample.shape[0], x_sample.shape[1]
    past = cache_k.shape[2]
    yp, s_p, k_p, v_p = _layer(x_prompt, p, dl, lam_init, jnp.zeros((bp, H_A, DK_A, DV_A), F32), None, None)
    ck = cache_k[l].reshape(bs, past, KB_W)
    cv = cache_v[l].reshape(bs, past, VB_W)
    ys, s_s, k_s, v_s = _layer(x_sample, p, dl, lam_init, state_gla[l], ck, cv)
    return (yp, ys, s_p[None], k_p[None], v_p[None], s_s[None], k_s[None], v_s[None])
```

```python
import functools
import math

import numpy as np
import jax
import jax.numpy as jnp
from jax import lax
from jax.experimental import pallas as pl
from jax.experimental.pallas import tpu as pltpu

F32 = jnp.float32
BF16 = jnp.bfloat16
U32 = jnp.uint32
I32 = jnp.int32

D_MODEL = 1024
CHUNK = 64
H_A = 4
DK_A = 128
DV_A = 256
GATE_RANK = 16
GATE_TAU = 16.0
H_B = 8
D_B = 64
N_EXPERTS = 32
TOP_K = 4
D_FF = D_MODEL
SWIGLU_LIMIT = 7.0
SWIGLU_ALPHA = 1.702
EPS = 1e-5
DEPTH = 1
DEEPNORM_ALPHA = (2.0 * DEPTH) ** 0.25

QA_W = H_A * DK_A
KA_W = H_A * DK_A
VA_W = H_A * DV_A
RA_W = H_A * DV_A
LR_W = GATE_RANK
QB_W = H_B * 2 * D_B
KB_W = H_B * 2 * D_B
VB_W = H_B * 2 * D_B
GT_W = 2 * D_MODEL
IN_SIZES = (QA_W, KA_W, VA_W, RA_W, LR_W, QB_W, KB_W, VB_W, GT_W)
IN_OFFS = tuple(int(v) for v in np.cumsum((0,) + IN_SIZES))

SLAB_QA, SLAB_KA, SLAB_VA, SLAB_RA, SLAB_QB, SLAB_GT = 0, 512, 1024, 2048, 3072, 4096
SLAB_W = 6144
LANES = 128
SUBLANES = 8
EXPERT_BM = 512
VMEM_LIMIT = 56 * 1024 * 1024

NEG_INF = float("-inf")
LOG2E = math.log2(math.e)


def _cparams(sem):
    return pltpu.CompilerParams(dimension_semantics=sem, vmem_limit_bytes=VMEM_LIMIT)


def _sigmoid(x):
    return 1.0 / (1.0 + jnp.exp(-x))


def _pack_bf16_pairs(x):
    n = x.shape[1] // 2
    xb = x.astype(BF16).astype(F32)
    lo = pltpu.bitcast(xb[:, :n], U32) >> 16
    hi = pltpu.bitcast(xb[:, n:], U32) & jnp.uint32(0xFFFF0000)
    return hi | lo


def _unpack_bf16_pairs(r):
    lo = pltpu.bitcast(r << 16, F32)
    hi = pltpu.bitcast(r & jnp.uint32(0xFFFF0000), F32)
    return jnp.concatenate([lo, hi], axis=1)


def _layer_norm(y, g, b):
    mu = jnp.mean(y, axis=-1, keepdims=True)
    d = y - mu
    var = jnp.mean(d * d, axis=-1, keepdims=True)
    return d * lax.rsqrt(var + EPS) * g + b


def _proj_slab_kernel(x_ref, w_ref, b_ref, s_ref, wlr_ref, blr_ref, wal_ref, bal_ref,
                      slab_ref, loga_ref, xb_ref):
    j = pl.program_id(1)

    @pl.when(j == 0)
    def _():
        xb_ref[...] = x_ref[...].astype(BF16)
        lra = jnp.dot(xb_ref[...], wlr_ref[...], preferred_element_type=F32) + blr_ref[...]
        z = jnp.dot(lra.astype(BF16), wal_ref[...], preferred_element_type=F32) + bal_ref[...]
        loga_ref[...] = (jnp.minimum(z, 0.0) - jnp.log(1.0 + jnp.exp(-jnp.abs(z)))) * (1.0 / GATE_TAU)

    acc = jnp.dot(xb_ref[...], w_ref[...], preferred_element_type=F32)
    slab_ref[...] = ((acc + b_ref[...]) * s_ref[...]).astype(BF16)


def _proj_slab(x2, p):
    n = x2.shape[0]
    tm = min(1024, n)
    tn = 1024
    grid = (n // tm, SLAB_W // tn)
    return pl.pallas_call(
        _proj_slab_kernel,
        grid=grid,
        in_specs=[
            pl.BlockSpec((tm, D_MODEL), lambda i, j: (i, 0)),
            pl.BlockSpec((D_MODEL, tn), lambda i, j: (0, j)),
            pl.BlockSpec((1, tn), lambda i, j: (0, j)),
            pl.BlockSpec((1, tn), lambda i, j: (0, j)),
            pl.BlockSpec((D_MODEL, LANES), lambda i, j: (0, 0)),
            pl.BlockSpec((1, LANES), lambda i, j: (0, 0)),
            pl.BlockSpec((LANES, QA_W), lambda i, j: (0, 0)),
            pl.BlockSpec((1, QA_W), lambda i, j: (0, 0)),
        ],
        out_specs=[
            pl.BlockSpec((tm, tn), lambda i, j: (i, j)),
            pl.BlockSpec((tm, QA_W), lambda i, j: (i, 0)),
        ],
        out_shape=[
            jax.ShapeDtypeStruct((n, SLAB_W), BF16),
            jax.ShapeDtypeStruct((n, QA_W), F32),
        ],
        scratch_shapes=[pltpu.VMEM((tm, D_MODEL), BF16)],
        compiler_params=_cparams(("parallel", "arbitrary")),
        name="proj_slab",
    )(x2, p["w_slab"], p["b_slab"], p["s_slab"], p["w_lr"], p["b_lr"], p["w_alpha"], p["b_alpha"])


def _proj_kv_kernel(x_ref, wk_ref, wv_ref, bk_ref, bv_ref, krows_ref, kb_ref, v_ref):
    xb = x_ref[...].astype(BF16)
    k = jnp.dot(xb, wk_ref[...], preferred_element_type=F32) + bk_ref[...]
    kb_ref[...] = k.astype(BF16)
    for h in range(H_B):
        for j in range(2):
            c0 = (2 * h + j) * D_B
            krows_ref[:, h, j, :] = k[:, c0:c0 + D_B]
    v_ref[...] = jnp.dot(xb, wv_ref[...], preferred_element_type=F32) + bv_ref[...]


def _proj_kv(x2, p):
    n = x2.shape[0]
    tm = min(512, n)
    const = lambda i: (0, 0)
    return pl.pallas_call(
        _proj_kv_kernel,
        grid=(n // tm,),
        in_specs=[
            pl.BlockSpec((tm, D_MODEL), lambda i: (i, 0)),
            pl.BlockSpec((D_MODEL, KB_W), const),
            pl.BlockSpec((D_MODEL, VB_W), const),
            pl.BlockSpec((1, KB_W), const),
            pl.BlockSpec((1, VB_W), const),
        ],
        out_specs=[
            pl.BlockSpec((tm, H_B, 2, D_B), lambda i: (i, 0, 0, 0)),
            pl.BlockSpec((tm, KB_W), lambda i: (i, 0)),
            pl.BlockSpec((tm, VB_W), lambda i: (i, 0)),
        ],
        out_shape=[
            jax.ShapeDtypeStruct((n, H_B, 2, D_B), F32),
            jax.ShapeDtypeStruct((n, KB_W), BF16),
            jax.ShapeDtypeStruct((n, VB_W), F32),
        ],
        compiler_params=_cparams(("parallel",)),
        name="proj_kv",
    )(x2, p["w_k"], p["w_v"], p["b_k"], p["b_v"])


def _gla_tables(L):
    nl = int(math.log2(L))
    t = np.arange(L)
    D = np.zeros(((nl + 2) * L, L), np.float32)
    masks = np.zeros((nl + 1, L, L), np.float32)
    for l in range(nl):
        m = L >> (l + 1)
        grp = t // (2 * m)
        mid = grp * 2 * m + m - 1
        upper = (t % (2 * m)) >= m
        for r in range(L):
            if upper[r]:
                D[l * L + r, mid[r] + 1:r + 1] = 1.0
            else:
                D[l * L + r, r + 1:mid[r] + 1] = 1.0
        masks[l] = (upper[:, None] & ~upper[None, :] & (grp[:, None] == grp[None, :])).astype(np.float32)
    D[nl * L:(nl + 1) * L] = np.tril(np.ones((L, L), np.float32))
    D[(nl + 1) * L:] = np.triu(np.ones((L, L), np.float32), 1)
    masks[nl] = np.eye(L, dtype=np.float32)
    return jnp.asarray(D, BF16), jnp.asarray(masks, F32)


def _gla_kernel(q_ref, k_ref, v_ref, la_ref, s0_ref, g_ref, d_ref, m_ref, o_ref, s_ref, *, L, n_chunks):
    c = pl.program_id(1)
    nl = int(math.log2(L))

    @pl.when(c == 0)
    def _():
        s_ref[...] = s0_ref[...]

    dmat = d_ref[...]
    ones_col = jnp.ones((L, LANES), BF16)
    g = g_ref[...]
    nt = (((1,), (1,)), ((), ()))
    tn = (((0,), (0,)), ((), ()))

    def chunk(ci, carry):
        r0 = pl.multiple_of(ci * L, L)
        rows = pl.ds(r0, L)
        for h in range(H_A):
            kc = slice(h * DK_A, (h + 1) * DK_A)
            vc = slice(h * DV_A, (h + 1) * DV_A)
            q = q_ref[rows, kc].astype(F32)
            k = k_ref[rows, kc].astype(F32)
            v = v_ref[rows, vc]
            la = la_ref[rows, kc]
            la_hi = la.astype(BF16)
            la_lo = (la - la_hi.astype(F32)).astype(BF16)
            e = (jnp.dot(dmat, la_hi, preferred_element_type=F32)
                 + jnp.dot(dmat, la_lo, preferred_element_type=F32))
            x = jnp.exp(e)
            a = m_ref[nl] * lax.dot_general(q.astype(BF16), k.astype(BF16), nt, preferred_element_type=F32)
            for l in range(nl):
                xl = x[l * L:(l + 1) * L]
                a = a + m_ref[l] * lax.dot_general((q * xl).astype(BF16), (k * xl).astype(BF16), nt,
                                                   preferred_element_type=F32)
            xb = x[nl * L:(nl + 1) * L]
            xs = x[(nl + 1) * L:]
            s_old = s_ref[0, h]
            o = (jnp.dot((q * xb).astype(BF16), s_old.astype(BF16), preferred_element_type=F32)
                 + jnp.dot(a.astype(BF16), v, preferred_element_type=F32))
            bl = (lax.dot_general(la_hi, ones_col, tn, preferred_element_type=F32)
                  + lax.dot_general(la_lo, ones_col, tn, preferred_element_type=F32))
            dec = jnp.exp(bl[:, 0:1])
            s_ref[0, h] = dec * s_old + lax.dot_general((k * xs).astype(BF16), v, tn, preferred_element_type=F32)
            ms = jnp.mean(o * o, axis=-1, keepdims=True)
            o_ref[rows, vc] = (o * lax.rsqrt(ms + EPS) * g).astype(BF16)
        return carry

    lax.fori_loop(0, n_chunks, chunk, 0, unroll=2 if n_chunks % 2 == 0 else 1)


def _gla(slab, loga, s0, g_norm, batch, t_len):
    L = min(CHUNK, t_len)
    tb = min(512, t_len)
    nb = t_len // tb
    dmat, masks = _gla_tables(L)
    kern = functools.partial(_gla_kernel, L=L, n_chunks=tb // L)
    row = lambda b, c: b * nb + c
    return pl.pallas_call(
        kern,
        grid=(batch, nb),
        in_specs=[
            pl.BlockSpec((tb, QA_W), lambda b, c: (row(b, c), SLAB_QA // QA_W)),
            pl.BlockSpec((tb, KA_W), lambda b, c: (row(b, c), SLAB_KA // KA_W)),
            pl.BlockSpec((tb, VA_W), lambda b, c: (row(b, c), SLAB_VA // VA_W)),
            pl.BlockSpec((tb, QA_W), lambda b, c: (row(b, c), 0)),
            pl.BlockSpec((1, H_A, DK_A, DV_A), lambda b, c: (b, 0, 0, 0)),
            pl.BlockSpec((1, DV_A), lambda b, c: (0, 0)),
            pl.BlockSpec(dmat.shape, lambda b, c: (0, 0)),
            pl.BlockSpec(masks.shape, lambda b, c: (0, 0, 0)),
        ],
        out_specs=[
            pl.BlockSpec((tb, VA_W), lambda b, c: (row(b, c), 0)),
            pl.BlockSpec((1, H_A, DK_A, DV_A), lambda b, c: (b, 0, 0, 0)),
        ],
        out_shape=[
            jax.ShapeDtypeStruct((batch * t_len, VA_W), BF16),
            jax.ShapeDtypeStruct((batch, H_A, DK_A, DV_A), F32),
        ],
        compiler_params=_cparams(("parallel", "arbitrary")),
        name="gla",
    )(slab, slab, slab, loga, s0, g_norm, dmat, masks)


def _lambda_from(dl_ref, lam_init):
    dl = dl_ref[...]
    a = jnp.sum(dl[0:1] * dl[1:2], axis=-1, keepdims=True)
    b = jnp.sum(dl[2:3] * dl[3:4], axis=-1, keepdims=True)
    return jnp.exp(a) - jnp.exp(b) + lam_init


def _split_maps(q):
    lane = lax.broadcasted_iota(I32, q.shape, 1)
    zero = jnp.zeros_like(q)
    return jnp.concatenate([jnp.where(lane < D_B, q, zero), jnp.where(lane >= D_B, q, zero)], axis=0)


def _alibi_coef(h):
    c = jnp.full((1, 1), LOG2E, F32) * jnp.exp2(-(h + 1).astype(F32))
    c_hi = c.astype(BF16).astype(F32)
    return c, c_hi, c - c_hi


def _finish_heads(acc, inv_l, lam, g, tq, lam_init):
    o = acc * inv_l
    out = o[:tq] - lam * o[tq:]
    ms = jnp.mean(out * out, axis=-1, keepdims=True)
    return (out * lax.rsqrt(ms + EPS) * g * (1.0 - lam_init)).astype(BF16)


def _attn_prompt_kernel(q_ref, k_ref, v_ref, g_ref, dl_ref, o_ref, ka_ref, vb_ref, qaug_ref, d_ref, m_ref, l_ref,
                        acc_ref, s0_ref, s1_ref, p0_ref, p1_ref,
                        *, tq, tk, lam_init):
    h = pl.program_id(1)
    qi = pl.program_id(2)
    c, c_hi, c_lo = _alibi_coef(h)
    hw = 2 * D_B

    def aug_lanes(shape, pos, sign):
        lane = lax.broadcasted_iota(I32, shape, 1)
        r = (pos & 255).astype(F32) * sign
        a = (pos >> 8).astype(F32) * sign
        coef = jnp.where(lane == 0, c_hi, jnp.where(lane == 1, c_lo,
                         jnp.where(lane == 2, 256.0 * c_hi, jnp.where(lane == 3, 256.0 * c_lo, 0.0))))
        ints = jnp.where(lane < 2, r, jnp.where(lane < 4, a, 0.0))
        return coef, ints, lane

    @pl.when(qi == 0)
    def _():
        t_len = ka_ref.shape[0]
        j_rel = lax.broadcasted_iota(I32, (t_len, hw), 0) & (tk - 1)
        coef, ints, lane = aug_lanes((t_len, hw), j_rel, 1.0)
        k_aug = jnp.where(lane < 4, ints, pltpu.roll(coef, 4, 1))
        ka_ref[:, :hw] = k_ref[0].astype(BF16)
        ka_ref[:, hw:] = k_aug.astype(BF16)
        vb_ref[...] = v_ref[0].astype(BF16)
        i_rel = lax.broadcasted_iota(I32, (2 * tq, hw), 0) & (tq - 1)
        coef, ints, lane = aug_lanes((2 * tq, hw), i_rel, -1.0)
        qaug_ref[...] = jnp.where(lane < 4, coef, pltpu.roll(ints, 4, 1)).astype(BF16)
        i = lax.broadcasted_iota(I32, (2 * tq, tk), 0) & (tq - 1)
        j = lax.broadcasted_iota(I32, (2 * tq, tk), 1)
        fwd = jnp.maximum(j - i, 0).astype(F32)
        d_ref[...] = jnp.where((j >> 6) <= (i >> 6), -2.0 * c * fwd, NEG_INF)

    qa = jnp.concatenate([_split_maps(q_ref[...]), qaug_ref[...]], axis=1)
    m_ref[...] = jnp.full(m_ref.shape, -1e30, F32)
    l_ref[...] = jnp.zeros(l_ref.shape, F32)

    def scores(kv):
        k0 = pl.multiple_of(kv * tk, tk)
        return lax.dot_general(qa, ka_ref[pl.ds(k0, tk), :], (((1,), (1,)), ((), ())), preferred_element_type=F32)

    def weighted_values(p_ref, kv):
        k0 = pl.multiple_of(kv * tk, tk)
        return jnp.dot(p_ref[...], vb_ref[pl.ds(k0, tk), :], preferred_element_type=F32)

    def softmax_step(s, off):
        m_old = m_ref[...]
        m_new = jnp.maximum(m_old, jnp.max(s, axis=-1, keepdims=True) - off)
        shift = m_new + off
        p = jnp.exp2(s - jnp.concatenate([shift] * (tk // hw), axis=1))
        alpha = jnp.exp2(m_old - m_new)
        l_ref[...] = alpha * l_ref[...] + jnp.sum(p, axis=-1, keepdims=True)
        m_ref[...] = m_new
        return p.astype(BF16), alpha

    p_diag, _ = softmax_step(scores(qi) + d_ref[...], jnp.zeros((1, 1), F32))
    p1_ref[...] = p_diag
    acc_ref[...] = jnp.zeros(acc_ref.shape, F32)

    @pl.when(qi > 0)
    def _():
        s0_ref[...] = scores(0)

    def step(kv, s_cur, s_nxt, p_prev, p_cur):
        pv_prev = weighted_values(p_prev, jnp.where(kv == 0, qi, kv - 1))
        s_nxt[...] = scores(jnp.minimum(kv + 1, qi - 1))
        p, alpha = softmax_step(s_cur[...], c * (qi * tq - kv * tk).astype(F32))
        acc_ref[...] = alpha * (acc_ref[...] + pv_prev)
        p_cur[...] = p

    def pair(j, carry):
        step(2 * j, s0_ref, s1_ref, p1_ref, p0_ref)
        step(2 * j + 1, s1_ref, s0_ref, p0_ref, p1_ref)
        return carry

    lax.fori_loop(0, qi // 2, pair, 0)

    @pl.when(qi % 2 == 1)
    def _():
        step(qi - 1, s0_ref, s1_ref, p1_ref, p0_ref)
        acc_ref[...] += weighted_values(p0_ref, qi - 1)

    @pl.when(qi % 2 == 0)
    def _():
        acc_ref[...] += weighted_values(p1_ref, jnp.maximum(qi - 1, 0))

    lam = _lambda_from(dl_ref, lam_init)
    o_ref[...] = _finish_heads(acc_ref[...], 1.0 / l_ref[...], lam, g_ref[...], tq, lam_init)


def _attn_prompt(slab, kb, vb, g_norm, dl, batch, t_len, lam_init):
    tq = min(512, t_len)
    tk = tq
    assert tq % CHUNK == 0 and CHUNK == 64 and tk <= 512
    nq = t_len // tq
    k3 = kb.reshape(batch, t_len, KB_W)
    v3 = vb.reshape(batch, t_len, VB_W)
    kern = functools.partial(_attn_prompt_kernel, tq=tq, tk=tk, lam_init=lam_init)
    hw = 2 * D_B
    return pl.pallas_call(
        kern,
        grid=(batch, H_B, nq),
        in_specs=[
            pl.BlockSpec((tq, hw), lambda b, h, q: (b * nq + q, SLAB_QB // hw + h)),
            pl.BlockSpec((1, t_len, hw), lambda b, h, q: (b, 0, h)),
            pl.BlockSpec((1, t_len, hw), lambda b, h, q: (b, 0, h)),
            pl.BlockSpec((1, hw), lambda b, h, q: (0, 0)),
            pl.BlockSpec((4, D_B), lambda b, h, q: (0, 0)),
        ],
        out_specs=pl.BlockSpec((tq, hw), lambda b, h, q: (b * nq + q, h)),
        out_shape=jax.ShapeDtypeStruct((batch * t_len, VB_W), BF16),
        scratch_shapes=[
            pltpu.VMEM((t_len, 2 * hw), BF16),
            pltpu.VMEM((t_len, hw), BF16),
            pltpu.VMEM((2 * tq, hw), BF16),
            pltpu.VMEM((2 * tq, tk), F32),
            pltpu.VMEM((2 * tq, hw), F32),
            pltpu.VMEM((2 * tq, hw), F32),
            pltpu.VMEM((2 * tq, hw), F32),
            pltpu.VMEM((2 * tq, tk), F32),
            pltpu.VMEM((2 * tq, tk), F32),
            pltpu.VMEM((2 * tq, tk), BF16),
            pltpu.VMEM((2 * tq, tk), BF16),
        ],
        compiler_params=_cparams(("parallel", "parallel", "arbitrary")),
        name="attn_prompt",
    )(slab, k3, v3, g_norm, dl)


def _attn_sample_kernel(q_ref, kc_ref, vc_ref, kn_ref, vn_ref, g_ref, dl_ref, o_ref, *, tq, past, lam_init):
    h = pl.program_id(1)
    c, _, _ = _alibi_coef(h)
    qq = _split_maps(q_ref[...])
    q_pos = past + lax.broadcasted_iota(I32, (2 * tq, 1), 0) % tq

    def scores(k, k_pos):
        s = lax.dot_general(qq, k.astype(BF16), (((1,), (1,)), ((), ())), preferred_element_type=F32)
        s = s - c * jnp.abs(q_pos - k_pos).astype(F32)
        return jnp.where((k_pos // CHUNK) <= (q_pos // CHUNK), s, NEG_INF)

    sc = scores(kc_ref[0], lax.broadcasted_iota(I32, (1, past), 1))
    sn = scores(kn_ref[0], past + lax.broadcasted_iota(I32, (1, tq), 1))
    m = jnp.maximum(jnp.max(sc, axis=-1, keepdims=True), jnp.max(sn, axis=-1, keepdims=True))
    pc = jnp.exp2(sc - m)
    pn = jnp.exp2(sn - m)
    l = jnp.sum(pc, axis=-1, keepdims=True) + jnp.sum(pn, axis=-1, keepdims=True)
    acc = (jnp.dot(pc.astype(BF16), vc_ref[0].astype(BF16), preferred_element_type=F32)
           + jnp.dot(pn.astype(BF16), vn_ref[0].astype(BF16), preferred_element_type=F32))
    lam = _lambda_from(dl_ref, lam_init)
    o_ref[...] = _finish_heads(acc, 1.0 / l, lam, g_ref[...], tq, lam_init)


def _attn_sample(slab, kb, vb, cache_k, cache_v, g_norm, dl, batch, t_len, lam_init):
    past = cache_k.shape[1]
    hw = 2 * D_B
    k3 = kb.reshape(batch, t_len, KB_W)
    v3 = vb.reshape(batch, t_len, VB_W)
    kern = functools.partial(_attn_sample_kernel, tq=t_len, past=past, lam_init=lam_init)
    return pl.pallas_call(
        kern,
        grid=(batch, H_B),
        in_specs=[
            pl.BlockSpec((t_len, hw), lambda b, h: (b, SLAB_QB // hw + h)),
            pl.BlockSpec((1, past, hw), lambda b, h: (b, 0, h)),
            pl.BlockSpec((1, past, hw), lambda b, h: (b, 0, h)),
            pl.BlockSpec((1, t_len, hw), lambda b, h: (b, 0, h)),
            pl.BlockSpec((1, t_len, hw), lambda b, h: (b, 0, h)),
            pl.BlockSpec((1, hw), lambda b, h: (0, 0)),
            pl.BlockSpec((4, D_B), lambda b, h: (0, 0)),
        ],
        out_specs=pl.BlockSpec((t_len, hw), lambda b, h: (b, h)),
        out_shape=jax.ShapeDtypeStruct((batch * t_len, VB_W), BF16),
        compiler_params=_cparams(("parallel", "parallel")),
        name="attn_sample",
    )(slab, cache_k, cache_v, k3, v3, g_norm, dl)


def _finish_kernel(x_ref, oa_ref, ra_ref, ob_ref, gt_ref, wpa_ref, wpb_ref, wo_ref, g_ref, b_ref,
                   wrh_ref, wrl_ref, br_ref, h_ref, hp_ref, idx_ref, tw_ref, cnt_ref):
    i = pl.program_id(0)
    ra = ra_ref[...].astype(F32)
    ua = ra * _sigmoid(ra) * oa_ref[...].astype(F32)
    ya = jnp.dot(ua.astype(BF16), wpa_ref[...], preferred_element_type=F32)
    yb = jnp.dot(ob_ref[...], wpb_ref[...], preferred_element_type=F32)
    gt = gt_ref[...].astype(F32)
    mixed = _sigmoid(gt[:, :D_MODEL]) * ya + _sigmoid(gt[:, D_MODEL:]) * yb
    mix = jnp.dot(mixed.astype(BF16), wo_ref[...], preferred_element_type=F32)
    hh = _layer_norm(DEEPNORM_ALPHA * x_ref[...] + mix, g_ref[...], b_ref[...])
    h_ref[...] = hh
    hp_ref[...] = _pack_bf16_pairs(hh)

    h_hi = hh.astype(BF16)
    h_lo = (hh - h_hi.astype(F32)).astype(BF16)
    lg = (jnp.dot(h_hi, wrh_ref[...], preferred_element_type=F32)
          + jnp.dot(h_lo, wrh_ref[...], preferred_element_type=F32)
          + jnp.dot(h_hi, wrl_ref[...], preferred_element_type=F32)) + br_ref[...]
    tm = lg.shape[0]
    lane = lax.broadcasted_iota(I32, (tm, LANES), 1)
    lane_f = lane.astype(F32)
    work = lg
    vals, idxs = [], []
    cnt = jnp.zeros((tm, LANES), F32)
    for _ in range(TOP_K):
        mx = jnp.max(work, axis=-1, keepdims=True)
        ix = jnp.min(jnp.where(work == mx, lane_f, float(LANES)), axis=-1, keepdims=True)
        hit = lane_f == ix
        cnt = cnt + hit.astype(F32)
        work = jnp.where(hit, NEG_INF, work)
        vals.append(mx)
        idxs.append(ix)
    es = [jnp.exp(v - vals[0]) for v in vals]
    den = es[0] + es[1] + es[2] + es[3]
    idx_full = jnp.zeros((tm, LANES), F32)
    tw_full = jnp.zeros((tm, LANES), F32)
    for r in range(TOP_K):
        idx_full = jnp.where(lane == r, idxs[r], idx_full)
        tw_full = jnp.where(lane == r, es[r] / den, tw_full)
    idx_ref[...] = idx_full.T[:SUBLANES].astype(I32)
    tw_ref[...] = tw_full[:, :TOP_K]

    @pl.when(i == 0)
    def _():
        cnt_ref[...] = jnp.zeros(cnt_ref.shape, F32)

    cnt_ref[...] += jnp.sum(cnt, axis=0, keepdims=True)


def _finish(x2, oa, slab, ob, p):
    n = x2.shape[0]
    tm = min(512, n)
    const = lambda i: (0, 0)
    return pl.pallas_call(
        _finish_kernel,
        grid=(n // tm,),
        in_specs=[
            pl.BlockSpec((tm, D_MODEL), lambda i: (i, 0)),
            pl.BlockSpec((tm, VA_W), lambda i: (i, 0)),
            pl.BlockSpec((tm, RA_W), lambda i: (i, SLAB_RA // RA_W)),
            pl.BlockSpec((tm, VB_W), lambda i: (i, 0)),
            pl.BlockSpec((tm, GT_W), lambda i: (i, SLAB_GT // GT_W)),
            pl.BlockSpec((VA_W, D_MODEL), const),
            pl.BlockSpec((VB_W, D_MODEL), const),
            pl.BlockSpec((D_MODEL, D_MODEL), const),
            pl.BlockSpec((1, D_MODEL), const),
            pl.BlockSpec((1, D_MODEL), const),
            pl.BlockSpec((D_MODEL, LANES), const),
            pl.BlockSpec((D_MODEL, LANES), const),
            pl.BlockSpec((1, LANES), const),
        ],
        out_specs=[
            pl.BlockSpec((tm, D_MODEL), lambda i: (i, 0)),
            pl.BlockSpec((tm, D_MODEL // 2), lambda i: (i, 0)),
            pl.BlockSpec((SUBLANES, tm), lambda i: (0, i)),
            pl.BlockSpec((tm, TOP_K), lambda i: (i, 0)),
            pl.BlockSpec((1, LANES), const),
        ],
        out_shape=[
            jax.ShapeDtypeStruct((n, D_MODEL), F32),
            jax.ShapeDtypeStruct((n, D_MODEL // 2), U32),
            jax.ShapeDtypeStruct((SUBLANES, n), I32),
            jax.ShapeDtypeStruct((n, TOP_K), F32),
            jax.ShapeDtypeStruct((1, LANES), F32),
        ],
        compiler_params=_cparams(("arbitrary",)),
        name="finish",
    )(x2, oa, slab, ob, slab, p["w_pa"], p["w_pb"], p["w_o"], p["ln1_g"], p["ln1_b"],
      p["w_r_hi"], p["w_r_lo"], p["b_r"])


def _dest_kernel(idx_ref, ps_ref, ut_ref, dest_ref, carry_ref):
    i = pl.program_id(0)

    @pl.when(i == 0)
    def _():
        carry_ref[...] = jnp.zeros(carry_ref.shape, F32)

    tm = idx_ref.shape[1]
    expert = lax.broadcasted_iota(I32, (N_EXPERTS, tm), 0)
    idx = idx_ref[...]
    hits = [expert == idx[k:k + 1, :] for k in range(TOP_K)]
    cnt = jnp.zeros((N_EXPERTS, tm), F32)
    for hit in hits:
        cnt = cnt + hit.astype(F32)
    before = jnp.dot(cnt.astype(BF16), ut_ref[...], preferred_element_type=F32)
    base = before + carry_ref[:, 0:1] + ps_ref[:, 0:1]
    row = lax.broadcasted_iota(I32, (SUBLANES, tm), 0)
    dest = jnp.zeros((SUBLANES, tm), F32)
    for k in range(TOP_K):
        d = jnp.sum(jnp.where(hits[k], base, 0.0), axis=0, keepdims=True)
        dest = jnp.where(row == k, d, dest)
    dest_ref[...] = dest.astype(I32)
    carry_ref[...] += jnp.sum(cnt, axis=1, keepdims=True)


def _dest(idx_t, pstart):
    n = idx_t.shape[1]
    tm = min(512, n)
    utri = jnp.asarray(np.triu(np.ones((tm, tm), np.float32), 1), BF16)
    return pl.pallas_call(
        _dest_kernel,
        grid=(n // tm,),
        in_specs=[
            pl.BlockSpec((SUBLANES, tm), lambda i: (0, i)),
            pl.BlockSpec((N_EXPERTS, LANES), lambda i: (0, 0)),
            pl.BlockSpec((tm, tm), lambda i: (0, 0)),
        ],
        out_specs=pl.BlockSpec((SUBLANES, tm), lambda i: (0, i)),
        out_shape=jax.ShapeDtypeStruct((SUBLANES, n), I32),
        scratch_shapes=[pltpu.VMEM((N_EXPERTS, LANES), F32)],
        compiler_params=_cparams(("arbitrary",)),
        name="dest",
    )(idx_t, pstart, utri)


def _scatter_kernel(dest_ref, hp_ref, rows_in_ref, rows_ref, sem):
    del rows_in_ref
    tm = hp_ref.shape[0]

    def row_copy(t, d):
        return pltpu.make_async_copy(hp_ref.at[pl.ds(t, 1)], rows_ref.at[pl.ds(d, 1)], sem)

    def issue(t, carry):
        for k in range(TOP_K):
            row_copy(t, dest_ref[k, t]).start(priority=k % 2)
        return carry

    lax.fori_loop(0, tm, issue, 0, unroll=4)
    for k in range(TOP_K):
        pltpu.make_async_copy(hp_ref, rows_ref.at[pl.ds(0, tm)], sem).wait()


def _scatter(dest_t, hp, n_rows):
    n = hp.shape[0]
    tm = min(512, n)
    rows0 = jnp.zeros((n_rows, D_MODEL // 2), U32)
    return pl.pallas_call(
        _scatter_kernel,
        grid=(n // tm,),
        in_specs=[
            pl.BlockSpec((SUBLANES, tm), lambda i: (0, i), memory_space=pltpu.SMEM),
            pl.BlockSpec((tm, D_MODEL // 2), lambda i: (i, 0)),
            pl.BlockSpec(memory_space=pl.ANY),
        ],
        out_specs=pl.BlockSpec(memory_space=pl.ANY),
        out_shape=jax.ShapeDtypeStruct((n_rows, D_MODEL // 2), U32),
        scratch_shapes=[pltpu.SemaphoreType.DMA(())],
        input_output_aliases={2: 0},
        compiler_params=_cparams(("arbitrary",)),
        name="scatter",
    )(dest_t, hp, rows0)


def _expert_kernel(be_ref, nu_ref, rows_ref, wgu_ref, bgu_ref, wd_ref, bd_ref, out_ref):
    del be_ref
    i = pl.program_id(0)

    @pl.when(i < nu_ref[0])
    def _():
        x = _unpack_bf16_pairs(rows_ref[...]).astype(BF16)
        gu = jnp.dot(x, wgu_ref[0], preferred_element_type=F32) + bgu_ref[0]
        gate = jnp.minimum(gu[:, :D_FF], SWIGLU_LIMIT)
        up = jnp.clip(gu[:, D_FF:], -SWIGLU_LIMIT, SWIGLU_LIMIT)
        hidden = (up + 1.0) * (gate * _sigmoid(SWIGLU_ALPHA * gate))
        o = jnp.dot(hidden.astype(BF16), wd_ref[0], preferred_element_type=F32) + bd_ref[0]
        out_ref[...] = _pack_bf16_pairs(o)

    @pl.when(i >= nu_ref[0])
    def _():
        out_ref[...] = jnp.zeros(out_ref.shape, U32)


def _experts(rows, block_e, n_used, p):
    n_rows = rows.shape[0]
    bm = EXPERT_BM
    n_blocks = n_rows // bm
    grid_spec = pltpu.PrefetchScalarGridSpec(
        num_scalar_prefetch=2,
        grid=(n_blocks,),
        in_specs=[
            pl.BlockSpec((bm, D_MODEL // 2), lambda i, be, nu: (i, 0)),
            pl.BlockSpec((1, D_MODEL, 2 * D_FF), lambda i, be, nu: (be[i], 0, 0)),
            pl.BlockSpec((1, 1, 2 * D_FF), lambda i, be, nu: (be[i], 0, 0)),
            pl.BlockSpec((1, D_FF, D_MODEL), lambda i, be, nu: (be[i], 0, 0)),
            pl.BlockSpec((1, 1, D_MODEL), lambda i, be, nu: (be[i], 0, 0)),
        ],
        out_specs=pl.BlockSpec((bm, D_MODEL // 2), lambda i, be, nu: (i, 0)),
    )
    return pl.pallas_call(
        _expert_kernel,
        grid_spec=grid_spec,
        out_shape=jax.ShapeDtypeStruct((n_rows, D_MODEL // 2), U32),
        compiler_params=_cparams(("arbitrary",)),
        name="experts",
    )(block_e, n_used, rows, p["w_gu"], p["b_gu"], p["w_down"], p["b_down"])


def _combine_kernel(dest_ref, tw_ref, h_ref, rows_ref, g_ref, b_ref, y_ref, buf_ref, sem):
    tm = h_ref.shape[0]

    def row_copy(t, k, d):
        return pltpu.make_async_copy(rows_ref.at[pl.ds(d, 1)], buf_ref.at[k, pl.ds(t, 1)], sem)

    def issue(t, carry):
        for k in range(TOP_K):
            row_copy(t, k, dest_ref[k, t]).start(priority=k % 2)
        return carry

    lax.fori_loop(0, tm, issue, 0, unroll=4)
    for k in range(TOP_K):
        pltpu.make_async_copy(rows_ref.at[pl.ds(0, tm)], buf_ref.at[k], sem).wait()
    tw = tw_ref[...]
    f = jnp.zeros((tm, D_MODEL), F32)
    for k in range(TOP_K):
        f = f + tw[:, k:k + 1] * _unpack_bf16_pairs(buf_ref[k])
    y_ref[...] = _layer_norm(DEEPNORM_ALPHA * h_ref[...] + f, g_ref[...], b_ref[...])


def _combine(dest_t, top_w, hh, out_rows, p):
    n = hh.shape[0]
    tm = min(512, n)
    const = lambda i: (0, 0)
    return pl.pallas_call(
        _combine_kernel,
        grid=(n // tm,),
        in_specs=[
            pl.BlockSpec((SUBLANES, tm), lambda i: (0, i), memory_space=pltpu.SMEM),
            pl.BlockSpec((tm, TOP_K), lambda i: (i, 0)),
            pl.BlockSpec((tm, D_MODEL), lambda i: (i, 0)),
            pl.BlockSpec(memory_space=pl.ANY),
            pl.BlockSpec((1, D_MODEL), const),
            pl.BlockSpec((1, D_MODEL), const),
        ],
        out_specs=pl.BlockSpec((tm, D_MODEL), lambda i: (i, 0)),
        out_shape=jax.ShapeDtypeStruct((n, D_MODEL), F32),
        scratch_shapes=[pltpu.VMEM((TOP_K, tm, D_MODEL // 2), U32), pltpu.SemaphoreType.DMA(())],
        compiler_params=_cparams(("arbitrary",)),
        name="combine",
    )(dest_t, top_w, hh, out_rows, p["ln2_g"], p["ln2_b"])


def _moe(hh, hp, top_idx, top_w, counts, p):
    n = hh.shape[0]
    bm = EXPERT_BM
    n_rows = n * TOP_K + N_EXPERTS * bm
    cnt = counts[0, :N_EXPERTS].astype(I32)
    padded = (cnt + bm - 1) // bm * bm
    pend = jnp.cumsum(padded)
    pstart = pend - padded
    ps_col = jnp.broadcast_to(pstart.astype(F32)[:, None], (N_EXPERTS, LANES))
    block_row0 = jnp.arange(n_rows // bm, dtype=I32) * bm
    block_e = jnp.minimum(jnp.sum((pend[None, :] <= block_row0[:, None]).astype(I32), axis=1), N_EXPERTS - 1)
    n_used = (pend[-1:] // bm).astype(I32)
    dest = _dest(top_idx, ps_col)
    rows = _scatter(dest, hp, n_rows)
    out_rows = _experts(rows, block_e, n_used, p)
    return _combine(dest, top_w, hh, out_rows, p)


def _prep_params(w_in, b_in, w_alpha, b_alpha, gla_norm_g, diff_norm_g, w_pa, w_pb, w_o, ln1_g, ln1_b,
                 w_router, b_router, w_gu, b_gu, w_down, b_down, ln2_g, ln2_b):
    o = IN_OFFS
    seg = lambda a, i: a[..., o[i]:o[i + 1]]
    order = (0, 1, 2, 3, 5, 8)
    w_slab = jnp.concatenate([seg(w_in, i) for i in order], axis=1).astype(BF16)
    b_slab = jnp.concatenate([seg(b_in, i) for i in order])[None, :]
    scale = np.ones((1, SLAB_W), np.float32)
    scale[0, SLAB_QA:SLAB_QA + QA_W] = DK_A ** -0.5
    scale[0, SLAB_QB:SLAB_QB + QB_W] = D_B ** -0.5 * LOG2E
    w_lr = jnp.zeros((D_MODEL, LANES), F32).at[:, :LR_W].set(seg(w_in, 4)).astype(BF16)
    b_lr = jnp.zeros((1, LANES), F32).at[0, :LR_W].set(seg(b_in, 4))
    w_al = jnp.zeros((LANES, QA_W), F32).at[:LR_W].set(w_alpha).astype(BF16)
    w_r = jnp.zeros((D_MODEL, LANES), F32).at[:, :N_EXPERTS].set(w_router)
    w_r_hi = w_r.astype(BF16)
    w_r_lo = (w_r - w_r_hi.astype(F32)).astype(BF16)
    b_r = jnp.full((1, LANES), NEG_INF, F32).at[0, :N_EXPERTS].set(b_router)
    return dict(
        w_slab=w_slab, b_slab=b_slab, s_slab=jnp.asarray(scale),
        w_lr=w_lr, b_lr=b_lr, w_alpha=w_al, b_alpha=b_alpha[None, :],
        w_k=seg(w_in, 6).astype(BF16), w_v=seg(w_in, 7).astype(BF16),
        b_k=seg(b_in, 6)[None, :], b_v=seg(b_in, 7)[None, :],
        gla_g=gla_norm_g[None, :], diff_g=diff_norm_g[None, :],
        w_pa=w_pa.astype(BF16), w_pb=w_pb.astype(BF16), w_o=w_o.astype(BF16),
        ln1_g=ln1_g[None, :], ln1_b=ln1_b[None, :],
        w_r_hi=w_r_hi, w_r_lo=w_r_lo, b_r=b_r,
        w_gu=w_gu.astype(BF16), b_gu=b_gu[:, None, :], w_down=w_down.astype(BF16), b_down=b_down[:, None, :],
        ln2_g=ln2_g[None, :], ln2_b=ln2_b[None, :],
    )


def _layer(x, p, dl, lam_init, s0, cache_k, cache_v):
    batch, t_len, _ = x.shape
    x2 = x.reshape(batch * t_len, D_MODEL)
    slab, loga = _proj_slab(x2, p)
    k_rows, kb, vb = _proj_kv(x2, p)
    oa, s_new = _gla(slab, loga, s0, p["gla_g"], batch, t_len)
    if cache_k is None:
        ob = _attn_prompt(slab, kb, vb, p["diff_g"], dl, batch, t_len, lam_init)
    else:
        ob = _attn_sample(slab, kb, vb, cache_k, cache_v, p["diff_g"], dl, batch, t_len, lam_init)
    hh, hp, top_idx, top_w, counts = _finish(x2, oa, slab, ob, p)
    y = _moe(hh, hp, top_idx, top_w, counts, p)
    return (y.reshape(batch, t_len, D_MODEL), s_new,
            k_rows.reshape(batch, t_len, H_B, 2, D_B), vb.reshape(batch, t_len, H_B, 2 * D_B))


def kernel(x_prompt, x_sample, cache_k, cache_v, state_gla, w_in, b_in, w_alpha, b_alpha, gla_norm_g,
           diff_lambda, diff_norm_g, w_pa, w_pb, w_o, ln1_g, ln1_b, w_router, b_router, w_gu, b_gu,
           w_down, b_down, ln2_g, ln2_b):
    assert w_in.shape[0] == DEPTH == 1
    l = 0
    lam_init = 0.8 - 0.6 * math.exp(-0.3 * l)
    p = _prep_params(w_in[l], b_in[l], w_alpha[l], b_alpha[l], gla_norm_g[l], diff_norm_g[l], w_pa[l],
                     w_pb[l], w_o[l], ln1_g[l], ln1_b[l], w_router[l], b_router[l], w_gu[l], b_gu[l],
                     w_down[l], b_down[l], ln2_g[l], ln2_b[l])
    dl = diff_lambda[l]
    bp = x_prompt.shape[0]
    bs, ts = x_sample.shape[0], x_sample.shape[1]
    past = cache_k.shape[2]
    yp, s_p, k_p, v_p = _layer(x_prompt, p, dl, lam_init, jnp.zeros((bp, H_A, DK_A, DV_A), F32), None, None)
    ck = cache_k[l].reshape(bs, past, KB_W)
    cv = cache_v[l].reshape(bs, past, VB_W)
    ys, s_s, k_s, v_s = _layer(x_sample, p, dl, lam_init, state_gla[l], ck, cv)
    return (yp, ys, s_p[None], k_p[None], v_p[None], s_s[None], k_s[None], v_s[None])
```

```python
import functools
import math

import numpy as np
import jax
import jax.numpy as jnp
from jax import lax
from jax.experimental import pallas as pl
from jax.experimental.pallas import tpu as pltpu

F32 = jnp.float32
BF16 = jnp.bfloat16
U32 = jnp.uint32
I32 = jnp.int32

D_MODEL = 1024
CHUNK = 64
H_A = 4
DK_A = 128
DV_A = 256
GATE_RANK = 16
GATE_TAU = 16.0
H_B = 8
D_B = 64
N_EXPERTS = 32
TOP_K = 4
D_FF = D_MODEL
SWIGLU_LIMIT = 7.0
SWIGLU_ALPHA = 1.702
EPS = 1e-5
DEPTH = 1
DEEPNORM_ALPHA = (2.0 * DEPTH) ** 0.25

QA_W = H_A * DK_A
KA_W = H_A * DK_A
VA_W = H_A * DV_A
RA_W = H_A * DV_A
LR_W = GATE_RANK
QB_W = H_B * 2 * D_B
KB_W = H_B * 2 * D_B
VB_W = H_B * 2 * D_B
GT_W = 2 * D_MODEL
IN_SIZES = (QA_W, KA_W, VA_W, RA_W, LR_W, QB_W, KB_W, VB_W, GT_W)
IN_OFFS = tuple(int(v) for v in np.cumsum((0,) + IN_SIZES))

SLAB_QA, SLAB_KA, SLAB_VA, SLAB_RA, SLAB_QB, SLAB_GT = 0, 512, 1024, 2048, 3072, 4096
SLAB_W = 6144
LANES = 128
SUBLANES = 8
EXPERT_BM = 512
VMEM_LIMIT = 56 * 1024 * 1024

NEG_INF = float("-inf")
LOG2E = math.log2(math.e)


def _cparams(sem):
    return pltpu.CompilerParams(dimension_semantics=sem, vmem_limit_bytes=VMEM_LIMIT)


def _sigmoid(x):
    return 1.0 / (1.0 + jnp.exp(-x))


def _pack_bf16_pairs(x):
    n = x.shape[1] // 2
    xb = x.astype(BF16).astype(F32)
    lo = pltpu.bitcast(xb[:, :n], U32) >> 16
    hi = pltpu.bitcast(xb[:, n:], U32) & jnp.uint32(0xFFFF0000)
    return hi | lo


def _unpack_bf16_pairs(r):
    lo = pltpu.bitcast(r << 16, F32)
    hi = pltpu.bitcast(r & jnp.uint32(0xFFFF0000), F32)
    return jnp.concatenate([lo, hi], axis=1)


def _layer_norm(y, g, b):
    mu = jnp.mean(y, axis=-1, keepdims=True)
    d = y - mu
    var = jnp.mean(d * d, axis=-1, keepdims=True)
    return d * lax.rsqrt(var + EPS) * g + b


def _proj_kernel(x_ref, w_ref, b_ref, s_ref, wlr_ref, blr_ref, wal_ref, bal_ref, wk_ref, wv_ref, bk_ref, bv_ref,
                 slab_ref, loga_ref, krows_ref, kb_ref, v_ref, *, tn):
    xb = x_ref[...].astype(BF16)
    lra = jnp.dot(xb, wlr_ref[...], preferred_element_type=F32) + blr_ref[...]
    z = jnp.dot(lra.astype(BF16), wal_ref[...], preferred_element_type=F32) + bal_ref[...]
    loga_ref[...] = (jnp.minimum(z, 0.0) - jnp.log(1.0 + jnp.exp(-jnp.abs(z)))) * (1.0 / GATE_TAU)
    for c0 in range(0, SLAB_W, tn):
        acc = jnp.dot(xb, w_ref[:, c0:c0 + tn], preferred_element_type=F32)
        slab_ref[:, c0:c0 + tn] = ((acc + b_ref[:, c0:c0 + tn]) * s_ref[:, c0:c0 + tn]).astype(BF16)
    k = jnp.dot(xb, wk_ref[...], preferred_element_type=F32) + bk_ref[...]
    kb_ref[...] = k.astype(BF16)
    for h in range(H_B):
        for j in range(2):
            c0 = (2 * h + j) * D_B
            krows_ref[:, h, j, :] = k[:, c0:c0 + D_B]
    v_ref[...] = jnp.dot(xb, wv_ref[...], preferred_element_type=F32) + bv_ref[...]


def _proj(x2, p):
    n = x2.shape[0]
    tm = min(512, n)
    const = lambda i: (0, 0)
    whole = lambda shape: pl.BlockSpec(shape, const, pipeline_mode=pl.Buffered(1))
    row = lambda w: pl.BlockSpec((tm, w), lambda i: (i, 0))
    return pl.pallas_call(
        functools.partial(_proj_kernel, tn=1024),
        grid=(n // tm,),
        in_specs=[
            row(D_MODEL),
            whole((D_MODEL, SLAB_W)), whole((1, SLAB_W)), whole((1, SLAB_W)),
            whole((D_MODEL, LANES)), whole((1, LANES)), whole((LANES, QA_W)), whole((1, QA_W)),
            whole((D_MODEL, KB_W)), whole((D_MODEL, VB_W)), whole((1, KB_W)), whole((1, VB_W)),
        ],
        out_specs=[
            row(SLAB_W), row(QA_W),
            pl.BlockSpec((tm, H_B, 2, D_B), lambda i: (i, 0, 0, 0)),
            row(KB_W), row(VB_W),
        ],
        out_shape=[
            jax.ShapeDtypeStruct((n, SLAB_W), BF16),
            jax.ShapeDtypeStruct((n, QA_W), F32),
            jax.ShapeDtypeStruct((n, H_B, 2, D_B), F32),
            jax.ShapeDtypeStruct((n, KB_W), BF16),
            jax.ShapeDtypeStruct((n, VB_W), F32),
        ],
        compiler_params=_cparams(("parallel",)),
        name="proj",
    )(x2, p["w_slab"], p["b_slab"], p["s_slab"], p["w_lr"], p["b_lr"], p["w_alpha"], p["b_alpha"],
      p["w_k"], p["w_v"], p["b_k"], p["b_v"])


def _gla_tables(L):
    nl = int(math.log2(L))
    t = np.arange(L)
    D = np.zeros(((nl + 2) * L, L), np.float32)
    masks = np.zeros((nl + 1, L, L), np.float32)
    for l in range(nl):
        m = L >> (l + 1)
        grp = t // (2 * m)
        mid = grp * 2 * m + m - 1
        upper = (t % (2 * m)) >= m
        for r in range(L):
            if upper[r]:
                D[l * L + r, mid[r] + 1:r + 1] = 1.0
            else:
                D[l * L + r, r + 1:mid[r] + 1] = 1.0
        masks[l] = (upper[:, None] & ~upper[None, :] & (grp[:, None] == grp[None, :])).astype(np.float32)
    D[nl * L:(nl + 1) * L] = np.tril(np.ones((L, L), np.float32))
    D[(nl + 1) * L:] = np.triu(np.ones((L, L), np.float32), 1)
    masks[nl] = np.eye(L, dtype=np.float32)
    return jnp.asarray(D, BF16), jnp.asarray(masks, F32)


def _gla_kernel(q_ref, k_ref, v_ref, la_ref, s0_ref, g_ref, d_ref, m_ref, o_ref, s_ref, *, L, n_chunks):
    c = pl.program_id(1)
    nl = int(math.log2(L))

    @pl.when(c == 0)
    def _():
        s_ref[...] = s0_ref[...]

    dmat = d_ref[...]
    ones_col = jnp.ones((L, LANES), BF16)
    g = g_ref[...]
    nt = (((1,), (1,)), ((), ()))
    tn = (((0,), (0,)), ((), ()))

    def chunk(ci, carry):
        r0 = pl.multiple_of(ci * L, L)
        rows = pl.ds(r0, L)
        for h in range(H_A):
            kc = slice(h * DK_A, (h + 1) * DK_A)
            vc = slice(h * DV_A, (h + 1) * DV_A)
            q = q_ref[rows, kc].astype(F32)
            k = k_ref[rows, kc].astype(F32)
            v = v_ref[rows, vc]
            la = la_ref[rows, kc]
            la_hi = la.astype(BF16)
            la_lo = (la - la_hi.astype(F32)).astype(BF16)
            e = (jnp.dot(dmat, la_hi, preferred_element_type=F32)
                 + jnp.dot(dmat, la_lo, preferred_element_type=F32))
            x = jnp.exp(e)
            a = m_ref[nl] * lax.dot_general(q.astype(BF16), k.astype(BF16), nt, preferred_element_type=F32)
            for l in range(nl):
                xl = x[l * L:(l + 1) * L]
                a = a + m_ref[l] * lax.dot_general((q * xl).astype(BF16), (k * xl).astype(BF16), nt,
                                                   preferred_element_type=F32)
            xb = x[nl * L:(nl + 1) * L]
            xs = x[(nl + 1) * L:]
            s_old = s_ref[0, h]
            o = (jnp.dot((q * xb).astype(BF16), s_old.astype(BF16), preferred_element_type=F32)
                 + jnp.dot(a.astype(BF16), v, preferred_element_type=F32))
            bl = (lax.dot_general(la_hi, ones_col, tn, preferred_element_type=F32)
                  + lax.dot_general(la_lo, ones_col, tn, preferred_element_type=F32))
            dec = jnp.exp(bl[:, 0:1])
            s_ref[0, h] = dec * s_old + lax.dot_general((k * xs).astype(BF16), v, tn, preferred_element_type=F32)
            ms = jnp.mean(o * o, axis=-1, keepdims=True)
            o_ref[rows, vc] = (o * lax.rsqrt(ms + EPS) * g).astype(BF16)
        return carry

    lax.fori_loop(0, n_chunks, chunk, 0, unroll=2 if n_chunks % 2 == 0 else 1)


def _gla(slab, loga, s0, g_norm, batch, t_len):
    L = min(CHUNK, t_len)
    tb = min(512, t_len)
    nb = t_len // tb
    dmat, masks = _gla_tables(L)
    kern = functools.partial(_gla_kernel, L=L, n_chunks=tb // L)
    row = lambda b, c: b * nb + c
    return pl.pallas_call(
        kern,
        grid=(batch, nb),
        in_specs=[
            pl.BlockSpec((tb, QA_W), lambda b, c: (row(b, c), SLAB_QA // QA_W)),
            pl.BlockSpec((tb, KA_W), lambda b, c: (row(b, c), SLAB_KA // KA_W)),
            pl.BlockSpec((tb, VA_W), lambda b, c: (row(b, c), SLAB_VA // VA_W)),
            pl.BlockSpec((tb, QA_W), lambda b, c: (row(b, c), 0)),
            pl.BlockSpec((1, H_A, DK_A, DV_A), lambda b, c: (b, 0, 0, 0)),
            pl.BlockSpec((1, DV_A), lambda b, c: (0, 0)),
            pl.BlockSpec(dmat.shape, lambda b, c: (0, 0)),
            pl.BlockSpec(masks.shape, lambda b, c: (0, 0, 0)),
        ],
        out_specs=[
            pl.BlockSpec((tb, VA_W), lambda b, c: (row(b, c), 0)),
            pl.BlockSpec((1, H_A, DK_A, DV_A), lambda b, c: (b, 0, 0, 0)),
        ],
        out_shape=[
            jax.ShapeDtypeStruct((batch * t_len, VA_W), BF16),
            jax.ShapeDtypeStruct((batch, H_A, DK_A, DV_A), F32),
        ],
        compiler_params=_cparams(("parallel", "arbitrary")),
        name="gla",
    )(slab, slab, slab, loga, s0, g_norm, dmat, masks)


def _lambda_from(dl_ref, lam_init):
    dl = dl_ref[...]
    a = jnp.sum(dl[0:1] * dl[1:2], axis=-1, keepdims=True)
    b = jnp.sum(dl[2:3] * dl[3:4], axis=-1, keepdims=True)
    return jnp.exp(a) - jnp.exp(b) + lam_init


def _split_maps(q):
    lane = lax.broadcasted_iota(I32, q.shape, 1)
    zero = jnp.zeros_like(q)
    return jnp.concatenate([jnp.where(lane < D_B, q, zero), jnp.where(lane >= D_B, q, zero)], axis=0)


def _alibi_coef(h):
    c = jnp.full((1, 1), LOG2E, F32) * jnp.exp2(-(h + 1).astype(F32))
    c_hi = c.astype(BF16).astype(F32)
    return c, c_hi, c - c_hi


def _finish_heads(acc, inv_l, lam, g, tq, lam_init):
    o = acc * inv_l
    out = o[:tq] - lam * o[tq:]
    ms = jnp.mean(out * out, axis=-1, keepdims=True)
    return (out * lax.rsqrt(ms + EPS) * g * (1.0 - lam_init)).astype(BF16)


def _attn_prompt_kernel(q_ref, k_ref, v_ref, g_ref, dl_ref, o_ref, ka_ref, vb_ref, qaug_ref, d_ref, m_ref, l_ref,
                        acc_ref, s0_ref, s1_ref, p0_ref, p1_ref,
                        *, tq, tk, lam_init):
    h = pl.program_id(1)
    qi = pl.program_id(2)
    c, c_hi, c_lo = _alibi_coef(h)
    hw = 2 * D_B

    def aug_lanes(shape, pos, sign):
        lane = lax.broadcasted_iota(I32, shape, 1)
        r = (pos & 255).astype(F32) * sign
        a = (pos >> 8).astype(F32) * sign
        coef = jnp.where(lane == 0, c_hi, jnp.where(lane == 1, c_lo,
                         jnp.where(lane == 2, 256.0 * c_hi, jnp.where(lane == 3, 256.0 * c_lo, 0.0))))
        ints = jnp.where(lane < 2, r, jnp.where(lane < 4, a, 0.0))
        return coef, ints, lane

    @pl.when(qi == 0)
    def _():
        t_len = ka_ref.shape[0]
        j_rel = lax.broadcasted_iota(I32, (t_len, hw), 0) & (tk - 1)
        coef, ints, lane = aug_lanes((t_len, hw), j_rel, 1.0)
        k_aug = jnp.where(lane < 4, ints, pltpu.roll(coef, 4, 1))
        ka_ref[:, :hw] = k_ref[0].astype(BF16)
        ka_ref[:, hw:] = k_aug.astype(BF16)
        vb_ref[...] = v_ref[0].astype(BF16)
        i_rel = lax.broadcasted_iota(I32, (2 * tq, hw), 0) & (tq - 1)
        coef, ints, lane = aug_lanes((2 * tq, hw), i_rel, -1.0)
        qaug_ref[...] = jnp.where(lane < 4, coef, pltpu.roll(ints, 4, 1)).astype(BF16)
        i = lax.broadcasted_iota(I32, (2 * tq, tk), 0) & (tq - 1)
        j = lax.broadcasted_iota(I32, (2 * tq, tk), 1)
        fwd = jnp.maximum(j - i, 0).astype(F32)
        d_ref[...] = jnp.where((j >> 6) <= (i >> 6), -2.0 * c * fwd, NEG_INF)

    qa = jnp.concatenate([_split_maps(q_ref[...]), qaug_ref[...]], axis=1)
    m_ref[...] = jnp.full(m_ref.shape, -1e30, F32)
    l_ref[...] = jnp.zeros(l_ref.shape, F32)

    def scores(kv):
        k0 = pl.multiple_of(kv * tk, tk)
        return lax.dot_general(qa, ka_ref[pl.ds(k0, tk), :], (((1,), (1,)), ((), ())), preferred_element_type=F32)

    def weighted_values(p_ref, kv):
        k0 = pl.multiple_of(kv * tk, tk)
        return jnp.dot(p_ref[...], vb_ref[pl.ds(k0, tk), :], preferred_element_type=F32)

    def softmax_step(s, off):
        m_old = m_ref[...]
        m_new = jnp.maximum(m_old, jnp.max(s, axis=-1, keepdims=True) - off)
        shift = m_new + off
        p = jnp.exp2(s - jnp.concatenate([shift] * (tk // hw), axis=1))
        alpha = jnp.exp2(m_old - m_new)
        l_ref[...] = alpha * l_ref[...] + jnp.sum(p, axis=-1, keepdims=True)
        m_ref[...] = m_new
        return p.astype(BF16), alpha

    p_diag, _ = softmax_step(scores(qi) + d_ref[...], jnp.zeros((1, 1), F32))
    p1_ref[...] = p_diag
    acc_ref[...] = jnp.zeros(acc_ref.shape, F32)

    @pl.when(qi > 0)
    def _():
        s0_ref[...] = scores(0)

    def step(kv, s_cur, s_nxt, p_prev, p_cur):
        pv_prev = weighted_values(p_prev, jnp.where(kv == 0, qi, kv - 1))
        s_nxt[...] = scores(jnp.minimum(kv + 1, qi - 1))
        p, alpha = softmax_step(s_cur[...], c * (qi * tq - kv * tk).astype(F32))
        acc_ref[...] = alpha * (acc_ref[...] + pv_prev)
        p_cur[...] = p

    def pair(j, carry):
        step(2 * j, s0_ref, s1_ref, p1_ref, p0_ref)
        step(2 * j + 1, s1_ref, s0_ref, p0_ref, p1_ref)
        return carry

    lax.fori_loop(0, qi // 2, pair, 0)

    @pl.when(qi % 2 == 1)
    def _():
        step(qi - 1, s0_ref, s1_ref, p1_ref, p0_ref)
        acc_ref[...] += weighted_values(p0_ref, qi - 1)

    @pl.when(qi % 2 == 0)
    def _():
        acc_ref[...] += weighted_values(p1_ref, jnp.maximum(qi - 1, 0))

    lam = _lambda_from(dl_ref, lam_init)
    o_ref[...] = _finish_heads(acc_ref[...], 1.0 / l_ref[...], lam, g_ref[...], tq, lam_init)


def _attn_prompt(slab, kb, vb, g_norm, dl, batch, t_len, lam_init):
    tq = min(512, t_len)
    tk = tq
    assert tq % CHUNK == 0 and CHUNK == 64 and tk <= 512
    nq = t_len // tq
    k3 = kb.reshape(batch, t_len, KB_W)
    v3 = vb.reshape(batch, t_len, VB_W)
    kern = functools.partial(_attn_prompt_kernel, tq=tq, tk=tk, lam_init=lam_init)
    hw = 2 * D_B
    return pl.pallas_call(
        kern,
        grid=(batch, H_B, nq),
        in_specs=[
            pl.BlockSpec((tq, hw), lambda b, h, q: (b * nq + q, SLAB_QB // hw + h)),
            pl.BlockSpec((1, t_len, hw), lambda b, h, q: (b, 0, h)),
            pl.BlockSpec((1, t_len, hw), lambda b, h, q: (b, 0, h)),
            pl.BlockSpec((1, hw), lambda b, h, q: (0, 0)),
            pl.BlockSpec((4, D_B), lambda b, h, q: (0, 0)),
        ],
        out_specs=pl.BlockSpec((tq, hw), lambda b, h, q: (b * nq + q, h)),
        out_shape=jax.ShapeDtypeStruct((batch * t_len, VB_W), BF16),
        scratch_shapes=[
            pltpu.VMEM((t_len, 2 * hw), BF16),
            pltpu.VMEM((t_len, hw), BF16),
            pltpu.VMEM((2 * tq, hw), BF16),
            pltpu.VMEM((2 * tq, tk), F32),
            pltpu.VMEM((2 * tq, hw), F32),
            pltpu.VMEM((2 * tq, hw), F32),
            pltpu.VMEM((2 * tq, hw), F32),
            pltpu.VMEM((2 * tq, tk), F32),
            pltpu.VMEM((2 * tq, tk), F32),
            pltpu.VMEM((2 * tq, tk), BF16),
            pltpu.VMEM((2 * tq, tk), BF16),
        ],
        compiler_params=_cparams(("parallel", "parallel", "arbitrary")),
        name="attn_prompt",
    )(slab, k3, v3, g_norm, dl)


def _attn_sample_kernel(q_ref, kc_ref, vc_ref, kn_ref, vn_ref, g_ref, dl_ref, o_ref, *, tq, past, lam_init):
    h = pl.program_id(1)
    c, _, _ = _alibi_coef(h)
    qq = _split_maps(q_ref[...])
    q_pos = past + lax.broadcasted_iota(I32, (2 * tq, 1), 0) % tq

    def scores(k, k_pos):
        s = lax.dot_general(qq, k.astype(BF16), (((1,), (1,)), ((), ())), preferred_element_type=F32)
        s = s - c * jnp.abs(q_pos - k_pos).astype(F32)
        return jnp.where((k_pos // CHUNK) <= (q_pos // CHUNK), s, NEG_INF)

    sc = scores(kc_ref[0], lax.broadcasted_iota(I32, (1, past), 1))
    sn = scores(kn_ref[0], past + lax.broadcasted_iota(I32, (1, tq), 1))
    m = jnp.maximum(jnp.max(sc, axis=-1, keepdims=True), jnp.max(sn, axis=-1, keepdims=True))
    pc = jnp.exp2(sc - m)
    pn = jnp.exp2(sn - m)
    l = jnp.sum(pc, axis=-1, keepdims=True) + jnp.sum(pn, axis=-1, keepdims=True)
    acc = (jnp.dot(pc.astype(BF16), vc_ref[0].astype(BF16), preferred_element_type=F32)
           + jnp.dot(pn.astype(BF16), vn_ref[0].astype(BF16), preferred_element_type=F32))
    lam = _lambda_from(dl_ref, lam_init)
    o_ref[...] = _finish_heads(acc, 1.0 / l, lam, g_ref[...], tq, lam_init)


def _attn_sample(slab, kb, vb, cache_k, cache_v, g_norm, dl, batch, t_len, lam_init):
    past = cache_k.shape[1]
    hw = 2 * D_B
    k3 = kb.reshape(batch, t_len, KB_W)
    v3 = vb.reshape(batch, t_len, VB_W)
    kern = functools.partial(_attn_sample_kernel, tq=t_len, past=past, lam_init=lam_init)
    return pl.pallas_call(
        kern,
        grid=(batch, H_B),
        in_specs=[
            pl.BlockSpec((t_len, hw), lambda b, h: (b, SLAB_QB // hw + h)),
            pl.BlockSpec((1, past, hw), lambda b, h: (b, 0, h)),
            pl.BlockSpec((1, past, hw), lambda b, h: (b, 0, h)),
            pl.BlockSpec((1, t_len, hw), lambda b, h: (b, 0, h)),
            pl.BlockSpec((1, t_len, hw), lambda b, h: (b, 0, h)),
            pl.BlockSpec((1, hw), lambda b, h: (0, 0)),
            pl.BlockSpec((4, D_B), lambda b, h: (0, 0)),
        ],
        out_specs=pl.BlockSpec((t_len, hw), lambda b, h: (b, h)),
        out_shape=jax.ShapeDtypeStruct((batch * t_len, VB_W), BF16),
        compiler_params=_cparams(("parallel", "parallel")),
        name="attn_sample",
    )(slab, cache_k, cache_v, k3, v3, g_norm, dl)


def _finish_kernel(x_ref, oa_ref, ra_ref, ob_ref, gt_ref, wpa_ref, wpb_ref, wo_ref, g_ref, b_ref,
                   wrh_ref, wrl_ref, br_ref, h_ref, hp_ref, idx_ref, tw_ref, cnt_ref):
    i = pl.program_id(0)
    ra = ra_ref[...].astype(F32)
    ua = ra * _sigmoid(ra) * oa_ref[...].astype(F32)
    ya = jnp.dot(ua.astype(BF16), wpa_ref[...], preferred_element_type=F32)
    yb = jnp.dot(ob_ref[...], wpb_ref[...], preferred_element_type=F32)
    gt = gt_ref[...].astype(F32)
    mixed = _sigmoid(gt[:, :D_MODEL]) * ya + _sigmoid(gt[:, D_MODEL:]) * yb
    mix = jnp.dot(mixed.astype(BF16), wo_ref[...], preferred_element_type=F32)
    hh = _layer_norm(DEEPNORM_ALPHA * x_ref[...] + mix, g_ref[...], b_ref[...])
    h_ref[...] = hh
    hp_ref[...] = _pack_bf16_pairs(hh)

    h_hi = hh.astype(BF16)
    h_lo = (hh - h_hi.astype(F32)).astype(BF16)
    lg = (jnp.dot(h_hi, wrh_ref[...], preferred_element_type=F32)
          + jnp.dot(h_lo, wrh_ref[...], preferred_element_type=F32)
          + jnp.dot(h_hi, wrl_ref[...], preferred_element_type=F32)) + br_ref[...]
    tm = lg.shape[0]
    lane = lax.broadcasted_iota(I32, (tm, LANES), 1)
    lane_f = lane.astype(F32)
    work = lg
    vals, idxs = [], []
    cnt = jnp.zeros((tm, LANES), F32)
    for _ in range(TOP_K):
        mx = jnp.max(work, axis=-1, keepdims=True)
        ix = jnp.min(jnp.where(work == mx, lane_f, float(LANES)), axis=-1, keepdims=True)
        hit = lane_f == ix
        cnt = cnt + hit.astype(F32)
        work = jnp.where(hit, NEG_INF, work)
        vals.append(mx)
        idxs.append(ix)
    es = [jnp.exp(v - vals[0]) for v in vals]
    den = es[0] + es[1] + es[2] + es[3]
    idx_full = jnp.zeros((tm, LANES), F32)
    tw_full = jnp.zeros((tm, LANES), F32)
    for r in range(TOP_K):
        idx_full = jnp.where(lane == r, idxs[r], idx_full)
        tw_full = jnp.where(lane == r, es[r] / den, tw_full)
    idx_ref[...] = idx_full.T[:SUBLANES].astype(I32)
    tw_ref[...] = tw_full[:, :TOP_K]

    @pl.when(i == 0)
    def _():
        cnt_ref[...] = jnp.zeros(cnt_ref.shape, F32)

    cnt_ref[...] += jnp.sum(cnt, axis=0, keepdims=True)


def _finish(x2, oa, slab, ob, p):
    n = x2.shape[0]
    tm = min(512, n)
    const = lambda i: (0, 0)
    return pl.pallas_call(
        _finish_kernel,
        grid=(n // tm,),
        in_specs=[
            pl.BlockSpec((tm, D_MODEL), lambda i: (i, 0)),
            pl.BlockSpec((tm, VA_W), lambda i: (i, 0)),
            pl.BlockSpec((tm, RA_W), lambda i: (i, SLAB_RA // RA_W)),
            pl.BlockSpec((tm, VB_W), lambda i: (i, 0)),
            pl.BlockSpec((tm, GT_W), lambda i: (i, SLAB_GT // GT_W)),
            pl.BlockSpec((VA_W, D_MODEL), const),
            pl.BlockSpec((VB_W, D_MODEL), const),
            pl.BlockSpec((D_MODEL, D_MODEL), const),
            pl.BlockSpec((1, D_MODEL), const),
            pl.BlockSpec((1, D_MODEL), const),
            pl.BlockSpec((D_MODEL, LANES), const),
            pl.BlockSpec((D_MODEL, LANES), const),
            pl.BlockSpec((1, LANES), const),
        ],
        out_specs=[
            pl.BlockSpec((tm, D_MODEL), lambda i: (i, 0)),
            pl.BlockSpec((tm, D_MODEL // 2), lambda i: (i, 0)),
            pl.BlockSpec((SUBLANES, tm), lambda i: (0, i)),
            pl.BlockSpec((tm, TOP_K), lambda i: (i, 0)),
            pl.BlockSpec((1, LANES), const),
        ],
        out_shape=[
            jax.ShapeDtypeStruct((n, D_MODEL), F32),
            jax.ShapeDtypeStruct((n, D_MODEL // 2), U32),
            jax.ShapeDtypeStruct((SUBLANES, n), I32),
            jax.ShapeDtypeStruct((n, TOP_K), F32),
            jax.ShapeDtypeStruct((1, LANES), F32),
        ],
        compiler_params=_cparams(("arbitrary",)),
        name="finish",
    )(x2, oa, slab, ob, slab, p["w_pa"], p["w_pb"], p["w_o"], p["ln1_g"], p["ln1_b"],
      p["w_r_hi"], p["w_r_lo"], p["b_r"])


def _dest_kernel(idx_ref, ps_ref, ut_ref, dest_ref, carry_ref):
    i = pl.program_id(0)

    @pl.when(i == 0)
    def _():
        carry_ref[...] = jnp.zeros(carry_ref.shape, F32)

    tm = idx_ref.shape[1]
    expert = lax.broadcasted_iota(I32, (N_EXPERTS, tm), 0)
    idx = idx_ref[...]
    hits = [expert == idx[k:k + 1, :] for k in range(TOP_K)]
    cnt = jnp.zeros((N_EXPERTS, tm), F32)
    for hit in hits:
        cnt = cnt + hit.astype(F32)
    before = jnp.dot(cnt.astype(BF16), ut_ref[...], preferred_element_type=F32)
    base = before + carry_ref[:, 0:1] + ps_ref[:, 0:1]
    row = lax.broadcasted_iota(I32, (SUBLANES, tm), 0)
    dest = jnp.zeros((SUBLANES, tm), F32)
    for k in range(TOP_K):
        d = jnp.sum(jnp.where(hits[k], base, 0.0), axis=0, keepdims=True)
        dest = jnp.where(row == k, d, dest)
    dest_ref[...] = dest.astype(I32)
    carry_ref[...] += jnp.sum(cnt, axis=1, keepdims=True)


def _dest(idx_t, pstart):
    n = idx_t.shape[1]
    tm = min(512, n)
    utri = jnp.asarray(np.triu(np.ones((tm, tm), np.float32), 1), BF16)
    return pl.pallas_call(
        _dest_kernel,
        grid=(n // tm,),
        in_specs=[
            pl.BlockSpec((SUBLANES, tm), lambda i: (0, i)),
            pl.BlockSpec((N_EXPERTS, LANES), lambda i: (0, 0)),
            pl.BlockSpec((tm, tm), lambda i: (0, 0)),
        ],
        out_specs=pl.BlockSpec((SUBLANES, tm), lambda i: (0, i)),
        out_shape=jax.ShapeDtypeStruct((SUBLANES, n), I32),
        scratch_shapes=[pltpu.VMEM((N_EXPERTS, LANES), F32)],
        compiler_params=_cparams(("arbitrary",)),
        name="dest",
    )(idx_t, pstart, utri)


def _scatter_kernel(dest_ref, hp_ref, rows_in_ref, rows_ref, sem):
    del rows_in_ref
    tm = hp_ref.shape[0]

    def row_copy(t, d):
        return pltpu.make_async_copy(hp_ref.at[pl.ds(t, 1)], rows_ref.at[pl.ds(d, 1)], sem)

    def issue(t, carry):
        for k in range(TOP_K):
            row_copy(t, dest_ref[k, t]).start(priority=k % 2)
        return carry

    lax.fori_loop(0, tm, issue, 0, unroll=4)
    for k in range(TOP_K):
        pltpu.make_async_copy(hp_ref, rows_ref.at[pl.ds(0, tm)], sem).wait()


def _scatter(dest_t, hp, n_rows):
    n = hp.shape[0]
    tm = min(512, n)
    rows0 = jnp.zeros((n_rows, D_MODEL // 2), U32)
    return pl.pallas_call(
        _scatter_kernel,
        grid=(n // tm,),
        in_specs=[
            pl.BlockSpec((SUBLANES, tm), lambda i: (0, i), memory_space=pltpu.SMEM),
            pl.BlockSpec((tm, D_MODEL // 2), lambda i: (i, 0)),
            pl.BlockSpec(memory_space=pl.ANY),
        ],
        out_specs=pl.BlockSpec(memory_space=pl.ANY),
        out_shape=jax.ShapeDtypeStruct((n_rows, D_MODEL // 2), U32),
        scratch_shapes=[pltpu.SemaphoreType.DMA(())],
        input_output_aliases={2: 0},
        compiler_params=_cparams(("arbitrary",)),
        name="scatter",
    )(dest_t, hp, rows0)


def _expert_kernel(be_ref, nu_ref, rows_ref, wgu_ref, bgu_ref, wd_ref, bd_ref, out_ref):
    del be_ref
    i = pl.program_id(0)

    @pl.when(i < nu_ref[0])
    def _():
        x = _unpack_bf16_pairs(rows_ref[...]).astype(BF16)
        gu = jnp.dot(x, wgu_ref[0], preferred_element_type=F32) + bgu_ref[0]
        gate = jnp.minimum(gu[:, :D_FF], SWIGLU_LIMIT)
        up = jnp.clip(gu[:, D_FF:], -SWIGLU_LIMIT, SWIGLU_LIMIT)
        hidden = (up + 1.0) * (gate * _sigmoid(SWIGLU_ALPHA * gate))
        o = jnp.dot(hidden.astype(BF16), wd_ref[0], preferred_element_type=F32) + bd_ref[0]
        out_ref[...] = _pack_bf16_pairs(o)

    @pl.when(i >= nu_ref[0])
    def _():
        out_ref[...] = jnp.zeros(out_ref.shape, U32)


def _experts(rows, block_e, n_used, p):
    n_rows = rows.shape[0]
    bm = EXPERT_BM
    n_blocks = n_rows // bm
    grid_spec = pltpu.PrefetchScalarGridSpec(
        num_scalar_prefetch=2,
        grid=(n_blocks,),
        in_specs=[
            pl.BlockSpec((bm, D_MODEL // 2), lambda i, be, nu: (i, 0)),
            pl.BlockSpec((1, D_MODEL, 2 * D_FF), lambda i, be, nu: (be[i], 0, 0)),
            pl.BlockSpec((1, 1, 2 * D_FF), lambda i, be, nu: (be[i], 0, 0)),
            pl.BlockSpec((1, D_FF, D_MODEL), lambda i, be, nu: (be[i], 0, 0)),
            pl.BlockSpec((1, 1, D_MODEL), lambda i, be, nu: (be[i], 0, 0)),
        ],
        out_specs=pl.BlockSpec((bm, D_MODEL // 2), lambda i, be, nu: (i, 0)),
    )
    return pl.pallas_call(
        _expert_kernel,
        grid_spec=grid_spec,
        out_shape=jax.ShapeDtypeStruct((n_rows, D_MODEL // 2), U32),
        compiler_params=_cparams(("arbitrary",)),
        name="experts",
    )(block_e, n_used, rows, p["w_gu"], p["b_gu"], p["w_down"], p["b_down"])


def _combine_kernel(dest_ref, tw_ref, h_ref, rows_ref, g_ref, b_ref, y_ref, buf_ref, sem):
    tm = h_ref.shape[0]

    def row_copy(t, k, d):
        return pltpu.make_async_copy(rows_ref.at[pl.ds(d, 1)], buf_ref.at[k, pl.ds(t, 1)], sem)

    def issue(t, carry):
        for k in range(TOP_K):
            row_copy(t, k, dest_ref[k, t]).start(priority=k % 2)
        return carry

    lax.fori_loop(0, tm, issue, 0, unroll=4)
    for k in range(TOP_K):
        pltpu.make_async_copy(rows_ref.at[pl.ds(0, tm)], buf_ref.at[k], sem).wait()
    tw = tw_ref[...]
    f = jnp.zeros((tm, D_MODEL), F32)
    for k in range(TOP_K):
        f = f + tw[:, k:k + 1] * _unpack_bf16_pairs(buf_ref[k])
    y_ref[...] = _layer_norm(DEEPNORM_ALPHA * h_ref[...] + f, g_ref[...], b_ref[...])


def _combine(dest_t, top_w, hh, out_rows, p):
    n = hh.shape[0]
    tm = min(512, n)
    const = lambda i: (0, 0)
    return pl.pallas_call(
        _combine_kernel,
        grid=(n // tm,),
        in_specs=[
            pl.BlockSpec((SUBLANES, tm), lambda i: (0, i), memory_space=pltpu.SMEM),
            pl.BlockSpec((tm, TOP_K), lambda i: (i, 0)),
            pl.BlockSpec((tm, D_MODEL), lambda i: (i, 0)),
            pl.BlockSpec(memory_space=pl.ANY),
            pl.BlockSpec((1, D_MODEL), const),
            pl.BlockSpec((1, D_MODEL), const),
        ],
        out_specs=pl.BlockSpec((tm, D_MODEL), lambda i: (i, 0)),
        out_shape=jax.ShapeDtypeStruct((n, D_MODEL), F32),
        scratch_shapes=[pltpu.VMEM((TOP_K, tm, D_MODEL // 2), U32), pltpu.SemaphoreType.DMA(())],
        compiler_params=_cparams(("arbitrary",)),
        name="combine",
    )(dest_t, top_w, hh, out_rows, p["ln2_g"], p["ln2_b"])


def _moe(hh, hp, top_idx, top_w, counts, p):
    n = hh.shape[0]
    bm = EXPERT_BM
    n_rows = n * TOP_K + N_EXPERTS * bm
    cnt = counts[0, :N_EXPERTS].astype(I32)
    padded = (cnt + bm - 1) // bm * bm
    pend = jnp.cumsum(padded)
    pstart = pend - padded
    ps_col = jnp.broadcast_to(pstart.astype(F32)[:, None], (N_EXPERTS, LANES))
    block_row0 = jnp.arange(n_rows // bm, dtype=I32) * bm
    block_e = jnp.minimum(jnp.sum((pend[None, :] <= block_row0[:, None]).astype(I32), axis=1), N_EXPERTS - 1)
    n_used = (pend[-1:] // bm).astype(I32)
    dest = _dest(top_idx, ps_col)
    rows = _scatter(dest, hp, n_rows)
    out_rows = _experts(rows, block_e, n_used, p)
    return _combine(dest, top_w, hh, out_rows, p)


def _prep_params(w_in, b_in, w_alpha, b_alpha, gla_norm_g, diff_norm_g, w_pa, w_pb, w_o, ln1_g, ln1_b,
                 w_router, b_router, w_gu, b_gu, w_down, b_down, ln2_g, ln2_b):
    o = IN_OFFS
    seg = lambda a, i: a[..., o[i]:o[i + 1]]
    order = (0, 1, 2, 3, 5, 8)
    w_slab = jnp.concatenate([seg(w_in, i) for i in order], axis=1).astype(BF16)
    b_slab = jnp.concatenate([seg(b_in, i) for i in order])[None, :]
    scale = np.ones((1, SLAB_W), np.float32)
    scale[0, SLAB_QA:SLAB_QA + QA_W] = DK_A ** -0.5
    scale[0, SLAB_QB:SLAB_QB + QB_W] = D_B ** -0.5 * LOG2E
    w_lr = jnp.zeros((D_MODEL, LANES), F32).at[:, :LR_W].set(seg(w_in, 4)).astype(BF16)
    b_lr = jnp.zeros((1, LANES), F32).at[0, :LR_W].set(seg(b_in, 4))
    w_al = jnp.zeros((LANES, QA_W), F32).at[:LR_W].set(w_alpha).astype(BF16)
    w_r = jnp.zeros((D_MODEL, LANES), F32).at[:, :N_EXPERTS].set(w_router)
    w_r_hi = w_r.astype(BF16)
    w_r_lo = (w_r - w_r_hi.astype(F32)).astype(BF16)
    b_r = jnp.full((1, LANES), NEG_INF, F32).at[0, :N_EXPERTS].set(b_router)
    return dict(
        w_slab=w_slab, b_slab=b_slab, s_slab=jnp.asarray(scale),
        w_lr=w_lr, b_lr=b_lr, w_alpha=w_al, b_alpha=b_alpha[None, :],
        w_k=seg(w_in, 6).astype(BF16), w_v=seg(w_in, 7).astype(BF16),
        b_k=seg(b_in, 6)[None, :], b_v=seg(b_in, 7)[None, :],
        gla_g=gla_norm_g[None, :], diff_g=diff_norm_g[None, :],
        w_pa=w_pa.astype(BF16), w_pb=w_pb.astype(BF16), w_o=w_o.astype(BF16),
        ln1_g=ln1_g[None, :], ln1_b=ln1_b[None, :],
        w_r_hi=w_r_hi, w_r_lo=w_r_lo, b_r=b_r,
        w_gu=w_gu.astype(BF16), b_gu=b_gu[:, None, :], w_down=w_down.astype(BF16), b_down=b_down[:, None, :],
        ln2_g=ln2_g[None, :], ln2_b=ln2_b[None, :],
    )


def _layer(x, p, dl, lam_init, s0, cache_k, cache_v):
    batch, t_len, _ = x.shape
    x2 = x.reshape(batch * t_len, D_MODEL)
    slab, loga, k_rows, kb, vb = _proj(x2, p)
    oa, s_new = _gla(slab, loga, s0, p["gla_g"], batch, t_len)
    if cache_k is None:
        ob = _attn_prompt(slab, kb, vb, p["diff_g"], dl, batch, t_len, lam_init)
    else:
        ob = _attn_sample(slab, kb, vb, cache_k, cache_v, p["diff_g"], dl, batch, t_len, lam_init)
    hh, hp, top_idx, top_w, counts = _finish(x2, oa, slab, ob, p)
    y = _moe(hh, hp, top_idx, top_w, counts, p)
    return (y.reshape(batch, t_len, D_MODEL), s_new,
            k_rows.reshape(batch, t_len, H_B, 2, D_B), vb.reshape(batch, t_len, H_B, 2 * D_B))


def kernel(x_prompt, x_sample, cache_k, cache_v, state_gla, w_in, b_in, w_alpha, b_alpha, gla_norm_g,
           diff_lambda, diff_norm_g, w_pa, w_pb, w_o, ln1_g, ln1_b, w_router, b_router, w_gu, b_gu,
           w_down, b_down, ln2_g, ln2_b):
    assert w_in.shape[0] == DEPTH == 1
    l = 0
    lam_init = 0.8 - 0.6 * math.exp(-0.3 * l)
    p = _prep_params(w_in[l], b_in[l], w_alpha[l], b_alpha[l], gla_norm_g[l], diff_norm_g[l], w_pa[l],
                     w_pb[l], w_o[l], ln1_g[l], ln1_b[l], w_router[l], b_router[l], w_gu[l], b_gu[l],
                     w_down[l], b_down[l], ln2_g[l], ln2_b[l])
    dl = diff_lambda[l]
    bp = x_prompt.shape[0]
    bs, ts = x_sample.shape[0], x_sample.shape[1]
    past = cache_k.shape[2]
    yp, s_p, k_p, v_p = _layer(x_prompt, p, dl, lam_init, jnp.zeros((bp, H_A, DK_A, DV_A), F32), None, None)
    ck = cache_k[l].reshape(bs, past, KB_W)
    cv = cache_v[l].reshape(bs, past, VB_W)
    ys, s_s, k_s, v_s = _layer(x_sample, p, dl, lam_init, state_gla[l], ck, cv)
    return (yp, ys, s_p[None], k_p[None], v_p[None], s_s[None], k_s[None], v_s[None])
```

```python
import functools
import math

import numpy as np
import jax
import jax.numpy as jnp
from jax import lax
from jax.experimental import pallas as pl
from jax.experimental.pallas import tpu as pltpu

F32 = jnp.float32
BF16 = jnp.bfloat16
U32 = jnp.uint32
I32 = jnp.int32

D_MODEL = 1024
CHUNK = 64
H_A = 4
DK_A = 128
DV_A = 256
GATE_RANK = 16
GATE_TAU = 16.0
H_B = 8
D_B = 64
N_EXPERTS = 32
TOP_K = 4
D_FF = D_MODEL
SWIGLU_LIMIT = 7.0
SWIGLU_ALPHA = 1.702
EPS = 1e-5
DEPTH = 1
DEEPNORM_ALPHA = (2.0 * DEPTH) ** 0.25

QA_W = H_A * DK_A
KA_W = H_A * DK_A
VA_W = H_A * DV_A
RA_W = H_A * DV_A
LR_W = GATE_RANK
QB_W = H_B * 2 * D_B
KB_W = H_B * 2 * D_B
VB_W = H_B * 2 * D_B
GT_W = 2 * D_MODEL
IN_SIZES = (QA_W, KA_W, VA_W, RA_W, LR_W, QB_W, KB_W, VB_W, GT_W)
IN_OFFS = tuple(int(v) for v in np.cumsum((0,) + IN_SIZES))

SLAB_QA, SLAB_KA, SLAB_VA, SLAB_RA, SLAB_QB, SLAB_GT = 0, 512, 1024, 2048, 3072, 4096
SLAB_W = 6144
LANES = 128
SUBLANES = 8
EXPERT_BM = 512
VMEM_LIMIT = 56 * 1024 * 1024

NEG_INF = float("-inf")
LOG2E = math.log2(math.e)


def _cparams(sem):
    return pltpu.CompilerParams(dimension_semantics=sem, vmem_limit_bytes=VMEM_LIMIT)


def _sigmoid(x):
    return 1.0 / (1.0 + jnp.exp(-x))


def _pack_bf16_pairs(x):
    n = x.shape[1] // 2
    xb = x.astype(BF16).astype(F32)
    lo = pltpu.bitcast(xb[:, :n], U32) >> 16
    hi = pltpu.bitcast(xb[:, n:], U32) & jnp.uint32(0xFFFF0000)
    return hi | lo


def _unpack_bf16_pairs(r):
    lo = pltpu.bitcast(r << 16, F32)
    hi = pltpu.bitcast(r & jnp.uint32(0xFFFF0000), F32)
    return jnp.concatenate([lo, hi], axis=1)


def _layer_norm(y, g, b):
    mu = jnp.mean(y, axis=-1, keepdims=True)
    d = y - mu
    var = jnp.mean(d * d, axis=-1, keepdims=True)
    return d * lax.rsqrt(var + EPS) * g + b


def _proj_kernel(x_ref, w_ref, b_ref, s_ref, wlr_ref, blr_ref, wal_ref, bal_ref, wk_ref, wv_ref, bk_ref, bv_ref,
                 slab_ref, loga_ref, krows_ref, kb_ref, v_ref, *, tn):
    xb = x_ref[...].astype(BF16)
    lra = jnp.dot(xb, wlr_ref[...], preferred_element_type=F32) + blr_ref[...]
    z = jnp.dot(lra.astype(BF16), wal_ref[...], preferred_element_type=F32) + bal_ref[...]
    loga_ref[...] = (jnp.minimum(z, 0.0) - jnp.log(1.0 + jnp.exp(-jnp.abs(z)))) * (1.0 / GATE_TAU)
    for c0 in range(0, SLAB_W, tn):
        acc = jnp.dot(xb, w_ref[:, c0:c0 + tn], preferred_element_type=F32)
        slab_ref[:, c0:c0 + tn] = ((acc + b_ref[:, c0:c0 + tn]) * s_ref[:, c0:c0 + tn]).astype(BF16)
    k = jnp.dot(xb, wk_ref[...], preferred_element_type=F32) + bk_ref[...]
    kb_ref[...] = k.astype(BF16)
    for h in range(H_B):
        for j in range(2):
            c0 = (2 * h + j) * D_B
            krows_ref[:, h, j, :] = k[:, c0:c0 + D_B]
    v_ref[...] = jnp.dot(xb, wv_ref[...], preferred_element_type=F32) + bv_ref[...]


def _proj(x2, p):
    n = x2.shape[0]
    tm = min(512, n)
    const = lambda i: (0, 0)
    whole = lambda shape: pl.BlockSpec(shape, const, pipeline_mode=pl.Buffered(1))
    row = lambda w: pl.BlockSpec((tm, w), lambda i: (i, 0))
    return pl.pallas_call(
        functools.partial(_proj_kernel, tn=1024),
        grid=(n // tm,),
        in_specs=[
            row(D_MODEL),
            whole((D_MODEL, SLAB_W)), whole((1, SLAB_W)), whole((1, SLAB_W)),
            whole((D_MODEL, LANES)), whole((1, LANES)), whole((LANES, QA_W)), whole((1, QA_W)),
            whole((D_MODEL, KB_W)), whole((D_MODEL, VB_W)), whole((1, KB_W)), whole((1, VB_W)),
        ],
        out_specs=[
            row(SLAB_W), row(QA_W),
            pl.BlockSpec((tm, H_B, 2, D_B), lambda i: (i, 0, 0, 0)),
            row(KB_W), row(VB_W),
        ],
        out_shape=[
            jax.ShapeDtypeStruct((n, SLAB_W), BF16),
            jax.ShapeDtypeStruct((n, QA_W), F32),
            jax.ShapeDtypeStruct((n, H_B, 2, D_B), F32),
            jax.ShapeDtypeStruct((n, KB_W), BF16),
            jax.ShapeDtypeStruct((n, VB_W), F32),
        ],
        compiler_params=_cparams(("parallel",)),
        name="proj",
    )(x2, p["w_slab"], p["b_slab"], p["s_slab"], p["w_lr"], p["b_lr"], p["w_alpha"], p["b_alpha"],
      p["w_k"], p["w_v"], p["b_k"], p["b_v"])


def _gla_tables(L):
    nl = int(math.log2(L))
    t = np.arange(L)
    D = np.zeros(((nl + 2) * L, L), np.float32)
    masks = np.zeros((nl + 1, L, L), np.float32)
    for l in range(nl):
        m = L >> (l + 1)
        grp = t // (2 * m)
        mid = grp * 2 * m + m - 1
        upper = (t % (2 * m)) >= m
        for r in range(L):
            if upper[r]:
                D[l * L + r, mid[r] + 1:r + 1] = 1.0
            else:
                D[l * L + r, r + 1:mid[r] + 1] = 1.0
        masks[l] = (upper[:, None] & ~upper[None, :] & (grp[:, None] == grp[None, :])).astype(np.float32)
    D[nl * L:(nl + 1) * L] = np.tril(np.ones((L, L), np.float32))
    D[(nl + 1) * L:] = np.triu(np.ones((L, L), np.float32), 1)
    masks[nl] = np.eye(L, dtype=np.float32)
    return jnp.asarray(D, BF16), jnp.asarray(masks, F32)


def _gla_kernel(q_ref, k_ref, v_ref, la_ref, s0_ref, g_ref, d_ref, m_ref, o_ref, s_ref, *, L, n_chunks):
    c = pl.program_id(1)
    nl = int(math.log2(L))

    @pl.when(c == 0)
    def _():
        s_ref[...] = s0_ref[...]

    dmat = d_ref[...]
    ones_col = jnp.ones((L, LANES), BF16)
    g = g_ref[...]
    nt = (((1,), (1,)), ((), ()))
    tn = (((0,), (0,)), ((), ()))

    def chunk(ci, carry):
        r0 = pl.multiple_of(ci * L, L)
        rows = pl.ds(r0, L)
        for h in range(H_A):
            kc = slice(h * DK_A, (h + 1) * DK_A)
            vc = slice(h * DV_A, (h + 1) * DV_A)
            q = q_ref[rows, kc].astype(F32)
            k = k_ref[rows, kc].astype(F32)
            v = v_ref[rows, vc]
            la = la_ref[rows, kc]
            la_hi = la.astype(BF16)
            la_lo = (la - la_hi.astype(F32)).astype(BF16)
            e = (jnp.dot(dmat, la_hi, preferred_element_type=F32)
                 + jnp.dot(dmat, la_lo, preferred_element_type=F32))
            x = jnp.exp(e)
            a = m_ref[nl] * lax.dot_general(q.astype(BF16), k.astype(BF16), nt, preferred_element_type=F32)
            for l in range(nl):
                xl = x[l * L:(l + 1) * L]
                a = a + m_ref[l] * lax.dot_general((q * xl).astype(BF16), (k * xl).astype(BF16), nt,
                                                   preferred_element_type=F32)
            xb = x[nl * L:(nl + 1) * L]
            xs = x[(nl + 1) * L:]
            s_old = s_ref[0, h]
            o = (jnp.dot((q * xb).astype(BF16), s_old.astype(BF16), preferred_element_type=F32)
                 + jnp.dot(a.astype(BF16), v, preferred_element_type=F32))
            bl = (lax.dot_general(la_hi, ones_col, tn, preferred_element_type=F32)
                  + lax.dot_general(la_lo, ones_col, tn, preferred_element_type=F32))
            dec = jnp.exp(bl[:, 0:1])
            s_ref[0, h] = dec * s_old + lax.dot_general((k * xs).astype(BF16), v, tn, preferred_element_type=F32)
            ms = jnp.mean(o * o, axis=-1, keepdims=True)
            o_ref[rows, vc] = (o * lax.rsqrt(ms + EPS) * g).astype(BF16)
        return carry

    lax.fori_loop(0, n_chunks, chunk, 0, unroll=2 if n_chunks % 2 == 0 else 1)


def _gla(slab, loga, s0, g_norm, batch, t_len):
    L = min(CHUNK, t_len)
    tb = min(512, t_len)
    nb = t_len // tb
    dmat, masks = _gla_tables(L)
    kern = functools.partial(_gla_kernel, L=L, n_chunks=tb // L)
    row = lambda b, c: b * nb + c
    return pl.pallas_call(
        kern,
        grid=(batch, nb),
        in_specs=[
            pl.BlockSpec((tb, QA_W), lambda b, c: (row(b, c), SLAB_QA // QA_W)),
            pl.BlockSpec((tb, KA_W), lambda b, c: (row(b, c), SLAB_KA // KA_W)),
            pl.BlockSpec((tb, VA_W), lambda b, c: (row(b, c), SLAB_VA // VA_W)),
            pl.BlockSpec((tb, QA_W), lambda b, c: (row(b, c), 0)),
            pl.BlockSpec((1, H_A, DK_A, DV_A), lambda b, c: (b, 0, 0, 0)),
            pl.BlockSpec((1, DV_A), lambda b, c: (0, 0)),
            pl.BlockSpec(dmat.shape, lambda b, c: (0, 0)),
            pl.BlockSpec(masks.shape, lambda b, c: (0, 0, 0)),
        ],
        out_specs=[
            pl.BlockSpec((tb, VA_W), lambda b, c: (row(b, c), 0)),
            pl.BlockSpec((1, H_A, DK_A, DV_A), lambda b, c: (b, 0, 0, 0)),
        ],
        out_shape=[
            jax.ShapeDtypeStruct((batch * t_len, VA_W), BF16),
            jax.ShapeDtypeStruct((batch, H_A, DK_A, DV_A), F32),
        ],
        compiler_params=_cparams(("parallel", "arbitrary")),
        name="gla",
    )(slab, slab, slab, loga, s0, g_norm, dmat, masks)


def _lambda_from(dl_ref, lam_init):
    dl = dl_ref[...]
    a = jnp.sum(dl[0:1] * dl[1:2], axis=-1, keepdims=True)
    b = jnp.sum(dl[2:3] * dl[3:4], axis=-1, keepdims=True)
    return jnp.exp(a) - jnp.exp(b) + lam_init


def _split_maps(q):
    lane = lax.broadcasted_iota(I32, q.shape, 1)
    zero = jnp.zeros_like(q)
    return jnp.concatenate([jnp.where(lane < D_B, q, zero), jnp.where(lane >= D_B, q, zero)], axis=0)


def _alibi_coef(h):
    c = jnp.full((1, 1), LOG2E, F32) * jnp.exp2(-(h + 1).astype(F32))
    c_hi = c.astype(BF16).astype(F32)
    return c, c_hi, c - c_hi


def _finish_heads(acc, inv_l, lam, g, tq, lam_init):
    o = acc * inv_l
    out = o[:tq] - lam * o[tq:]
    ms = jnp.mean(out * out, axis=-1, keepdims=True)
    return (out * lax.rsqrt(ms + EPS) * g * (1.0 - lam_init)).astype(BF16)


def _attn_prompt_kernel(q_ref, k_ref, v_ref, g_ref, dl_ref, o_ref, ka_ref, vb_ref, qaug_ref, d_ref, m_ref,
                        acc_ref, s0_ref, s1_ref, p0_ref, p1_ref,
                        *, tq, tk, lam_init):
    h = pl.program_id(1)
    qi = pl.program_id(2)
    c, c_hi, c_lo = _alibi_coef(h)
    hw = 2 * D_B

    def aug_lanes(shape, pos, sign):
        lane = lax.broadcasted_iota(I32, shape, 1)
        r = (pos & 255).astype(F32) * sign
        a = (pos >> 8).astype(F32) * sign
        coef = jnp.where(lane == 0, c_hi, jnp.where(lane == 1, c_lo,
                         jnp.where(lane == 2, 256.0 * c_hi, jnp.where(lane == 3, 256.0 * c_lo, 0.0))))
        ints = jnp.where(lane < 2, r, jnp.where(lane < 4, a, 0.0))
        return coef, ints, lane

    @pl.when(qi == 0)
    def _():
        t_len = ka_ref.shape[0]
        j_rel = lax.broadcasted_iota(I32, (t_len, hw), 0) & (tk - 1)
        coef, ints, lane = aug_lanes((t_len, hw), j_rel, 1.0)
        k_aug = jnp.where(lane < 4, ints, pltpu.roll(coef, 4, 1))
        ka_ref[:, :hw] = k_ref[0].astype(BF16)
        ka_ref[:, hw:] = k_aug.astype(BF16)
        vb_ref[:, :hw] = v_ref[0].astype(BF16)
        vb_ref[:, hw:] = jnp.where(lane == 0, 1.0, 0.0).astype(BF16)
        i_rel = lax.broadcasted_iota(I32, (2 * tq, hw), 0) & (tq - 1)
        coef, ints, lane = aug_lanes((2 * tq, hw), i_rel, -1.0)
        qaug_ref[...] = jnp.where(lane < 4, coef, pltpu.roll(ints, 4, 1)).astype(BF16)
        i = lax.broadcasted_iota(I32, (2 * tq, tq), 0) & (tq - 1)
        j = lax.broadcasted_iota(I32, (2 * tq, tq), 1)
        fwd = jnp.maximum(j - i, 0).astype(F32)
        d_ref[...] = jnp.where((j >> 6) <= (i >> 6), -2.0 * c * fwd, NEG_INF)

    qa = jnp.concatenate([_split_maps(q_ref[...]), qaug_ref[...]], axis=1)
    m_ref[...] = jnp.full(m_ref.shape, -1e30, F32)

    def scores(kv):
        k0 = pl.multiple_of(kv * tk, tk)
        return lax.dot_general(qa, ka_ref[pl.ds(k0, tk), :], (((1,), (1,)), ((), ())), preferred_element_type=F32)

    def weighted_values(p_ref, kv):
        k0 = pl.multiple_of(kv * tk, tk)
        return jnp.dot(p_ref[...], vb_ref[pl.ds(k0, tk), :], preferred_element_type=F32)

    def softmax_step(s, off):
        m_old = m_ref[...]
        m_new = jnp.maximum(m_old, jnp.max(s, axis=-1, keepdims=True) - off)
        shift = m_new + off
        p = jnp.exp2(s - jnp.concatenate([shift] * (tk // hw), axis=1))
        alpha = jnp.exp2(m_old - m_new)
        m_ref[...] = m_new
        return p.astype(BF16), jnp.concatenate([alpha, alpha], axis=1)

    p_diag, _ = softmax_step(scores(qi) + d_ref[...], jnp.zeros((1, 1), F32))
    p1_ref[...] = p_diag
    acc_ref[...] = jnp.zeros(acc_ref.shape, F32)

    @pl.when(qi > 0)
    def _():
        s0_ref[...] = scores(0)

    def step(kv, s_cur, s_nxt, p_prev, p_cur):
        pv_prev = weighted_values(p_prev, jnp.where(kv == 0, qi, kv - 1))
        s_nxt[...] = scores(jnp.minimum(kv + 1, qi - 1))
        p, alpha = softmax_step(s_cur[...], c * (qi * tq - kv * tk).astype(F32))
        acc_ref[...] = alpha * (acc_ref[...] + pv_prev)
        p_cur[...] = p

    def pair(j, carry):
        step(2 * j, s0_ref, s1_ref, p1_ref, p0_ref)
        step(2 * j + 1, s1_ref, s0_ref, p0_ref, p1_ref)
        return carry

    lax.fori_loop(0, qi // 2, pair, 0)

    @pl.when(qi % 2 == 1)
    def _():
        step(qi - 1, s0_ref, s1_ref, p1_ref, p0_ref)
        acc_ref[...] += weighted_values(p0_ref, qi - 1)

    @pl.when(qi % 2 == 0)
    def _():
        acc_ref[...] += weighted_values(p1_ref, jnp.maximum(qi - 1, 0))

    lam = _lambda_from(dl_ref, lam_init)
    acc = acc_ref[...]
    o_ref[...] = _finish_heads(acc[:, :hw], 1.0 / acc[:, hw:hw + 1], lam, g_ref[...], tq, lam_init)


def _attn_prompt(slab, kb, vb, g_norm, dl, batch, t_len, lam_init):
    tq = min(512, t_len)
    tk = tq
    assert tq % CHUNK == 0 and CHUNK == 64
    nq = t_len // tq
    k3 = kb.reshape(batch, t_len, KB_W)
    v3 = vb.reshape(batch, t_len, VB_W)
    kern = functools.partial(_attn_prompt_kernel, tq=tq, tk=tk, lam_init=lam_init)
    hw = 2 * D_B
    return pl.pallas_call(
        kern,
        grid=(batch, H_B, nq),
        in_specs=[
            pl.BlockSpec((tq, hw), lambda b, h, q: (b * nq + q, SLAB_QB // hw + h)),
            pl.BlockSpec((1, t_len, hw), lambda b, h, q: (b, 0, h)),
            pl.BlockSpec((1, t_len, hw), lambda b, h, q: (b, 0, h)),
            pl.BlockSpec((1, hw), lambda b, h, q: (0, 0)),
            pl.BlockSpec((4, D_B), lambda b, h, q: (0, 0)),
        ],
        out_specs=pl.BlockSpec((tq, hw), lambda b, h, q: (b * nq + q, h)),
        out_shape=jax.ShapeDtypeStruct((batch * t_len, VB_W), BF16),
        scratch_shapes=[
            pltpu.VMEM((t_len, 2 * hw), BF16),
            pltpu.VMEM((t_len, 2 * hw), BF16),
            pltpu.VMEM((2 * tq, hw), BF16),
            pltpu.VMEM((2 * tq, tq), F32),
            pltpu.VMEM((2 * tq, hw), F32),
            pltpu.VMEM((2 * tq, 2 * hw), F32),
            pltpu.VMEM((2 * tq, tk), F32),
            pltpu.VMEM((2 * tq, tk), F32),
            pltpu.VMEM((2 * tq, tk), BF16),
            pltpu.VMEM((2 * tq, tk), BF16),
        ],
        compiler_params=_cparams(("parallel", "parallel", "arbitrary")),
        name="attn_prompt",
    )(slab, k3, v3, g_norm, dl)


def _attn_sample_kernel(q_ref, kc_ref, vc_ref, kn_ref, vn_ref, g_ref, dl_ref, o_ref, *, tq, past, lam_init):
    h = pl.program_id(1)
    c, _, _ = _alibi_coef(h)
    qq = _split_maps(q_ref[...])
    q_pos = past + lax.broadcasted_iota(I32, (2 * tq, 1), 0) % tq

    def scores(k, k_pos):
        s = lax.dot_general(qq, k.astype(BF16), (((1,), (1,)), ((), ())), preferred_element_type=F32)
        s = s - c * jnp.abs(q_pos - k_pos).astype(F32)
        return jnp.where((k_pos // CHUNK) <= (q_pos // CHUNK), s, NEG_INF)

    sc = scores(kc_ref[0], lax.broadcasted_iota(I32, (1, past), 1))
    sn = scores(kn_ref[0], past + lax.broadcasted_iota(I32, (1, tq), 1))
    m = jnp.maximum(jnp.max(sc, axis=-1, keepdims=True), jnp.max(sn, axis=-1, keepdims=True))
    pc = jnp.exp2(sc - m)
    pn = jnp.exp2(sn - m)
    l = jnp.sum(pc, axis=-1, keepdims=True) + jnp.sum(pn, axis=-1, keepdims=True)
    acc = (jnp.dot(pc.astype(BF16), vc_ref[0].astype(BF16), preferred_element_type=F32)
           + jnp.dot(pn.astype(BF16), vn_ref[0].astype(BF16), preferred_element_type=F32))
    lam = _lambda_from(dl_ref, lam_init)
    o_ref[...] = _finish_heads(acc, 1.0 / l, lam, g_ref[...], tq, lam_init)


def _attn_sample(slab, kb, vb, cache_k, cache_v, g_norm, dl, batch, t_len, lam_init):
    past = cache_k.shape[1]
    hw = 2 * D_B
    k3 = kb.reshape(batch, t_len, KB_W)
    v3 = vb.reshape(batch, t_len, VB_W)
    kern = functools.partial(_attn_sample_kernel, tq=t_len, past=past, lam_init=lam_init)
    return pl.pallas_call(
        kern,
        grid=(batch, H_B),
        in_specs=[
            pl.BlockSpec((t_len, hw), lambda b, h: (b, SLAB_QB // hw + h)),
            pl.BlockSpec((1, past, hw), lambda b, h: (b, 0, h)),
            pl.BlockSpec((1, past, hw), lambda b, h: (b, 0, h)),
            pl.BlockSpec((1, t_len, hw), lambda b, h: (b, 0, h)),
            pl.BlockSpec((1, t_len, hw), lambda b, h: (b, 0, h)),
            pl.BlockSpec((1, hw), lambda b, h: (0, 0)),
            pl.BlockSpec((4, D_B), lambda b, h: (0, 0)),
        ],
        out_specs=pl.BlockSpec((t_len, hw), lambda b, h: (b, h)),
        out_shape=jax.ShapeDtypeStruct((batch * t_len, VB_W), BF16),
        compiler_params=_cparams(("parallel", "parallel")),
        name="attn_sample",
    )(slab, cache_k, cache_v, k3, v3, g_norm, dl)


def _finish_kernel(x_ref, oa_ref, ra_ref, ob_ref, gt_ref, wpa_ref, wpb_ref, wo_ref, g_ref, b_ref,
                   wrh_ref, wrl_ref, br_ref, h_ref, hp_ref, idx_ref, tw_ref, cnt_ref):
    i = pl.program_id(0)
    ra = ra_ref[...].astype(F32)
    ua = ra * _sigmoid(ra) * oa_ref[...].astype(F32)
    ya = jnp.dot(ua.astype(BF16), wpa_ref[...], preferred_element_type=F32)
    yb = jnp.dot(ob_ref[...], wpb_ref[...], preferred_element_type=F32)
    gt = gt_ref[...].astype(F32)
    mixed = _sigmoid(gt[:, :D_MODEL]) * ya + _sigmoid(gt[:, D_MODEL:]) * yb
    mix = jnp.dot(mixed.astype(BF16), wo_ref[...], preferred_element_type=F32)
    hh = _layer_norm(DEEPNORM_ALPHA * x_ref[...] + mix, g_ref[...], b_ref[...])
    h_ref[...] = hh
    hp_ref[...] = _pack_bf16_pairs(hh)

    h_hi = hh.astype(BF16)
    h_lo = (hh - h_hi.astype(F32)).astype(BF16)
    lg = (jnp.dot(h_hi, wrh_ref[...], preferred_element_type=F32)
          + jnp.dot(h_lo, wrh_ref[...], preferred_element_type=F32)
          + jnp.dot(h_hi, wrl_ref[...], preferred_element_type=F32)) + br_ref[...]
    tm = lg.shape[0]
    lane = lax.broadcasted_iota(I32, (tm, LANES), 1)
    lane_f = lane.astype(F32)
    work = lg
    vals, idxs = [], []
    cnt = jnp.zeros((tm, LANES), F32)
    for _ in range(TOP_K):
        mx = jnp.max(work, axis=-1, keepdims=True)
        ix = jnp.min(jnp.where(work == mx, lane_f, float(LANES)), axis=-1, keepdims=True)
        hit = lane_f == ix
        cnt = cnt + hit.astype(F32)
        work = jnp.where(hit, NEG_INF, work)
        vals.append(mx)
        idxs.append(ix)
    es = [jnp.exp(v - vals[0]) for v in vals]
    den = es[0] + es[1] + es[2] + es[3]
    idx_full = jnp.zeros((tm, LANES), F32)
    tw_full = jnp.zeros((tm, LANES), F32)
    for r in range(TOP_K):
        idx_full = jnp.where(lane == r, idxs[r], idx_full)
        tw_full = jnp.where(lane == r, es[r] / den, tw_full)
    idx_ref[...] = idx_full.T[:SUBLANES].astype(I32)
    tw_ref[...] = tw_full[:, :TOP_K]

    @pl.when(i == 0)
    def _():
        cnt_ref[...] = jnp.zeros(cnt_ref.shape, F32)

    cnt_ref[...] += jnp.sum(cnt, axis=0, keepdims=True)


def _finish(x2, oa, slab, ob, p):
    n = x2.shape[0]
    tm = min(512, n)
    const = lambda i: (0, 0)
    return pl.pallas_call(
        _finish_kernel,
        grid=(n // tm,),
        in_specs=[
            pl.BlockSpec((tm, D_MODEL), lambda i: (i, 0)),
            pl.BlockSpec((tm, VA_W), lambda i: (i, 0)),
            pl.BlockSpec((tm, RA_W), lambda i: (i, SLAB_RA // RA_W)),
            pl.BlockSpec((tm, VB_W), lambda i: (i, 0)),
            pl.BlockSpec((tm, GT_W), lambda i: (i, SLAB_GT // GT_W)),
            pl.BlockSpec((VA_W, D_MODEL), const),
            pl.BlockSpec((VB_W, D_MODEL), const),
            pl.BlockSpec((D_MODEL, D_MODEL), const),
            pl.BlockSpec((1, D_MODEL), const),
            pl.BlockSpec((1, D_MODEL), const),
            pl.BlockSpec((D_MODEL, LANES), const),
            pl.BlockSpec((D_MODEL, LANES), const),
            pl.BlockSpec((1, LANES), const),
        ],
        out_specs=[
            pl.BlockSpec((tm, D_MODEL), lambda i: (i, 0)),
            pl.BlockSpec((tm, D_MODEL // 2), lambda i: (i, 0)),
            pl.BlockSpec((SUBLANES, tm), lambda i: (0, i)),
            pl.BlockSpec((tm, TOP_K), lambda i: (i, 0)),
            pl.BlockSpec((1, LANES), const),
        ],
        out_shape=[
            jax.ShapeDtypeStruct((n, D_MODEL), F32),
            jax.ShapeDtypeStruct((n, D_MODEL // 2), U32),
            jax.ShapeDtypeStruct((SUBLANES, n), I32),
            jax.ShapeDtypeStruct((n, TOP_K), F32),
            jax.ShapeDtypeStruct((1, LANES), F32),
        ],
        compiler_params=_cparams(("arbitrary",)),
        name="finish",
    )(x2, oa, slab, ob, slab, p["w_pa"], p["w_pb"], p["w_o"], p["ln1_g"], p["ln1_b"],
      p["w_r_hi"], p["w_r_lo"], p["b_r"])


def _dest_kernel(idx_ref, ps_ref, ut_ref, dest_ref, carry_ref):
    i = pl.program_id(0)

    @pl.when(i == 0)
    def _():
        carry_ref[...] = jnp.zeros(carry_ref.shape, F32)

    tm = idx_ref.shape[1]
    expert = lax.broadcasted_iota(I32, (N_EXPERTS, tm), 0)
    idx = idx_ref[...]
    hits = [expert == idx[k:k + 1, :] for k in range(TOP_K)]
    cnt = jnp.zeros((N_EXPERTS, tm), F32)
    for hit in hits:
        cnt = cnt + hit.astype(F32)
    before = jnp.dot(cnt.astype(BF16), ut_ref[...], preferred_element_type=F32)
    base = before + carry_ref[:, 0:1] + ps_ref[:, 0:1]
    row = lax.broadcasted_iota(I32, (SUBLANES, tm), 0)
    dest = jnp.zeros((SUBLANES, tm), F32)
    for k in range(TOP_K):
        d = jnp.sum(jnp.where(hits[k], base, 0.0), axis=0, keepdims=True)
        dest = jnp.where(row == k, d, dest)
    dest_ref[...] = dest.astype(I32)
    carry_ref[...] += jnp.sum(cnt, axis=1, keepdims=True)


def _dest(idx_t, pstart):
    n = idx_t.shape[1]
    tm = min(512, n)
    utri = jnp.asarray(np.triu(np.ones((tm, tm), np.float32), 1), BF16)
    return pl.pallas_call(
        _dest_kernel,
        grid=(n // tm,),
        in_specs=[
            pl.BlockSpec((SUBLANES, tm), lambda i: (0, i)),
            pl.BlockSpec((N_EXPERTS, LANES), lambda i: (0, 0)),
            pl.BlockSpec((tm, tm), lambda i: (0, 0)),
        ],
        out_specs=pl.BlockSpec((SUBLANES, tm), lambda i: (0, i)),
        out_shape=jax.ShapeDtypeStruct((SUBLANES, n), I32),
        scratch_shapes=[pltpu.VMEM((N_EXPERTS, LANES), F32)],
        compiler_params=_cparams(("arbitrary",)),
        name="dest",
    )(idx_t, pstart, utri)


def _scatter_kernel(dest_ref, hp_ref, rows_in_ref, rows_ref, sem):
    del rows_in_ref
    tm = hp_ref.shape[0]

    def row_copy(t, d):
        return pltpu.make_async_copy(hp_ref.at[pl.ds(t, 1)], rows_ref.at[pl.ds(d, 1)], sem)

    def issue(t, carry):
        for k in range(TOP_K):
            row_copy(t, dest_ref[k, t]).start(priority=k % 2)
        return carry

    lax.fori_loop(0, tm, issue, 0, unroll=4)
    for k in range(TOP_K):
        pltpu.make_async_copy(hp_ref, rows_ref.at[pl.ds(0, tm)], sem).wait()


def _scatter(dest_t, hp, n_rows):
    n = hp.shape[0]
    tm = min(512, n)
    rows0 = jnp.zeros((n_rows, D_MODEL // 2), U32)
    return pl.pallas_call(
        _scatter_kernel,
        grid=(n // tm,),
        in_specs=[
            pl.BlockSpec((SUBLANES, tm), lambda i: (0, i), memory_space=pltpu.SMEM),
            pl.BlockSpec((tm, D_MODEL // 2), lambda i: (i, 0)),
            pl.BlockSpec(memory_space=pl.ANY),
        ],
        out_specs=pl.BlockSpec(memory_space=pl.ANY),
        out_shape=jax.ShapeDtypeStruct((n_rows, D_MODEL // 2), U32),
        scratch_shapes=[pltpu.SemaphoreType.DMA(())],
        input_output_aliases={2: 0},
        compiler_params=_cparams(("arbitrary",)),
        name="scatter",
    )(dest_t, hp, rows0)


def _expert_kernel(be_ref, nu_ref, rows_ref, wgu_ref, bgu_ref, wd_ref, bd_ref, out_ref):
    del be_ref
    i = pl.program_id(0)

    @pl.when(i < nu_ref[0])
    def _():
        x = _unpack_bf16_pairs(rows_ref[...]).astype(BF16)
        gu = jnp.dot(x, wgu_ref[0], preferred_element_type=F32) + bgu_ref[0]
        gate = jnp.minimum(gu[:, :D_FF], SWIGLU_LIMIT)
        up = jnp.clip(gu[:, D_FF:], -SWIGLU_LIMIT, SWIGLU_LIMIT)
        hidden = (up + 1.0) * (gate * _sigmoid(SWIGLU_ALPHA * gate))
        o = jnp.dot(hidden.astype(BF16), wd_ref[0], preferred_element_type=F32) + bd_ref[0]
        out_ref[...] = _pack_bf16_pairs(o)

    @pl.when(i >= nu_ref[0])
    def _():
        out_ref[...] = jnp.zeros(out_ref.shape, U32)


def _experts(rows, block_e, n_used, p):
    n_rows = rows.shape[0]
    bm = EXPERT_BM
    n_blocks = n_rows // bm
    grid_spec = pltpu.PrefetchScalarGridSpec(
        num_scalar_prefetch=2,
        grid=(n_blocks,),
        in_specs=[
            pl.BlockSpec((bm, D_MODEL // 2), lambda i, be, nu: (i, 0)),
            pl.BlockSpec((1, D_MODEL, 2 * D_FF), lambda i, be, nu: (be[i], 0, 0)),
            pl.BlockSpec((1, 1, 2 * D_FF), lambda i, be, nu: (be[i], 0, 0)),
            pl.BlockSpec((1, D_FF, D_MODEL), lambda i, be, nu: (be[i], 0, 0)),
            pl.BlockSpec((1, 1, D_MODEL), lambda i, be, nu: (be[i], 0, 0)),
        ],
        out_specs=pl.BlockSpec((bm, D_MODEL // 2), lambda i, be, nu: (i, 0)),
    )
    return pl.pallas_call(
        _expert_kernel,
        grid_spec=grid_spec,
        out_shape=jax.ShapeDtypeStruct((n_rows, D_MODEL // 2), U32),
        compiler_params=_cparams(("arbitrary",)),
        name="experts",
    )(block_e, n_used, rows, p["w_gu"], p["b_gu"], p["w_down"], p["b_down"])


def _combine_kernel(dest_ref, tw_ref, h_ref, rows_ref, g_ref, b_ref, y_ref, buf_ref, sem):
    tm = h_ref.shape[0]

    def row_copy(t, k, d):
        return pltpu.make_async_copy(rows_ref.at[pl.ds(d, 1)], buf_ref.at[k, pl.ds(t, 1)], sem)

    def issue(t, carry):
        for k in range(TOP_K):
            row_copy(t, k, dest_ref[k, t]).start(priority=k % 2)
        return carry

    lax.fori_loop(0, tm, issue, 0, unroll=4)
    for k in range(TOP_K):
        pltpu.make_async_copy(rows_ref.at[pl.ds(0, tm)], buf_ref.at[k], sem).wait()
    tw = tw_ref[...]
    f = jnp.zeros((tm, D_MODEL), F32)
    for k in range(TOP_K):
        f = f + tw[:, k:k + 1] * _unpack_bf16_pairs(buf_ref[k])
    y_ref[...] = _layer_norm(DEEPNORM_ALPHA * h_ref[...] + f, g_ref[...], b_ref[...])


def _combine(dest_t, top_w, hh, out_rows, p):
    n = hh.shape[0]
    tm = min(512, n)
    const = lambda i: (0, 0)
    return pl.pallas_call(
        _combine_kernel,
        grid=(n // tm,),
        in_specs=[
            pl.BlockSpec((SUBLANES, tm), lambda i: (0, i), memory_space=pltpu.SMEM),
            pl.BlockSpec((tm, TOP_K), lambda i: (i, 0)),
            pl.BlockSpec((tm, D_MODEL), lambda i: (i, 0)),
            pl.BlockSpec(memory_space=pl.ANY),
            pl.BlockSpec((1, D_MODEL), const),
            pl.BlockSpec((1, D_MODEL), const),
        ],
        out_specs=pl.BlockSpec((tm, D_MODEL), lambda i: (i, 0)),
        out_shape=jax.ShapeDtypeStruct((n, D_MODEL), F32),
        scratch_shapes=[pltpu.VMEM((TOP_K, tm, D_MODEL // 2), U32), pltpu.SemaphoreType.DMA(())],
        compiler_params=_cparams(("arbitrary",)),
        name="combine",
    )(dest_t, top_w, hh, out_rows, p["ln2_g"], p["ln2_b"])


def _moe(hh, hp, top_idx, top_w, counts, p):
    n = hh.shape[0]
    bm = EXPERT_BM
    n_rows = n * TOP_K + N_EXPERTS * bm
    cnt = counts[0, :N_EXPERTS].astype(I32)
    padded = (cnt + bm - 1) // bm * bm
    pend = jnp.cumsum(padded)
    pstart = pend - padded
    ps_col = jnp.broadcast_to(pstart.astype(F32)[:, None], (N_EXPERTS, LANES))
    block_row0 = jnp.arange(n_rows // bm, dtype=I32) * bm
    block_e = jnp.minimum(jnp.sum((pend[None, :] <= block_row0[:, None]).astype(I32), axis=1), N_EXPERTS - 1)
    n_used = (pend[-1:] // bm).astype(I32)
    dest = _dest(top_idx, ps_col)
    rows = _scatter(dest, hp, n_rows)
    out_rows = _experts(rows, block_e, n_used, p)
    return _combine(dest, top_w, hh, out_rows, p)


def _prep_params(w_in, b_in, w_alpha, b_alpha, gla_norm_g, diff_norm_g, w_pa, w_pb, w_o, ln1_g, ln1_b,
                 w_router, b_router, w_gu, b_gu, w_down, b_down, ln2_g, ln2_b):
    o = IN_OFFS
    seg = lambda a, i: a[..., o[i]:o[i + 1]]
    order = (0, 1, 2, 3, 5, 8)
    w_slab = jnp.concatenate([seg(w_in, i) for i in order], axis=1).astype(BF16)
    b_slab = jnp.concatenate([seg(b_in, i) for i in order])[None, :]
    scale = np.ones((1, SLAB_W), np.float32)
    scale[0, SLAB_QA:SLAB_QA + QA_W] = DK_A ** -0.5
    scale[0, SLAB_QB:SLAB_QB + QB_W] = D_B ** -0.5 * LOG2E
    w_lr = jnp.zeros((D_MODEL, LANES), F32).at[:, :LR_W].set(seg(w_in, 4)).astype(BF16)
    b_lr = jnp.zeros((1, LANES), F32).at[0, :LR_W].set(seg(b_in, 4))
    w_al = jnp.zeros((LANES, QA_W), F32).at[:LR_W].set(w_alpha).astype(BF16)
    w_r = jnp.zeros((D_MODEL, LANES), F32).at[:, :N_EXPERTS].set(w_router)
    w_r_hi = w_r.astype(BF16)
    w_r_lo = (w_r - w_r_hi.astype(F32)).astype(BF16)
    b_r = jnp.full((1, LANES), NEG_INF, F32).at[0, :N_EXPERTS].set(b_router)
    return dict(
        w_slab=w_slab, b_slab=b_slab, s_slab=jnp.asarray(scale),
        w_lr=w_lr, b_lr=b_lr, w_alpha=w_al, b_alpha=b_alpha[None, :],
        w_k=seg(w_in, 6).astype(BF16), w_v=seg(w_in, 7).astype(BF16),
        b_k=seg(b_in, 6)[None, :], b_v=seg(b_in, 7)[None, :],
        gla_g=gla_norm_g[None, :], diff_g=diff_norm_g[None, :],
        w_pa=w_pa.astype(BF16), w_pb=w_pb.astype(BF16), w_o=w_o.astype(BF16),
        ln1_g=ln1_g[None, :], ln1_b=ln1_b[None, :],
        w_r_hi=w_r_hi, w_r_lo=w_r_lo, b_r=b_r,
        w_gu=w_gu.astype(BF16), b_gu=b_gu[:, None, :], w_down=w_down.astype(BF16), b_down=b_down[:, None, :],
        ln2_g=ln2_g[None, :], ln2_b=ln2_b[None, :],
    )


def _layer(x, p, dl, lam_init, s0, cache_k, cache_v):
    batch, t_len, _ = x.shape
    x2 = x.reshape(batch * t_len, D_MODEL)
    slab, loga, k_rows, kb, vb = _proj(x2, p)
    oa, s_new = _gla(slab, loga, s0, p["gla_g"], batch, t_len)
    if cache_k is None:
        ob = _attn_prompt(slab, kb, vb, p["diff_g"], dl, batch, t_len, lam_init)
    else:
        ob = _attn_sample(slab, kb, vb, cache_k, cache_v, p["diff_g"], dl, batch, t_len, lam_init)
    hh, hp, top_idx, top_w, counts = _finish(x2, oa, slab, ob, p)
    y = _moe(hh, hp, top_idx, top_w, counts, p)
    return (y.reshape(batch, t_len, D_MODEL), s_new,
            k_rows.reshape(batch, t_len, H_B, 2, D_B), vb.reshape(batch, t_len, H_B, 2 * D_B))


def kernel(x_prompt, x_sample, cache_k, cache_v, state_gla, w_in, b_in, w_alpha, b_alpha, gla_norm_g,
           diff_lambda, diff_norm_g, w_pa, w_pb, w_o, ln1_g, ln1_b, w_router, b_router, w_gu, b_gu,
           w_down, b_down, ln2_g, ln2_b):
    assert w_in.shape[0] == DEPTH == 1
    l = 0
    lam_init = 0.8 - 0.6 * math.exp(-0.3 * l)
    p = _prep_params(w_in[l], b_in[l], w_alpha[l], b_alpha[l], gla_norm_g[l], diff_norm_g[l], w_pa[l],
                     w_pb[l], w_o[l], ln1_g[l], ln1_b[l], w_router[l], b_router[l], w_gu[l], b_gu[l],
                     w_down[l], b_down[l], ln2_g[l], ln2_b[l])
    dl = diff_lambda[l]
    bp = x_prompt.shape[0]
    bs, ts = x_sample.shape[0], x_sample.shape[1]
    past = cache_k.shape[2]
    yp, s_p, k_p, v_p = _layer(x_prompt, p, dl, lam_init, jnp.zeros((bp, H_A, DK_A, DV_A), F32), None, None)
    ck = cache_k[l].reshape(bs, past, KB_W)
    cv = cache_v[l].reshape(bs, past, VB_W)
    ys, s_s, k_s, v_s = _layer(x_sample, p, dl, lam_init, state_gla[l], ck, cv)
    return (yp, ys, s_p[None], k_p[None], v_p[None], s_s[None], k_s[None], v_s[None])
```

```python
import functools
import math

import numpy as np
import jax
import jax.numpy as jnp
from jax import lax
from jax.experimental import pallas as pl
from jax.experimental.pallas import tpu as pltpu

F32 = jnp.float32
BF16 = jnp.bfloat16
U32 = jnp.uint32
I32 = jnp.int32

D_MODEL = 1024
CHUNK = 64
H_A = 4
DK_A = 128
DV_A = 256
GATE_RANK = 16
GATE_TAU = 16.0
H_B = 8
D_B = 64
N_EXPERTS = 32
TOP_K = 4
D_FF = D_MODEL
SWIGLU_LIMIT = 7.0
SWIGLU_ALPHA = 1.702
EPS = 1e-5
DEPTH = 1
DEEPNORM_ALPHA = (2.0 * DEPTH) ** 0.25

QA_W = H_A * DK_A
KA_W = H_A * DK_A
VA_W = H_A * DV_A
RA_W = H_A * DV_A
LR_W = GATE_RANK
QB_W = H_B * 2 * D_B
KB_W = H_B * 2 * D_B
VB_W = H_B * 2 * D_B
GT_W = 2 * D_MODEL
IN_SIZES = (QA_W, KA_W, VA_W, RA_W, LR_W, QB_W, KB_W, VB_W, GT_W)
IN_OFFS = tuple(int(v) for v in np.cumsum((0,) + IN_SIZES))

SLAB_QA, SLAB_KA, SLAB_VA, SLAB_RA, SLAB_QB, SLAB_GT = 0, 512, 1024, 2048, 3072, 4096
SLAB_W = 6144
LANES = 128
SUBLANES = 8
EXPERT_BM = 512
VMEM_LIMIT = 56 * 1024 * 1024

NEG_INF = float("-inf")
LOG2E = math.log2(math.e)


def _cparams(sem):
    return pltpu.CompilerParams(dimension_semantics=sem, vmem_limit_bytes=VMEM_LIMIT)


def _sigmoid(x):
    return 1.0 / (1.0 + jnp.exp(-x))


def _pack_bf16_pairs(x):
    n = x.shape[1] // 2
    xb = x.astype(BF16).astype(F32)
    lo = pltpu.bitcast(xb[:, :n], U32) >> 16
    hi = pltpu.bitcast(xb[:, n:], U32) & jnp.uint32(0xFFFF0000)
    return hi | lo


def _unpack_bf16_pairs(r):
    lo = pltpu.bitcast(r << 16, F32)
    hi = pltpu.bitcast(r & jnp.uint32(0xFFFF0000), F32)
    return jnp.concatenate([lo, hi], axis=1)


def _layer_norm(y, g, b):
    mu = jnp.mean(y, axis=-1, keepdims=True)
    d = y - mu
    var = jnp.mean(d * d, axis=-1, keepdims=True)
    return d * lax.rsqrt(var + EPS) * g + b


def _proj_kernel(x_ref, w_ref, b_ref, s_ref, wlr_ref, blr_ref, wal_ref, bal_ref, wk_ref, wv_ref, bk_ref, bv_ref,
                 slab_ref, loga_ref, krows_ref, kb_ref, v_ref, *, tn):
    xb = x_ref[...].astype(BF16)
    lra = jnp.dot(xb, wlr_ref[...], preferred_element_type=F32) + blr_ref[...]
    z = jnp.dot(lra.astype(BF16), wal_ref[...], preferred_element_type=F32) + bal_ref[...]
    loga_ref[...] = (jnp.minimum(z, 0.0) - jnp.log(1.0 + jnp.exp(-jnp.abs(z)))) * (1.0 / GATE_TAU)
    for c0 in range(0, SLAB_W, tn):
        acc = jnp.dot(xb, w_ref[:, c0:c0 + tn], preferred_element_type=F32)
        slab_ref[:, c0:c0 + tn] = ((acc + b_ref[:, c0:c0 + tn]) * s_ref[:, c0:c0 + tn]).astype(BF16)
    k = jnp.dot(xb, wk_ref[...], preferred_element_type=F32) + bk_ref[...]
    kb_ref[...] = k.astype(BF16)
    for h in range(H_B):
        for j in range(2):
            c0 = (2 * h + j) * D_B
            krows_ref[:, h, j, :] = k[:, c0:c0 + D_B]
    v_ref[...] = jnp.dot(xb, wv_ref[...], preferred_element_type=F32) + bv_ref[...]


def _proj(x2, p):
    n = x2.shape[0]
    tm = min(512, n)
    const = lambda i: (0, 0)
    whole = lambda shape: pl.BlockSpec(shape, const, pipeline_mode=pl.Buffered(1))
    row = lambda w: pl.BlockSpec((tm, w), lambda i: (i, 0))
    return pl.pallas_call(
        functools.partial(_proj_kernel, tn=1024),
        grid=(n // tm,),
        in_specs=[
            row(D_MODEL),
            whole((D_MODEL, SLAB_W)), whole((1, SLAB_W)), whole((1, SLAB_W)),
            whole((D_MODEL, LANES)), whole((1, LANES)), whole((LANES, QA_W)), whole((1, QA_W)),
            whole((D_MODEL, KB_W)), whole((D_MODEL, VB_W)), whole((1, KB_W)), whole((1, VB_W)),
        ],
        out_specs=[
            row(SLAB_W), row(QA_W),
            pl.BlockSpec((tm, H_B, 2, D_B), lambda i: (i, 0, 0, 0)),
            row(KB_W), row(VB_W),
        ],
        out_shape=[
            jax.ShapeDtypeStruct((n, SLAB_W), BF16),
            jax.ShapeDtypeStruct((n, QA_W), F32),
            jax.ShapeDtypeStruct((n, H_B, 2, D_B), F32),
            jax.ShapeDtypeStruct((n, KB_W), BF16),
            jax.ShapeDtypeStruct((n, VB_W), F32),
        ],
        compiler_params=_cparams(("parallel",)),
        name="proj",
    )(x2, p["w_slab"], p["b_slab"], p["s_slab"], p["w_lr"], p["b_lr"], p["w_alpha"], p["b_alpha"],
      p["w_k"], p["w_v"], p["b_k"], p["b_v"])


def _gla_tables(L):
    nl = int(math.log2(L))
    t = np.arange(L)
    D = np.zeros(((nl + 2) * L, L), np.float32)
    masks = np.zeros((nl + 1, L, L), np.float32)
    for l in range(nl):
        m = L >> (l + 1)
        grp = t // (2 * m)
        mid = grp * 2 * m + m - 1
        upper = (t % (2 * m)) >= m
        for r in range(L):
            if upper[r]:
                D[l * L + r, mid[r] + 1:r + 1] = 1.0
            else:
                D[l * L + r, r + 1:mid[r] + 1] = 1.0
        masks[l] = (upper[:, None] & ~upper[None, :] & (grp[:, None] == grp[None, :])).astype(np.float32)
    D[nl * L:(nl + 1) * L] = np.tril(np.ones((L, L), np.float32))
    D[(nl + 1) * L:] = np.triu(np.ones((L, L), np.float32), 1)
    masks[nl] = np.eye(L, dtype=np.float32)
    return jnp.asarray(D, BF16), jnp.asarray(masks, F32)


def _gla_kernel(q_ref, k_ref, v_ref, la_ref, s0_ref, g_ref, d_ref, m_ref, o_ref, s_ref, *, L, n_chunks):
    c = pl.program_id(1)
    nl = int(math.log2(L))

    @pl.when(c == 0)
    def _():
        s_ref[...] = s0_ref[...]

    dmat = d_ref[...]
    ones_col = jnp.ones((L, LANES), BF16)
    g = g_ref[...]
    nt = (((1,), (1,)), ((), ()))
    tn = (((0,), (0,)), ((), ()))

    def chunk(ci, carry):
        r0 = pl.multiple_of(ci * L, L)
        rows = pl.ds(r0, L)
        for h in range(H_A):
            kc = slice(h * DK_A, (h + 1) * DK_A)
            vc = slice(h * DV_A, (h + 1) * DV_A)
            q = q_ref[rows, kc].astype(F32)
            k = k_ref[rows, kc].astype(F32)
            v = v_ref[rows, vc]
            la = la_ref[rows, kc]
            la_hi = la.astype(BF16)
            la_lo = (la - la_hi.astype(F32)).astype(BF16)
            e = (jnp.dot(dmat, la_hi, preferred_element_type=F32)
                 + jnp.dot(dmat, la_lo, preferred_element_type=F32))
            x = jnp.exp(e)
            a = m_ref[nl] * lax.dot_general(q.astype(BF16), k.astype(BF16), nt, preferred_element_type=F32)
            for l in range(nl):
                xl = x[l * L:(l + 1) * L]
                a = a + m_ref[l] * lax.dot_general((q * xl).astype(BF16), (k * xl).astype(BF16), nt,
                                                   preferred_element_type=F32)
            xb = x[nl * L:(nl + 1) * L]
            xs = x[(nl + 1) * L:]
            s_old = s_ref[0, h]
            o = (jnp.dot((q * xb).astype(BF16), s_old.astype(BF16), preferred_element_type=F32)
                 + jnp.dot(a.astype(BF16), v, preferred_element_type=F32))
            bl = (lax.dot_general(la_hi, ones_col, tn, preferred_element_type=F32)
                  + lax.dot_general(la_lo, ones_col, tn, preferred_element_type=F32))
            dec = jnp.exp(bl[:, 0:1])
            s_ref[0, h] = dec * s_old + lax.dot_general((k * xs).astype(BF16), v, tn, preferred_element_type=F32)
            ms = jnp.mean(o * o, axis=-1, keepdims=True)
            o_ref[rows, vc] = (o * lax.rsqrt(ms + EPS) * g).astype(BF16)
        return carry

    lax.fori_loop(0, n_chunks, chunk, 0, unroll=2 if n_chunks % 2 == 0 else 1)


def _gla(slab, loga, s0, g_norm, batch, t_len):
    L = min(CHUNK, t_len)
    tb = min(512, t_len)
    nb = t_len // tb
    dmat, masks = _gla_tables(L)
    kern = functools.partial(_gla_kernel, L=L, n_chunks=tb // L)
    row = lambda b, c: b * nb + c
    return pl.pallas_call(
        kern,
        grid=(batch, nb),
        in_specs=[
            pl.BlockSpec((tb, QA_W), lambda b, c: (row(b, c), SLAB_QA // QA_W)),
            pl.BlockSpec((tb, KA_W), lambda b, c: (row(b, c), SLAB_KA // KA_W)),
            pl.BlockSpec((tb, VA_W), lambda b, c: (row(b, c), SLAB_VA // VA_W)),
            pl.BlockSpec((tb, QA_W), lambda b, c: (row(b, c), 0)),
            pl.BlockSpec((1, H_A, DK_A, DV_A), lambda b, c: (b, 0, 0, 0)),
            pl.BlockSpec((1, DV_A), lambda b, c: (0, 0)),
            pl.BlockSpec(dmat.shape, lambda b, c: (0, 0)),
            pl.BlockSpec(masks.shape, lambda b, c: (0, 0, 0)),
        ],
        out_specs=[
            pl.BlockSpec((tb, VA_W), lambda b, c: (row(b, c), 0)),
            pl.BlockSpec((1, H_A, DK_A, DV_A), lambda b, c: (b, 0, 0, 0)),
        ],
        out_shape=[
            jax.ShapeDtypeStruct((batch * t_len, VA_W), BF16),
            jax.ShapeDtypeStruct((batch, H_A, DK_A, DV_A), F32),
        ],
        compiler_params=_cparams(("parallel", "arbitrary")),
        name="gla",
    )(slab, slab, slab, loga, s0, g_norm, dmat, masks)


def _lambda_from(dl_ref, lam_init):
    dl = dl_ref[...]
    a = jnp.sum(dl[0:1] * dl[1:2], axis=-1, keepdims=True)
    b = jnp.sum(dl[2:3] * dl[3:4], axis=-1, keepdims=True)
    return jnp.exp(a) - jnp.exp(b) + lam_init


def _split_maps(q):
    lane = lax.broadcasted_iota(I32, q.shape, 1)
    zero = jnp.zeros_like(q)
    return jnp.concatenate([jnp.where(lane < D_B, q, zero), jnp.where(lane >= D_B, q, zero)], axis=0)


def _alibi_coef(h):
    c = jnp.full((1, 1), LOG2E, F32) * jnp.exp2(-(h + 1).astype(F32))
    c_hi = c.astype(BF16).astype(F32)
    return c, c_hi, c - c_hi


def _finish_heads(acc, inv_l, lam, g, tq, lam_init):
    o = acc * inv_l
    out = o[:tq] - lam * o[tq:]
    ms = jnp.mean(out * out, axis=-1, keepdims=True)
    return (out * lax.rsqrt(ms + EPS) * g * (1.0 - lam_init)).astype(BF16)


def _attn_prompt_kernel(q_ref, k_ref, v_ref, g_ref, dl_ref, o_ref, ka_ref, vb_ref, qaug_ref, d_ref, m_ref,
                        acc_ref, s0_ref, s1_ref, p0_ref, p1_ref,
                        *, tq, tk, lam_init):
    h = pl.program_id(1)
    qi = pl.program_id(2)
    c, c_hi, c_lo = _alibi_coef(h)
    hw = 2 * D_B

    def aug_lanes(shape, pos, sign):
        lane = lax.broadcasted_iota(I32, shape, 1)
        r = (pos & 255).astype(F32) * sign
        a = (pos >> 8).astype(F32) * sign
        coef = jnp.where(lane == 0, c_hi, jnp.where(lane == 1, c_lo,
                         jnp.where(lane == 2, 256.0 * c_hi, jnp.where(lane == 3, 256.0 * c_lo, 0.0))))
        ints = jnp.where(lane < 2, r, jnp.where(lane < 4, a, 0.0))
        return coef, ints, lane

    @pl.when(qi == 0)
    def _():
        t_len = vb_ref.shape[0]
        lane = lax.broadcasted_iota(I32, (t_len, hw), 1)
        row = lax.broadcasted_iota(I32, (hw, tk), 0)
        j_rel = lax.broadcasted_iota(I32, (hw, tk), 1)
        k_aug_t = jnp.where(row < 2, (j_rel & 255).astype(F32), jnp.where(row < 4, (j_rel >> 8).astype(F32),
                  jnp.where(row == 4, c_hi, jnp.where(row == 5, c_lo,
                  jnp.where(row == 6, 256.0 * c_hi, jnp.where(row == 7, 256.0 * c_lo, 0.0)))))).astype(BF16)
        for blk in range(t_len // tk):
            rows = slice(blk * tk, (blk + 1) * tk)
            ka_ref[blk, :hw, :] = k_ref[0, rows, :].astype(F32).T.astype(BF16)
            ka_ref[blk, hw:, :] = k_aug_t
        vb_ref[:, :hw] = v_ref[0].astype(BF16)
        vb_ref[:, hw:] = jnp.where(lane == 0, 1.0, 0.0).astype(BF16)
        i_rel = lax.broadcasted_iota(I32, (2 * tq, hw), 0) & (tq - 1)
        coef, ints, lane = aug_lanes((2 * tq, hw), i_rel, -1.0)
        qaug_ref[...] = jnp.where(lane < 4, coef, pltpu.roll(ints, 4, 1)).astype(BF16)
        i = lax.broadcasted_iota(I32, (2 * tq, tq), 0) & (tq - 1)
        j = lax.broadcasted_iota(I32, (2 * tq, tq), 1)
        fwd = jnp.maximum(j - i, 0).astype(F32)
        d_ref[...] = jnp.where((j >> 6) <= (i >> 6), -2.0 * c * fwd, NEG_INF)

    qa = jnp.concatenate([_split_maps(q_ref[...]), qaug_ref[...]], axis=1)
    m_ref[...] = jnp.full(m_ref.shape, -1e30, F32)

    def scores(kv):
        return jnp.dot(qa, ka_ref[kv], preferred_element_type=F32)

    def weighted_values(p_ref, kv):
        k0 = pl.multiple_of(kv * tk, tk)
        return jnp.dot(p_ref[...], vb_ref[pl.ds(k0, tk), :], preferred_element_type=F32)

    def softmax_step(s, off):
        m_old = m_ref[...]
        m_new = jnp.maximum(m_old, jnp.max(s, axis=-1, keepdims=True) - off)
        shift = m_new + off
        p = jnp.exp2(s - jnp.concatenate([shift] * (tk // hw), axis=1))
        alpha = jnp.exp2(m_old - m_new)
        m_ref[...] = m_new
        return p.astype(BF16), jnp.concatenate([alpha, alpha], axis=1)

    p_diag, _ = softmax_step(scores(qi) + d_ref[...], jnp.zeros((1, 1), F32))
    p1_ref[...] = p_diag
    acc_ref[...] = jnp.zeros(acc_ref.shape, F32)

    @pl.when(qi > 0)
    def _():
        s0_ref[...] = scores(0)

    def step(kv, s_cur, s_nxt, p_prev, p_cur):
        pv_prev = weighted_values(p_prev, jnp.where(kv == 0, qi, kv - 1))
        s_nxt[...] = scores(jnp.minimum(kv + 1, qi - 1))
        p, alpha = softmax_step(s_cur[...], c * (qi * tq - kv * tk).astype(F32))
        acc_ref[...] = alpha * (acc_ref[...] + pv_prev)
        p_cur[...] = p

    def pair(j, carry):
        step(2 * j, s0_ref, s1_ref, p1_ref, p0_ref)
        step(2 * j + 1, s1_ref, s0_ref, p0_ref, p1_ref)
        return carry

    lax.fori_loop(0, qi // 2, pair, 0)

    @pl.when(qi % 2 == 1)
    def _():
        step(qi - 1, s0_ref, s1_ref, p1_ref, p0_ref)
        acc_ref[...] += weighted_values(p0_ref, qi - 1)

    @pl.when(qi % 2 == 0)
    def _():
        acc_ref[...] += weighted_values(p1_ref, jnp.maximum(qi - 1, 0))

    lam = _lambda_from(dl_ref, lam_init)
    acc = acc_ref[...]
    o_ref[...] = _finish_heads(acc[:, :hw], 1.0 / acc[:, hw:hw + 1], lam, g_ref[...], tq, lam_init)


def _attn_prompt(slab, kb, vb, g_norm, dl, batch, t_len, lam_init):
    tq = min(512, t_len)
    tk = tq
    assert tq % CHUNK == 0 and CHUNK == 64
    nq = t_len // tq
    k3 = kb.reshape(batch, t_len, KB_W)
    v3 = vb.reshape(batch, t_len, VB_W)
    kern = functools.partial(_attn_prompt_kernel, tq=tq, tk=tk, lam_init=lam_init)
    hw = 2 * D_B
    return pl.pallas_call(
        kern,
        grid=(batch, H_B, nq),
        in_specs=[
            pl.BlockSpec((tq, hw), lambda b, h, q: (b * nq + q, SLAB_QB // hw + h)),
            pl.BlockSpec((1, t_len, hw), lambda b, h, q: (b, 0, h)),
            pl.BlockSpec((1, t_len, hw), lambda b, h, q: (b, 0, h)),
            pl.BlockSpec((1, hw), lambda b, h, q: (0, 0)),
            pl.BlockSpec((4, D_B), lambda b, h, q: (0, 0)),
        ],
        out_specs=pl.BlockSpec((tq, hw), lambda b, h, q: (b * nq + q, h)),
        out_shape=jax.ShapeDtypeStruct((batch * t_len, VB_W), BF16),
        scratch_shapes=[
            pltpu.VMEM((t_len // tk, 2 * hw, tk), BF16),
            pltpu.VMEM((t_len, 2 * hw), BF16),
            pltpu.VMEM((2 * tq, hw), BF16),
            pltpu.VMEM((2 * tq, tq), F32),
            pltpu.VMEM((2 * tq, hw), F32),
            pltpu.VMEM((2 * tq, 2 * hw), F32),
            pltpu.VMEM((2 * tq, tk), F32),
            pltpu.VMEM((2 * tq, tk), F32),
            pltpu.VMEM((2 * tq, tk), BF16),
            pltpu.VMEM((2 * tq, tk), BF16),
        ],
        compiler_params=_cparams(("parallel", "parallel", "arbitrary")),
        name="attn_prompt",
    )(slab, k3, v3, g_norm, dl)


def _attn_sample_kernel(q_ref, kc_ref, vc_ref, kn_ref, vn_ref, g_ref, dl_ref, o_ref, *, tq, past, lam_init):
    h = pl.program_id(1)
    c, _, _ = _alibi_coef(h)
    qq = _split_maps(q_ref[...])
    q_pos = past + lax.broadcasted_iota(I32, (2 * tq, 1), 0) % tq

    def scores(k, k_pos):
        s = lax.dot_general(qq, k.astype(BF16), (((1,), (1,)), ((), ())), preferred_element_type=F32)
        s = s - c * jnp.abs(q_pos - k_pos).astype(F32)
        return jnp.where((k_pos // CHUNK) <= (q_pos // CHUNK), s, NEG_INF)

    sc = scores(kc_ref[0], lax.broadcasted_iota(I32, (1, past), 1))
    sn = scores(kn_ref[0], past + lax.broadcasted_iota(I32, (1, tq), 1))
    m = jnp.maximum(jnp.max(sc, axis=-1, keepdims=True), jnp.max(sn, axis=-1, keepdims=True))
    pc = jnp.exp2(sc - m)
    pn = jnp.exp2(sn - m)
    l = jnp.sum(pc, axis=-1, keepdims=True) + jnp.sum(pn, axis=-1, keepdims=True)
    acc = (jnp.dot(pc.astype(BF16), vc_ref[0].astype(BF16), preferred_element_type=F32)
           + jnp.dot(pn.astype(BF16), vn_ref[0].astype(BF16), preferred_element_type=F32))
    lam = _lambda_from(dl_ref, lam_init)
    o_ref[...] = _finish_heads(acc, 1.0 / l, lam, g_ref[...], tq, lam_init)


def _attn_sample(slab, kb, vb, cache_k, cache_v, g_norm, dl, batch, t_len, lam_init):
    past = cache_k.shape[1]
    hw = 2 * D_B
    k3 = kb.reshape(batch, t_len, KB_W)
    v3 = vb.reshape(batch, t_len, VB_W)
    kern = functools.partial(_attn_sample_kernel, tq=t_len, past=past, lam_init=lam_init)
    return pl.pallas_call(
        kern,
        grid=(batch, H_B),
        in_specs=[
            pl.BlockSpec((t_len, hw), lambda b, h: (b, SLAB_QB // hw + h)),
            pl.BlockSpec((1, past, hw), lambda b, h: (b, 0, h)),
            pl.BlockSpec((1, past, hw), lambda b, h: (b, 0, h)),
            pl.BlockSpec((1, t_len, hw), lambda b, h: (b, 0, h)),
            pl.BlockSpec((1, t_len, hw), lambda b, h: (b, 0, h)),
            pl.BlockSpec((1, hw), lambda b, h: (0, 0)),
            pl.BlockSpec((4, D_B), lambda b, h: (0, 0)),
        ],
        out_specs=pl.BlockSpec((t_len, hw), lambda b, h: (b, h)),
        out_shape=jax.ShapeDtypeStruct((batch * t_len, VB_W), BF16),
        compiler_params=_cparams(("parallel", "parallel")),
        name="attn_sample",
    )(slab, cache_k, cache_v, k3, v3, g_norm, dl)


def _finish_kernel(x_ref, oa_ref, ra_ref, ob_ref, gt_ref, wpa_ref, wpb_ref, wo_ref, g_ref, b_ref,
                   wrh_ref, wrl_ref, br_ref, h_ref, hp_ref, idx_ref, tw_ref, cnt_ref):
    i = pl.program_id(0)
    ra = ra_ref[...].astype(F32)
    ua = ra * _sigmoid(ra) * oa_ref[...].astype(F32)
    ya = jnp.dot(ua.astype(BF16), wpa_ref[...], preferred_element_type=F32)
    yb = jnp.dot(ob_ref[...], wpb_ref[...], preferred_element_type=F32)
    gt = gt_ref[...].astype(F32)
    mixed = _sigmoid(gt[:, :D_MODEL]) * ya + _sigmoid(gt[:, D_MODEL:]) * yb
    mix = jnp.dot(mixed.astype(BF16), wo_ref[...], preferred_element_type=F32)
    hh = _layer_norm(DEEPNORM_ALPHA * x_ref[...] + mix, g_ref[...], b_ref[...])
    h_ref[...] = hh
    hp_ref[...] = _pack_bf16_pairs(hh)

    h_hi = hh.astype(BF16)
    h_lo = (hh - h_hi.astype(F32)).astype(BF16)
    lg = (jnp.dot(h_hi, wrh_ref[...], preferred_element_type=F32)
          + jnp.dot(h_lo, wrh_ref[...], preferred_element_type=F32)
          + jnp.dot(h_hi, wrl_ref[...], preferred_element_type=F32)) + br_ref[...]
    tm = lg.shape[0]
    lane = lax.broadcasted_iota(I32, (tm, LANES), 1)
    lane_f = lane.astype(F32)
    work = lg
    vals, idxs = [], []
    cnt = jnp.zeros((tm, LANES), F32)
    for _ in range(TOP_K):
        mx = jnp.max(work, axis=-1, keepdims=True)
        ix = jnp.min(jnp.where(work == mx, lane_f, float(LANES)), axis=-1, keepdims=True)
        hit = lane_f == ix
        cnt = cnt + hit.astype(F32)
        work = jnp.where(hit, NEG_INF, work)
        vals.append(mx)
        idxs.append(ix)
    es = [jnp.exp(v - vals[0]) for v in vals]
    den = es[0] + es[1] + es[2] + es[3]
    idx_full = jnp.zeros((tm, LANES), F32)
    tw_full = jnp.zeros((tm, LANES), F32)
    for r in range(TOP_K):
        idx_full = jnp.where(lane == r, idxs[r], idx_full)
        tw_full = jnp.where(lane == r, es[r] / den, tw_full)
    idx_ref[...] = idx_full.T[:SUBLANES].astype(I32)
    tw_ref[...] = tw_full[:, :TOP_K]

    @pl.when(i == 0)
    def _():
        cnt_ref[...] = jnp.zeros(cnt_ref.shape, F32)

    cnt_ref[...] += jnp.sum(cnt, axis=0, keepdims=True)


def _finish(x2, oa, slab, ob, p):
    n = x2.shape[0]
    tm = min(512, n)
    const = lambda i: (0, 0)
    return pl.pallas_call(
        _finish_kernel,
        grid=(n // tm,),
        in_specs=[
            pl.BlockSpec((tm, D_MODEL), lambda i: (i, 0)),
            pl.BlockSpec((tm, VA_W), lambda i: (i, 0)),
            pl.BlockSpec((tm, RA_W), lambda i: (i, SLAB_RA // RA_W)),
            pl.BlockSpec((tm, VB_W), lambda i: (i, 0)),
            pl.BlockSpec((tm, GT_W), lambda i: (i, SLAB_GT // GT_W)),
            pl.BlockSpec((VA_W, D_MODEL), const),
            pl.BlockSpec((VB_W, D_MODEL), const),
            pl.BlockSpec((D_MODEL, D_MODEL), const),
            pl.BlockSpec((1, D_MODEL), const),
            pl.BlockSpec((1, D_MODEL), const),
            pl.BlockSpec((D_MODEL, LANES), const),
            pl.BlockSpec((D_MODEL, LANES), const),
            pl.BlockSpec((1, LANES), const),
        ],
        out_specs=[
            pl.BlockSpec((tm, D_MODEL), lambda i: (i, 0)),
            pl.BlockSpec((tm, D_MODEL // 2), lambda i: (i, 0)),
            pl.BlockSpec((SUBLANES, tm), lambda i: (0, i)),
            pl.BlockSpec((tm, TOP_K), lambda i: (i, 0)),
            pl.BlockSpec((1, LANES), const),
        ],
        out_shape=[
            jax.ShapeDtypeStruct((n, D_MODEL), F32),
            jax.ShapeDtypeStruct((n, D_MODEL // 2), U32),
            jax.ShapeDtypeStruct((SUBLANES, n), I32),
            jax.ShapeDtypeStruct((n, TOP_K), F32),
            jax.ShapeDtypeStruct((1, LANES), F32),
        ],
        compiler_params=_cparams(("arbitrary",)),
        name="finish",
    )(x2, oa, slab, ob, slab, p["w_pa"], p["w_pb"], p["w_o"], p["ln1_g"], p["ln1_b"],
      p["w_r_hi"], p["w_r_lo"], p["b_r"])


def _dest_kernel(idx_ref, ps_ref, ut_ref, dest_ref, carry_ref):
    i = pl.program_id(0)

    @pl.when(i == 0)
    def _():
        carry_ref[...] = jnp.zeros(carry_ref.shape, F32)

    tm = idx_ref.shape[1]
    expert = lax.broadcasted_iota(I32, (N_EXPERTS, tm), 0)
    idx = idx_ref[...]
    hits = [expert == idx[k:k + 1, :] for k in range(TOP_K)]
    cnt = jnp.zeros((N_EXPERTS, tm), F32)
    for hit in hits:
        cnt = cnt + hit.astype(F32)
    before = jnp.dot(cnt.astype(BF16), ut_ref[...], preferred_element_type=F32)
    base = before + carry_ref[:, 0:1] + ps_ref[:, 0:1]
    row = lax.broadcasted_iota(I32, (SUBLANES, tm), 0)
    dest = jnp.zeros((SUBLANES, tm), F32)
    for k in range(TOP_K):
        d = jnp.sum(jnp.where(hits[k], base, 0.0), axis=0, keepdims=True)
        dest = jnp.where(row == k, d, dest)
    dest_ref[...] = dest.astype(I32)
    carry_ref[...] += jnp.sum(cnt, axis=1, keepdims=True)


def _dest(idx_t, pstart):
    n = idx_t.shape[1]
    tm = min(512, n)
    utri = jnp.asarray(np.triu(np.ones((tm, tm), np.float32), 1), BF16)
    return pl.pallas_call(
        _dest_kernel,
        grid=(n // tm,),
        in_specs=[
            pl.BlockSpec((SUBLANES, tm), lambda i: (0, i)),
            pl.BlockSpec((N_EXPERTS, LANES), lambda i: (0, 0)),
            pl.BlockSpec((tm, tm), lambda i: (0, 0)),
        ],
        out_specs=pl.BlockSpec((SUBLANES, tm), lambda i: (0, i)),
        out_shape=jax.ShapeDtypeStruct((SUBLANES, n), I32),
        scratch_shapes=[pltpu.VMEM((N_EXPERTS, LANES), F32)],
        compiler_params=_cparams(("arbitrary",)),
        name="dest",
    )(idx_t, pstart, utri)


def _scatter_kernel(dest_ref, hp_ref, rows_in_ref, rows_ref, sem):
    del rows_in_ref
    tm = hp_ref.shape[0]

    def row_copy(t, d):
        return pltpu.make_async_copy(hp_ref.at[pl.ds(t, 1)], rows_ref.at[pl.ds(d, 1)], sem)

    def issue(t, carry):
        for k in range(TOP_K):
            row_copy(t, dest_ref[k, t]).start(priority=k % 2)
        return carry

    lax.fori_loop(0, tm, issue, 0, unroll=4)
    for k in range(TOP_K):
        pltpu.make_async_copy(hp_ref, rows_ref.at[pl.ds(0, tm)], sem).wait()


def _scatter(dest_t, hp, n_rows):
    n = hp.shape[0]
    tm = min(512, n)
    rows0 = jnp.zeros((n_rows, D_MODEL // 2), U32)
    return pl.pallas_call(
        _scatter_kernel,
        grid=(n // tm,),
        in_specs=[
            pl.BlockSpec((SUBLANES, tm), lambda i: (0, i), memory_space=pltpu.SMEM),
            pl.BlockSpec((tm, D_MODEL // 2), lambda i: (i, 0)),
            pl.BlockSpec(memory_space=pl.ANY),
        ],
        out_specs=pl.BlockSpec(memory_space=pl.ANY),
        out_shape=jax.ShapeDtypeStruct((n_rows, D_MODEL // 2), U32),
        scratch_shapes=[pltpu.SemaphoreType.DMA(())],
        input_output_aliases={2: 0},
        compiler_params=_cparams(("arbitrary",)),
        name="scatter",
    )(dest_t, hp, rows0)


def _expert_kernel(be_ref, nu_ref, rows_ref, wgu_ref, bgu_ref, wd_ref, bd_ref, out_ref):
    del be_ref
    i = pl.program_id(0)

    @pl.when(i < nu_ref[0])
    def _():
        x = _unpack_bf16_pairs(rows_ref[...]).astype(BF16)
        gu = jnp.dot(x, wgu_ref[0], preferred_element_type=F32) + bgu_ref[0]
        gate = jnp.minimum(gu[:, :D_FF], SWIGLU_LIMIT)
        up = jnp.clip(gu[:, D_FF:], -SWIGLU_LIMIT, SWIGLU_LIMIT)
        hidden = (up + 1.0) * (gate * _sigmoid(SWIGLU_ALPHA * gate))
        o = jnp.dot(hidden.astype(BF16), wd_ref[0], preferred_element_type=F32) + bd_ref[0]
        out_ref[...] = _pack_bf16_pairs(o)

    @pl.when(i >= nu_ref[0])
    def _():
        out_ref[...] = jnp.zeros(out_ref.shape, U32)


def _experts(rows, block_e, n_used, p):
    n_rows = rows.shape[0]
    bm = EXPERT_BM
    n_blocks = n_rows // bm
    grid_spec = pltpu.PrefetchScalarGridSpec(
        num_scalar_prefetch=2,
        grid=(n_blocks,),
        in_specs=[
            pl.BlockSpec((bm, D_MODEL // 2), lambda i, be, nu: (i, 0)),
            pl.BlockSpec((1, D_MODEL, 2 * D_FF), lambda i, be, nu: (be[i], 0, 0)),
            pl.BlockSpec((1, 1, 2 * D_FF), lambda i, be, nu: (be[i], 0, 0)),
            pl.BlockSpec((1, D_FF, D_MODEL), lambda i, be, nu: (be[i], 0, 0)),
            pl.BlockSpec((1, 1, D_MODEL), lambda i, be, nu: (be[i], 0, 0)),
        ],
        out_specs=pl.BlockSpec((bm, D_MODEL // 2), lambda i, be, nu: (i, 0)),
    )
    return pl.pallas_call(
        _expert_kernel,
        grid_spec=grid_spec,
        out_shape=jax.ShapeDtypeStruct((n_rows, D_MODEL // 2), U32),
        compiler_params=_cparams(("arbitrary",)),
        name="experts",
    )(block_e, n_used, rows, p["w_gu"], p["b_gu"], p["w_down"], p["b_down"])


def _combine_kernel(dest_ref, tw_ref, h_ref, rows_ref, g_ref, b_ref, y_ref, buf_ref, sem):
    tm = h_ref.shape[0]

    def row_copy(t, k, d):
        return pltpu.make_async_copy(rows_ref.at[pl.ds(d, 1)], buf_ref.at[k, pl.ds(t, 1)], sem)

    def issue(t, carry):
        for k in range(TOP_K):
            row_copy(t, k, dest_ref[k, t]).start(priority=k % 2)
        return carry

    lax.fori_loop(0, tm, issue, 0, unroll=4)
    for k in range(TOP_K):
        pltpu.make_async_copy(rows_ref.at[pl.ds(0, tm)], buf_ref.at[k], sem).wait()
    tw = tw_ref[...]
    f = jnp.zeros((tm, D_MODEL), F32)
    for k in range(TOP_K):
        f = f + tw[:, k:k + 1] * _unpack_bf16_pairs(buf_ref[k])
    y_ref[...] = _layer_norm(DEEPNORM_ALPHA * h_ref[...] + f, g_ref[...], b_ref[...])


def _combine(dest_t, top_w, hh, out_rows, p):
    n = hh.shape[0]
    tm = min(512, n)
    const = lambda i: (0, 0)
    return pl.pallas_call(
        _combine_kernel,
        grid=(n // tm,),
        in_specs=[
            pl.BlockSpec((SUBLANES, tm), lambda i: (0, i), memory_space=pltpu.SMEM),
            pl.BlockSpec((tm, TOP_K), lambda i: (i, 0)),
            pl.BlockSpec((tm, D_MODEL), lambda i: (i, 0)),
            pl.BlockSpec(memory_space=pl.ANY),
            pl.BlockSpec((1, D_MODEL), const),
            pl.BlockSpec((1, D_MODEL), const),
        ],
        out_specs=pl.BlockSpec((tm, D_MODEL), lambda i: (i, 0)),
        out_shape=jax.ShapeDtypeStruct((n, D_MODEL), F32),
        scratch_shapes=[pltpu.VMEM((TOP_K, tm, D_MODEL // 2), U32), pltpu.SemaphoreType.DMA(())],
        compiler_params=_cparams(("arbitrary",)),
        name="combine",
    )(dest_t, top_w, hh, out_rows, p["ln2_g"], p["ln2_b"])


def _moe(hh, hp, top_idx, top_w, counts, p):
    n = hh.shape[0]
    bm = EXPERT_BM
    n_rows = n * TOP_K + N_EXPERTS * bm
    cnt = counts[0, :N_EXPERTS].astype(I32)
    padded = (cnt + bm - 1) // bm * bm
    pend = jnp.cumsum(padded)
    pstart = pend - padded
    ps_col = jnp.broadcast_to(pstart.astype(F32)[:, None], (N_EXPERTS, LANES))
    block_row0 = jnp.arange(n_rows // bm, dtype=I32) * bm
    block_e = jnp.minimum(jnp.sum((pend[None, :] <= block_row0[:, None]).astype(I32), axis=1), N_EXPERTS - 1)
    n_used = (pend[-1:] // bm).astype(I32)
    dest = _dest(top_idx, ps_col)
    rows = _scatter(dest, hp, n_rows)
    out_rows = _experts(rows, block_e, n_used, p)
    return _combine(dest, top_w, hh, out_rows, p)


def _prep_params(w_in, b_in, w_alpha, b_alpha, gla_norm_g, diff_norm_g, w_pa, w_pb, w_o, ln1_g, ln1_b,
                 w_router, b_router, w_gu, b_gu, w_down, b_down, ln2_g, ln2_b):
    o = IN_OFFS
    seg = lambda a, i: a[..., o[i]:o[i + 1]]
    order = (0, 1, 2, 3, 5, 8)
    w_slab = jnp.concatenate([seg(w_in, i) for i in order], axis=1).astype(BF16)
    b_slab = jnp.concatenate([seg(b_in, i) for i in order])[None, :]
    scale = np.ones((1, SLAB_W), np.float32)
    scale[0, SLAB_QA:SLAB_QA + QA_W] = DK_A ** -0.5
    scale[0, SLAB_QB:SLAB_QB + QB_W] = D_B ** -0.5 * LOG2E
    w_lr = jnp.zeros((D_MODEL, LANES), F32).at[:, :LR_W].set(seg(w_in, 4)).astype(BF16)
    b_lr = jnp.zeros((1, LANES), F32).at[0, :LR_W].set(seg(b_in, 4))
    w_al = jnp.zeros((LANES, QA_W), F32).at[:LR_W].set(w_alpha).astype(BF16)
    w_r = jnp.zeros((D_MODEL, LANES), F32).at[:, :N_EXPERTS].set(w_router)
    w_r_hi = w_r.astype(BF16)
    w_r_lo = (w_r - w_r_hi.astype(F32)).astype(BF16)
    b_r = jnp.full((1, LANES), NEG_INF, F32).at[0, :N_EXPERTS].set(b_router)
    return dict(
        w_slab=w_slab, b_slab=b_slab, s_slab=jnp.asarray(scale),
        w_lr=w_lr, b_lr=b_lr, w_alpha=w_al, b_alpha=b_alpha[None, :],
        w_k=seg(w_in, 6).astype(BF16), w_v=seg(w_in, 7).astype(BF16),
        b_k=seg(b_in, 6)[None, :], b_v=seg(b_in, 7)[None, :],
        gla_g=gla_norm_g[None, :], diff_g=diff_norm_g[None, :],
        w_pa=w_pa.astype(BF16), w_pb=w_pb.astype(BF16), w_o=w_o.astype(BF16),
        ln1_g=ln1_g[None, :], ln1_b=ln1_b[None, :],
        w_r_hi=w_r_hi, w_r_lo=w_r_lo, b_r=b_r,
        w_gu=w_gu.astype(BF16), b_gu=b_gu[:, None, :], w_down=w_down.astype(BF16), b_down=b_down[:, None, :],
        ln2_g=ln2_g[None, :], ln2_b=ln2_b[None, :],
    )


def _layer(x, p, dl, lam_init, s0, cache_k, cache_v):
    batch, t_len, _ = x.shape
    x2 = x.reshape(batch * t_len, D_MODEL)
    slab, loga, k_rows, kb, vb = _proj(x2, p)
    oa, s_new = _gla(slab, loga, s0, p["gla_g"], batch, t_len)
    if cache_k is None:
        ob = _attn_prompt(slab, kb, vb, p["diff_g"], dl, batch, t_len, lam_init)
    else:
        ob = _attn_sample(slab, kb, vb, cache_k, cache_v, p["diff_g"], dl, batch, t_len, lam_init)
    hh, hp, top_idx, top_w, counts = _finish(x2, oa, slab, ob, p)
    y = _moe(hh, hp, top_idx, top_w, counts, p)
    return (y.reshape(batch, t_len, D_MODEL), s_new,
            k_rows.reshape(batch, t_len, H_B, 2, D_B), vb.reshape(batch, t_len, H_B, 2 * D_B))


def kernel(x_prompt, x_sample, cache_k, cache_v, state_gla, w_in, b_in, w_alpha, b_alpha, gla_norm_g,
           diff_lambda, diff_norm_g, w_pa, w_pb, w_o, ln1_g, ln1_b, w_router, b_router, w_gu, b_gu,
           w_down, b_down, ln2_g, ln2_b):
    assert w_in.shape[0] == DEPTH == 1
    l = 0
    lam_init = 0.8 - 0.6 * math.exp(-0.3 * l)
    p = _prep_params(w_in[l], b_in[l], w_alpha[l], b_alpha[l], gla_norm_g[l], diff_norm_g[l], w_pa[l],
                     w_pb[l], w_o[l], ln1_g[l], ln1_b[l], w_router[l], b_router[l], w_gu[l], b_gu[l],
                     w_down[l], b_down[l], ln2_g[l], ln2_b[l])
    dl = diff_lambda[l]
    bp = x_prompt.shape[0]
    bs, ts = x_sample.shape[0], x_sample.shape[1]
    past = cache_k.shape[2]
    yp, s_p, k_p, v_p = _layer(x_prompt, p, dl, lam_init, jnp.zeros((bp, H_A, DK_A, DV_A), F32), None, None)
    ck = cache_k[l].reshape(bs, past, KB_W)
    cv = cache_v[l].reshape(bs, past, VB_W)
    ys, s_s, k_s, v_s = _layer(x_sample, p, dl, lam_init, state_gla[l], ck, cv)
    return (yp, ys, s_p[None], k_p[None], v_p[None], s_s[None], k_s[None], v_s[None])
```

```python
import functools
import math

import numpy as np
import jax
import jax.numpy as jnp
from jax import lax
from jax.experimental import pallas as pl
from jax.experimental.pallas import tpu as pltpu

F32 = jnp.float32
BF16 = jnp.bfloat16
U32 = jnp.uint32
I32 = jnp.int32

D_MODEL = 1024
CHUNK = 64
H_A = 4
DK_A = 128
DV_A = 256
GATE_RANK = 16
GATE_TAU = 16.0
H_B = 8
D_B = 64
N_EXPERTS = 32
TOP_K = 4
D_FF = D_MODEL
SWIGLU_LIMIT = 7.0
SWIGLU_ALPHA = 1.702
EPS = 1e-5
DEPTH = 1
DEEPNORM_ALPHA = (2.0 * DEPTH) ** 0.25

QA_W = H_A * DK_A
KA_W = H_A * DK_A
VA_W = H_A * DV_A
RA_W = H_A * DV_A
LR_W = GATE_RANK
QB_W = H_B * 2 * D_B
KB_W = H_B * 2 * D_B
VB_W = H_B * 2 * D_B
GT_W = 2 * D_MODEL
IN_SIZES = (QA_W, KA_W, VA_W, RA_W, LR_W, QB_W, KB_W, VB_W, GT_W)
IN_OFFS = tuple(int(v) for v in np.cumsum((0,) + IN_SIZES))

SLAB_QA, SLAB_KA, SLAB_VA, SLAB_RA, SLAB_QB, SLAB_GT = 0, 512, 1024, 2048, 3072, 4096
SLAB_W = 6144
LANES = 128
SUBLANES = 8
EXPERT_BM = 512
VMEM_LIMIT = 56 * 1024 * 1024
PROJ_VMEM_LIMIT = 61 * 1024 * 1024

NEG_INF = float("-inf")
LOG2E = math.log2(math.e)


def _cparams(sem, vmem_limit=VMEM_LIMIT):
    return pltpu.CompilerParams(dimension_semantics=sem, vmem_limit_bytes=vmem_limit)


def _sigmoid(x):
    return 1.0 / (1.0 + jnp.exp(-x))


def _pack_bf16_pairs(x):
    n = x.shape[1] // 2
    xb = x.astype(BF16).astype(F32)
    lo = pltpu.bitcast(xb[:, :n], U32) >> 16
    hi = pltpu.bitcast(xb[:, n:], U32) & jnp.uint32(0xFFFF0000)
    return hi | lo


def _unpack_bf16_pairs(r):
    lo = pltpu.bitcast(r << 16, F32)
    hi = pltpu.bitcast(r & jnp.uint32(0xFFFF0000), F32)
    return jnp.concatenate([lo, hi], axis=1)


def _layer_norm(y, g, b):
    mu = jnp.mean(y, axis=-1, keepdims=True)
    d = y - mu
    var = jnp.mean(d * d, axis=-1, keepdims=True)
    return d * lax.rsqrt(var + EPS) * g + b


def _proj_kernel(x_ref, w_ref, b_ref, s_ref, wlr_ref, blr_ref, wal_ref, bal_ref, wk_ref, wv_ref, bk_ref, bv_ref,
                 slab_ref, loga_ref, krows_ref, kb_ref, v_ref, *, tn):
    xb = x_ref[...].astype(BF16)
    k = jnp.dot(xb, wk_ref[...], preferred_element_type=F32) + bk_ref[...]
    kb_ref[...] = k.astype(BF16)
    for h in range(H_B):
        for j in range(2):
            c0 = (2 * h + j) * D_B
            krows_ref[:, h, j, :] = k[:, c0:c0 + D_B]
    v_ref[...] = jnp.dot(xb, wv_ref[...], preferred_element_type=F32) + bv_ref[...]
    lra = jnp.dot(xb, wlr_ref[...], preferred_element_type=F32) + blr_ref[...]
    z = jnp.dot(lra.astype(BF16), wal_ref[...], preferred_element_type=F32) + bal_ref[...]
    loga_ref[...] = (jnp.minimum(z, 0.0) - jnp.log(1.0 + jnp.exp(-jnp.abs(z)))) * (1.0 / GATE_TAU)
    for c0 in range(0, SLAB_W, tn):
        acc = jnp.dot(xb, w_ref[:, c0:c0 + tn], preferred_element_type=F32)
        slab_ref[:, c0:c0 + tn] = ((acc + b_ref[:, c0:c0 + tn]) * s_ref[:, c0:c0 + tn]).astype(BF16)


def _proj(x2, p):
    n = x2.shape[0]
    tm = min(512, n)
    const = lambda i: (0, 0)
    whole = lambda shape: pl.BlockSpec(shape, const, pipeline_mode=pl.Buffered(1))
    row = lambda w: pl.BlockSpec((tm, w), lambda i: (i, 0))
    return pl.pallas_call(
        functools.partial(_proj_kernel, tn=1024),
        grid=(n // tm,),
        in_specs=[
            row(D_MODEL),
            whole((D_MODEL, SLAB_W)), whole((1, SLAB_W)), whole((1, SLAB_W)),
            whole((D_MODEL, LANES)), whole((1, LANES)), whole((LANES, QA_W)), whole((1, QA_W)),
            whole((D_MODEL, KB_W)), whole((D_MODEL, VB_W)), whole((1, KB_W)), whole((1, VB_W)),
        ],
        out_specs=[
            row(SLAB_W), row(QA_W),
            pl.BlockSpec((tm, H_B, 2, D_B), lambda i: (i, 0, 0, 0)),
            row(KB_W), row(VB_W),
        ],
        out_shape=[
            jax.ShapeDtypeStruct((n, SLAB_W), BF16),
            jax.ShapeDtypeStruct((n, QA_W), F32),
            jax.ShapeDtypeStruct((n, H_B, 2, D_B), F32),
            jax.ShapeDtypeStruct((n, KB_W), BF16),
            jax.ShapeDtypeStruct((n, VB_W), F32),
        ],
        compiler_params=_cparams(("parallel",), PROJ_VMEM_LIMIT),
        name="proj",
    )(x2, p["w_slab"], p["b_slab"], p["s_slab"], p["w_lr"], p["b_lr"], p["w_alpha"], p["b_alpha"],
      p["w_k"], p["w_v"], p["b_k"], p["b_v"])


def _gla_tables(L):
    nl = int(math.log2(L))
    t = np.arange(L)
    D = np.zeros(((nl + 2) * L, L), np.float32)
    masks = np.zeros((nl + 1, L, L), np.float32)
    for l in range(nl):
        m = L >> (l + 1)
        grp = t // (2 * m)
        mid = grp * 2 * m + m - 1
        upper = (t % (2 * m)) >= m
        for r in range(L):
            if upper[r]:
                D[l * L + r, mid[r] + 1:r + 1] = 1.0
            else:
                D[l * L + r, r + 1:mid[r] + 1] = 1.0
        masks[l] = (upper[:, None] & ~upper[None, :] & (grp[:, None] == grp[None, :])).astype(np.float32)
    D[nl * L:(nl + 1) * L] = np.tril(np.ones((L, L), np.float32))
    D[(nl + 1) * L:] = np.triu(np.ones((L, L), np.float32), 1)
    masks[nl] = np.eye(L, dtype=np.float32)
    return jnp.asarray(D, BF16), jnp.asarray(masks, F32)


def _gla_kernel(q_ref, k_ref, v_ref, la_ref, s0_ref, g_ref, d_ref, m_ref, o_ref, s_ref, *, L, n_chunks):
    c = pl.program_id(1)
    nl = int(math.log2(L))

    @pl.when(c == 0)
    def _():
        s_ref[...] = s0_ref[...]

    dmat = d_ref[...]
    ones_col = jnp.ones((L, LANES), BF16)
    g = g_ref[...]
    nt = (((1,), (1,)), ((), ()))
    tn = (((0,), (0,)), ((), ()))

    def chunk(ci, carry):
        r0 = pl.multiple_of(ci * L, L)
        rows = pl.ds(r0, L)
        for h in range(H_A):
            kc = slice(h * DK_A, (h + 1) * DK_A)
            vc = slice(h * DV_A, (h + 1) * DV_A)
            q = q_ref[rows, kc].astype(F32)
            k = k_ref[rows, kc].astype(F32)
            v = v_ref[rows, vc]
            la = la_ref[rows, kc]
            la_hi = la.astype(BF16)
            la_lo = (la - la_hi.astype(F32)).astype(BF16)
            e = (jnp.dot(dmat, la_hi, preferred_element_type=F32)
                 + jnp.dot(dmat, la_lo, preferred_element_type=F32))
            x = jnp.exp(e)
            a = m_ref[nl] * lax.dot_general(q.astype(BF16), k.astype(BF16), nt, preferred_element_type=F32)
            for l in range(nl):
                xl = x[l * L:(l + 1) * L]
                a = a + m_ref[l] * lax.dot_general((q * xl).astype(BF16), (k * xl).astype(BF16), nt,
                                                   preferred_element_type=F32)
            xb = x[nl * L:(nl + 1) * L]
            xs = x[(nl + 1) * L:]
            s_old = s_ref[0, h]
            o = (jnp.dot((q * xb).astype(BF16), s_old.astype(BF16), preferred_element_type=F32)
                 + jnp.dot(a.astype(BF16), v, preferred_element_type=F32))
            bl = (lax.dot_general(la_hi, ones_col, tn, preferred_element_type=F32)
                  + lax.dot_general(la_lo, ones_col, tn, preferred_element_type=F32))
            dec = jnp.exp(bl[:, 0:1])
            s_ref[0, h] = dec * s_old + lax.dot_general((k * xs).astype(BF16), v, tn, preferred_element_type=F32)
            ms = jnp.mean(o * o, axis=-1, keepdims=True)
            o_ref[rows, vc] = (o * lax.rsqrt(ms + EPS) * g).astype(BF16)
        return carry

    lax.fori_loop(0, n_chunks, chunk, 0, unroll=2 if n_chunks % 2 == 0 else 1)


def _gla(slab, loga, s0, g_norm, batch, t_len):
    L = min(CHUNK, t_len)
    tb = min(512, t_len)
    nb = t_len // tb
    dmat, masks = _gla_tables(L)
    kern = functools.partial(_gla_kernel, L=L, n_chunks=tb // L)
    row = lambda b, c: b * nb + c
    return pl.pallas_call(
        kern,
        grid=(batch, nb),
        in_specs=[
            pl.BlockSpec((tb, QA_W), lambda b, c: (row(b, c), SLAB_QA // QA_W)),
            pl.BlockSpec((tb, KA_W), lambda b, c: (row(b, c), SLAB_KA // KA_W)),
            pl.BlockSpec((tb, VA_W), lambda b, c: (row(b, c), SLAB_VA // VA_W)),
            pl.BlockSpec((tb, QA_W), lambda b, c: (row(b, c), 0)),
            pl.BlockSpec((1, H_A, DK_A, DV_A), lambda b, c: (b, 0, 0, 0)),
            pl.BlockSpec((1, DV_A), lambda b, c: (0, 0)),
            pl.BlockSpec(dmat.shape, lambda b, c: (0, 0)),
            pl.BlockSpec(masks.shape, lambda b, c: (0, 0, 0)),
        ],
        out_specs=[
            pl.BlockSpec((tb, VA_W), lambda b, c: (row(b, c), 0)),
            pl.BlockSpec((1, H_A, DK_A, DV_A), lambda b, c: (b, 0, 0, 0)),
        ],
        out_shape=[
            jax.ShapeDtypeStruct((batch * t_len, VA_W), BF16),
            jax.ShapeDtypeStruct((batch, H_A, DK_A, DV_A), F32),
        ],
        compiler_params=_cparams(("parallel", "arbitrary")),
        name="gla",
    )(slab, slab, slab, loga, s0, g_norm, dmat, masks)


def _lambda_from(dl_ref, lam_init):
    dl = dl_ref[...]
    a = jnp.sum(dl[0:1] * dl[1:2], axis=-1, keepdims=True)
    b = jnp.sum(dl[2:3] * dl[3:4], axis=-1, keepdims=True)
    return jnp.exp(a) - jnp.exp(b) + lam_init


def _split_maps(q):
    lane = lax.broadcasted_iota(I32, q.shape, 1)
    zero = jnp.zeros_like(q)
    return jnp.concatenate([jnp.where(lane < D_B, q, zero), jnp.where(lane >= D_B, q, zero)], axis=0)


def _alibi_coef(h):
    c = jnp.full((1, 1), LOG2E, F32) * jnp.exp2(-(h + 1).astype(F32))
    c_hi = c.astype(BF16).astype(F32)
    return c, c_hi, c - c_hi


def _finish_heads(acc, inv_l, lam, g, tq, lam_init):
    o = acc * inv_l
    out = o[:tq] - lam * o[tq:]
    ms = jnp.mean(out * out, axis=-1, keepdims=True)
    return (out * lax.rsqrt(ms + EPS) * g * (1.0 - lam_init)).astype(BF16)


def _attn_prompt_kernel(q_ref, k_ref, v_ref, g_ref, dl_ref, o_ref, ka_ref, vb_ref, qaug_ref, d_ref, m_ref,
                        acc_ref, s0_ref, s1_ref, p0_ref, p1_ref,
                        *, tq, tk, lam_init):
    h = pl.program_id(1)
    qi = pl.program_id(2)
    c, c_hi, c_lo = _alibi_coef(h)
    hw = 2 * D_B

    def aug_lanes(shape, pos, sign):
        lane = lax.broadcasted_iota(I32, shape, 1)
        r = (pos & 255).astype(F32) * sign
        a = (pos >> 8).astype(F32) * sign
        coef = jnp.where(lane == 0, c_hi, jnp.where(lane == 1, c_lo,
                         jnp.where(lane == 2, 256.0 * c_hi, jnp.where(lane == 3, 256.0 * c_lo, 0.0))))
        ints = jnp.where(lane < 2, r, jnp.where(lane < 4, a, 0.0))
        return coef, ints, lane

    @pl.when(qi == 0)
    def _():
        t_len = vb_ref.shape[0]
        lane = lax.broadcasted_iota(I32, (t_len, hw), 1)
        row = lax.broadcasted_iota(I32, (hw, tk), 0)
        j_rel = lax.broadcasted_iota(I32, (hw, tk), 1)
        k_aug_t = jnp.where(row < 2, (j_rel & 255).astype(F32), jnp.where(row < 4, (j_rel >> 8).astype(F32),
                  jnp.where(row == 4, c_hi, jnp.where(row == 5, c_lo,
                  jnp.where(row == 6, 256.0 * c_hi, jnp.where(row == 7, 256.0 * c_lo, 0.0)))))).astype(BF16)
        for blk in range(t_len // tk):
            rows = slice(blk * tk, (blk + 1) * tk)
            ka_ref[blk, :hw, :] = k_ref[0, rows, :].astype(F32).T.astype(BF16)
            ka_ref[blk, hw:, :] = k_aug_t
        vb_ref[:, :hw] = v_ref[0].astype(BF16)
        vb_ref[:, hw:] = jnp.where(lane == 0, 1.0, 0.0).astype(BF16)
        i_rel = lax.broadcasted_iota(I32, (2 * tq, hw), 0) & (tq - 1)
        coef, ints, lane = aug_lanes((2 * tq, hw), i_rel, -1.0)
        qaug_ref[...] = jnp.where(lane < 4, coef, pltpu.roll(ints, 4, 1)).astype(BF16)
        i = lax.broadcasted_iota(I32, (2 * tq, tq), 0) & (tq - 1)
        j = lax.broadcasted_iota(I32, (2 * tq, tq), 1)
        fwd = jnp.maximum(j - i, 0).astype(F32)
        d_ref[...] = jnp.where((j >> 6) <= (i >> 6), -2.0 * c * fwd, NEG_INF)

    qa = jnp.concatenate([_split_maps(q_ref[...]), qaug_ref[...]], axis=1)
    m_ref[...] = jnp.full(m_ref.shape, -1e30, F32)

    def scores(kv):
        return jnp.dot(qa, ka_ref[kv], preferred_element_type=F32)

    def weighted_values(p_ref, kv):
        k0 = pl.multiple_of(kv * tk, tk)
        return jnp.dot(p_ref[...], vb_ref[pl.ds(k0, tk), :], preferred_element_type=F32)

    def softmax_step(s, off):
        m_old = m_ref[...]
        m_new = jnp.maximum(m_old, jnp.max(s, axis=-1, keepdims=True) - off)
        shift = m_new + off
        p = jnp.exp2(s - jnp.concatenate([shift] * (tk // hw), axis=1))
        alpha = jnp.exp2(m_old - m_new)
        m_ref[...] = m_new
        return p.astype(BF16), jnp.concatenate([alpha, alpha], axis=1)

    s_first = scores(0)
    p_diag, _ = softmax_step(scores(qi) + d_ref[...], jnp.zeros((1, 1), F32))
    p1_ref[...] = p_diag
    s0_ref[...] = s_first
    acc_ref[...] = jnp.zeros(acc_ref.shape, F32)

    def step(kv, s_cur, s_nxt, p_prev, p_cur):
        pv_prev = weighted_values(p_prev, jnp.where(kv == 0, qi, kv - 1))
        s_nxt[...] = scores(jnp.minimum(kv + 1, qi - 1))
        p, alpha = softmax_step(s_cur[...], c * (qi * tq - kv * tk).astype(F32))
        acc_ref[...] = alpha * (acc_ref[...] + pv_prev)
        p_cur[...] = p

    def pair(j, carry):
        step(2 * j, s0_ref, s1_ref, p1_ref, p0_ref)
        step(2 * j + 1, s1_ref, s0_ref, p0_ref, p1_ref)
        return carry

    lax.fori_loop(0, qi // 2, pair, 0)

    @pl.when(qi % 2 == 1)
    def _():
        step(qi - 1, s0_ref, s1_ref, p1_ref, p0_ref)
        acc_ref[...] += weighted_values(p0_ref, qi - 1)

    @pl.when(qi % 2 == 0)
    def _():
        acc_ref[...] += weighted_values(p1_ref, jnp.maximum(qi - 1, 0))

    lam = _lambda_from(dl_ref, lam_init)
    acc = acc_ref[...]
    o_ref[...] = _finish_heads(acc[:, :hw], 1.0 / acc[:, hw:hw + 1], lam, g_ref[...], tq, lam_init)


def _attn_prompt(slab, kb, vb, g_norm, dl, batch, t_len, lam_init):
    tq = min(512, t_len)
    tk = tq
    assert tq % CHUNK == 0 and CHUNK == 64
    nq = t_len // tq
    k3 = kb.reshape(batch, t_len, KB_W)
    v3 = vb.reshape(batch, t_len, VB_W)
    kern = functools.partial(_attn_prompt_kernel, tq=tq, tk=tk, lam_init=lam_init)
    hw = 2 * D_B
    return pl.pallas_call(
        kern,
        grid=(batch, H_B, nq),
        in_specs=[
            pl.BlockSpec((tq, hw), lambda b, h, q: (b * nq + q, SLAB_QB // hw + h)),
            pl.BlockSpec((1, t_len, hw), lambda b, h, q: (b, 0, h)),
            pl.BlockSpec((1, t_len, hw), lambda b, h, q: (b, 0, h)),
            pl.BlockSpec((1, hw), lambda b, h, q: (0, 0)),
            pl.BlockSpec((4, D_B), lambda b, h, q: (0, 0)),
        ],
        out_specs=pl.BlockSpec((tq, hw), lambda b, h, q: (b * nq + q, h)),
        out_shape=jax.ShapeDtypeStruct((batch * t_len, VB_W), BF16),
        scratch_shapes=[
            pltpu.VMEM((t_len // tk, 2 * hw, tk), BF16),
            pltpu.VMEM((t_len, 2 * hw), BF16),
            pltpu.VMEM((2 * tq, hw), BF16),
            pltpu.VMEM((2 * tq, tq), F32),
            pltpu.VMEM((2 * tq, hw), F32),
            pltpu.VMEM((2 * tq, 2 * hw), F32),
            pltpu.VMEM((2 * tq, tk), F32),
            pltpu.VMEM((2 * tq, tk), F32),
            pltpu.VMEM((2 * tq, tk), BF16),
            pltpu.VMEM((2 * tq, tk), BF16),
        ],
        compiler_params=_cparams(("parallel", "parallel", "arbitrary")),
        name="attn_prompt",
    )(slab, k3, v3, g_norm, dl)


def _attn_sample_kernel(q_ref, kc_ref, vc_ref, kn_ref, vn_ref, g_ref, dl_ref, o_ref, *, tq, past, lam_init):
    h = pl.program_id(1)
    c, _, _ = _alibi_coef(h)
    qq = _split_maps(q_ref[...])
    q_pos = past + lax.broadcasted_iota(I32, (2 * tq, 1), 0) % tq

    def scores(k, k_pos):
        s = lax.dot_general(qq, k.astype(BF16), (((1,), (1,)), ((), ())), preferred_element_type=F32)
        s = s - c * jnp.abs(q_pos - k_pos).astype(F32)
        return jnp.where((k_pos // CHUNK) <= (q_pos // CHUNK), s, NEG_INF)

    sc = scores(kc_ref[0], lax.broadcasted_iota(I32, (1, past), 1))
    sn = scores(kn_ref[0], past + lax.broadcasted_iota(I32, (1, tq), 1))
    m = jnp.maximum(jnp.max(sc, axis=-1, keepdims=True), jnp.max(sn, axis=-1, keepdims=True))
    pc = jnp.exp2(sc - m)
    pn = jnp.exp2(sn - m)
    l = jnp.sum(pc, axis=-1, keepdims=True) + jnp.sum(pn, axis=-1, keepdims=True)
    acc = (jnp.dot(pc.astype(BF16), vc_ref[0].astype(BF16), preferred_element_type=F32)
           + jnp.dot(pn.astype(BF16), vn_ref[0].astype(BF16), preferred_element_type=F32))
    lam = _lambda_from(dl_ref, lam_init)
    o_ref[...] = _finish_heads(acc, 1.0 / l, lam, g_ref[...], tq, lam_init)


def _attn_sample(slab, kb, vb, cache_k, cache_v, g_norm, dl, batch, t_len, lam_init):
    past = cache_k.shape[1]
    hw = 2 * D_B
    k3 = kb.reshape(batch, t_len, KB_W)
    v3 = vb.reshape(batch, t_len, VB_W)
    kern = functools.partial(_attn_sample_kernel, tq=t_len, past=past, lam_init=lam_init)
    return pl.pallas_call(
        kern,
        grid=(batch, H_B),
        in_specs=[
            pl.BlockSpec((t_len, hw), lambda b, h: (b, SLAB_QB // hw + h)),
            pl.BlockSpec((1, past, hw), lambda b, h: (b, 0, h)),
            pl.BlockSpec((1, past, hw), lambda b, h: (b, 0, h)),
            pl.BlockSpec((1, t_len, hw), lambda b, h: (b, 0, h)),
            pl.BlockSpec((1, t_len, hw), lambda b, h: (b, 0, h)),
            pl.BlockSpec((1, hw), lambda b, h: (0, 0)),
            pl.BlockSpec((4, D_B), lambda b, h: (0, 0)),
        ],
        out_specs=pl.BlockSpec((t_len, hw), lambda b, h: (b, h)),
        out_shape=jax.ShapeDtypeStruct((batch * t_len, VB_W), BF16),
        compiler_params=_cparams(("parallel", "parallel")),
        name="attn_sample",
    )(slab, cache_k, cache_v, k3, v3, g_norm, dl)


def _finish_kernel(x_ref, oa_ref, ra_ref, ob_ref, gt_ref, wpa_ref, wpb_ref, wo_ref, g_ref, b_ref,
                   wrh_ref, wrl_ref, br_ref, h_ref, hp_ref, idx_ref, tw_ref, cnt_ref):
    i = pl.program_id(0)
    ra = ra_ref[...].astype(F32)
    ua = ra * _sigmoid(ra) * oa_ref[...].astype(F32)
    ya = jnp.dot(ua.astype(BF16), wpa_ref[...], preferred_element_type=F32)
    yb = jnp.dot(ob_ref[...], wpb_ref[...], preferred_element_type=F32)
    gt = gt_ref[...].astype(F32)
    mixed = _sigmoid(gt[:, :D_MODEL]) * ya + _sigmoid(gt[:, D_MODEL:]) * yb
    mix = jnp.dot(mixed.astype(BF16), wo_ref[...], preferred_element_type=F32)
    hh = _layer_norm(DEEPNORM_ALPHA * x_ref[...] + mix, g_ref[...], b_ref[...])
    h_ref[...] = hh
    hp_ref[...] = _pack_bf16_pairs(hh)

    h_hi = hh.astype(BF16)
    h_lo = (hh - h_hi.astype(F32)).astype(BF16)
    lg = (jnp.dot(h_hi, wrh_ref[...], preferred_element_type=F32)
          + jnp.dot(h_lo, wrh_ref[...], preferred_element_type=F32)
          + jnp.dot(h_hi, wrl_ref[...], preferred_element_type=F32)) + br_ref[...]
    tm = lg.shape[0]
    lane = lax.broadcasted_iota(I32, (tm, LANES), 1)
    lane_f = lane.astype(F32)
    work = lg
    vals, idxs = [], []
    cnt = jnp.zeros((tm, LANES), F32)
    for _ in range(TOP_K):
        mx = jnp.max(work, axis=-1, keepdims=True)
        ix = jnp.min(jnp.where(work == mx, lane_f, float(LANES)), axis=-1, keepdims=True)
        hit = lane_f == ix
        cnt = cnt + hit.astype(F32)
        work = jnp.where(hit, NEG_INF, work)
        vals.append(mx)
        idxs.append(ix)
    es = [jnp.exp(v - vals[0]) for v in vals]
    den = es[0] + es[1] + es[2] + es[3]
    idx_full = jnp.zeros((tm, LANES), F32)
    tw_full = jnp.zeros((tm, LANES), F32)
    for r in range(TOP_K):
        idx_full = jnp.where(lane == r, idxs[r], idx_full)
        tw_full = jnp.where(lane == r, es[r] / den, tw_full)
    idx_ref[...] = idx_full.T[:SUBLANES].astype(I32)
    tw_ref[...] = tw_full[:, :TOP_K]

    @pl.when(i == 0)
    def _():
        cnt_ref[...] = jnp.zeros(cnt_ref.shape, F32)

    cnt_ref[...] += jnp.sum(cnt, axis=0, keepdims=True)


def _finish(x2, oa, slab, ob, p):
    n = x2.shape[0]
    tm = min(512, n)
    const = lambda i: (0, 0)
    return pl.pallas_call(
        _finish_kernel,
        grid=(n // tm,),
        in_specs=[
            pl.BlockSpec((tm, D_MODEL), lambda i: (i, 0)),
            pl.BlockSpec((tm, VA_W), lambda i: (i, 0)),
            pl.BlockSpec((tm, RA_W), lambda i: (i, SLAB_RA // RA_W)),
            pl.BlockSpec((tm, VB_W), lambda i: (i, 0)),
            pl.BlockSpec((tm, GT_W), lambda i: (i, SLAB_GT // GT_W)),
            pl.BlockSpec((VA_W, D_MODEL), const),
            pl.BlockSpec((VB_W, D_MODEL), const),
            pl.BlockSpec((D_MODEL, D_MODEL), const),
            pl.BlockSpec((1, D_MODEL), const),
            pl.BlockSpec((1, D_MODEL), const),
            pl.BlockSpec((D_MODEL, LANES), const),
            pl.BlockSpec((D_MODEL, LANES), const),
            pl.BlockSpec((1, LANES), const),
        ],
        out_specs=[
            pl.BlockSpec((tm, D_MODEL), lambda i: (i, 0)),
            pl.BlockSpec((tm, D_MODEL // 2), lambda i: (i, 0)),
            pl.BlockSpec((SUBLANES, tm), lambda i: (0, i)),
            pl.BlockSpec((tm, TOP_K), lambda i: (i, 0)),
            pl.BlockSpec((1, LANES), const),
        ],
        out_shape=[
            jax.ShapeDtypeStruct((n, D_MODEL), F32),
            jax.ShapeDtypeStruct((n, D_MODEL // 2), U32),
            jax.ShapeDtypeStruct((SUBLANES, n), I32),
            jax.ShapeDtypeStruct((n, TOP_K), F32),
            jax.ShapeDtypeStruct((1, LANES), F32),
        ],
        compiler_params=_cparams(("arbitrary",)),
        name="finish",
    )(x2, oa, slab, ob, slab, p["w_pa"], p["w_pb"], p["w_o"], p["ln1_g"], p["ln1_b"],
      p["w_r_hi"], p["w_r_lo"], p["b_r"])


def _dest_kernel(idx_ref, ps_ref, ut_ref, dest_ref, carry_ref):
    i = pl.program_id(0)

    @pl.when(i == 0)
    def _():
        carry_ref[...] = jnp.zeros(carry_ref.shape, F32)

    tm = idx_ref.shape[1]
    expert = lax.broadcasted_iota(I32, (N_EXPERTS, tm), 0)
    idx = idx_ref[...]
    hits = [expert == idx[k:k + 1, :] for k in range(TOP_K)]
    cnt = jnp.zeros((N_EXPERTS, tm), F32)
    for hit in hits:
        cnt = cnt + hit.astype(F32)
    before = jnp.dot(cnt.astype(BF16), ut_ref[...], preferred_element_type=F32)
    base = before + carry_ref[:, 0:1] + ps_ref[:, 0:1]
    row = lax.broadcasted_iota(I32, (SUBLANES, tm), 0)
    dest = jnp.zeros((SUBLANES, tm), F32)
    for k in range(TOP_K):
        d = jnp.sum(jnp.where(hits[k], base, 0.0), axis=0, keepdims=True)
        dest = jnp.where(row == k, d, dest)
    dest_ref[...] = dest.astype(I32)
    carry_ref[...] += jnp.sum(cnt, axis=1, keepdims=True)


def _dest(idx_t, pstart):
    n = idx_t.shape[1]
    tm = min(512, n)
    utri = jnp.asarray(np.triu(np.ones((tm, tm), np.float32), 1), BF16)
    return pl.pallas_call(
        _dest_kernel,
        grid=(n // tm,),
        in_specs=[
            pl.BlockSpec((SUBLANES, tm), lambda i: (0, i)),
            pl.BlockSpec((N_EXPERTS, LANES), lambda i: (0, 0)),
            pl.BlockSpec((tm, tm), lambda i: (0, 0)),
        ],
        out_specs=pl.BlockSpec((SUBLANES, tm), lambda i: (0, i)),
        out_shape=jax.ShapeDtypeStruct((SUBLANES, n), I32),
        scratch_shapes=[pltpu.VMEM((N_EXPERTS, LANES), F32)],
        compiler_params=_cparams(("arbitrary",)),
        name="dest",
    )(idx_t, pstart, utri)


def _scatter_kernel(dest_ref, hp_ref, rows_in_ref, rows_ref, sem):
    del rows_in_ref
    tm = hp_ref.shape[0]

    def row_copy(t, d):
        return pltpu.make_async_copy(hp_ref.at[pl.ds(t, 1)], rows_ref.at[pl.ds(d, 1)], sem)

    def issue(t, carry):
        for k in range(TOP_K):
            row_copy(t, dest_ref[k, t]).start(priority=k % 2)
        return carry

    lax.fori_loop(0, tm, issue, 0, unroll=4)
    for k in range(TOP_K):
        pltpu.make_async_copy(hp_ref, rows_ref.at[pl.ds(0, tm)], sem).wait()


def _scatter(dest_t, hp, n_rows):
    n = hp.shape[0]
    tm = min(512, n)
    rows0 = jnp.zeros((n_rows, D_MODEL // 2), U32)
    return pl.pallas_call(
        _scatter_kernel,
        grid=(n // tm,),
        in_specs=[
            pl.BlockSpec((SUBLANES, tm), lambda i: (0, i), memory_space=pltpu.SMEM),
            pl.BlockSpec((tm, D_MODEL // 2), lambda i: (i, 0)),
            pl.BlockSpec(memory_space=pl.ANY),
        ],
        out_specs=pl.BlockSpec(memory_space=pl.ANY),
        out_shape=jax.ShapeDtypeStruct((n_rows, D_MODEL // 2), U32),
        scratch_shapes=[pltpu.SemaphoreType.DMA(())],
        input_output_aliases={2: 0},
        compiler_params=_cparams(("arbitrary",)),
        name="scatter",
    )(dest_t, hp, rows0)


def _expert_kernel(be_ref, nu_ref, rows_ref, wgu_ref, bgu_ref, wd_ref, bd_ref, out_ref):
    del be_ref
    i = pl.program_id(0)

    @pl.when(i < nu_ref[0])
    def _():
        x = _unpack_bf16_pairs(rows_ref[...]).astype(BF16)
        gu = jnp.dot(x, wgu_ref[0], preferred_element_type=F32) + bgu_ref[0]
        gate = jnp.minimum(gu[:, :D_FF], SWIGLU_LIMIT)
        up = jnp.clip(gu[:, D_FF:], -SWIGLU_LIMIT, SWIGLU_LIMIT)
        hidden = (up + 1.0) * (gate * _sigmoid(SWIGLU_ALPHA * gate))
        o = jnp.dot(hidden.astype(BF16), wd_ref[0], preferred_element_type=F32) + bd_ref[0]
        out_ref[...] = _pack_bf16_pairs(o)

    @pl.when(i >= nu_ref[0])
    def _():
        out_ref[...] = jnp.zeros(out_ref.shape, U32)


def _experts(rows, block_e, n_used, p):
    n_rows = rows.shape[0]
    bm = EXPERT_BM
    n_blocks = n_rows // bm
    grid_spec = pltpu.PrefetchScalarGridSpec(
        num_scalar_prefetch=2,
        grid=(n_blocks,),
        in_specs=[
            pl.BlockSpec((bm, D_MODEL // 2), lambda i, be, nu: (i, 0)),
            pl.BlockSpec((1, D_MODEL, 2 * D_FF), lambda i, be, nu: (be[i], 0, 0)),
            pl.BlockSpec((1, 1, 2 * D_FF), lambda i, be, nu: (be[i], 0, 0)),
            pl.BlockSpec((1, D_FF, D_MODEL), lambda i, be, nu: (be[i], 0, 0)),
            pl.BlockSpec((1, 1, D_MODEL), lambda i, be, nu: (be[i], 0, 0)),
        ],
        out_specs=pl.BlockSpec((bm, D_MODEL // 2), lambda i, be, nu: (i, 0)),
    )
    return pl.pallas_call(
        _expert_kernel,
        grid_spec=grid_spec,
        out_shape=jax.ShapeDtypeStruct((n_rows, D_MODEL // 2), U32),
        compiler_params=_cparams(("arbitrary",)),
        name="experts",
    )(block_e, n_used, rows, p["w_gu"], p["b_gu"], p["w_down"], p["b_down"])


def _combine_kernel(dest_ref, tw_ref, h_ref, rows_ref, g_ref, b_ref, y_ref, buf_ref, sem):
    tm = h_ref.shape[0]

    def row_copy(t, k, d):
        return pltpu.make_async_copy(rows_ref.at[pl.ds(d, 1)], buf_ref.at[k, pl.ds(t, 1)], sem)

    def issue(t, carry):
        for k in range(TOP_K):
            row_copy(t, k, dest_ref[k, t]).start(priority=k % 2)
        return carry

    lax.fori_loop(0, tm, issue, 0, unroll=4)
    for k in range(TOP_K):
        pltpu.make_async_copy(rows_ref.at[pl.ds(0, tm)], buf_ref.at[k], sem).wait()
    tw = tw_ref[...]
    f = jnp.zeros((tm, D_MODEL), F32)
    for k in range(TOP_K):
        f = f + tw[:, k:k + 1] * _unpack_bf16_pairs(buf_ref[k])
    y_ref[...] = _layer_norm(DEEPNORM_ALPHA * h_ref[...] + f, g_ref[...], b_ref[...])


def _combine(dest_t, top_w, hh, out_rows, p):
    n = hh.shape[0]
    tm = min(512, n)
    const = lambda i: (0, 0)
    return pl.pallas_call(
        _combine_kernel,
        grid=(n // tm,),
        in_specs=[
            pl.BlockSpec((SUBLANES, tm), lambda i: (0, i), memory_space=pltpu.SMEM),
            pl.BlockSpec((tm, TOP_K), lambda i: (i, 0)),
            pl.BlockSpec((tm, D_MODEL), lambda i: (i, 0)),
            pl.BlockSpec(memory_space=pl.ANY),
            pl.BlockSpec((1, D_MODEL), const),
            pl.BlockSpec((1, D_MODEL), const),
        ],
        out_specs=pl.BlockSpec((tm, D_MODEL), lambda i: (i, 0)),
        out_shape=jax.ShapeDtypeStruct((n, D_MODEL), F32),
        scratch_shapes=[pltpu.VMEM((TOP_K, tm, D_MODEL // 2), U32), pltpu.SemaphoreType.DMA(())],
        compiler_params=_cparams(("arbitrary",)),
        name="combine",
    )(dest_t, top_w, hh, out_rows, p["ln2_g"], p["ln2_b"])


def _moe(hh, hp, top_idx, top_w, counts, p):
    n = hh.shape[0]
    bm = EXPERT_BM
    n_rows = n * TOP_K + N_EXPERTS * bm
    cnt = counts[0, :N_EXPERTS].astype(I32)
    padded = (cnt + bm - 1) // bm * bm
    pend = jnp.cumsum(padded)
    pstart = pend - padded
    ps_col = jnp.broadcast_to(pstart.astype(F32)[:, None], (N_EXPERTS, LANES))
    block_row0 = jnp.arange(n_rows // bm, dtype=I32) * bm
    block_e = jnp.minimum(jnp.sum((pend[None, :] <= block_row0[:, None]).astype(I32), axis=1), N_EXPERTS - 1)
    n_used = (pend[-1:] // bm).astype(I32)
    dest = _dest(top_idx, ps_col)
    rows = _scatter(dest, hp, n_rows)
    out_rows = _experts(rows, block_e, n_used, p)
    return _combine(dest, top_w, hh, out_rows, p)


def _prep_params(w_in, b_in, w_alpha, b_alpha, gla_norm_g, diff_norm_g, w_pa, w_pb, w_o, ln1_g, ln1_b,
                 w_router, b_router, w_gu, b_gu, w_down, b_down, ln2_g, ln2_b):
    o = IN_OFFS
    seg = lambda a, i: a[..., o[i]:o[i + 1]]
    order = (0, 1, 2, 3, 5, 8)
    w_slab = jnp.concatenate([seg(w_in, i) for i in order], axis=1).astype(BF16)
    b_slab = jnp.concatenate([seg(b_in, i) for i in order])[None, :]
    scale = np.ones((1, SLAB_W), np.float32)
    scale[0, SLAB_QA:SLAB_QA + QA_W] = DK_A ** -0.5
    scale[0, SLAB_QB:SLAB_QB + QB_W] = D_B ** -0.5 * LOG2E
    w_lr = jnp.zeros((D_MODEL, LANES), F32).at[:, :LR_W].set(seg(w_in, 4)).astype(BF16)
    b_lr = jnp.zeros((1, LANES), F32).at[0, :LR_W].set(seg(b_in, 4))
    w_al = jnp.zeros((LANES, QA_W), F32).at[:LR_W].set(w_alpha).astype(BF16)
    w_r = jnp.zeros((D_MODEL, LANES), F32).at[:, :N_EXPERTS].set(w_router)
    w_r_hi = w_r.astype(BF16)
    w_r_lo = (w_r - w_r_hi.astype(F32)).astype(BF16)
    b_r = jnp.full((1, LANES), NEG_INF, F32).at[0, :N_EXPERTS].set(b_router)
    return dict(
        w_slab=w_slab, b_slab=b_slab, s_slab=jnp.asarray(scale),
        w_lr=w_lr, b_lr=b_lr, w_alpha=w_al, b_alpha=b_alpha[None, :],
        w_k=seg(w_in, 6).astype(BF16), w_v=seg(w_in, 7).astype(BF16),
        b_k=seg(b_in, 6)[None, :], b_v=seg(b_in, 7)[None, :],
        gla_g=gla_norm_g[None, :], diff_g=diff_norm_g[None, :],
        w_pa=w_pa.astype(BF16), w_pb=w_pb.astype(BF16), w_o=w_o.astype(BF16),
        ln1_g=ln1_g[None, :], ln1_b=ln1_b[None, :],
        w_r_hi=w_r_hi, w_r_lo=w_r_lo, b_r=b_r,
        w_gu=w_gu.astype(BF16), b_gu=b_gu[:, None, :], w_down=w_down.astype(BF16), b_down=b_down[:, None, :],
        ln2_g=ln2_g[None, :], ln2_b=ln2_b[None, :],
    )


def _layer(x, p, dl, lam_init, s0, cache_k, cache_v):
    batch, t_len, _ = x.shape
    x2 = x.reshape(batch * t_len, D_MODEL)
    slab, loga, k_rows, kb, vb = _proj(x2, p)
    oa, s_new = _gla(slab, loga, s0, p["gla_g"], batch, t_len)
    if cache_k is None:
        ob = _attn_prompt(slab, kb, vb, p["diff_g"], dl, batch, t_len, lam_init)
    else:
        ob = _attn_sample(slab, kb, vb, cache_k, cache_v, p["diff_g"], dl, batch, t_len, lam_init)
    hh, hp, top_idx, top_w, counts = _finish(x2, oa, slab, ob, p)
    y = _moe(hh, hp, top_idx, top_w, counts, p)
    return (y.reshape(batch, t_len, D_MODEL), s_new,
            k_rows.reshape(batch, t_len, H_B, 2, D_B), vb.reshape(batch, t_len, H_B, 2 * D_B))


def kernel(x_prompt, x_sample, cache_k, cache_v, state_gla, w_in, b_in, w_alpha, b_alpha, gla_norm_g,
           diff_lambda, diff_norm_g, w_pa, w_pb, w_o, ln1_g, ln1_b, w_router, b_router, w_gu, b_gu,
           w_down, b_down, ln2_g, ln2_b):
    assert w_in.shape[0] == DEPTH == 1
    l = 0
    lam_init = 0.8 - 0.6 * math.exp(-0.3 * l)
    p = _prep_params(w_in[l], b_in[l], w_alpha[l], b_alpha[l], gla_norm_g[l], diff_norm_g[l], w_pa[l],
                     w_pb[l], w_o[l], ln1_g[l], ln1_b[l], w_router[l], b_router[l], w_gu[l], b_gu[l],
                     w_down[l], b_down[l], ln2_g[l], ln2_b[l])
    dl = diff_lambda[l]
    bp = x_prompt.shape[0]
    bs, ts = x_sample.shape[0], x_sample.shape[1]
    past = cache_k.shape[2]
    yp, s_p, k_p, v_p = _layer(x_prompt, p, dl, lam_init, jnp.zeros((bp, H_A, DK_A, DV_A), F32), None, None)
    ck = cache_k[l].reshape(bs, past, KB_W)
    cv = cache_v[l].reshape(bs, past, VB_W)
    ys, s_s, k_s, v_s = _layer(x_sample, p, dl, lam_init, state_gla[l], ck, cv)
    return (yp, ys, s_p[None], k_p[None], v_p[None], s_s[None], k_s[None], v_s[None])
```

```python
import functools
import math

import numpy as np
import jax
import jax.numpy as jnp
from jax import lax
from jax.experimental import pallas as pl
from jax.experimental.pallas import tpu as pltpu

F32 = jnp.float32
BF16 = jnp.bfloat16
U32 = jnp.uint32
I32 = jnp.int32

D_MODEL = 1024
CHUNK = 64
H_A = 4
DK_A = 128
DV_A = 256
GATE_RANK = 16
GATE_TAU = 16.0
H_B = 8
D_B = 64
N_EXPERTS = 32
TOP_K = 4
D_FF = D_MODEL
SWIGLU_LIMIT = 7.0
SWIGLU_ALPHA = 1.702
EPS = 1e-5
DEPTH = 1
DEEPNORM_ALPHA = (2.0 * DEPTH) ** 0.25

QA_W = H_A * DK_A
KA_W = H_A * DK_A
VA_W = H_A * DV_A
RA_W = H_A * DV_A
LR_W = GATE_RANK
QB_W = H_B * 2 * D_B
KB_W = H_B * 2 * D_B
VB_W = H_B * 2 * D_B
GT_W = 2 * D_MODEL
IN_SIZES = (QA_W, KA_W, VA_W, RA_W, LR_W, QB_W, KB_W, VB_W, GT_W)
IN_OFFS = tuple(int(v) for v in np.cumsum((0,) + IN_SIZES))

SLAB_QA, SLAB_KA, SLAB_VA, SLAB_RA, SLAB_QB, SLAB_GT = 0, 512, 1024, 2048, 3072, 4096
SLAB_W = 6144
LANES = 128
SUBLANES = 8
EXPERT_BM = 512
VMEM_LIMIT = 56 * 1024 * 1024
PROJ_VMEM_LIMIT = 61 * 1024 * 1024

NEG_INF = float("-inf")
LOG2E = math.log2(math.e)


def _cparams(sem, vmem_limit=VMEM_LIMIT):
    return pltpu.CompilerParams(dimension_semantics=sem, vmem_limit_bytes=vmem_limit)


def _sigmoid(x):
    return 1.0 / (1.0 + jnp.exp(-x))


def _pack_bf16_pairs(x):
    n = x.shape[1] // 2
    xb = x.astype(BF16).astype(F32)
    lo = pltpu.bitcast(xb[:, :n], U32) >> 16
    hi = pltpu.bitcast(xb[:, n:], U32) & jnp.uint32(0xFFFF0000)
    return hi | lo


def _unpack_bf16_pairs(r):
    lo = pltpu.bitcast(r << 16, F32)
    hi = pltpu.bitcast(r & jnp.uint32(0xFFFF0000), F32)
    return jnp.concatenate([lo, hi], axis=1)


ROW_TILES = D_MODEL // 2 // LANES


def _load_rows(ref):
    m = ref.shape[0] // ROW_TILES
    return jnp.concatenate([ref[pl.ds(c, m, stride=ROW_TILES), :] for c in range(ROW_TILES)], axis=1)


def _store_rows(ref, val):
    m = ref.shape[0] // ROW_TILES
    for c in range(ROW_TILES):
        ref[pl.ds(c, m, stride=ROW_TILES), :] = val[:, c * LANES:(c + 1) * LANES]


def _layer_norm(y, g, b):
    mu = jnp.mean(y, axis=-1, keepdims=True)
    d = y - mu
    var = jnp.mean(d * d, axis=-1, keepdims=True)
    return d * lax.rsqrt(var + EPS) * g + b


def _proj_kernel(x_ref, w_ref, b_ref, s_ref, wlr_ref, blr_ref, wal_ref, bal_ref, wk_ref, wv_ref, bk_ref, bv_ref,
                 slab_ref, loga_ref, krows_ref, kb_ref, v_ref, *, tn):
    xb = x_ref[...].astype(BF16)
    k = jnp.dot(xb, wk_ref[...], preferred_element_type=F32) + bk_ref[...]
    kb_ref[...] = k.astype(BF16)
    for h in range(H_B):
        for j in range(2):
            c0 = (2 * h + j) * D_B
            krows_ref[:, h, j, :] = k[:, c0:c0 + D_B]
    v_ref[...] = jnp.dot(xb, wv_ref[...], preferred_element_type=F32) + bv_ref[...]
    lra = jnp.dot(xb, wlr_ref[...], preferred_element_type=F32) + blr_ref[...]
    z = jnp.dot(lra.astype(BF16), wal_ref[...], preferred_element_type=F32) + bal_ref[...]
    loga_ref[...] = (jnp.minimum(z, 0.0) - jnp.log(1.0 + jnp.exp(-jnp.abs(z)))) * (1.0 / GATE_TAU)
    for c0 in range(0, SLAB_W, tn):
        acc = jnp.dot(xb, w_ref[:, c0:c0 + tn], preferred_element_type=F32)
        slab_ref[:, c0:c0 + tn] = ((acc + b_ref[:, c0:c0 + tn]) * s_ref[:, c0:c0 + tn]).astype(BF16)


def _proj(x2, p):
    n = x2.shape[0]
    tm = min(512, n)
    const = lambda i: (0, 0)
    whole = lambda shape: pl.BlockSpec(shape, const, pipeline_mode=pl.Buffered(1))
    row = lambda w: pl.BlockSpec((tm, w), lambda i: (i, 0))
    return pl.pallas_call(
        functools.partial(_proj_kernel, tn=1024),
        grid=(n // tm,),
        in_specs=[
            row(D_MODEL),
            whole((D_MODEL, SLAB_W)), whole((1, SLAB_W)), whole((1, SLAB_W)),
            whole((D_MODEL, LANES)), whole((1, LANES)), whole((LANES, QA_W)), whole((1, QA_W)),
            whole((D_MODEL, KB_W)), whole((D_MODEL, VB_W)), whole((1, KB_W)), whole((1, VB_W)),
        ],
        out_specs=[
            row(SLAB_W), row(QA_W),
            pl.BlockSpec((tm, H_B, 2, D_B), lambda i: (i, 0, 0, 0)),
            row(KB_W), row(VB_W),
        ],
        out_shape=[
            jax.ShapeDtypeStruct((n, SLAB_W), BF16),
            jax.ShapeDtypeStruct((n, QA_W), F32),
            jax.ShapeDtypeStruct((n, H_B, 2, D_B), F32),
            jax.ShapeDtypeStruct((n, KB_W), BF16),
            jax.ShapeDtypeStruct((n, VB_W), F32),
        ],
        compiler_params=_cparams(("parallel",), PROJ_VMEM_LIMIT),
        name="proj",
    )(x2, p["w_slab"], p["b_slab"], p["s_slab"], p["w_lr"], p["b_lr"], p["w_alpha"], p["b_alpha"],
      p["w_k"], p["w_v"], p["b_k"], p["b_v"])


def _gla_tables(L):
    nl = int(math.log2(L))
    t = np.arange(L)
    D = np.zeros(((nl + 2) * L, L), np.float32)
    masks = np.zeros((nl + 1, L, L), np.float32)
    for l in range(nl):
        m = L >> (l + 1)
        grp = t // (2 * m)
        mid = grp * 2 * m + m - 1
        upper = (t % (2 * m)) >= m
        for r in range(L):
            if upper[r]:
                D[l * L + r, mid[r] + 1:r + 1] = 1.0
            else:
                D[l * L + r, r + 1:mid[r] + 1] = 1.0
        masks[l] = (upper[:, None] & ~upper[None, :] & (grp[:, None] == grp[None, :])).astype(np.float32)
    D[nl * L:(nl + 1) * L] = np.tril(np.ones((L, L), np.float32))
    D[(nl + 1) * L:] = np.triu(np.ones((L, L), np.float32), 1)
    masks[nl] = np.eye(L, dtype=np.float32)
    return jnp.asarray(D, BF16), jnp.asarray(masks, F32)


def _gla_kernel(q_ref, k_ref, v_ref, la_ref, s0_ref, g_ref, d_ref, m_ref, o_ref, s_ref, *, L, n_chunks):
    c = pl.program_id(1)
    nl = int(math.log2(L))

    @pl.when(c == 0)
    def _():
        s_ref[...] = s0_ref[...]

    dmat = d_ref[...]
    ones_col = jnp.ones((L, LANES), BF16)
    g = g_ref[...]
    nt = (((1,), (1,)), ((), ()))
    tn = (((0,), (0,)), ((), ()))

    def chunk(ci, carry):
        r0 = pl.multiple_of(ci * L, L)
        rows = pl.ds(r0, L)
        for h in range(H_A):
            kc = slice(h * DK_A, (h + 1) * DK_A)
            vc = slice(h * DV_A, (h + 1) * DV_A)
            q = q_ref[rows, kc].astype(F32)
            k = k_ref[rows, kc].astype(F32)
            v = v_ref[rows, vc]
            la = la_ref[rows, kc]
            la_hi = la.astype(BF16)
            la_lo = (la - la_hi.astype(F32)).astype(BF16)
            e = (jnp.dot(dmat, la_hi, preferred_element_type=F32)
                 + jnp.dot(dmat, la_lo, preferred_element_type=F32))
            x = jnp.exp(e)
            a = m_ref[nl] * lax.dot_general(q.astype(BF16), k.astype(BF16), nt, preferred_element_type=F32)
            for l in range(nl):
                xl = x[l * L:(l + 1) * L]
                a = a + m_ref[l] * lax.dot_general((q * xl).astype(BF16), (k * xl).astype(BF16), nt,
                                                   preferred_element_type=F32)
            xb = x[nl * L:(nl + 1) * L]
            xs = x[(nl + 1) * L:]
            s_old = s_ref[0, h]
            o = (jnp.dot((q * xb).astype(BF16), s_old.astype(BF16), preferred_element_type=F32)
                 + jnp.dot(a.astype(BF16), v, preferred_element_type=F32))
            bl = (lax.dot_general(la_hi, ones_col, tn, preferred_element_type=F32)
                  + lax.dot_general(la_lo, ones_col, tn, preferred_element_type=F32))
            dec = jnp.exp(bl[:, 0:1])
            s_ref[0, h] = dec * s_old + lax.dot_general((k * xs).astype(BF16), v, tn, preferred_element_type=F32)
            ms = jnp.mean(o * o, axis=-1, keepdims=True)
            o_ref[rows, vc] = (o * lax.rsqrt(ms + EPS) * g).astype(BF16)
        return carry

    lax.fori_loop(0, n_chunks, chunk, 0, unroll=2 if n_chunks % 2 == 0 else 1)


def _gla(slab, loga, s0, g_norm, batch, t_len):
    L = min(CHUNK, t_len)
    tb = min(512, t_len)
    nb = t_len // tb
    dmat, masks = _gla_tables(L)
    kern = functools.partial(_gla_kernel, L=L, n_chunks=tb // L)
    row = lambda b, c: b * nb + c
    return pl.pallas_call(
        kern,
        grid=(batch, nb),
        in_specs=[
            pl.BlockSpec((tb, QA_W), lambda b, c: (row(b, c), SLAB_QA // QA_W)),
            pl.BlockSpec((tb, KA_W), lambda b, c: (row(b, c), SLAB_KA // KA_W)),
            pl.BlockSpec((tb, VA_W), lambda b, c: (row(b, c), SLAB_VA // VA_W)),
            pl.BlockSpec((tb, QA_W), lambda b, c: (row(b, c), 0)),
            pl.BlockSpec((1, H_A, DK_A, DV_A), lambda b, c: (b, 0, 0, 0)),
            pl.BlockSpec((1, DV_A), lambda b, c: (0, 0)),
            pl.BlockSpec(dmat.shape, lambda b, c: (0, 0)),
            pl.BlockSpec(masks.shape, lambda b, c: (0, 0, 0)),
        ],
        out_specs=[
            pl.BlockSpec((tb, VA_W), lambda b, c: (row(b, c), 0)),
            pl.BlockSpec((1, H_A, DK_A, DV_A), lambda b, c: (b, 0, 0, 0)),
        ],
        out_shape=[
            jax.ShapeDtypeStruct((batch * t_len, VA_W), BF16),
            jax.ShapeDtypeStruct((batch, H_A, DK_A, DV_A), F32),
        ],
        compiler_params=_cparams(("parallel", "arbitrary")),
        name="gla",
    )(slab, slab, slab, loga, s0, g_norm, dmat, masks)


def _lambda_from(dl_ref, lam_init):
    dl = dl_ref[...]
    a = jnp.sum(dl[0:1] * dl[1:2], axis=-1, keepdims=True)
    b = jnp.sum(dl[2:3] * dl[3:4], axis=-1, keepdims=True)
    return jnp.exp(a) - jnp.exp(b) + lam_init


def _split_maps(q):
    lane = lax.broadcasted_iota(I32, q.shape, 1)
    zero = jnp.zeros_like(q)
    return jnp.concatenate([jnp.where(lane < D_B, q, zero), jnp.where(lane >= D_B, q, zero)], axis=0)


def _alibi_coef(h):
    c = jnp.full((1, 1), LOG2E, F32) * jnp.exp2(-(h + 1).astype(F32))
    c_hi = c.astype(BF16).astype(F32)
    return c, c_hi, c - c_hi


def _finish_heads(acc, inv_l, lam, g, tq, lam_init):
    o = acc * inv_l
    out = o[:tq] - lam * o[tq:]
    ms = jnp.mean(out * out, axis=-1, keepdims=True)
    return (out * lax.rsqrt(ms + EPS) * g * (1.0 - lam_init)).astype(BF16)


def _attn_prompt_kernel(q_ref, k_ref, v_ref, g_ref, dl_ref, o_ref, ka_ref, vb_ref, qaug_ref, d_ref, m_ref,
                        acc_ref, s0_ref, s1_ref, p0_ref, p1_ref,
                        *, tq, tk, lam_init):
    h = pl.program_id(1)
    qi = pl.program_id(2)
    c, c_hi, c_lo = _alibi_coef(h)
    hw = 2 * D_B

    def aug_lanes(shape, pos, sign):
        lane = lax.broadcasted_iota(I32, shape, 1)
        r = (pos & 255).astype(F32) * sign
        a = (pos >> 8).astype(F32) * sign
        coef = jnp.where(lane == 0, c_hi, jnp.where(lane == 1, c_lo,
                         jnp.where(lane == 2, 256.0 * c_hi, jnp.where(lane == 3, 256.0 * c_lo, 0.0))))
        ints = jnp.where(lane < 2, r, jnp.where(lane < 4, a, 0.0))
        return coef, ints, lane

    @pl.when(qi == 0)
    def _():
        t_len = vb_ref.shape[0]
        lane = lax.broadcasted_iota(I32, (t_len, hw), 1)
        row = lax.broadcasted_iota(I32, (hw, tk), 0)
        j_rel = lax.broadcasted_iota(I32, (hw, tk), 1)
        k_aug_t = jnp.where(row < 2, (j_rel & 255).astype(F32), jnp.where(row < 4, (j_rel >> 8).astype(F32),
                  jnp.where(row == 4, c_hi, jnp.where(row == 5, c_lo,
                  jnp.where(row == 6, 256.0 * c_hi, jnp.where(row == 7, 256.0 * c_lo, 0.0)))))).astype(BF16)
        for blk in range(t_len // tk):
            rows = slice(blk * tk, (blk + 1) * tk)
            ka_ref[blk, :hw, :] = k_ref[0, rows, :].astype(F32).T.astype(BF16)
            ka_ref[blk, hw:, :] = k_aug_t
        vb_ref[:, :hw] = v_ref[0].astype(BF16)
        vb_ref[:, hw:] = jnp.where(lane == 0, 1.0, 0.0).astype(BF16)
        i_rel = lax.broadcasted_iota(I32, (2 * tq, hw), 0) & (tq - 1)
        coef, ints, lane = aug_lanes((2 * tq, hw), i_rel, -1.0)
        qaug_ref[...] = jnp.where(lane < 4, coef, pltpu.roll(ints, 4, 1)).astype(BF16)
        i = lax.broadcasted_iota(I32, (2 * tq, tq), 0) & (tq - 1)
        j = lax.broadcasted_iota(I32, (2 * tq, tq), 1)
        fwd = jnp.maximum(j - i, 0).astype(F32)
        d_ref[...] = jnp.where((j >> 6) <= (i >> 6), -2.0 * c * fwd, NEG_INF)

    qa = jnp.concatenate([_split_maps(q_ref[...]), qaug_ref[...]], axis=1)
    m_ref[...] = jnp.full(m_ref.shape, -1e30, F32)

    def scores(kv):
        return jnp.dot(qa, ka_ref[kv], preferred_element_type=F32)

    def weighted_values(p_ref, kv):
        k0 = pl.multiple_of(kv * tk, tk)
        return jnp.dot(p_ref[...], vb_ref[pl.ds(k0, tk), :], preferred_element_type=F32)

    def softmax_step(s, off):
        m_old = m_ref[...]
        m_new = jnp.maximum(m_old, jnp.max(s, axis=-1, keepdims=True) - off)
        shift = m_new + off
        p = jnp.exp2(s - jnp.concatenate([shift] * (tk // hw), axis=1))
        alpha = jnp.exp2(m_old - m_new)
        m_ref[...] = m_new
        return p.astype(BF16), jnp.concatenate([alpha, alpha], axis=1)

    s_first = scores(0)
    p_diag, _ = softmax_step(scores(qi) + d_ref[...], jnp.zeros((1, 1), F32))
    p1_ref[...] = p_diag
    s0_ref[...] = s_first
    acc_ref[...] = jnp.zeros(acc_ref.shape, F32)

    def step(kv, s_cur, s_nxt, p_prev, p_cur):
        pv_prev = weighted_values(p_prev, jnp.where(kv == 0, qi, kv - 1))
        s_nxt[...] = scores(jnp.minimum(kv + 1, qi - 1))
        p, alpha = softmax_step(s_cur[...], c * (qi * tq - kv * tk).astype(F32))
        acc_ref[...] = alpha * (acc_ref[...] + pv_prev)
        p_cur[...] = p

    def pair(j, carry):
        step(2 * j, s0_ref, s1_ref, p1_ref, p0_ref)
        step(2 * j + 1, s1_ref, s0_ref, p0_ref, p1_ref)
        return carry

    lax.fori_loop(0, qi // 2, pair, 0)

    @pl.when(qi % 2 == 1)
    def _():
        step(qi - 1, s0_ref, s1_ref, p1_ref, p0_ref)
        acc_ref[...] += weighted_values(p0_ref, qi - 1)

    @pl.when(qi % 2 == 0)
    def _():
        acc_ref[...] += weighted_values(p1_ref, jnp.maximum(qi - 1, 0))

    lam = _lambda_from(dl_ref, lam_init)
    acc = acc_ref[...]
    o_ref[...] = _finish_heads(acc[:, :hw], 1.0 / acc[:, hw:hw + 1], lam, g_ref[...], tq, lam_init)


def _attn_prompt(slab, kb, vb, g_norm, dl, batch, t_len, lam_init):
    tq = min(512, t_len)
    tk = tq
    assert tq % CHUNK == 0 and CHUNK == 64
    nq = t_len // tq
    k3 = kb.reshape(batch, t_len, KB_W)
    v3 = vb.reshape(batch, t_len, VB_W)
    kern = functools.partial(_attn_prompt_kernel, tq=tq, tk=tk, lam_init=lam_init)
    hw = 2 * D_B
    return pl.pallas_call(
        kern,
        grid=(batch, H_B, nq),
        in_specs=[
            pl.BlockSpec((tq, hw), lambda b, h, q: (b * nq + q, SLAB_QB // hw + h)),
            pl.BlockSpec((1, t_len, hw), lambda b, h, q: (b, 0, h)),
            pl.BlockSpec((1, t_len, hw), lambda b, h, q: (b, 0, h)),
            pl.BlockSpec((1, hw), lambda b, h, q: (0, 0)),
            pl.BlockSpec((4, D_B), lambda b, h, q: (0, 0)),
        ],
        out_specs=pl.BlockSpec((tq, hw), lambda b, h, q: (b * nq + q, h)),
        out_shape=jax.ShapeDtypeStruct((batch * t_len, VB_W), BF16),
        scratch_shapes=[
            pltpu.VMEM((t_len // tk, 2 * hw, tk), BF16),
            pltpu.VMEM((t_len, 2 * hw), BF16),
            pltpu.VMEM((2 * tq, hw), BF16),
            pltpu.VMEM((2 * tq, tq), F32),
            pltpu.VMEM((2 * tq, hw), F32),
            pltpu.VMEM((2 * tq, 2 * hw), F32),
            pltpu.VMEM((2 * tq, tk), F32),
            pltpu.VMEM((2 * tq, tk), F32),
            pltpu.VMEM((2 * tq, tk), BF16),
            pltpu.VMEM((2 * tq, tk), BF16),
        ],
        compiler_params=_cparams(("parallel", "parallel", "arbitrary")),
        name="attn_prompt",
    )(slab, k3, v3, g_norm, dl)


def _attn_sample_kernel(q_ref, kc_ref, vc_ref, kn_ref, vn_ref, g_ref, dl_ref, o_ref, *, tq, past, lam_init):
    h = pl.program_id(1)
    c, _, _ = _alibi_coef(h)
    qq = _split_maps(q_ref[...])
    q_pos = past + lax.broadcasted_iota(I32, (2 * tq, 1), 0) % tq

    def scores(k, k_pos):
        s = lax.dot_general(qq, k.astype(BF16), (((1,), (1,)), ((), ())), preferred_element_type=F32)
        s = s - c * jnp.abs(q_pos - k_pos).astype(F32)
        return jnp.where((k_pos // CHUNK) <= (q_pos // CHUNK), s, NEG_INF)

    sc = scores(kc_ref[0], lax.broadcasted_iota(I32, (1, past), 1))
    sn = scores(kn_ref[0], past + lax.broadcasted_iota(I32, (1, tq), 1))
    m = jnp.maximum(jnp.max(sc, axis=-1, keepdims=True), jnp.max(sn, axis=-1, keepdims=True))
    pc = jnp.exp2(sc - m)
    pn = jnp.exp2(sn - m)
    l = jnp.sum(pc, axis=-1, keepdims=True) + jnp.sum(pn, axis=-1, keepdims=True)
    acc = (jnp.dot(pc.astype(BF16), vc_ref[0].astype(BF16), preferred_element_type=F32)
           + jnp.dot(pn.astype(BF16), vn_ref[0].astype(BF16), preferred_element_type=F32))
    lam = _lambda_from(dl_ref, lam_init)
    o_ref[...] = _finish_heads(acc, 1.0 / l, lam, g_ref[...], tq, lam_init)


def _attn_sample(slab, kb, vb, cache_k, cache_v, g_norm, dl, batch, t_len, lam_init):
    past = cache_k.shape[1]
    hw = 2 * D_B
    k3 = kb.reshape(batch, t_len, KB_W)
    v3 = vb.reshape(batch, t_len, VB_W)
    kern = functools.partial(_attn_sample_kernel, tq=t_len, past=past, lam_init=lam_init)
    return pl.pallas_call(
        kern,
        grid=(batch, H_B),
        in_specs=[
            pl.BlockSpec((t_len, hw), lambda b, h: (b, SLAB_QB // hw + h)),
            pl.BlockSpec((1, past, hw), lambda b, h: (b, 0, h)),
            pl.BlockSpec((1, past, hw), lambda b, h: (b, 0, h)),
            pl.BlockSpec((1, t_len, hw), lambda b, h: (b, 0, h)),
            pl.BlockSpec((1, t_len, hw), lambda b, h: (b, 0, h)),
            pl.BlockSpec((1, hw), lambda b, h: (0, 0)),
            pl.BlockSpec((4, D_B), lambda b, h: (0, 0)),
        ],
        out_specs=pl.BlockSpec((t_len, hw), lambda b, h: (b, h)),
        out_shape=jax.ShapeDtypeStruct((batch * t_len, VB_W), BF16),
        compiler_params=_cparams(("parallel", "parallel")),
        name="attn_sample",
    )(slab, cache_k, cache_v, k3, v3, g_norm, dl)


def _finish_kernel(x_ref, oa_ref, ra_ref, ob_ref, gt_ref, wpa_ref, wpb_ref, wo_ref, g_ref, b_ref,
                   wrh_ref, wrl_ref, br_ref, h_ref, hp_ref, idx_ref, tw_ref, cnt_ref):
    i = pl.program_id(0)
    ra = ra_ref[...].astype(F32)
    ua = ra * _sigmoid(ra) * oa_ref[...].astype(F32)
    ya = jnp.dot(ua.astype(BF16), wpa_ref[...], preferred_element_type=F32)
    yb = jnp.dot(ob_ref[...], wpb_ref[...], preferred_element_type=F32)
    gt = gt_ref[...].astype(F32)
    mixed = _sigmoid(gt[:, :D_MODEL]) * ya + _sigmoid(gt[:, D_MODEL:]) * yb
    mix = jnp.dot(mixed.astype(BF16), wo_ref[...], preferred_element_type=F32)
    hh = _layer_norm(DEEPNORM_ALPHA * x_ref[...] + mix, g_ref[...], b_ref[...])
    h_ref[...] = hh
    _store_rows(hp_ref, _pack_bf16_pairs(hh))

    h_hi = hh.astype(BF16)
    h_lo = (hh - h_hi.astype(F32)).astype(BF16)
    lg = (jnp.dot(h_hi, wrh_ref[...], preferred_element_type=F32)
          + jnp.dot(h_lo, wrh_ref[...], preferred_element_type=F32)
          + jnp.dot(h_hi, wrl_ref[...], preferred_element_type=F32)) + br_ref[...]
    tm = lg.shape[0]
    lane = lax.broadcasted_iota(I32, (tm, LANES), 1)
    lane_f = lane.astype(F32)
    work = lg
    vals, idxs = [], []
    cnt = jnp.zeros((tm, LANES), F32)
    for _ in range(TOP_K):
        mx = jnp.max(work, axis=-1, keepdims=True)
        ix = jnp.min(jnp.where(work == mx, lane_f, float(LANES)), axis=-1, keepdims=True)
        hit = lane_f == ix
        cnt = cnt + hit.astype(F32)
        work = jnp.where(hit, NEG_INF, work)
        vals.append(mx)
        idxs.append(ix)
    es = [jnp.exp(v - vals[0]) for v in vals]
    den = es[0] + es[1] + es[2] + es[3]
    idx_full = jnp.zeros((tm, LANES), F32)
    tw_full = jnp.zeros((tm, LANES), F32)
    for r in range(TOP_K):
        idx_full = jnp.where(lane == r, idxs[r], idx_full)
        tw_full = jnp.where(lane == r, es[r] / den, tw_full)
    idx_ref[...] = idx_full.T[:SUBLANES].astype(I32)
    tw_ref[...] = tw_full[:, :TOP_K]

    @pl.when(i == 0)
    def _():
        cnt_ref[...] = jnp.zeros(cnt_ref.shape, F32)

    cnt_ref[...] += jnp.sum(cnt, axis=0, keepdims=True)


def _finish(x2, oa, slab, ob, p):
    n = x2.shape[0]
    tm = min(512, n)
    const = lambda i: (0, 0)
    return pl.pallas_call(
        _finish_kernel,
        grid=(n // tm,),
        in_specs=[
            pl.BlockSpec((tm, D_MODEL), lambda i: (i, 0)),
            pl.BlockSpec((tm, VA_W), lambda i: (i, 0)),
            pl.BlockSpec((tm, RA_W), lambda i: (i, SLAB_RA // RA_W)),
            pl.BlockSpec((tm, VB_W), lambda i: (i, 0)),
            pl.BlockSpec((tm, GT_W), lambda i: (i, SLAB_GT // GT_W)),
            pl.BlockSpec((VA_W, D_MODEL), const),
            pl.BlockSpec((VB_W, D_MODEL), const),
            pl.BlockSpec((D_MODEL, D_MODEL), const),
            pl.BlockSpec((1, D_MODEL), const),
            pl.BlockSpec((1, D_MODEL), const),
            pl.BlockSpec((D_MODEL, LANES), const),
            pl.BlockSpec((D_MODEL, LANES), const),
            pl.BlockSpec((1, LANES), const),
        ],
        out_specs=[
            pl.BlockSpec((tm, D_MODEL), lambda i: (i, 0)),
            pl.BlockSpec((tm * ROW_TILES, LANES), lambda i: (i, 0)),
            pl.BlockSpec((SUBLANES, tm), lambda i: (0, i)),
            pl.BlockSpec((tm, TOP_K), lambda i: (i, 0)),
            pl.BlockSpec((1, LANES), const),
        ],
        out_shape=[
            jax.ShapeDtypeStruct((n, D_MODEL), F32),
            jax.ShapeDtypeStruct((n * ROW_TILES, LANES), U32),
            jax.ShapeDtypeStruct((SUBLANES, n), I32),
            jax.ShapeDtypeStruct((n, TOP_K), F32),
            jax.ShapeDtypeStruct((1, LANES), F32),
        ],
        compiler_params=_cparams(("arbitrary",)),
        name="finish",
    )(x2, oa, slab, ob, slab, p["w_pa"], p["w_pb"], p["w_o"], p["ln1_g"], p["ln1_b"],
      p["w_r_hi"], p["w_r_lo"], p["b_r"])


def _dest_kernel(idx_ref, ps_ref, ut_ref, dest_ref, carry_ref):
    i = pl.program_id(0)

    @pl.when(i == 0)
    def _():
        carry_ref[...] = jnp.zeros(carry_ref.shape, F32)

    tm = idx_ref.shape[1]
    expert = lax.broadcasted_iota(I32, (N_EXPERTS, tm), 0)
    idx = idx_ref[...]
    hits = [expert == idx[k:k + 1, :] for k in range(TOP_K)]
    cnt = jnp.zeros((N_EXPERTS, tm), F32)
    for hit in hits:
        cnt = cnt + hit.astype(F32)
    before = jnp.dot(cnt.astype(BF16), ut_ref[...], preferred_element_type=F32)
    base = before + carry_ref[:, 0:1] + ps_ref[:, 0:1]
    row = lax.broadcasted_iota(I32, (SUBLANES, tm), 0)
    dest = jnp.zeros((SUBLANES, tm), F32)
    for k in range(TOP_K):
        d = jnp.sum(jnp.where(hits[k], base, 0.0), axis=0, keepdims=True)
        dest = jnp.where(row == k, d, dest)
    dest_ref[...] = dest.astype(I32)
    carry_ref[...] += jnp.sum(cnt, axis=1, keepdims=True)


def _dest(idx_t, pstart):
    n = idx_t.shape[1]
    tm = min(512, n)
    utri = jnp.asarray(np.triu(np.ones((tm, tm), np.float32), 1), BF16)
    return pl.pallas_call(
        _dest_kernel,
        grid=(n // tm,),
        in_specs=[
            pl.BlockSpec((SUBLANES, tm), lambda i: (0, i)),
            pl.BlockSpec((N_EXPERTS, LANES), lambda i: (0, 0)),
            pl.BlockSpec((tm, tm), lambda i: (0, 0)),
        ],
        out_specs=pl.BlockSpec((SUBLANES, tm), lambda i: (0, i)),
        out_shape=jax.ShapeDtypeStruct((SUBLANES, n), I32),
        scratch_shapes=[pltpu.VMEM((N_EXPERTS, LANES), F32)],
        compiler_params=_cparams(("arbitrary",)),
        name="dest",
    )(idx_t, pstart, utri)


def _scatter_kernel(dest_ref, hp_ref, rows_in_ref, rows_ref, sem):
    del rows_in_ref
    tm = hp_ref.shape[0] // ROW_TILES

    def row_copy(t, d):
        return pltpu.make_async_copy(hp_ref.at[pl.ds(pl.multiple_of(t * ROW_TILES, ROW_TILES), ROW_TILES)],
                                     rows_ref.at[pl.ds(pl.multiple_of(d * ROW_TILES, ROW_TILES), ROW_TILES)], sem)

    def issue(t, carry):
        for k in range(TOP_K):
            row_copy(t, dest_ref[k, t]).start(priority=k % 2)
        return carry

    lax.fori_loop(0, tm, issue, 0, unroll=4)
    for k in range(TOP_K):
        pltpu.make_async_copy(hp_ref, rows_ref.at[pl.ds(0, tm * ROW_TILES)], sem).wait()


def _scatter(dest_t, hp, n_rows):
    n = hp.shape[0] // ROW_TILES
    tm = min(512, n)
    rows0 = jnp.zeros((n_rows * ROW_TILES, LANES), U32)
    return pl.pallas_call(
        _scatter_kernel,
        grid=(n // tm,),
        in_specs=[
            pl.BlockSpec((SUBLANES, tm), lambda i: (0, i), memory_space=pltpu.SMEM),
            pl.BlockSpec((tm * ROW_TILES, LANES), lambda i: (i, 0)),
            pl.BlockSpec(memory_space=pl.ANY),
        ],
        out_specs=pl.BlockSpec(memory_space=pl.ANY),
        out_shape=jax.ShapeDtypeStruct((n_rows * ROW_TILES, LANES), U32),
        scratch_shapes=[pltpu.SemaphoreType.DMA(())],
        input_output_aliases={2: 0},
        compiler_params=_cparams(("arbitrary",)),
        name="scatter",
    )(dest_t, hp, rows0)


def _expert_kernel(be_ref, nu_ref, rows_ref, wgu_ref, bgu_ref, wd_ref, bd_ref, out_ref):
    del be_ref
    i = pl.program_id(0)

    @pl.when(i < nu_ref[0])
    def _():
        x = _unpack_bf16_pairs(_load_rows(rows_ref)).astype(BF16)
        gu = jnp.dot(x, wgu_ref[0], preferred_element_type=F32) + bgu_ref[0]
        gate = jnp.minimum(gu[:, :D_FF], SWIGLU_LIMIT)
        up = jnp.clip(gu[:, D_FF:], -SWIGLU_LIMIT, SWIGLU_LIMIT)
        hidden = (up + 1.0) * (gate * _sigmoid(SWIGLU_ALPHA * gate))
        o = jnp.dot(hidden.astype(BF16), wd_ref[0], preferred_element_type=F32) + bd_ref[0]
        _store_rows(out_ref, _pack_bf16_pairs(o))

    @pl.when(i >= nu_ref[0])
    def _():
        out_ref[...] = jnp.zeros(out_ref.shape, U32)


def _experts(rows, block_e, n_used, p):
    n_rows = rows.shape[0] // ROW_TILES
    bm = EXPERT_BM
    n_blocks = n_rows // bm
    grid_spec = pltpu.PrefetchScalarGridSpec(
        num_scalar_prefetch=2,
        grid=(n_blocks,),
        in_specs=[
            pl.BlockSpec((bm * ROW_TILES, LANES), lambda i, be, nu: (i, 0)),
            pl.BlockSpec((1, D_MODEL, 2 * D_FF), lambda i, be, nu: (be[i], 0, 0)),
            pl.BlockSpec((1, 1, 2 * D_FF), lambda i, be, nu: (be[i], 0, 0)),
            pl.BlockSpec((1, D_FF, D_MODEL), lambda i, be, nu: (be[i], 0, 0)),
            pl.BlockSpec((1, 1, D_MODEL), lambda i, be, nu: (be[i], 0, 0)),
        ],
        out_specs=pl.BlockSpec((bm * ROW_TILES, LANES), lambda i, be, nu: (i, 0)),
    )
    return pl.pallas_call(
        _expert_kernel,
        grid_spec=grid_spec,
        out_shape=jax.ShapeDtypeStruct((n_rows * ROW_TILES, LANES), U32),
        compiler_params=_cparams(("arbitrary",)),
        name="experts",
    )(block_e, n_used, rows, p["w_gu"], p["b_gu"], p["w_down"], p["b_down"])


def _combine_kernel(dest_ref, tw_ref, h_ref, rows_ref, g_ref, b_ref, y_ref, buf_ref, sem):
    tm = h_ref.shape[0]

    def row_copy(t, k, d):
        return pltpu.make_async_copy(rows_ref.at[pl.ds(pl.multiple_of(d * ROW_TILES, ROW_TILES), ROW_TILES)],
                                     buf_ref.at[k, pl.ds(pl.multiple_of(t * ROW_TILES, ROW_TILES), ROW_TILES)], sem)

    def issue(t, carry):
        for k in range(TOP_K):
            row_copy(t, k, dest_ref[k, t]).start(priority=k % 2)
        return carry

    lax.fori_loop(0, tm, issue, 0, unroll=4)
    for k in range(TOP_K):
        pltpu.make_async_copy(rows_ref.at[pl.ds(0, tm * ROW_TILES)], buf_ref.at[k], sem).wait()
    tw = tw_ref[...]
    f = jnp.zeros((tm, D_MODEL), F32)
    for k in range(TOP_K):
        f = f + tw[:, k:k + 1] * _unpack_bf16_pairs(_load_rows(buf_ref.at[k]))
    y_ref[...] = _layer_norm(DEEPNORM_ALPHA * h_ref[...] + f, g_ref[...], b_ref[...])


def _combine(dest_t, top_w, hh, out_rows, p):
    n = hh.shape[0]
    tm = min(512, n)
    const = lambda i: (0, 0)
    return pl.pallas_call(
        _combine_kernel,
        grid=(n // tm,),
        in_specs=[
            pl.BlockSpec((SUBLANES, tm), lambda i: (0, i), memory_space=pltpu.SMEM),
            pl.BlockSpec((tm, TOP_K), lambda i: (i, 0)),
            pl.BlockSpec((tm, D_MODEL), lambda i: (i, 0)),
            pl.BlockSpec(memory_space=pl.ANY),
            pl.BlockSpec((1, D_MODEL), const),
            pl.BlockSpec((1, D_MODEL), const),
        ],
        out_specs=pl.BlockSpec((tm, D_MODEL), lambda i: (i, 0)),
        out_shape=jax.ShapeDtypeStruct((n, D_MODEL), F32),
        scratch_shapes=[pltpu.VMEM((TOP_K, tm * ROW_TILES, LANES), U32), pltpu.SemaphoreType.DMA(())],
        compiler_params=_cparams(("arbitrary",)),
        name="combine",
    )(dest_t, top_w, hh, out_rows, p["ln2_g"], p["ln2_b"])


def _moe(hh, hp, top_idx, top_w, counts, p):
    n = hh.shape[0]
    bm = EXPERT_BM
    n_rows = n * TOP_K + N_EXPERTS * bm
    cnt = counts[0, :N_EXPERTS].astype(I32)
    padded = (cnt + bm - 1) // bm * bm
    pend = jnp.cumsum(padded)
    pstart = pend - padded
    ps_col = jnp.broadcast_to(pstart.astype(F32)[:, None], (N_EXPERTS, LANES))
    block_row0 = jnp.arange(n_rows // bm, dtype=I32) * bm
    block_e = jnp.minimum(jnp.sum((pend[None, :] <= block_row0[:, None]).astype(I32), axis=1), N_EXPERTS - 1)
    n_used = (pend[-1:] // bm).astype(I32)
    dest = _dest(top_idx, ps_col)
    rows = _scatter(dest, hp, n_rows)
    out_rows = _experts(rows, block_e, n_used, p)
    return _combine(dest, top_w, hh, out_rows, p)


def _prep_params(w_in, b_in, w_alpha, b_alpha, gla_norm_g, diff_norm_g, w_pa, w_pb, w_o, ln1_g, ln1_b,
                 w_router, b_router, w_gu, b_gu, w_down, b_down, ln2_g, ln2_b):
    o = IN_OFFS
    seg = lambda a, i: a[..., o[i]:o[i + 1]]
    order = (0, 1, 2, 3, 5, 8)
    w_slab = jnp.concatenate([seg(w_in, i) for i in order], axis=1).astype(BF16)
    b_slab = jnp.concatenate([seg(b_in, i) for i in order])[None, :]
    scale = np.ones((1, SLAB_W), np.float32)
    scale[0, SLAB_QA:SLAB_QA + QA_W] = DK_A ** -0.5
    scale[0, SLAB_QB:SLAB_QB + QB_W] = D_B ** -0.5 * LOG2E
    w_lr = jnp.zeros((D_MODEL, LANES), F32).at[:, :LR_W].set(seg(w_in, 4)).astype(BF16)
    b_lr = jnp.zeros((1, LANES), F32).at[0, :LR_W].set(seg(b_in, 4))
    w_al = jnp.zeros((LANES, QA_W), F32).at[:LR_W].set(w_alpha).astype(BF16)
    w_r = jnp.zeros((D_MODEL, LANES), F32).at[:, :N_EXPERTS].set(w_router)
    w_r_hi = w_r.astype(BF16)
    w_r_lo = (w_r - w_r_hi.astype(F32)).astype(BF16)
    b_r = jnp.full((1, LANES), NEG_INF, F32).at[0, :N_EXPERTS].set(b_router)
    return dict(
        w_slab=w_slab, b_slab=b_slab, s_slab=jnp.asarray(scale),
        w_lr=w_lr, b_lr=b_lr, w_alpha=w_al, b_alpha=b_alpha[None, :],
        w_k=seg(w_in, 6).astype(BF16), w_v=seg(w_in, 7).astype(BF16),
        b_k=seg(b_in, 6)[None, :], b_v=seg(b_in, 7)[None, :],
        gla_g=gla_norm_g[None, :], diff_g=diff_norm_g[None, :],
        w_pa=w_pa.astype(BF16), w_pb=w_pb.astype(BF16), w_o=w_o.astype(BF16),
        ln1_g=ln1_g[None, :], ln1_b=ln1_b[None, :],
        w_r_hi=w_r_hi, w_r_lo=w_r_lo, b_r=b_r,
        w_gu=w_gu.astype(BF16), b_gu=b_gu[:, None, :], w_down=w_down.astype(BF16), b_down=b_down[:, None, :],
        ln2_g=ln2_g[None, :], ln2_b=ln2_b[None, :],
    )


def _layer(x, p, dl, lam_init, s0, cache_k, cache_v):
    batch, t_len, _ = x.shape
    x2 = x.reshape(batch * t_len, D_MODEL)
    slab, loga, k_rows, kb, vb = _proj(x2, p)
    oa, s_new = _gla(slab, loga, s0, p["gla_g"], batch, t_len)
    if cache_k is None:
        ob = _attn_prompt(slab, kb, vb, p["diff_g"], dl, batch, t_len, lam_init)
    else:
        ob = _attn_sample(slab, kb, vb, cache_k, cache_v, p["diff_g"], dl, batch, t_len, lam_init)
    hh, hp, top_idx, top_w, counts = _finish(x2, oa, slab, ob, p)
    y = _moe(hh, hp, top_idx, top_w, counts, p)
    return (y.reshape(batch, t_len, D_MODEL), s_new,
            k_rows.reshape(batch, t_len, H_B, 2, D_B), vb.reshape(batch, t_len, H_B, 2 * D_B))


def kernel(x_prompt, x_sample, cache_k, cache_v, state_gla, w_in, b_in, w_alpha, b_alpha, gla_norm_g,
           diff_lambda, diff_norm_g, w_pa, w_pb, w_o, ln1_g, ln1_b, w_router, b_router, w_gu, b_gu,
           w_down, b_down, ln2_g, ln2_b):
    assert w_in.shape[0] == DEPTH == 1
    l = 0
    lam_init = 0.8 - 0.6 * math.exp(-0.3 * l)
    p = _prep_params(w_in[l], b_in[l], w_alpha[l], b_alpha[l], gla_norm_g[l], diff_norm_g[l], w_pa[l],
                     w_pb[l], w_o[l], ln1_g[l], ln1_b[l], w_router[l], b_router[l], w_gu[l], b_gu[l],
                     w_down[l], b_down[l], ln2_g[l], ln2_b[l])
    dl = diff_lambda[l]
    bp = x_prompt.shape[0]
    bs, ts = x_sample.shape[0], x_sample.shape[1]
    past = cache_k.shape[2]
    yp, s_p, k_p, v_p = _layer(x_prompt, p, dl, lam_init, jnp.zeros((bp, H_A, DK_A, DV_A), F32), None, None)
    ck = cache_k[l].reshape(bs, past, KB_W)
    cv = cache_v[l].reshape(bs, past, VB_W)
    ys, s_s, k_s, v_s = _layer(x_sample, p, dl, lam_init, state_gla[l], ck, cv)
    return (yp, ys, s_p[None], k_p[None], v_p[None], s_s[None], k_s[None], v_s[None])
```

```python
import functools
import math

import numpy as np
import jax
import jax.numpy as jnp
from jax import lax
from jax.experimental import pallas as pl
from jax.experimental.pallas import tpu as pltpu

F32 = jnp.float32
BF16 = jnp.bfloat16
U32 = jnp.uint32
I32 = jnp.int32

D_MODEL = 1024
CHUNK = 64
H_A = 4
DK_A = 128
DV_A = 256
GATE_RANK = 16
GATE_TAU = 16.0
H_B = 8
D_B = 64
N_EXPERTS = 32
TOP_K = 4
D_FF = D_MODEL
SWIGLU_LIMIT = 7.0
SWIGLU_ALPHA = 1.702
EPS = 1e-5
DEPTH = 1
DEEPNORM_ALPHA = (2.0 * DEPTH) ** 0.25

QA_W = H_A * DK_A
KA_W = H_A * DK_A
VA_W = H_A * DV_A
RA_W = H_A * DV_A
LR_W = GATE_RANK
QB_W = H_B * 2 * D_B
KB_W = H_B * 2 * D_B
VB_W = H_B * 2 * D_B
GT_W = 2 * D_MODEL
IN_SIZES = (QA_W, KA_W, VA_W, RA_W, LR_W, QB_W, KB_W, VB_W, GT_W)
IN_OFFS = tuple(int(v) for v in np.cumsum((0,) + IN_SIZES))

SLAB_QA, SLAB_KA, SLAB_VA, SLAB_RA, SLAB_QB, SLAB_GT = 0, 512, 1024, 2048, 3072, 4096
SLAB_W = 6144
LANES = 128
SUBLANES = 8
EXPERT_BM = 512
VMEM_LIMIT = 56 * 1024 * 1024
PROJ_VMEM_LIMIT = 61 * 1024 * 1024

NEG_INF = float("-inf")
LOG2E = math.log2(math.e)


def _cparams(sem, vmem_limit=VMEM_LIMIT):
    return pltpu.CompilerParams(dimension_semantics=sem, vmem_limit_bytes=vmem_limit)


def _sigmoid(x):
    return 1.0 / (1.0 + jnp.exp(-x))


def _pack_bf16_pairs(x):
    n = x.shape[1] // 2
    xb = x.astype(BF16).astype(F32)
    lo = pltpu.bitcast(xb[:, :n], U32) >> 16
    hi = pltpu.bitcast(xb[:, n:], U32) & jnp.uint32(0xFFFF0000)
    return hi | lo


def _unpack_bf16_pairs(r):
    lo = pltpu.bitcast(r << 16, F32)
    hi = pltpu.bitcast(r & jnp.uint32(0xFFFF0000), F32)
    return jnp.concatenate([lo, hi], axis=1)


ROW_TILES = D_MODEL // 2 // LANES


def _load_rows(ref):
    m = ref.shape[0] // ROW_TILES
    return jnp.concatenate([ref[pl.ds(c, m, stride=ROW_TILES), :] for c in range(ROW_TILES)], axis=1)


def _store_rows(ref, val):
    m = ref.shape[0] // ROW_TILES
    for c in range(ROW_TILES):
        ref[pl.ds(c, m, stride=ROW_TILES), :] = val[:, c * LANES:(c + 1) * LANES]


def _layer_norm(y, g, b):
    mu = jnp.mean(y, axis=-1, keepdims=True)
    d = y - mu
    var = jnp.mean(d * d, axis=-1, keepdims=True)
    return d * lax.rsqrt(var + EPS) * g + b


def _proj_kernel(x_ref, w_ref, b_ref, s_ref, wlr_ref, blr_ref, wal_ref, bal_ref, wk_ref, wv_ref, bk_ref, bv_ref,
                 slab_ref, loga_ref, krows_ref, kb_ref, v_ref, *, tn):
    xb = x_ref[...].astype(BF16)
    k = jnp.dot(xb, wk_ref[...], preferred_element_type=F32) + bk_ref[...]
    kb_ref[...] = k.astype(BF16)
    for h in range(H_B):
        for j in range(2):
            c0 = (2 * h + j) * D_B
            krows_ref[:, h, j, :] = k[:, c0:c0 + D_B]
    v_ref[...] = jnp.dot(xb, wv_ref[...], preferred_element_type=F32) + bv_ref[...]
    lra = jnp.dot(xb, wlr_ref[...], preferred_element_type=F32) + blr_ref[...]
    z = jnp.dot(lra.astype(BF16), wal_ref[...], preferred_element_type=F32) + bal_ref[...]
    loga_ref[...] = (jnp.minimum(z, 0.0) - jnp.log(1.0 + jnp.exp(-jnp.abs(z)))) * (1.0 / GATE_TAU)
    for c0 in range(0, SLAB_W, tn):
        acc = jnp.dot(xb, w_ref[:, c0:c0 + tn], preferred_element_type=F32)
        slab_ref[:, c0:c0 + tn] = ((acc + b_ref[:, c0:c0 + tn]) * s_ref[:, c0:c0 + tn]).astype(BF16)


def _proj(x2, p):
    n = x2.shape[0]
    tm = min(512, n)
    const = lambda i: (0, 0)
    whole = lambda shape: pl.BlockSpec(shape, const, pipeline_mode=pl.Buffered(1))
    row = lambda w: pl.BlockSpec((tm, w), lambda i: (i, 0))
    return pl.pallas_call(
        functools.partial(_proj_kernel, tn=1024),
        grid=(n // tm,),
        in_specs=[
            row(D_MODEL),
            whole((D_MODEL, SLAB_W)), whole((1, SLAB_W)), whole((1, SLAB_W)),
            whole((D_MODEL, LANES)), whole((1, LANES)), whole((LANES, QA_W)), whole((1, QA_W)),
            whole((D_MODEL, KB_W)), whole((D_MODEL, VB_W)), whole((1, KB_W)), whole((1, VB_W)),
        ],
        out_specs=[
            row(SLAB_W), row(QA_W),
            pl.BlockSpec((tm, H_B, 2, D_B), lambda i: (i, 0, 0, 0)),
            row(KB_W), row(VB_W),
        ],
        out_shape=[
            jax.ShapeDtypeStruct((n, SLAB_W), BF16),
            jax.ShapeDtypeStruct((n, QA_W), F32),
            jax.ShapeDtypeStruct((n, H_B, 2, D_B), F32),
            jax.ShapeDtypeStruct((n, KB_W), BF16),
            jax.ShapeDtypeStruct((n, VB_W), F32),
        ],
        compiler_params=_cparams(("parallel",), PROJ_VMEM_LIMIT),
        name="proj",
    )(x2, p["w_slab"], p["b_slab"], p["s_slab"], p["w_lr"], p["b_lr"], p["w_alpha"], p["b_alpha"],
      p["w_k"], p["w_v"], p["b_k"], p["b_v"])


def _gla_tables(L):
    nl = int(math.log2(L))
    t = np.arange(L)
    D = np.zeros(((nl + 2) * L, L), np.float32)
    masks = np.zeros((nl + 1, L, L), np.float32)
    for l in range(nl):
        m = L >> (l + 1)
        grp = t // (2 * m)
        mid = grp * 2 * m + m - 1
        upper = (t % (2 * m)) >= m
        for r in range(L):
            if upper[r]:
                D[l * L + r, mid[r] + 1:r + 1] = 1.0
            else:
                D[l * L + r, r + 1:mid[r] + 1] = 1.0
        masks[l] = (upper[:, None] & ~upper[None, :] & (grp[:, None] == grp[None, :])).astype(np.float32)
    D[nl * L:(nl + 1) * L] = np.tril(np.ones((L, L), np.float32))
    D[(nl + 1) * L:] = np.triu(np.ones((L, L), np.float32), 1)
    masks[nl] = np.eye(L, dtype=np.float32)
    return jnp.asarray(D, BF16), jnp.asarray(masks, F32)


def _gla_kernel(q_ref, k_ref, v_ref, la_ref, s0_ref, g_ref, d_ref, m_ref, o_ref, s_ref, *, L, n_chunks):
    c = pl.program_id(1)
    nl = int(math.log2(L))

    @pl.when(c == 0)
    def _():
        s_ref[...] = s0_ref[...]

    dmat = d_ref[...]
    ones_col = jnp.ones((L, LANES), BF16)
    g = g_ref[...]
    nt = (((1,), (1,)), ((), ()))
    tn = (((0,), (0,)), ((), ()))

    def chunk(ci, carry):
        r0 = pl.multiple_of(ci * L, L)
        rows = pl.ds(r0, L)
        la_all = la_ref[rows, :]
        la_hi = la_all.astype(BF16)
        la_lo = (la_all - la_hi.astype(F32)).astype(BF16)
        x_all = jnp.exp(jnp.dot(dmat, la_hi, preferred_element_type=F32)
                        + jnp.dot(dmat, la_lo, preferred_element_type=F32))
        bl_all = (lax.dot_general(la_hi, ones_col, tn, preferred_element_type=F32)
                  + lax.dot_general(la_lo, ones_col, tn, preferred_element_type=F32))
        dec_all = jnp.exp(bl_all[:, 0:1])
        for h in range(H_A):
            kc = slice(h * DK_A, (h + 1) * DK_A)
            vc = slice(h * DV_A, (h + 1) * DV_A)
            q = q_ref[rows, kc].astype(F32)
            k = k_ref[rows, kc].astype(F32)
            v = v_ref[rows, vc]
            x = x_all[:, kc]
            a = m_ref[nl] * lax.dot_general(q.astype(BF16), k.astype(BF16), nt, preferred_element_type=F32)
            for l in range(nl):
                xl = x[l * L:(l + 1) * L]
                a = a + m_ref[l] * lax.dot_general((q * xl).astype(BF16), (k * xl).astype(BF16), nt,
                                                   preferred_element_type=F32)
            xb = x[nl * L:(nl + 1) * L]
            xs = x[(nl + 1) * L:]
            s_old = s_ref[0, h]
            o = (jnp.dot((q * xb).astype(BF16), s_old.astype(BF16), preferred_element_type=F32)
                 + jnp.dot(a.astype(BF16), v, preferred_element_type=F32))
            s_ref[0, h] = dec_all[kc] * s_old + lax.dot_general((k * xs).astype(BF16), v, tn, preferred_element_type=F32)
            ms = jnp.mean(o * o, axis=-1, keepdims=True)
            o_ref[rows, vc] = (o * lax.rsqrt(ms + EPS) * g).astype(BF16)
        return carry

    lax.fori_loop(0, n_chunks, chunk, 0, unroll=2 if n_chunks % 2 == 0 else 1)


def _gla(slab, loga, s0, g_norm, batch, t_len):
    L = min(CHUNK, t_len)
    tb = min(512, t_len)
    nb = t_len // tb
    dmat, masks = _gla_tables(L)
    kern = functools.partial(_gla_kernel, L=L, n_chunks=tb // L)
    row = lambda b, c: b * nb + c
    return pl.pallas_call(
        kern,
        grid=(batch, nb),
        in_specs=[
            pl.BlockSpec((tb, QA_W), lambda b, c: (row(b, c), SLAB_QA // QA_W)),
            pl.BlockSpec((tb, KA_W), lambda b, c: (row(b, c), SLAB_KA // KA_W)),
            pl.BlockSpec((tb, VA_W), lambda b, c: (row(b, c), SLAB_VA // VA_W)),
            pl.BlockSpec((tb, QA_W), lambda b, c: (row(b, c), 0)),
            pl.BlockSpec((1, H_A, DK_A, DV_A), lambda b, c: (b, 0, 0, 0)),
            pl.BlockSpec((1, DV_A), lambda b, c: (0, 0)),
            pl.BlockSpec(dmat.shape, lambda b, c: (0, 0)),
            pl.BlockSpec(masks.shape, lambda b, c: (0, 0, 0)),
        ],
        out_specs=[
            pl.BlockSpec((tb, VA_W), lambda b, c: (row(b, c), 0)),
            pl.BlockSpec((1, H_A, DK_A, DV_A), lambda b, c: (b, 0, 0, 0)),
        ],
        out_shape=[
            jax.ShapeDtypeStruct((batch * t_len, VA_W), BF16),
            jax.ShapeDtypeStruct((batch, H_A, DK_A, DV_A), F32),
        ],
        compiler_params=_cparams(("parallel", "arbitrary")),
        name="gla",
    )(slab, slab, slab, loga, s0, g_norm, dmat, masks)


def _lambda_from(dl_ref, lam_init):
    dl = dl_ref[...]
    a = jnp.sum(dl[0:1] * dl[1:2], axis=-1, keepdims=True)
    b = jnp.sum(dl[2:3] * dl[3:4], axis=-1, keepdims=True)
    return jnp.exp(a) - jnp.exp(b) + lam_init


def _split_maps(q):
    lane = lax.broadcasted_iota(I32, q.shape, 1)
    zero = jnp.zeros_like(q)
    return jnp.concatenate([jnp.where(lane < D_B, q, zero), jnp.where(lane >= D_B, q, zero)], axis=0)


def _alibi_coef(h):
    c = jnp.full((1, 1), LOG2E, F32) * jnp.exp2(-(h + 1).astype(F32))
    c_hi = c.astype(BF16).astype(F32)
    return c, c_hi, c - c_hi


def _finish_heads(acc, inv_l, lam, g, tq, lam_init):
    o = acc * inv_l
    out = o[:tq] - lam * o[tq:]
    ms = jnp.mean(out * out, axis=-1, keepdims=True)
    return (out * lax.rsqrt(ms + EPS) * g * (1.0 - lam_init)).astype(BF16)


def _attn_prompt_kernel(q_ref, k_ref, v_ref, g_ref, dl_ref, o_ref, ka_ref, vb_ref, qaug_ref, d_ref, m_ref,
                        acc_ref, s0_ref, s1_ref, p0_ref, p1_ref,
                        *, tq, tk, lam_init):
    h = pl.program_id(1)
    qi = pl.program_id(2)
    c, c_hi, c_lo = _alibi_coef(h)
    hw = 2 * D_B

    def aug_lanes(shape, pos, sign):
        lane = lax.broadcasted_iota(I32, shape, 1)
        r = (pos & 255).astype(F32) * sign
        a = (pos >> 8).astype(F32) * sign
        coef = jnp.where(lane == 0, c_hi, jnp.where(lane == 1, c_lo,
                         jnp.where(lane == 2, 256.0 * c_hi, jnp.where(lane == 3, 256.0 * c_lo, 0.0))))
        ints = jnp.where(lane < 2, r, jnp.where(lane < 4, a, 0.0))
        return coef, ints, lane

    @pl.when(qi == 0)
    def _():
        t_len = vb_ref.shape[0]
        lane = lax.broadcasted_iota(I32, (t_len, hw), 1)
        row = lax.broadcasted_iota(I32, (hw, tk), 0)
        j_rel = lax.broadcasted_iota(I32, (hw, tk), 1)
        k_aug_t = jnp.where(row < 2, (j_rel & 255).astype(F32), jnp.where(row < 4, (j_rel >> 8).astype(F32),
                  jnp.where(row == 4, c_hi, jnp.where(row == 5, c_lo,
                  jnp.where(row == 6, 256.0 * c_hi, jnp.where(row == 7, 256.0 * c_lo, 0.0)))))).astype(BF16)
        for blk in range(t_len // tk):
            rows = slice(blk * tk, (blk + 1) * tk)
            ka_ref[blk, :hw, :] = k_ref[0, rows, :].astype(F32).T.astype(BF16)
            ka_ref[blk, hw:, :] = k_aug_t
        vb_ref[:, :hw] = v_ref[0].astype(BF16)
        vb_ref[:, hw:] = jnp.where(lane == 0, 1.0, 0.0).astype(BF16)
        i_rel = lax.broadcasted_iota(I32, (2 * tq, hw), 0) & (tq - 1)
        coef, ints, lane = aug_lanes((2 * tq, hw), i_rel, -1.0)
        qaug_ref[...] = jnp.where(lane < 4, coef, pltpu.roll(ints, 4, 1)).astype(BF16)
        i = lax.broadcasted_iota(I32, (2 * tq, tq), 0) & (tq - 1)
        j = lax.broadcasted_iota(I32, (2 * tq, tq), 1)
        fwd = jnp.maximum(j - i, 0).astype(F32)
        d_ref[...] = jnp.where((j >> 6) <= (i >> 6), -2.0 * c * fwd, NEG_INF)

    qa = jnp.concatenate([_split_maps(q_ref[...]), qaug_ref[...]], axis=1)
    m_ref[...] = jnp.full(m_ref.shape, -1e30, F32)

    def scores(kv):
        return jnp.dot(qa, ka_ref[kv], preferred_element_type=F32)

    def weighted_values(p_ref, kv):
        k0 = pl.multiple_of(kv * tk, tk)
        return jnp.dot(p_ref[...], vb_ref[pl.ds(k0, tk), :], preferred_element_type=F32)

    def softmax_step(s, off):
        m_old = m_ref[...]
        m_new = jnp.maximum(m_old, jnp.max(s, axis=-1, keepdims=True) - off)
        shift = m_new + off
        p = jnp.exp2(s - jnp.concatenate([shift] * (tk // hw), axis=1))
        alpha = jnp.exp2(m_old - m_new)
        m_ref[...] = m_new
        return p.astype(BF16), jnp.concatenate([alpha, alpha], axis=1)

    s_first = scores(0)
    p_diag, _ = softmax_step(scores(qi) + d_ref[...], jnp.zeros((1, 1), F32))
    p1_ref[...] = p_diag
    s0_ref[...] = s_first
    acc_ref[...] = jnp.zeros(acc_ref.shape, F32)

    def step(kv, s_cur, s_nxt, p_prev, p_cur):
        pv_prev = weighted_values(p_prev, jnp.where(kv == 0, qi, kv - 1))
        s_nxt[...] = scores(jnp.minimum(kv + 1, qi - 1))
        p, alpha = softmax_step(s_cur[...], c * (qi * tq - kv * tk).astype(F32))
        acc_ref[...] = alpha * (acc_ref[...] + pv_prev)
        p_cur[...] = p

    def pair(j, carry):
        step(2 * j, s0_ref, s1_ref, p1_ref, p0_ref)
        step(2 * j + 1, s1_ref, s0_ref, p0_ref, p1_ref)
        return carry

    lax.fori_loop(0, qi // 2, pair, 0)

    @pl.when(qi % 2 == 1)
    def _():
        step(qi - 1, s0_ref, s1_ref, p1_ref, p0_ref)
        acc_ref[...] += weighted_values(p0_ref, qi - 1)

    @pl.when(qi % 2 == 0)
    def _():
        acc_ref[...] += weighted_values(p1_ref, jnp.maximum(qi - 1, 0))

    lam = _lambda_from(dl_ref, lam_init)
    acc = acc_ref[...]
    o_ref[...] = _finish_heads(acc[:, :hw], 1.0 / acc[:, hw:hw + 1], lam, g_ref[...], tq, lam_init)


def _attn_prompt(slab, kb, vb, g_norm, dl, batch, t_len, lam_init):
    tq = min(512, t_len)
    tk = tq
    assert tq % CHUNK == 0 and CHUNK == 64
    nq = t_len // tq
    k3 = kb.reshape(batch, t_len, KB_W)
    v3 = vb.reshape(batch, t_len, VB_W)
    kern = functools.partial(_attn_prompt_kernel, tq=tq, tk=tk, lam_init=lam_init)
    hw = 2 * D_B
    return pl.pallas_call(
        kern,
        grid=(batch, H_B, nq),
        in_specs=[
            pl.BlockSpec((tq, hw), lambda b, h, q: (b * nq + q, SLAB_QB // hw + h)),
            pl.BlockSpec((1, t_len, hw), lambda b, h, q: (b, 0, h)),
            pl.BlockSpec((1, t_len, hw), lambda b, h, q: (b, 0, h)),
            pl.BlockSpec((1, hw), lambda b, h, q: (0, 0)),
            pl.BlockSpec((4, D_B), lambda b, h, q: (0, 0)),
        ],
        out_specs=pl.BlockSpec((tq, hw), lambda b, h, q: (b * nq + q, h)),
        out_shape=jax.ShapeDtypeStruct((batch * t_len, VB_W), BF16),
        scratch_shapes=[
            pltpu.VMEM((t_len // tk, 2 * hw, tk), BF16),
            pltpu.VMEM((t_len, 2 * hw), BF16),
            pltpu.VMEM((2 * tq, hw), BF16),
            pltpu.VMEM((2 * tq, tq), F32),
            pltpu.VMEM((2 * tq, hw), F32),
            pltpu.VMEM((2 * tq, 2 * hw), F32),
            pltpu.VMEM((2 * tq, tk), F32),
            pltpu.VMEM((2 * tq, tk), F32),
            pltpu.VMEM((2 * tq, tk), BF16),
            pltpu.VMEM((2 * tq, tk), BF16),
        ],
        compiler_params=_cparams(("parallel", "parallel", "arbitrary")),
        name="attn_prompt",
    )(slab, k3, v3, g_norm, dl)


def _attn_sample_kernel(q_ref, kct_ref, vc_ref, kn_ref, vn_ref, g_ref, dl_ref, o_ref, *, tq, past, lam_init):
    hw = 2 * D_B
    q_pos = past + lax.broadcasted_iota(I32, (2 * tq, 1), 0) % tq
    lam = _lambda_from(dl_ref, lam_init)

    def biased(s, k_pos, c):
        s = s - c * jnp.abs(q_pos - k_pos).astype(F32)
        return jnp.where((k_pos // CHUNK) <= (q_pos // CHUNK), s, NEG_INF)

    for h in range(H_B):
        cols = slice(h * hw, (h + 1) * hw)
        c = LOG2E * 2.0 ** -(h + 1)
        qq = _split_maps(q_ref[:, cols])
        sc = biased(jnp.dot(qq, kct_ref[0, h].astype(BF16), preferred_element_type=F32),
                    lax.broadcasted_iota(I32, (1, past), 1), c)
        sn = biased(lax.dot_general(qq, kn_ref[:, cols], (((1,), (1,)), ((), ())), preferred_element_type=F32),
                    past + lax.broadcasted_iota(I32, (1, tq), 1), c)
        m = jnp.maximum(jnp.max(sc, axis=-1, keepdims=True), jnp.max(sn, axis=-1, keepdims=True))
        pc = jnp.exp2(sc - m)
        pn = jnp.exp2(sn - m)
        l = jnp.sum(pc, axis=-1, keepdims=True) + jnp.sum(pn, axis=-1, keepdims=True)
        vc = vc_ref[0, pl.ds(h, past, stride=H_B), :].astype(BF16)
        acc = (jnp.dot(pc.astype(BF16), vc, preferred_element_type=F32)
               + jnp.dot(pn.astype(BF16), vn_ref[:, cols].astype(BF16), preferred_element_type=F32))
        o_ref[:, cols] = _finish_heads(acc, 1.0 / l, lam, g_ref[...], tq, lam_init)


def _attn_sample(slab, kb, vb, cache_kt, cache_v2, g_norm, dl, batch, t_len, lam_init):
    past = cache_kt.shape[3]
    hw = 2 * D_B
    kern = functools.partial(_attn_sample_kernel, tq=t_len, past=past, lam_init=lam_init)
    return pl.pallas_call(
        kern,
        grid=(batch,),
        in_specs=[
            pl.BlockSpec((t_len, QB_W), lambda b: (b, SLAB_QB // QB_W)),
            pl.BlockSpec((1, H_B, hw, past), lambda b: (b, 0, 0, 0)),
            pl.BlockSpec((1, past * H_B, hw), lambda b: (b, 0, 0)),
            pl.BlockSpec((t_len, KB_W), lambda b: (b, 0)),
            pl.BlockSpec((t_len, VB_W), lambda b: (b, 0)),
            pl.BlockSpec((1, hw), lambda b: (0, 0)),
            pl.BlockSpec((4, D_B), lambda b: (0, 0)),
        ],
        out_specs=pl.BlockSpec((t_len, VB_W), lambda b: (b, 0)),
        out_shape=jax.ShapeDtypeStruct((batch * t_len, VB_W), BF16),
        compiler_params=_cparams(("parallel",)),
        name="attn_sample",
    )(slab, cache_kt, cache_v2, kb, vb, g_norm, dl)


def _finish_kernel(x_ref, oa_ref, ra_ref, ob_ref, gt_ref, wpa_ref, wpb_ref, wo_ref, g_ref, b_ref,
                   wrh_ref, wrl_ref, br_ref, h_ref, hp_ref, idx_ref, tw_ref, cnt_ref):
    i = pl.program_id(0)
    ra = ra_ref[...].astype(F32)
    ua = ra * _sigmoid(ra) * oa_ref[...].astype(F32)
    ya = jnp.dot(ua.astype(BF16), wpa_ref[...], preferred_element_type=F32)
    yb = jnp.dot(ob_ref[...], wpb_ref[...], preferred_element_type=F32)
    gt = gt_ref[...].astype(F32)
    mixed = _sigmoid(gt[:, :D_MODEL]) * ya + _sigmoid(gt[:, D_MODEL:]) * yb
    mix = jnp.dot(mixed.astype(BF16), wo_ref[...], preferred_element_type=F32)
    hh = _layer_norm(DEEPNORM_ALPHA * x_ref[...] + mix, g_ref[...], b_ref[...])
    h_ref[...] = hh
    _store_rows(hp_ref, _pack_bf16_pairs(hh))

    h_hi = hh.astype(BF16)
    h_lo = (hh - h_hi.astype(F32)).astype(BF16)
    lg = (jnp.dot(h_hi, wrh_ref[...], preferred_element_type=F32)
          + jnp.dot(h_lo, wrh_ref[...], preferred_element_type=F32)
          + jnp.dot(h_hi, wrl_ref[...], preferred_element_type=F32)) + br_ref[...]
    tm = lg.shape[0]
    lane = lax.broadcasted_iota(I32, (tm, LANES), 1)
    lane_f = lane.astype(F32)
    work = lg
    vals, idxs = [], []
    cnt = jnp.zeros((tm, LANES), F32)
    for _ in range(TOP_K):
        mx = jnp.max(work, axis=-1, keepdims=True)
        ix = jnp.min(jnp.where(work == mx, lane_f, float(LANES)), axis=-1, keepdims=True)
        hit = lane_f == ix
        cnt = cnt + hit.astype(F32)
        work = jnp.where(hit, NEG_INF, work)
        vals.append(mx)
        idxs.append(ix)
    es = [jnp.exp(v - vals[0]) for v in vals]
    den = es[0] + es[1] + es[2] + es[3]
    idx_full = jnp.zeros((tm, LANES), F32)
    tw_full = jnp.zeros((tm, LANES), F32)
    for r in range(TOP_K):
        idx_full = jnp.where(lane == r, idxs[r], idx_full)
        tw_full = jnp.where(lane == r, es[r] / den, tw_full)
    idx_ref[...] = idx_full.T[:SUBLANES].astype(I32)
    tw_ref[...] = tw_full[:, :TOP_K]

    @pl.when(i == 0)
    def _():
        cnt_ref[...] = jnp.zeros(cnt_ref.shape, F32)

    cnt_ref[...] += jnp.sum(cnt, axis=0, keepdims=True)


def _finish(x2, oa, slab, ob, p):
    n = x2.shape[0]
    tm = min(512, n)
    const = lambda i: (0, 0)
    return pl.pallas_call(
        _finish_kernel,
        grid=(n // tm,),
        in_specs=[
            pl.BlockSpec((tm, D_MODEL), lambda i: (i, 0)),
            pl.BlockSpec((tm, VA_W), lambda i: (i, 0)),
            pl.BlockSpec((tm, RA_W), lambda i: (i, SLAB_RA // RA_W)),
            pl.BlockSpec((tm, VB_W), lambda i: (i, 0)),
            pl.BlockSpec((tm, GT_W), lambda i: (i, SLAB_GT // GT_W)),
            pl.BlockSpec((VA_W, D_MODEL), const),
            pl.BlockSpec((VB_W, D_MODEL), const),
            pl.BlockSpec((D_MODEL, D_MODEL), const),
            pl.BlockSpec((1, D_MODEL), const),
            pl.BlockSpec((1, D_MODEL), const),
            pl.BlockSpec((D_MODEL, LANES), const),
            pl.BlockSpec((D_MODEL, LANES), const),
            pl.BlockSpec((1, LANES), const),
        ],
        out_specs=[
            pl.BlockSpec((tm, D_MODEL), lambda i: (i, 0)),
            pl.BlockSpec((tm * ROW_TILES, LANES), lambda i: (i, 0)),
            pl.BlockSpec((SUBLANES, tm), lambda i: (0, i)),
            pl.BlockSpec((tm, TOP_K), lambda i: (i, 0)),
            pl.BlockSpec((1, LANES), const),
        ],
        out_shape=[
            jax.ShapeDtypeStruct((n, D_MODEL), F32),
            jax.ShapeDtypeStruct((n * ROW_TILES, LANES), U32),
            jax.ShapeDtypeStruct((SUBLANES, n), I32),
            jax.ShapeDtypeStruct((n, TOP_K), F32),
            jax.ShapeDtypeStruct((1, LANES), F32),
        ],
        compiler_params=_cparams(("arbitrary",)),
        name="finish",
    )(x2, oa, slab, ob, slab, p["w_pa"], p["w_pb"], p["w_o"], p["ln1_g"], p["ln1_b"],
      p["w_r_hi"], p["w_r_lo"], p["b_r"])


def _dest_kernel(idx_ref, ps_ref, ut_ref, dest_ref, carry_ref):
    i = pl.program_id(0)

    @pl.when(i == 0)
    def _():
        carry_ref[...] = jnp.zeros(carry_ref.shape, F32)

    tm = idx_ref.shape[1]
    expert = lax.broadcasted_iota(I32, (N_EXPERTS, tm), 0)
    idx = idx_ref[...]
    hits = [expert == idx[k:k + 1, :] for k in range(TOP_K)]
    cnt = jnp.zeros((N_EXPERTS, tm), F32)
    for hit in hits:
        cnt = cnt + hit.astype(F32)
    before = jnp.dot(cnt.astype(BF16), ut_ref[...], preferred_element_type=F32)
    base = before + carry_ref[:, 0:1] + ps_ref[:, 0:1]
    row = lax.broadcasted_iota(I32, (SUBLANES, tm), 0)
    dest = jnp.zeros((SUBLANES, tm), F32)
    for k in range(TOP_K):
        d = jnp.sum(jnp.where(hits[k], base, 0.0), axis=0, keepdims=True)
        dest = jnp.where(row == k, d, dest)
    dest_ref[...] = dest.astype(I32)
    carry_ref[...] += jnp.sum(cnt, axis=1, keepdims=True)


def _dest(idx_t, pstart):
    n = idx_t.shape[1]
    tm = min(512, n)
    utri = jnp.asarray(np.triu(np.ones((tm, tm), np.float32), 1), BF16)
    return pl.pallas_call(
        _dest_kernel,
        grid=(n // tm,),
        in_specs=[
            pl.BlockSpec((SUBLANES, tm), lambda i: (0, i)),
            pl.BlockSpec((N_EXPERTS, LANES), lambda i: (0, 0)),
            pl.BlockSpec((tm, tm), lambda i: (0, 0)),
        ],
        out_specs=pl.BlockSpec((SUBLANES, tm), lambda i: (0, i)),
        out_shape=jax.ShapeDtypeStruct((SUBLANES, n), I32),
        scratch_shapes=[pltpu.VMEM((N_EXPERTS, LANES), F32)],
        compiler_params=_cparams(("arbitrary",)),
        name="dest",
    )(idx_t, pstart, utri)


def _scatter_kernel(dest_ref, pad0_ref, padn_ref, hp_ref, rows_ref, zero_ref, sem, pad_sem):
    tm = hp_ref.shape[0] // ROW_TILES

    def row_copy(t, d):
        return pltpu.make_async_copy(hp_ref.at[pl.ds(pl.multiple_of(t * ROW_TILES, ROW_TILES), ROW_TILES)],
                                     rows_ref.at[pl.ds(pl.multiple_of(d * ROW_TILES, ROW_TILES), ROW_TILES)], sem)

    def issue(t, carry):
        for k in range(TOP_K):
            row_copy(t, dest_ref[k, t]).start(priority=k % 2)
        return carry

    lax.fori_loop(0, tm, issue, 0, unroll=4)

    @pl.when(pl.program_id(0) == pl.num_programs(0) - 1)
    def _():
        zero_ref[...] = jnp.zeros(zero_ref.shape, U32)

        def pad_copy(d):
            return pltpu.make_async_copy(
                zero_ref, rows_ref.at[pl.ds(pl.multiple_of(d * ROW_TILES, ROW_TILES), ROW_TILES)], pad_sem)

        def for_each_pad_row(fn):
            def per_expert(e, carry):
                def per_row(r, c):
                    fn(pad0_ref[e] + r)
                    return c
                return lax.fori_loop(0, padn_ref[e], per_row, carry)
            lax.fori_loop(0, N_EXPERTS, per_expert, 0)

        for_each_pad_row(lambda d: pad_copy(d).start())
        for_each_pad_row(lambda d: pad_copy(d).wait())

    for k in range(TOP_K):
        pltpu.make_async_copy(hp_ref, rows_ref.at[pl.ds(0, tm * ROW_TILES)], sem).wait()


def _scatter(dest_t, pad_start, pad_len, hp, n_rows):
    n = hp.shape[0] // ROW_TILES
    tm = min(512, n)
    smem = pl.BlockSpec(memory_space=pltpu.SMEM)
    return pl.pallas_call(
        _scatter_kernel,
        grid=(n // tm,),
        in_specs=[
            pl.BlockSpec((SUBLANES, tm), lambda i: (0, i), memory_space=pltpu.SMEM),
            smem, smem,
            pl.BlockSpec((tm * ROW_TILES, LANES), lambda i: (i, 0)),
        ],
        out_specs=pl.BlockSpec(memory_space=pl.ANY),
        out_shape=jax.ShapeDtypeStruct((n_rows * ROW_TILES, LANES), U32),
        scratch_shapes=[pltpu.VMEM((ROW_TILES, LANES), U32), pltpu.SemaphoreType.DMA(()),
                        pltpu.SemaphoreType.DMA(())],
        compiler_params=_cparams(("arbitrary",)),
        name="scatter",
    )(dest_t, pad_start, pad_len, hp)


def _expert_kernel(be_ref, nu_ref, rows_ref, wgu_ref, bgu_ref, wd_ref, bd_ref, out_ref):
    del be_ref
    i = pl.program_id(0)

    @pl.when(i < nu_ref[0])
    def _():
        x = _unpack_bf16_pairs(_load_rows(rows_ref)).astype(BF16)
        gu = jnp.dot(x, wgu_ref[0], preferred_element_type=F32) + bgu_ref[0]
        gate = jnp.minimum(gu[:, :D_FF], SWIGLU_LIMIT)
        up = jnp.clip(gu[:, D_FF:], -SWIGLU_LIMIT, SWIGLU_LIMIT)
        hidden = (up + 1.0) * (gate * _sigmoid(SWIGLU_ALPHA * gate))
        o = jnp.dot(hidden.astype(BF16), wd_ref[0], preferred_element_type=F32) + bd_ref[0]
        _store_rows(out_ref, _pack_bf16_pairs(o))

    @pl.when(i >= nu_ref[0])
    def _():
        out_ref[...] = jnp.zeros(out_ref.shape, U32)


def _experts(rows, block_e, n_used, p):
    n_rows = rows.shape[0] // ROW_TILES
    bm = EXPERT_BM
    n_blocks = n_rows // bm
    grid_spec = pltpu.PrefetchScalarGridSpec(
        num_scalar_prefetch=2,
        grid=(n_blocks,),
        in_specs=[
            pl.BlockSpec((bm * ROW_TILES, LANES), lambda i, be, nu: (jnp.minimum(i, nu[0] - 1), 0)),
            pl.BlockSpec((1, D_MODEL, 2 * D_FF), lambda i, be, nu: (be[i], 0, 0)),
            pl.BlockSpec((1, 1, 2 * D_FF), lambda i, be, nu: (be[i], 0, 0)),
            pl.BlockSpec((1, D_FF, D_MODEL), lambda i, be, nu: (be[i], 0, 0)),
            pl.BlockSpec((1, 1, D_MODEL), lambda i, be, nu: (be[i], 0, 0)),
        ],
        out_specs=pl.BlockSpec((bm * ROW_TILES, LANES), lambda i, be, nu: (i, 0)),
    )
    return pl.pallas_call(
        _expert_kernel,
        grid_spec=grid_spec,
        out_shape=jax.ShapeDtypeStruct((n_rows * ROW_TILES, LANES), U32),
        compiler_params=_cparams(("arbitrary",)),
        name="experts",
    )(block_e, n_used, rows, p["w_gu"], p["b_gu"], p["w_down"], p["b_down"])


def _combine_kernel(dest_ref, tw_ref, h_ref, rows_ref, g_ref, b_ref, y_ref, buf_ref, sem):
    tm = h_ref.shape[0]

    def row_copy(t, k, d):
        return pltpu.make_async_copy(rows_ref.at[pl.ds(pl.multiple_of(d * ROW_TILES, ROW_TILES), ROW_TILES)],
                                     buf_ref.at[k, pl.ds(pl.multiple_of(t * ROW_TILES, ROW_TILES), ROW_TILES)], sem)

    def issue(t, carry):
        for k in range(TOP_K):
            row_copy(t, k, dest_ref[k, t]).start(priority=k % 2)
        return carry

    lax.fori_loop(0, tm, issue, 0, unroll=4)
    for k in range(TOP_K):
        pltpu.make_async_copy(rows_ref.at[pl.ds(0, tm * ROW_TILES)], buf_ref.at[k], sem).wait()
    tw = tw_ref[...]
    f = jnp.zeros((tm, D_MODEL), F32)
    for k in range(TOP_K):
        f = f + tw[:, k:k + 1] * _unpack_bf16_pairs(_load_rows(buf_ref.at[k]))
    y_ref[...] = _layer_norm(DEEPNORM_ALPHA * h_ref[...] + f, g_ref[...], b_ref[...])


def _combine(dest_t, top_w, hh, out_rows, p):
    n = hh.shape[0]
    tm = min(512, n)
    const = lambda i: (0, 0)
    return pl.pallas_call(
        _combine_kernel,
        grid=(n // tm,),
        in_specs=[
            pl.BlockSpec((SUBLANES, tm), lambda i: (0, i), memory_space=pltpu.SMEM),
            pl.BlockSpec((tm, TOP_K), lambda i: (i, 0)),
            pl.BlockSpec((tm, D_MODEL), lambda i: (i, 0)),
            pl.BlockSpec(memory_space=pl.ANY),
            pl.BlockSpec((1, D_MODEL), const),
            pl.BlockSpec((1, D_MODEL), const),
        ],
        out_specs=pl.BlockSpec((tm, D_MODEL), lambda i: (i, 0)),
        out_shape=jax.ShapeDtypeStruct((n, D_MODEL), F32),
        scratch_shapes=[pltpu.VMEM((TOP_K, tm * ROW_TILES, LANES), U32), pltpu.SemaphoreType.DMA(())],
        compiler_params=_cparams(("arbitrary",)),
        name="combine",
    )(dest_t, top_w, hh, out_rows, p["ln2_g"], p["ln2_b"])


def _moe(hh, hp, top_idx, top_w, counts, p):
    n = hh.shape[0]
    bm = EXPERT_BM
    n_rows = n * TOP_K + N_EXPERTS * bm
    cnt = counts[0, :N_EXPERTS].astype(I32)
    padded = (cnt + bm - 1) // bm * bm
    pend = jnp.cumsum(padded)
    pstart = pend - padded
    ps_col = jnp.broadcast_to(pstart.astype(F32)[:, None], (N_EXPERTS, LANES))
    block_row0 = jnp.arange(n_rows // bm, dtype=I32) * bm
    block_e = jnp.minimum(jnp.sum((pend[None, :] <= block_row0[:, None]).astype(I32), axis=1), N_EXPERTS - 1)
    n_used = (pend[-1:] // bm).astype(I32)
    dest = _dest(top_idx, ps_col)
    rows = _scatter(dest, pstart + cnt, padded - cnt, hp, n_rows)
    out_rows = _experts(rows, block_e, n_used, p)
    return _combine(dest, top_w, hh, out_rows, p)


def _prep_params(w_in, b_in, w_alpha, b_alpha, gla_norm_g, diff_norm_g, w_pa, w_pb, w_o, ln1_g, ln1_b,
                 w_router, b_router, w_gu, b_gu, w_down, b_down, ln2_g, ln2_b):
    o = IN_OFFS
    seg = lambda a, i: a[..., o[i]:o[i + 1]]
    order = (0, 1, 2, 3, 5, 8)
    w_slab = jnp.concatenate([seg(w_in, i) for i in order], axis=1).astype(BF16)
    b_slab = jnp.concatenate([seg(b_in, i) for i in order])[None, :]
    scale = np.ones((1, SLAB_W), np.float32)
    scale[0, SLAB_QA:SLAB_QA + QA_W] = DK_A ** -0.5
    scale[0, SLAB_QB:SLAB_QB + QB_W] = D_B ** -0.5 * LOG2E
    w_lr = jnp.zeros((D_MODEL, LANES), F32).at[:, :LR_W].set(seg(w_in, 4)).astype(BF16)
    b_lr = jnp.zeros((1, LANES), F32).at[0, :LR_W].set(seg(b_in, 4))
    w_al = jnp.zeros((LANES, QA_W), F32).at[:LR_W].set(w_alpha).astype(BF16)
    w_r = jnp.zeros((D_MODEL, LANES), F32).at[:, :N_EXPERTS].set(w_router)
    w_r_hi = w_r.astype(BF16)
    w_r_lo = (w_r - w_r_hi.astype(F32)).astype(BF16)
    b_r = jnp.full((1, LANES), NEG_INF, F32).at[0, :N_EXPERTS].set(b_router)
    return dict(
        w_slab=w_slab, b_slab=b_slab, s_slab=jnp.asarray(scale),
        w_lr=w_lr, b_lr=b_lr, w_alpha=w_al, b_alpha=b_alpha[None, :],
        w_k=seg(w_in, 6).astype(BF16), w_v=seg(w_in, 7).astype(BF16),
        b_k=seg(b_in, 6)[None, :], b_v=seg(b_in, 7)[None, :],
        gla_g=gla_norm_g[None, :], diff_g=diff_norm_g[None, :],
        w_pa=w_pa.astype(BF16), w_pb=w_pb.astype(BF16), w_o=w_o.astype(BF16),
        ln1_g=ln1_g[None, :], ln1_b=ln1_b[None, :],
        w_r_hi=w_r_hi, w_r_lo=w_r_lo, b_r=b_r,
        w_gu=w_gu.astype(BF16), b_gu=b_gu[:, None, :], w_down=w_down.astype(BF16), b_down=b_down[:, None, :],
        ln2_g=ln2_g[None, :], ln2_b=ln2_b[None, :],
    )


def _layer(x, p, dl, lam_init, s0, cache_k, cache_v):
    batch, t_len, _ = x.shape
    x2 = x.reshape(batch * t_len, D_MODEL)
    slab, loga, k_rows, kb, vb = _proj(x2, p)
    oa, s_new = _gla(slab, loga, s0, p["gla_g"], batch, t_len)
    if cache_k is None:
        ob = _attn_prompt(slab, kb, vb, p["diff_g"], dl, batch, t_len, lam_init)
    else:
        ob = _attn_sample(slab, kb, vb, cache_k, cache_v, p["diff_g"], dl, batch, t_len, lam_init)
    hh, hp, top_idx, top_w, counts = _finish(x2, oa, slab, ob, p)
    y = _moe(hh, hp, top_idx, top_w, counts, p)
    return (y.reshape(batch, t_len, D_MODEL), s_new,
            k_rows.reshape(batch, t_len, H_B, 2, D_B), vb.reshape(batch, t_len, H_B, 2 * D_B))


def kernel(x_prompt, x_sample, cache_k, cache_v, state_gla, w_in, b_in, w_alpha, b_alpha, gla_norm_g,
           diff_lambda, diff_norm_g, w_pa, w_pb, w_o, ln1_g, ln1_b, w_router, b_router, w_gu, b_gu,
           w_down, b_down, ln2_g, ln2_b):
    assert w_in.shape[0] == DEPTH == 1
    l = 0
    lam_init = 0.8 - 0.6 * math.exp(-0.3 * l)
    p = _prep_params(w_in[l], b_in[l], w_alpha[l], b_alpha[l], gla_norm_g[l], diff_norm_g[l], w_pa[l],
                     w_pb[l], w_o[l], ln1_g[l], ln1_b[l], w_router[l], b_router[l], w_gu[l], b_gu[l],
                     w_down[l], b_down[l], ln2_g[l], ln2_b[l])
    dl = diff_lambda[l]
    bp = x_prompt.shape[0]
    bs, ts = x_sample.shape[0], x_sample.shape[1]
    past = cache_k.shape[2]
    yp, s_p, k_p, v_p = _layer(x_prompt, p, dl, lam_init, jnp.zeros((bp, H_A, DK_A, DV_A), F32), None, None)
    ck = jnp.transpose(cache_k[l], (0, 2, 3, 4, 1)).reshape(bs, H_B, 2 * D_B, past)
    cv = cache_v[l].reshape(bs, past * H_B, 2 * D_B)
    ys, s_s, k_s, v_s = _layer(x_sample, p, dl, lam_init, state_gla[l], ck, cv)
    return (yp, ys, s_p[None], k_p[None], v_p[None], s_s[None], k_s[None], v_s[None])
```

```python
import functools
import math

import numpy as np
import jax
import jax.numpy as jnp
from jax import lax
from jax.experimental import pallas as pl
from jax.experimental.pallas import tpu as pltpu

F32 = jnp.float32
BF16 = jnp.bfloat16
U32 = jnp.uint32
I32 = jnp.int32

D_MODEL = 1024
CHUNK = 64
H_A = 4
DK_A = 128
DV_A = 256
GATE_RANK = 16
GATE_TAU = 16.0
H_B = 8
D_B = 64
N_EXPERTS = 32
TOP_K = 4
D_FF = D_MODEL
SWIGLU_LIMIT = 7.0
SWIGLU_ALPHA = 1.702
EPS = 1e-5
DEPTH = 1
DEEPNORM_ALPHA = (2.0 * DEPTH) ** 0.25

QA_W = H_A * DK_A
KA_W = H_A * DK_A
VA_W = H_A * DV_A
RA_W = H_A * DV_A
LR_W = GATE_RANK
QB_W = H_B * 2 * D_B
KB_W = H_B * 2 * D_B
VB_W = H_B * 2 * D_B
GT_W = 2 * D_MODEL
IN_SIZES = (QA_W, KA_W, VA_W, RA_W, LR_W, QB_W, KB_W, VB_W, GT_W)
IN_OFFS = tuple(int(v) for v in np.cumsum((0,) + IN_SIZES))

SLAB_QA, SLAB_KA, SLAB_VA, SLAB_RA, SLAB_QB, SLAB_GT = 0, 512, 1024, 2048, 3072, 4096
SLAB_W = 6144
LANES = 128
SUBLANES = 8
EXPERT_BM = 512
VMEM_LIMIT = 56 * 1024 * 1024
PROJ_VMEM_LIMIT = 61 * 1024 * 1024

NEG_INF = float("-inf")
LOG2E = math.log2(math.e)


def _cparams(sem, vmem_limit=VMEM_LIMIT):
    return pltpu.CompilerParams(dimension_semantics=sem, vmem_limit_bytes=vmem_limit)


def _sigmoid(x):
    return 1.0 / (1.0 + jnp.exp(-x))


def _pack_bf16_pairs(x):
    n = x.shape[1] // 2
    xb = x.astype(BF16).astype(F32)
    lo = pltpu.bitcast(xb[:, :n], U32) >> 16
    hi = pltpu.bitcast(xb[:, n:], U32) & jnp.uint32(0xFFFF0000)
    return hi | lo


def _unpack_bf16_pairs(r):
    lo = pltpu.bitcast(r << 16, F32)
    hi = pltpu.bitcast(r & jnp.uint32(0xFFFF0000), F32)
    return jnp.concatenate([lo, hi], axis=1)


ROW_TILES = D_MODEL // 2 // LANES


def _load_rows(ref):
    m = ref.shape[0] // ROW_TILES
    return jnp.concatenate([ref[pl.ds(c, m, stride=ROW_TILES), :] for c in range(ROW_TILES)], axis=1)


def _store_rows(ref, val):
    m = ref.shape[0] // ROW_TILES
    for c in range(ROW_TILES):
        ref[pl.ds(c, m, stride=ROW_TILES), :] = val[:, c * LANES:(c + 1) * LANES]


def _layer_norm(y, g, b):
    mu = jnp.mean(y, axis=-1, keepdims=True)
    d = y - mu
    var = jnp.mean(d * d, axis=-1, keepdims=True)
    return d * lax.rsqrt(var + EPS) * g + b


def _proj_kernel(x_ref, w_ref, b_ref, s_ref, wlr_ref, blr_ref, wal_ref, bal_ref, wk_ref, wv_ref, bk_ref, bv_ref,
                 slab_ref, loga_ref, krows_ref, kb_ref, v_ref, *, tn):
    xb = x_ref[...].astype(BF16)
    k = jnp.dot(xb, wk_ref[...], preferred_element_type=F32) + bk_ref[...]
    kb_ref[...] = k.astype(BF16)
    for h in range(H_B):
        for j in range(2):
            c0 = (2 * h + j) * D_B
            krows_ref[:, h, j, :] = k[:, c0:c0 + D_B]
    v_ref[...] = jnp.dot(xb, wv_ref[...], preferred_element_type=F32) + bv_ref[...]
    lra = jnp.dot(xb, wlr_ref[...], preferred_element_type=F32) + blr_ref[...]
    z = jnp.dot(lra.astype(BF16), wal_ref[...], preferred_element_type=F32) + bal_ref[...]
    loga_ref[...] = (jnp.minimum(z, 0.0) - jnp.log(1.0 + jnp.exp(-jnp.abs(z)))) * (1.0 / GATE_TAU)
    for c0 in range(0, SLAB_W, tn):
        acc = jnp.dot(xb, w_ref[:, c0:c0 + tn], preferred_element_type=F32)
        slab_ref[:, c0:c0 + tn] = ((acc + b_ref[:, c0:c0 + tn]) * s_ref[:, c0:c0 + tn]).astype(BF16)


def _proj(x2, p):
    n = x2.shape[0]
    tm = min(512, n)
    const = lambda i: (0, 0)
    whole = lambda shape: pl.BlockSpec(shape, const, pipeline_mode=pl.Buffered(1))
    row = lambda w: pl.BlockSpec((tm, w), lambda i: (i, 0))
    return pl.pallas_call(
        functools.partial(_proj_kernel, tn=1024),
        grid=(n // tm,),
        in_specs=[
            row(D_MODEL),
            whole((D_MODEL, SLAB_W)), whole((1, SLAB_W)), whole((1, SLAB_W)),
            whole((D_MODEL, LANES)), whole((1, LANES)), whole((LANES, QA_W)), whole((1, QA_W)),
            whole((D_MODEL, KB_W)), whole((D_MODEL, VB_W)), whole((1, KB_W)), whole((1, VB_W)),
        ],
        out_specs=[
            row(SLAB_W), row(QA_W),
            pl.BlockSpec((tm, H_B, 2, D_B), lambda i: (i, 0, 0, 0)),
            row(KB_W), row(VB_W),
        ],
        out_shape=[
            jax.ShapeDtypeStruct((n, SLAB_W), BF16),
            jax.ShapeDtypeStruct((n, QA_W), F32),
            jax.ShapeDtypeStruct((n, H_B, 2, D_B), F32),
            jax.ShapeDtypeStruct((n, KB_W), BF16),
            jax.ShapeDtypeStruct((n, VB_W), F32),
        ],
        compiler_params=_cparams(("parallel",), PROJ_VMEM_LIMIT),
        name="proj",
    )(x2, p["w_slab"], p["b_slab"], p["s_slab"], p["w_lr"], p["b_lr"], p["w_alpha"], p["b_alpha"],
      p["w_k"], p["w_v"], p["b_k"], p["b_v"])


def _gla_tables(L):
    nl = int(math.log2(L))
    t = np.arange(L)
    D = np.zeros(((nl + 2) * L, L), np.float32)
    masks = np.zeros((nl + 1, L, L), np.float32)
    for l in range(nl):
        m = L >> (l + 1)
        grp = t // (2 * m)
        mid = grp * 2 * m + m - 1
        upper = (t % (2 * m)) >= m
        for r in range(L):
            if upper[r]:
                D[l * L + r, mid[r] + 1:r + 1] = 1.0
            else:
                D[l * L + r, r + 1:mid[r] + 1] = 1.0
        masks[l] = (upper[:, None] & ~upper[None, :] & (grp[:, None] == grp[None, :])).astype(np.float32)
    D[nl * L:(nl + 1) * L] = np.tril(np.ones((L, L), np.float32))
    D[(nl + 1) * L:] = np.triu(np.ones((L, L), np.float32), 1)
    masks[nl] = np.eye(L, dtype=np.float32)
    return jnp.asarray(D, BF16), jnp.asarray(masks, F32)


def _gla_kernel(q_ref, k_ref, v_ref, la_ref, s0_ref, g_ref, d_ref, m_ref, o_ref, s_ref, *, L, n_chunks):
    c = pl.program_id(1)
    nl = int(math.log2(L))

    @pl.when(c == 0)
    def _():
        s_ref[...] = s0_ref[...]

    dmat = d_ref[...]
    ones_col = jnp.ones((L, LANES), BF16)
    g = g_ref[...]
    nt = (((1,), (1,)), ((), ()))
    tn = (((0,), (0,)), ((), ()))

    def chunk(ci, carry):
        r0 = pl.multiple_of(ci * L, L)
        rows = pl.ds(r0, L)
        la_all = la_ref[rows, :]
        la_hi = la_all.astype(BF16)
        la_lo = (la_all - la_hi.astype(F32)).astype(BF16)
        x_all = jnp.exp(jnp.dot(dmat, la_hi, preferred_element_type=F32)
                        + jnp.dot(dmat, la_lo, preferred_element_type=F32))
        bl_all = (lax.dot_general(la_hi, ones_col, tn, preferred_element_type=F32)
                  + lax.dot_general(la_lo, ones_col, tn, preferred_element_type=F32))
        dec_all = jnp.exp(bl_all[:, 0:1])
        for h in range(H_A):
            kc = slice(h * DK_A, (h + 1) * DK_A)
            vc = slice(h * DV_A, (h + 1) * DV_A)
            q = q_ref[rows, kc].astype(F32)
            k = k_ref[rows, kc].astype(F32)
            v = v_ref[rows, vc]
            x = x_all[:, kc]
            a = m_ref[nl] * lax.dot_general(q.astype(BF16), k.astype(BF16), nt, preferred_element_type=F32)
            for l in range(nl):
                xl = x[l * L:(l + 1) * L]
                a = a + m_ref[l] * lax.dot_general((q * xl).astype(BF16), (k * xl).astype(BF16), nt,
                                                   preferred_element_type=F32)
            xb = x[nl * L:(nl + 1) * L]
            xs = x[(nl + 1) * L:]
            s_old = s_ref[0, h]
            o = (jnp.dot((q * xb).astype(BF16), s_old.astype(BF16), preferred_element_type=F32)
                 + jnp.dot(a.astype(BF16), v, preferred_element_type=F32))
            s_ref[0, h] = dec_all[kc] * s_old + lax.dot_general((k * xs).astype(BF16), v, tn, preferred_element_type=F32)
            ms = jnp.mean(o * o, axis=-1, keepdims=True)
            o_ref[rows, vc] = (o * lax.rsqrt(ms + EPS) * g).astype(BF16)
        return carry

    lax.fori_loop(0, n_chunks, chunk, 0, unroll=2 if n_chunks % 2 == 0 else 1)


def _gla(slab, loga, s0, g_norm, batch, t_len):
    L = min(CHUNK, t_len)
    tb = min(512, t_len)
    nb = t_len // tb
    dmat, masks = _gla_tables(L)
    kern = functools.partial(_gla_kernel, L=L, n_chunks=tb // L)
    row = lambda b, c: b * nb + c
    return pl.pallas_call(
        kern,
        grid=(batch, nb),
        in_specs=[
            pl.BlockSpec((tb, QA_W), lambda b, c: (row(b, c), SLAB_QA // QA_W)),
            pl.BlockSpec((tb, KA_W), lambda b, c: (row(b, c), SLAB_KA // KA_W)),
            pl.BlockSpec((tb, VA_W), lambda b, c: (row(b, c), SLAB_VA // VA_W)),
            pl.BlockSpec((tb, QA_W), lambda b, c: (row(b, c), 0)),
            pl.BlockSpec((1, H_A, DK_A, DV_A), lambda b, c: (b, 0, 0, 0)),
            pl.BlockSpec((1, DV_A), lambda b, c: (0, 0)),
            pl.BlockSpec(dmat.shape, lambda b, c: (0, 0)),
            pl.BlockSpec(masks.shape, lambda b, c: (0, 0, 0)),
        ],
        out_specs=[
            pl.BlockSpec((tb, VA_W), lambda b, c: (row(b, c), 0)),
            pl.BlockSpec((1, H_A, DK_A, DV_A), lambda b, c: (b, 0, 0, 0)),
        ],
        out_shape=[
            jax.ShapeDtypeStruct((batch * t_len, VA_W), BF16),
            jax.ShapeDtypeStruct((batch, H_A, DK_A, DV_A), F32),
        ],
        compiler_params=_cparams(("parallel", "arbitrary")),
        name="gla",
    )(slab, slab, slab, loga, s0, g_norm, dmat, masks)


def _lambda_from(dl_ref, lam_init):
    dl = dl_ref[...]
    a = jnp.sum(dl[0:1] * dl[1:2], axis=-1, keepdims=True)
    b = jnp.sum(dl[2:3] * dl[3:4], axis=-1, keepdims=True)
    return jnp.exp(a) - jnp.exp(b) + lam_init


def _split_maps(q):
    lane = lax.broadcasted_iota(I32, q.shape, 1)
    zero = jnp.zeros_like(q)
    return jnp.concatenate([jnp.where(lane < D_B, q, zero), jnp.where(lane >= D_B, q, zero)], axis=0)


def _alibi_coef(h):
    c = jnp.full((1, 1), LOG2E, F32) * jnp.exp2(-(h + 1).astype(F32))
    c_hi = c.astype(BF16).astype(F32)
    return c, c_hi, c - c_hi


def _finish_heads(acc, inv_l, lam, g, tq, lam_init):
    o = acc * inv_l
    out = o[:tq] - lam * o[tq:]
    ms = jnp.mean(out * out, axis=-1, keepdims=True)
    return (out * lax.rsqrt(ms + EPS) * g * (1.0 - lam_init)).astype(BF16)


def _attn_prompt_kernel(q_ref, k_ref, v_ref, g_ref, dl_ref, o_ref, ka_ref, vb_ref, qaug_ref, d_ref, m_ref,
                        acc_ref, s0_ref, s1_ref, p0_ref, p1_ref,
                        *, tq, tk, lam_init):
    h = pl.program_id(1)
    qi = pl.program_id(2)
    c, c_hi, c_lo = _alibi_coef(h)
    hw = 2 * D_B

    def aug_lanes(shape, pos, sign):
        lane = lax.broadcasted_iota(I32, shape, 1)
        r = (pos & 255).astype(F32) * sign
        a = (pos >> 8).astype(F32) * sign
        coef = jnp.where(lane == 0, c_hi, jnp.where(lane == 1, c_lo,
                         jnp.where(lane == 2, 256.0 * c_hi, jnp.where(lane == 3, 256.0 * c_lo, 0.0))))
        ints = jnp.where(lane < 2, r, jnp.where(lane < 4, a, 0.0))
        return coef, ints, lane

    @pl.when(qi == 0)
    def _():
        t_len = vb_ref.shape[0]
        lane = lax.broadcasted_iota(I32, (t_len, hw), 1)
        row = lax.broadcasted_iota(I32, (hw, tk), 0)
        j_rel = lax.broadcasted_iota(I32, (hw, tk), 1)
        k_aug_t = jnp.where(row < 2, (j_rel & 255).astype(F32), jnp.where(row < 4, (j_rel >> 8).astype(F32),
                  jnp.where(row == 4, c_hi, jnp.where(row == 5, c_lo,
                  jnp.where(row == 6, 256.0 * c_hi, jnp.where(row == 7, 256.0 * c_lo, 0.0)))))).astype(BF16)
        for blk in range(t_len // tk):
            rows = slice(blk * tk, (blk + 1) * tk)
            ka_ref[blk, :hw, :] = k_ref[0, rows, :].astype(F32).T.astype(BF16)
            ka_ref[blk, hw:, :] = k_aug_t
        vb_ref[:, :hw] = v_ref[0].astype(BF16)
        vb_ref[:, hw:] = jnp.where(lane == 0, 1.0, 0.0).astype(BF16)
        i_rel = lax.broadcasted_iota(I32, (2 * tq, hw), 0) & (tq - 1)
        coef, ints, lane = aug_lanes((2 * tq, hw), i_rel, -1.0)
        qaug_ref[...] = jnp.where(lane < 4, coef, pltpu.roll(ints, 4, 1)).astype(BF16)
        i = lax.broadcasted_iota(I32, (2 * tq, tq), 0) & (tq - 1)
        j = lax.broadcasted_iota(I32, (2 * tq, tq), 1)
        fwd = jnp.maximum(j - i, 0).astype(F32)
        d_ref[...] = jnp.where((j >> 6) <= (i >> 6), -2.0 * c * fwd, NEG_INF)

    qa = jnp.concatenate([_split_maps(q_ref[...]), qaug_ref[...]], axis=1)
    m_ref[...] = jnp.full(m_ref.shape, -1e30, F32)

    def scores(kv):
        return jnp.dot(qa, ka_ref[kv], preferred_element_type=F32)

    def weighted_values(p_ref, kv):
        k0 = pl.multiple_of(kv * tk, tk)
        return jnp.dot(p_ref[...], vb_ref[pl.ds(k0, tk), :], preferred_element_type=F32)

    def softmax_step(s, off):
        m_old = m_ref[...]
        m_new = jnp.maximum(m_old, jnp.max(s, axis=-1, keepdims=True) - off)
        shift = m_new + off
        p = jnp.exp2(s - jnp.concatenate([shift] * (tk // hw), axis=1))
        alpha = jnp.exp2(m_old - m_new)
        m_ref[...] = m_new
        return p.astype(BF16), jnp.concatenate([alpha, alpha], axis=1)

    s_first = scores(0)
    p_diag, _ = softmax_step(scores(qi) + d_ref[...], jnp.zeros((1, 1), F32))
    p1_ref[...] = p_diag
    s0_ref[...] = s_first
    acc_ref[...] = jnp.zeros(acc_ref.shape, F32)

    def step(kv, s_cur, s_nxt, p_prev, p_cur):
        pv_prev = weighted_values(p_prev, jnp.where(kv == 0, qi, kv - 1))
        s_nxt[...] = scores(jnp.minimum(kv + 1, qi - 1))
        p, alpha = softmax_step(s_cur[...], c * (qi * tq - kv * tk).astype(F32))
        acc_ref[...] = alpha * (acc_ref[...] + pv_prev)
        p_cur[...] = p

    def pair(j, carry):
        step(2 * j, s0_ref, s1_ref, p1_ref, p0_ref)
        step(2 * j + 1, s1_ref, s0_ref, p0_ref, p1_ref)
        return carry

    lax.fori_loop(0, qi // 2, pair, 0)

    @pl.when(qi % 2 == 1)
    def _():
        step(qi - 1, s0_ref, s1_ref, p1_ref, p0_ref)
        acc_ref[...] += weighted_values(p0_ref, qi - 1)

    @pl.when(qi % 2 == 0)
    def _():
        acc_ref[...] += weighted_values(p1_ref, jnp.maximum(qi - 1, 0))

    lam = _lambda_from(dl_ref, lam_init)
    acc = acc_ref[...]
    o_ref[...] = _finish_heads(acc[:, :hw], 1.0 / acc[:, hw:hw + 1], lam, g_ref[...], tq, lam_init)


def _attn_prompt(slab, kb, vb, g_norm, dl, batch, t_len, lam_init):
    tq = min(512, t_len)
    tk = tq
    assert tq % CHUNK == 0 and CHUNK == 64
    nq = t_len // tq
    k3 = kb.reshape(batch, t_len, KB_W)
    v3 = vb.reshape(batch, t_len, VB_W)
    kern = functools.partial(_attn_prompt_kernel, tq=tq, tk=tk, lam_init=lam_init)
    hw = 2 * D_B
    return pl.pallas_call(
        kern,
        grid=(batch, H_B, nq),
        in_specs=[
            pl.BlockSpec((tq, hw), lambda b, h, q: (b * nq + q, SLAB_QB // hw + h)),
            pl.BlockSpec((1, t_len, hw), lambda b, h, q: (b, 0, h)),
            pl.BlockSpec((1, t_len, hw), lambda b, h, q: (b, 0, h)),
            pl.BlockSpec((1, hw), lambda b, h, q: (0, 0)),
            pl.BlockSpec((4, D_B), lambda b, h, q: (0, 0)),
        ],
        out_specs=pl.BlockSpec((tq, hw), lambda b, h, q: (b * nq + q, h)),
        out_shape=jax.ShapeDtypeStruct((batch * t_len, VB_W), BF16),
        scratch_shapes=[
            pltpu.VMEM((t_len // tk, 2 * hw, tk), BF16),
            pltpu.VMEM((t_len, 2 * hw), BF16),
            pltpu.VMEM((2 * tq, hw), BF16),
            pltpu.VMEM((2 * tq, tq), F32),
            pltpu.VMEM((2 * tq, hw), F32),
            pltpu.VMEM((2 * tq, 2 * hw), F32),
            pltpu.VMEM((2 * tq, tk), F32),
            pltpu.VMEM((2 * tq, tk), F32),
            pltpu.VMEM((2 * tq, tk), BF16),
            pltpu.VMEM((2 * tq, tk), BF16),
        ],
        compiler_params=_cparams(("parallel", "parallel", "arbitrary")),
        name="attn_prompt",
    )(slab, k3, v3, g_norm, dl)


def _attn_sample_kernel(q_ref, kct_ref, vc_ref, kn_ref, vn_ref, g_ref, dl_ref, o_ref, *, tq, past, lam_init):
    hw = 2 * D_B
    q_pos = past + lax.broadcasted_iota(I32, (2 * tq, 1), 0) % tq
    lam = _lambda_from(dl_ref, lam_init)

    def biased(s, k_pos, c):
        s = s - c * jnp.abs(q_pos - k_pos).astype(F32)
        return jnp.where((k_pos // CHUNK) <= (q_pos // CHUNK), s, NEG_INF)

    for h in range(H_B):
        cols = slice(h * hw, (h + 1) * hw)
        c = LOG2E * 2.0 ** -(h + 1)
        qq = _split_maps(q_ref[:, cols])
        sc = biased(jnp.dot(qq, kct_ref[0, h].astype(BF16), preferred_element_type=F32),
                    lax.broadcasted_iota(I32, (1, past), 1), c)
        sn = biased(lax.dot_general(qq, kn_ref[:, cols], (((1,), (1,)), ((), ())), preferred_element_type=F32),
                    past + lax.broadcasted_iota(I32, (1, tq), 1), c)
        m = jnp.maximum(jnp.max(sc, axis=-1, keepdims=True), jnp.max(sn, axis=-1, keepdims=True))
        pc = jnp.exp2(sc - m)
        pn = jnp.exp2(sn - m)
        l = jnp.sum(pc, axis=-1, keepdims=True) + jnp.sum(pn, axis=-1, keepdims=True)
        vc = vc_ref[0, pl.ds(h, past, stride=H_B), :].astype(BF16)
        acc = (jnp.dot(pc.astype(BF16), vc, preferred_element_type=F32)
               + jnp.dot(pn.astype(BF16), vn_ref[:, cols].astype(BF16), preferred_element_type=F32))
        o_ref[:, cols] = _finish_heads(acc, 1.0 / l, lam, g_ref[...], tq, lam_init)


def _attn_sample(slab, kb, vb, cache_kt, cache_v2, g_norm, dl, batch, t_len, lam_init):
    past = cache_kt.shape[3]
    hw = 2 * D_B
    kern = functools.partial(_attn_sample_kernel, tq=t_len, past=past, lam_init=lam_init)
    return pl.pallas_call(
        kern,
        grid=(batch,),
        in_specs=[
            pl.BlockSpec((t_len, QB_W), lambda b: (b, SLAB_QB // QB_W)),
            pl.BlockSpec((1, H_B, hw, past), lambda b: (b, 0, 0, 0)),
            pl.BlockSpec((1, past * H_B, hw), lambda b: (b, 0, 0)),
            pl.BlockSpec((t_len, KB_W), lambda b: (b, 0)),
            pl.BlockSpec((t_len, VB_W), lambda b: (b, 0)),
            pl.BlockSpec((1, hw), lambda b: (0, 0)),
            pl.BlockSpec((4, D_B), lambda b: (0, 0)),
        ],
        out_specs=pl.BlockSpec((t_len, VB_W), lambda b: (b, 0)),
        out_shape=jax.ShapeDtypeStruct((batch * t_len, VB_W), BF16),
        compiler_params=_cparams(("parallel",)),
        name="attn_sample",
    )(slab, cache_kt, cache_v2, kb, vb, g_norm, dl)


def _finish_kernel(x_ref, oa_ref, ra_ref, ob_ref, gt_ref, wpa_ref, wpb_ref, wo_ref, g_ref, b_ref,
                   wrh_ref, wrl_ref, br_ref, h_ref, hp_ref, idx_ref, tw_ref, cnt_ref):
    i = pl.program_id(0)
    ra = ra_ref[...].astype(F32)
    ua = ra * _sigmoid(ra) * oa_ref[...].astype(F32)
    ya = jnp.dot(ua.astype(BF16), wpa_ref[...], preferred_element_type=F32)
    yb = jnp.dot(ob_ref[...], wpb_ref[...], preferred_element_type=F32)
    gt = gt_ref[...].astype(F32)
    mixed = _sigmoid(gt[:, :D_MODEL]) * ya + _sigmoid(gt[:, D_MODEL:]) * yb
    mix = jnp.dot(mixed.astype(BF16), wo_ref[...], preferred_element_type=F32)
    hh = _layer_norm(DEEPNORM_ALPHA * x_ref[...] + mix, g_ref[...], b_ref[...])
    h_ref[...] = hh
    _store_rows(hp_ref, _pack_bf16_pairs(hh))

    h_hi = hh.astype(BF16)
    h_lo = (hh - h_hi.astype(F32)).astype(BF16)
    lg = (jnp.dot(h_hi, wrh_ref[...], preferred_element_type=F32)
          + jnp.dot(h_lo, wrh_ref[...], preferred_element_type=F32)
          + jnp.dot(h_hi, wrl_ref[...], preferred_element_type=F32)) + br_ref[...]
    tm = lg.shape[0]
    lane = lax.broadcasted_iota(I32, (tm, LANES), 1)
    lane_f = lane.astype(F32)
    work = lg
    vals, idxs = [], []
    cnt = jnp.zeros((tm, LANES), F32)
    for _ in range(TOP_K):
        mx = jnp.max(work, axis=-1, keepdims=True)
        ix = jnp.min(jnp.where(work == mx, lane_f, float(LANES)), axis=-1, keepdims=True)
        hit = lane_f == ix
        cnt = cnt + hit.astype(F32)
        work = jnp.where(hit, NEG_INF, work)
        vals.append(mx)
        idxs.append(ix)
    es = [jnp.exp(v - vals[0]) for v in vals]
    den = es[0] + es[1] + es[2] + es[3]
    idx_full = jnp.zeros((tm, LANES), F32)
    tw_full = jnp.zeros((tm, LANES), F32)
    for r in range(TOP_K):
        idx_full = jnp.where(lane == r, idxs[r], idx_full)
        tw_full = jnp.where(lane == r, es[r] / den, tw_full)
    idx_ref[...] = idx_full.T[:SUBLANES].astype(I32)
    tw_ref[...] = tw_full[:, :TOP_K]

    @pl.when(i == 0)
    def _():
        cnt_ref[...] = jnp.zeros(cnt_ref.shape, F32)

    cnt_ref[...] += jnp.sum(cnt, axis=0, keepdims=True)


def _finish(x2, oa, slab, ob, p):
    n = x2.shape[0]
    tm = min(512, n)
    const = lambda i: (0, 0)
    return pl.pallas_call(
        _finish_kernel,
        grid=(n // tm,),
        in_specs=[
            pl.BlockSpec((tm, D_MODEL), lambda i: (i, 0)),
            pl.BlockSpec((tm, VA_W), lambda i: (i, 0)),
            pl.BlockSpec((tm, RA_W), lambda i: (i, SLAB_RA // RA_W)),
            pl.BlockSpec((tm, VB_W), lambda i: (i, 0)),
            pl.BlockSpec((tm, GT_W), lambda i: (i, SLAB_GT // GT_W)),
            pl.BlockSpec((VA_W, D_MODEL), const),
            pl.BlockSpec((VB_W, D_MODEL), const),
            pl.BlockSpec((D_MODEL, D_MODEL), const),
            pl.BlockSpec((1, D_MODEL), const),
            pl.BlockSpec((1, D_MODEL), const),
            pl.BlockSpec((D_MODEL, LANES), const),
            pl.BlockSpec((D_MODEL, LANES), const),
            pl.BlockSpec((1, LANES), const),
        ],
        out_specs=[
            pl.BlockSpec((tm, D_MODEL), lambda i: (i, 0)),
            pl.BlockSpec((tm * ROW_TILES, LANES), lambda i: (i, 0)),
            pl.BlockSpec((SUBLANES, tm), lambda i: (0, i)),
            pl.BlockSpec((tm, TOP_K), lambda i: (i, 0)),
            pl.BlockSpec((1, LANES), const),
        ],
        out_shape=[
            jax.ShapeDtypeStruct((n, D_MODEL), F32),
            jax.ShapeDtypeStruct((n * ROW_TILES, LANES), U32),
            jax.ShapeDtypeStruct((SUBLANES, n), I32),
            jax.ShapeDtypeStruct((n, TOP_K), F32),
            jax.ShapeDtypeStruct((1, LANES), F32),
        ],
        compiler_params=_cparams(("arbitrary",)),
        name="finish",
    )(x2, oa, slab, ob, slab, p["w_pa"], p["w_pb"], p["w_o"], p["ln1_g"], p["ln1_b"],
      p["w_r_hi"], p["w_r_lo"], p["b_r"])


def _dest_kernel(idx_ref, ps_ref, ut_ref, dest_ref, carry_ref):
    i = pl.program_id(0)

    @pl.when(i == 0)
    def _():
        carry_ref[...] = jnp.zeros(carry_ref.shape, F32)

    tm = idx_ref.shape[1]
    expert = lax.broadcasted_iota(I32, (N_EXPERTS, tm), 0)
    idx = idx_ref[...]
    hits = [expert == idx[k:k + 1, :] for k in range(TOP_K)]
    cnt = jnp.zeros((N_EXPERTS, tm), F32)
    for hit in hits:
        cnt = cnt + hit.astype(F32)
    before = jnp.dot(cnt.astype(BF16), ut_ref[...], preferred_element_type=F32)
    base = before + carry_ref[:, 0:1] + ps_ref[:, 0:1]
    row = lax.broadcasted_iota(I32, (SUBLANES, tm), 0)
    dest = jnp.zeros((SUBLANES, tm), F32)
    for k in range(TOP_K):
        d = jnp.sum(jnp.where(hits[k], base, 0.0), axis=0, keepdims=True)
        dest = jnp.where(row == k, d, dest)
    dest_ref[...] = dest.astype(I32)
    carry_ref[...] += jnp.sum(cnt, axis=1, keepdims=True)


def _dest(idx_t, pstart):
    n = idx_t.shape[1]
    tm = min(512, n)
    utri = jnp.asarray(np.triu(np.ones((tm, tm), np.float32), 1), BF16)
    return pl.pallas_call(
        _dest_kernel,
        grid=(n // tm,),
        in_specs=[
            pl.BlockSpec((SUBLANES, tm), lambda i: (0, i)),
            pl.BlockSpec((N_EXPERTS, LANES), lambda i: (0, 0)),
            pl.BlockSpec((tm, tm), lambda i: (0, 0)),
        ],
        out_specs=pl.BlockSpec((SUBLANES, tm), lambda i: (0, i)),
        out_shape=jax.ShapeDtypeStruct((SUBLANES, n), I32),
        scratch_shapes=[pltpu.VMEM((N_EXPERTS, LANES), F32)],
        compiler_params=_cparams(("arbitrary",)),
        name="dest",
    )(idx_t, pstart, utri)


def _scatter_kernel(dest_ref, pad0_ref, padn_ref, hp_ref, rows_ref, zero_ref, sem, pad_sem):
    tm = hp_ref.shape[0] // ROW_TILES

    def row_copy(t, d):
        return pltpu.make_async_copy(hp_ref.at[pl.ds(pl.multiple_of(t * ROW_TILES, ROW_TILES), ROW_TILES)],
                                     rows_ref.at[pl.ds(pl.multiple_of(d * ROW_TILES, ROW_TILES), ROW_TILES)], sem)

    def issue(t, carry):
        for k in range(TOP_K):
            row_copy(t, dest_ref[k, t]).start(priority=k % 2)
        return carry

    lax.fori_loop(0, tm, issue, 0, unroll=4)

    @pl.when(pl.program_id(0) == pl.num_programs(0) - 1)
    def _():
        zero_ref[...] = jnp.zeros(zero_ref.shape, U32)

        def pad_copy(d, p):
            return pltpu.make_async_copy(
                zero_ref.at[pl.ds(0, p * ROW_TILES)],
                rows_ref.at[pl.ds(pl.multiple_of(d * ROW_TILES, ROW_TILES), p * ROW_TILES)], pad_sem)

        def for_each_pad_run(fn):
            def per_expert(e, carry):
                d, n = pad0_ref[e], padn_ref[e]
                p = EXPERT_BM // 2
                while p:
                    @pl.when((n & p) != 0)
                    def _(d=d, p=p):
                        fn(d, p)
                    d = d + (n & p)
                    p //= 2
                return carry
            lax.fori_loop(0, N_EXPERTS, per_expert, 0)

        for_each_pad_run(lambda d, p: pad_copy(d, p).start())
        for_each_pad_run(lambda d, p: pad_copy(d, p).wait())

    for k in range(TOP_K):
        pltpu.make_async_copy(hp_ref, rows_ref.at[pl.ds(0, tm * ROW_TILES)], sem).wait()


def _scatter(dest_t, pad_start, pad_len, hp, n_rows):
    n = hp.shape[0] // ROW_TILES
    tm = min(512, n)
    smem = pl.BlockSpec(memory_space=pltpu.SMEM)
    return pl.pallas_call(
        _scatter_kernel,
        grid=(n // tm,),
        in_specs=[
            pl.BlockSpec((SUBLANES, tm), lambda i: (0, i), memory_space=pltpu.SMEM),
            smem, smem,
            pl.BlockSpec((tm * ROW_TILES, LANES), lambda i: (i, 0)),
        ],
        out_specs=pl.BlockSpec(memory_space=pl.ANY),
        out_shape=jax.ShapeDtypeStruct((n_rows * ROW_TILES, LANES), U32),
        scratch_shapes=[pltpu.VMEM((EXPERT_BM // 2 * ROW_TILES, LANES), U32), pltpu.SemaphoreType.DMA(()),
                        pltpu.SemaphoreType.DMA(())],
        compiler_params=_cparams(("arbitrary",)),
        name="scatter",
    )(dest_t, pad_start, pad_len, hp)


def _expert_kernel(be_ref, nu_ref, rows_ref, wgu_ref, bgu_ref, wd_ref, bd_ref, out_ref, wgu_bf_ref, wd_bf_ref):
    i = pl.program_id(0)

    @pl.when((i == 0) | (be_ref[i] != be_ref[jnp.maximum(i - 1, 0)]))
    def _():
        wgu_bf_ref[...] = wgu_ref[0].astype(BF16)
        wd_bf_ref[...] = wd_ref[0].astype(BF16)

    @pl.when(i < nu_ref[0])
    def _():
        x = _unpack_bf16_pairs(_load_rows(rows_ref)).astype(BF16)
        gu = jnp.dot(x, wgu_bf_ref[...], preferred_element_type=F32) + bgu_ref[0]
        gate = jnp.minimum(gu[:, :D_FF], SWIGLU_LIMIT)
        up = jnp.clip(gu[:, D_FF:], -SWIGLU_LIMIT, SWIGLU_LIMIT)
        hidden = (up + 1.0) * (gate * _sigmoid(SWIGLU_ALPHA * gate))
        o = jnp.dot(hidden.astype(BF16), wd_bf_ref[...], preferred_element_type=F32) + bd_ref[0]
        _store_rows(out_ref, _pack_bf16_pairs(o))

    @pl.when(i >= nu_ref[0])
    def _():
        out_ref[...] = jnp.zeros(out_ref.shape, U32)


def _experts(rows, block_e, n_used, p):
    n_rows = rows.shape[0] // ROW_TILES
    bm = EXPERT_BM
    n_blocks = n_rows // bm
    grid_spec = pltpu.PrefetchScalarGridSpec(
        num_scalar_prefetch=2,
        grid=(n_blocks,),
        in_specs=[
            pl.BlockSpec((bm * ROW_TILES, LANES), lambda i, be, nu: (jnp.minimum(i, nu[0] - 1), 0)),
            pl.BlockSpec((1, D_MODEL, 2 * D_FF), lambda i, be, nu: (be[i], 0, 0)),
            pl.BlockSpec((1, 1, 2 * D_FF), lambda i, be, nu: (be[i], 0, 0)),
            pl.BlockSpec((1, D_FF, D_MODEL), lambda i, be, nu: (be[i], 0, 0)),
            pl.BlockSpec((1, 1, D_MODEL), lambda i, be, nu: (be[i], 0, 0)),
        ],
        out_specs=pl.BlockSpec((bm * ROW_TILES, LANES), lambda i, be, nu: (i, 0)),
        scratch_shapes=[pltpu.VMEM((D_MODEL, 2 * D_FF), BF16), pltpu.VMEM((D_FF, D_MODEL), BF16)],
    )
    return pl.pallas_call(
        _expert_kernel,
        grid_spec=grid_spec,
        out_shape=jax.ShapeDtypeStruct((n_rows * ROW_TILES, LANES), U32),
        compiler_params=_cparams(("arbitrary",)),
        name="experts",
    )(block_e, n_used, rows, p["w_gu"], p["b_gu"], p["w_down"], p["b_down"])


def _combine_kernel(dest_ref, tw_ref, h_ref, rows_ref, g_ref, b_ref, y_ref, buf_ref, sem):
    tm = h_ref.shape[0]

    def row_copy(t, k, d):
        return pltpu.make_async_copy(rows_ref.at[pl.ds(pl.multiple_of(d * ROW_TILES, ROW_TILES), ROW_TILES)],
                                     buf_ref.at[k, pl.ds(pl.multiple_of(t * ROW_TILES, ROW_TILES), ROW_TILES)], sem)

    def issue(t, carry):
        for k in range(TOP_K):
            row_copy(t, k, dest_ref[k, t]).start(priority=k % 2)
        return carry

    lax.fori_loop(0, tm, issue, 0, unroll=4)
    for k in range(TOP_K):
        pltpu.make_async_copy(rows_ref.at[pl.ds(0, tm * ROW_TILES)], buf_ref.at[k], sem).wait()
    tw = tw_ref[...]
    f = jnp.zeros((tm, D_MODEL), F32)
    for k in range(TOP_K):
        f = f + tw[:, k:k + 1] * _unpack_bf16_pairs(_load_rows(buf_ref.at[k]))
    y_ref[...] = _layer_norm(DEEPNORM_ALPHA * h_ref[...] + f, g_ref[...], b_ref[...])


def _combine(dest_t, top_w, hh, out_rows, p):
    n = hh.shape[0]
    tm = min(512, n)
    const = lambda i: (0, 0)
    return pl.pallas_call(
        _combine_kernel,
        grid=(n // tm,),
        in_specs=[
            pl.BlockSpec((SUBLANES, tm), lambda i: (0, i), memory_space=pltpu.SMEM),
            pl.BlockSpec((tm, TOP_K), lambda i: (i, 0)),
            pl.BlockSpec((tm, D_MODEL), lambda i: (i, 0)),
            pl.BlockSpec(memory_space=pl.ANY),
            pl.BlockSpec((1, D_MODEL), const),
            pl.BlockSpec((1, D_MODEL), const),
        ],
        out_specs=pl.BlockSpec((tm, D_MODEL), lambda i: (i, 0)),
        out_shape=jax.ShapeDtypeStruct((n, D_MODEL), F32),
        scratch_shapes=[pltpu.VMEM((TOP_K, tm * ROW_TILES, LANES), U32), pltpu.SemaphoreType.DMA(())],
        compiler_params=_cparams(("arbitrary",)),
        name="combine",
    )(dest_t, top_w, hh, out_rows, p["ln2_g"], p["ln2_b"])


def _moe(hh, hp, top_idx, top_w, counts, p):
    n = hh.shape[0]
    bm = EXPERT_BM
    n_rows = n * TOP_K + N_EXPERTS * bm
    cnt = counts[0, :N_EXPERTS].astype(I32)
    padded = (cnt + bm - 1) // bm * bm
    pend = jnp.cumsum(padded)
    pstart = pend - padded
    ps_col = jnp.broadcast_to(pstart.astype(F32)[:, None], (N_EXPERTS, LANES))
    block_row0 = jnp.arange(n_rows // bm, dtype=I32) * bm
    block_e = jnp.minimum(jnp.sum((pend[None, :] <= block_row0[:, None]).astype(I32), axis=1), N_EXPERTS - 1)
    n_used = (pend[-1:] // bm).astype(I32)
    dest = _dest(top_idx, ps_col)
    rows = _scatter(dest, pstart + cnt, padded - cnt, hp, n_rows)
    out_rows = _experts(rows, block_e, n_used, p)
    return _combine(dest, top_w, hh, out_rows, p)


def _prep_params(w_in, b_in, w_alpha, b_alpha, gla_norm_g, diff_norm_g, w_pa, w_pb, w_o, ln1_g, ln1_b,
                 w_router, b_router, w_gu, b_gu, w_down, b_down, ln2_g, ln2_b):
    o = IN_OFFS
    seg = lambda a, i: a[..., o[i]:o[i + 1]]
    order = (0, 1, 2, 3, 5, 8)
    w_slab = jnp.concatenate([seg(w_in, i) for i in order], axis=1).astype(BF16)
    b_slab = jnp.concatenate([seg(b_in, i) for i in order])[None, :]
    scale = np.ones((1, SLAB_W), np.float32)
    scale[0, SLAB_QA:SLAB_QA + QA_W] = DK_A ** -0.5
    scale[0, SLAB_QB:SLAB_QB + QB_W] = D_B ** -0.5 * LOG2E
    w_lr = jnp.zeros((D_MODEL, LANES), F32).at[:, :LR_W].set(seg(w_in, 4)).astype(BF16)
    b_lr = jnp.zeros((1, LANES), F32).at[0, :LR_W].set(seg(b_in, 4))
    w_al = jnp.zeros((LANES, QA_W), F32).at[:LR_W].set(w_alpha).astype(BF16)
    w_r = jnp.zeros((D_MODEL, LANES), F32).at[:, :N_EXPERTS].set(w_router)
    w_r_hi = w_r.astype(BF16)
    w_r_lo = (w_r - w_r_hi.astype(F32)).astype(BF16)
    b_r = jnp.full((1, LANES), NEG_INF, F32).at[0, :N_EXPERTS].set(b_router)
    return dict(
        w_slab=w_slab, b_slab=b_slab, s_slab=jnp.asarray(scale),
        w_lr=w_lr, b_lr=b_lr, w_alpha=w_al, b_alpha=b_alpha[None, :],
        w_k=seg(w_in, 6).astype(BF16), w_v=seg(w_in, 7).astype(BF16),
        b_k=seg(b_in, 6)[None, :], b_v=seg(b_in, 7)[None, :],
        gla_g=gla_norm_g[None, :], diff_g=diff_norm_g[None, :],
        w_pa=w_pa.astype(BF16), w_pb=w_pb.astype(BF16), w_o=w_o.astype(BF16),
        ln1_g=ln1_g[None, :], ln1_b=ln1_b[None, :],
        w_r_hi=w_r_hi, w_r_lo=w_r_lo, b_r=b_r,
        w_gu=w_gu, b_gu=b_gu[:, None, :], w_down=w_down, b_down=b_down[:, None, :],
        ln2_g=ln2_g[None, :], ln2_b=ln2_b[None, :],
    )


def _layer(x, p, dl, lam_init, s0, cache_k, cache_v):
    batch, t_len, _ = x.shape
    x2 = x.reshape(batch * t_len, D_MODEL)
    slab, loga, k_rows, kb, vb = _proj(x2, p)
    oa, s_new = _gla(slab, loga, s0, p["gla_g"], batch, t_len)
    if cache_k is None:
        ob = _attn_prompt(slab, kb, vb, p["diff_g"], dl, batch, t_len, lam_init)
    else:
        ob = _attn_sample(slab, kb, vb, cache_k, cache_v, p["diff_g"], dl, batch, t_len, lam_init)
    hh, hp, top_idx, top_w, counts = _finish(x2, oa, slab, ob, p)
    y = _moe(hh, hp, top_idx, top_w, counts, p)
    return (y.reshape(batch, t_len, D_MODEL), s_new,
            k_rows.reshape(batch, t_len, H_B, 2, D_B), vb.reshape(batch, t_len, H_B, 2 * D_B))


def kernel(x_prompt, x_sample, cache_k, cache_v, state_gla, w_in, b_in, w_alpha, b_alpha, gla_norm_g,
           diff_lambda, diff_norm_g, w_pa, w_pb, w_o, ln1_g, ln1_b, w_router, b_router, w_gu, b_gu,
           w_down, b_down, ln2_g, ln2_b):
    assert w_in.shape[0] == DEPTH == 1
    l = 0
    lam_init = 0.8 - 0.6 * math.exp(-0.3 * l)
    p = _prep_params(w_in[l], b_in[l], w_alpha[l], b_alpha[l], gla_norm_g[l], diff_norm_g[l], w_pa[l],
                     w_pb[l], w_o[l], ln1_g[l], ln1_b[l], w_router[l], b_router[l], w_gu[l], b_gu[l],
                     w_down[l], b_down[l], ln2_g[l], ln2_b[l])
    dl = diff_lambda[l]
    bp = x_prompt.shape[0]
    bs, ts = x_sample.shape[0], x_sample.shape[1]
    past = cache_k.shape[2]
    yp, s_p, k_p, v_p = _layer(x_prompt, p, dl, lam_init, jnp.zeros((bp, H_A, DK_A, DV_A), F32), None, None)
    ck = jnp.transpose(cache_k[l], (0, 2, 3, 4, 1)).reshape(bs, H_B, 2 * D_B, past)
    cv = cache_v[l].reshape(bs, past * H_B, 2 * D_B)
    ys, s_s, k_s, v_s = _layer(x_sample, p, dl, lam_init, state_gla[l], ck, cv)
    return (yp, ys, s_p[None], k_p[None], v_p[None], s_s[None], k_s[None], v_s[None])
```

```python
import functools
import math

import numpy as np
import jax
import jax.numpy as jnp
from jax import lax
from jax.experimental import pallas as pl
from jax.experimental.pallas import tpu as pltpu

F32 = jnp.float32
BF16 = jnp.bfloat16
U32 = jnp.uint32
I32 = jnp.int32

D_MODEL = 1024
CHUNK = 64
H_A = 4
DK_A = 128
DV_A = 256
GATE_RANK = 16
GATE_TAU = 16.0
H_B = 8
D_B = 64
N_EXPERTS = 32
TOP_K = 4
D_FF = D_MODEL
SWIGLU_LIMIT = 7.0
SWIGLU_ALPHA = 1.702
EPS = 1e-5
DEPTH = 1
DEEPNORM_ALPHA = (2.0 * DEPTH) ** 0.25

QA_W = H_A * DK_A
KA_W = H_A * DK_A
VA_W = H_A * DV_A
RA_W = H_A * DV_A
LR_W = GATE_RANK
QB_W = H_B * 2 * D_B
KB_W = H_B * 2 * D_B
VB_W = H_B * 2 * D_B
GT_W = 2 * D_MODEL
IN_SIZES = (QA_W, KA_W, VA_W, RA_W, LR_W, QB_W, KB_W, VB_W, GT_W)
IN_OFFS = tuple(int(v) for v in np.cumsum((0,) + IN_SIZES))

SLAB_QA, SLAB_KA, SLAB_VA, SLAB_RA, SLAB_QB, SLAB_GT = 0, 512, 1024, 2048, 3072, 4096
SLAB_W = 6144
LANES = 128
SUBLANES = 8
EXPERT_BM = 512
VMEM_LIMIT = 56 * 1024 * 1024
PROJ_VMEM_LIMIT = 61 * 1024 * 1024

NEG_INF = float("-inf")
LOG2E = math.log2(math.e)


def _cparams(sem, vmem_limit=VMEM_LIMIT):
    return pltpu.CompilerParams(dimension_semantics=sem, vmem_limit_bytes=vmem_limit)


def _sigmoid(x):
    return 1.0 / (1.0 + jnp.exp(-x))


def _pack_bf16_pairs(x):
    n = x.shape[1] // 2
    xb = x.astype(BF16).astype(F32)
    lo = pltpu.bitcast(xb[:, :n], U32) >> 16
    hi = pltpu.bitcast(xb[:, n:], U32) & jnp.uint32(0xFFFF0000)
    return hi | lo


def _unpack_bf16_pairs(r):
    lo = pltpu.bitcast(r << 16, F32)
    hi = pltpu.bitcast(r & jnp.uint32(0xFFFF0000), F32)
    return jnp.concatenate([lo, hi], axis=1)


ROW_TILES = D_MODEL // 2 // LANES


def _load_rows(ref):
    m = ref.shape[0] // ROW_TILES
    return jnp.concatenate([ref[pl.ds(c, m, stride=ROW_TILES), :] for c in range(ROW_TILES)], axis=1)


def _store_rows(ref, val):
    m = ref.shape[0] // ROW_TILES
    for c in range(ROW_TILES):
        ref[pl.ds(c, m, stride=ROW_TILES), :] = val[:, c * LANES:(c + 1) * LANES]


def _layer_norm(y, g, b):
    mu = jnp.mean(y, axis=-1, keepdims=True)
    d = y - mu
    var = jnp.mean(d * d, axis=-1, keepdims=True)
    return d * lax.rsqrt(var + EPS) * g + b


def _proj_kernel(x_ref, w_ref, b_ref, s_ref, wlr_ref, blr_ref, wal_ref, bal_ref, wk_ref, wv_ref, bk_ref, bv_ref,
                 slab_ref, loga_ref, krows_ref, kb_ref, v_ref, *, tn):
    xb = x_ref[...].astype(BF16)
    k = jnp.dot(xb, wk_ref[...], preferred_element_type=F32) + bk_ref[...]
    kb_ref[...] = k.astype(BF16)
    for h in range(H_B):
        for j in range(2):
            c0 = (2 * h + j) * D_B
            krows_ref[:, h, j, :] = k[:, c0:c0 + D_B]
    v_ref[...] = jnp.dot(xb, wv_ref[...], preferred_element_type=F32) + bv_ref[...]
    lra = jnp.dot(xb, wlr_ref[...], preferred_element_type=F32) + blr_ref[...]
    z = jnp.dot(lra.astype(BF16), wal_ref[...], preferred_element_type=F32) + bal_ref[...]
    loga_ref[...] = (jnp.minimum(z, 0.0) - jnp.log(1.0 + jnp.exp(-jnp.abs(z)))) * (1.0 / GATE_TAU)
    for c0 in range(0, SLAB_W, tn):
        acc = jnp.dot(xb, w_ref[:, c0:c0 + tn], preferred_element_type=F32)
        slab_ref[:, c0:c0 + tn] = ((acc + b_ref[:, c0:c0 + tn]) * s_ref[:, c0:c0 + tn]).astype(BF16)


def _proj(x2, p):
    n = x2.shape[0]
    tm = min(512, n)
    const = lambda i: (0, 0)
    whole = lambda shape: pl.BlockSpec(shape, const, pipeline_mode=pl.Buffered(1))
    row = lambda w: pl.BlockSpec((tm, w), lambda i: (i, 0))
    return pl.pallas_call(
        functools.partial(_proj_kernel, tn=1024),
        grid=(n // tm,),
        in_specs=[
            row(D_MODEL),
            whole((D_MODEL, SLAB_W)), whole((1, SLAB_W)), whole((1, SLAB_W)),
            whole((D_MODEL, LANES)), whole((1, LANES)), whole((LANES, QA_W)), whole((1, QA_W)),
            whole((D_MODEL, KB_W)), whole((D_MODEL, VB_W)), whole((1, KB_W)), whole((1, VB_W)),
        ],
        out_specs=[
            row(SLAB_W), row(QA_W),
            pl.BlockSpec((tm, H_B, 2, D_B), lambda i: (i, 0, 0, 0)),
            row(KB_W), row(VB_W),
        ],
        out_shape=[
            jax.ShapeDtypeStruct((n, SLAB_W), BF16),
            jax.ShapeDtypeStruct((n, QA_W), F32),
            jax.ShapeDtypeStruct((n, H_B, 2, D_B), F32),
            jax.ShapeDtypeStruct((n, KB_W), BF16),
            jax.ShapeDtypeStruct((n, VB_W), F32),
        ],
        compiler_params=_cparams(("parallel",), PROJ_VMEM_LIMIT),
        name="proj",
    )(x2, p["w_slab"], p["b_slab"], p["s_slab"], p["w_lr"], p["b_lr"], p["w_alpha"], p["b_alpha"],
      p["w_k"], p["w_v"], p["b_k"], p["b_v"])


def _gla_tables(L):
    nl = int(math.log2(L))
    t = np.arange(L)
    D = np.zeros(((nl + 2) * L, L), np.float32)
    masks = np.zeros((nl + 1, L, L), np.float32)
    for l in range(nl):
        m = L >> (l + 1)
        grp = t // (2 * m)
        mid = grp * 2 * m + m - 1
        upper = (t % (2 * m)) >= m
        for r in range(L):
            if upper[r]:
                D[l * L + r, mid[r] + 1:r + 1] = 1.0
            else:
                D[l * L + r, r + 1:mid[r] + 1] = 1.0
        masks[l] = (upper[:, None] & ~upper[None, :] & (grp[:, None] == grp[None, :])).astype(np.float32)
    D[nl * L:(nl + 1) * L] = np.tril(np.ones((L, L), np.float32))
    D[(nl + 1) * L:] = np.triu(np.ones((L, L), np.float32), 1)
    masks[nl] = np.eye(L, dtype=np.float32)
    return jnp.asarray(D, BF16), jnp.asarray(masks, F32)


def _gla_kernel(q_ref, k_ref, v_ref, la_ref, s0_ref, g_ref, d_ref, m_ref, o_ref, s_ref, *, L, n_chunks):
    c = pl.program_id(1)
    nl = int(math.log2(L))

    @pl.when(c == 0)
    def _():
        s_ref[...] = s0_ref[...]

    dmat = d_ref[...]
    ones_col = jnp.ones((L, LANES), BF16)
    g = g_ref[...]
    nt = (((1,), (1,)), ((), ()))
    tn = (((0,), (0,)), ((), ()))

    def chunk(ci, carry):
        r0 = pl.multiple_of(ci * L, L)
        rows = pl.ds(r0, L)
        la_all = la_ref[rows, :]
        la_hi = la_all.astype(BF16)
        la_lo = (la_all - la_hi.astype(F32)).astype(BF16)
        x_all = jnp.exp(jnp.dot(dmat, la_hi, preferred_element_type=F32)
                        + jnp.dot(dmat, la_lo, preferred_element_type=F32))
        bl_all = (lax.dot_general(la_hi, ones_col, tn, preferred_element_type=F32)
                  + lax.dot_general(la_lo, ones_col, tn, preferred_element_type=F32))
        dec_all = jnp.exp(bl_all[:, 0:1])
        for h in range(H_A):
            kc = slice(h * DK_A, (h + 1) * DK_A)
            vc = slice(h * DV_A, (h + 1) * DV_A)
            q = q_ref[rows, kc].astype(F32)
            k = k_ref[rows, kc].astype(F32)
            v = v_ref[rows, vc]
            x = x_all[:, kc]
            a = m_ref[nl] * lax.dot_general(q.astype(BF16), k.astype(BF16), nt, preferred_element_type=F32)
            for l in range(nl):
                xl = x[l * L:(l + 1) * L]
                a = a + m_ref[l] * lax.dot_general((q * xl).astype(BF16), (k * xl).astype(BF16), nt,
                                                   preferred_element_type=F32)
            xb = x[nl * L:(nl + 1) * L]
            xs = x[(nl + 1) * L:]
            s_old = s_ref[0, h]
            o = (jnp.dot((q * xb).astype(BF16), s_old.astype(BF16), preferred_element_type=F32)
                 + jnp.dot(a.astype(BF16), v, preferred_element_type=F32))
            s_ref[0, h] = dec_all[kc] * s_old + lax.dot_general((k * xs).astype(BF16), v, tn, preferred_element_type=F32)
            ms = jnp.mean(o * o, axis=-1, keepdims=True)
            o_ref[rows, vc] = (o * lax.rsqrt(ms + EPS) * g).astype(BF16)
        return carry

    lax.fori_loop(0, n_chunks, chunk, 0, unroll=4 if n_chunks % 4 == 0 else 1)


def _gla(slab, loga, s0, g_norm, batch, t_len):
    L = min(CHUNK, t_len)
    tb = min(512, t_len)
    nb = t_len // tb
    dmat, masks = _gla_tables(L)
    kern = functools.partial(_gla_kernel, L=L, n_chunks=tb // L)
    row = lambda b, c: b * nb + c
    return pl.pallas_call(
        kern,
        grid=(batch, nb),
        in_specs=[
            pl.BlockSpec((tb, QA_W), lambda b, c: (row(b, c), SLAB_QA // QA_W)),
            pl.BlockSpec((tb, KA_W), lambda b, c: (row(b, c), SLAB_KA // KA_W)),
            pl.BlockSpec((tb, VA_W), lambda b, c: (row(b, c), SLAB_VA // VA_W)),
            pl.BlockSpec((tb, QA_W), lambda b, c: (row(b, c), 0)),
            pl.BlockSpec((1, H_A, DK_A, DV_A), lambda b, c: (b, 0, 0, 0)),
            pl.BlockSpec((1, DV_A), lambda b, c: (0, 0)),
            pl.BlockSpec(dmat.shape, lambda b, c: (0, 0)),
            pl.BlockSpec(masks.shape, lambda b, c: (0, 0, 0)),
        ],
        out_specs=[
            pl.BlockSpec((tb, VA_W), lambda b, c: (row(b, c), 0)),
            pl.BlockSpec((1, H_A, DK_A, DV_A), lambda b, c: (b, 0, 0, 0)),
        ],
        out_shape=[
            jax.ShapeDtypeStruct((batch * t_len, VA_W), BF16),
            jax.ShapeDtypeStruct((batch, H_A, DK_A, DV_A), F32),
        ],
        compiler_params=_cparams(("parallel", "arbitrary")),
        name="gla",
    )(slab, slab, slab, loga, s0, g_norm, dmat, masks)


def _lambda_from(dl_ref, lam_init):
    dl = dl_ref[...]
    a = jnp.sum(dl[0:1] * dl[1:2], axis=-1, keepdims=True)
    b = jnp.sum(dl[2:3] * dl[3:4], axis=-1, keepdims=True)
    return jnp.exp(a) - jnp.exp(b) + lam_init


def _split_maps(q):
    lane = lax.broadcasted_iota(I32, q.shape, 1)
    zero = jnp.zeros_like(q)
    return jnp.concatenate([jnp.where(lane < D_B, q, zero), jnp.where(lane >= D_B, q, zero)], axis=0)


def _alibi_coef(h):
    c = jnp.full((1, 1), LOG2E, F32) * jnp.exp2(-(h + 1).astype(F32))
    c_hi = c.astype(BF16).astype(F32)
    return c, c_hi, c - c_hi


def _finish_heads(acc, inv_l, lam, g, tq, lam_init):
    o = acc * inv_l
    out = o[:tq] - lam * o[tq:]
    ms = jnp.mean(out * out, axis=-1, keepdims=True)
    return (out * lax.rsqrt(ms + EPS) * g * (1.0 - lam_init)).astype(BF16)


def _attn_prompt_kernel(q_ref, k_ref, v_ref, g_ref, dl_ref, o_ref, ka_ref, vb_ref, qaug_ref, d_ref, m_ref,
                        acc_ref, s0_ref, s1_ref, p0_ref, p1_ref,
                        *, tq, tk, lam_init):
    h = pl.program_id(1)
    qi = pl.program_id(2)
    c, c_hi, c_lo = _alibi_coef(h)
    hw = 2 * D_B

    def aug_lanes(shape, pos, sign):
        lane = lax.broadcasted_iota(I32, shape, 1)
        r = (pos & 255).astype(F32) * sign
        a = (pos >> 8).astype(F32) * sign
        coef = jnp.where(lane == 0, c_hi, jnp.where(lane == 1, c_lo,
                         jnp.where(lane == 2, 256.0 * c_hi, jnp.where(lane == 3, 256.0 * c_lo, 0.0))))
        ints = jnp.where(lane < 2, r, jnp.where(lane < 4, a, 0.0))
        return coef, ints, lane

    @pl.when(qi == 0)
    def _():
        t_len = vb_ref.shape[0]
        lane = lax.broadcasted_iota(I32, (t_len, hw), 1)
        row = lax.broadcasted_iota(I32, (hw, tk), 0)
        j_rel = lax.broadcasted_iota(I32, (hw, tk), 1)
        k_aug_t = jnp.where(row < 2, (j_rel & 255).astype(F32), jnp.where(row < 4, (j_rel >> 8).astype(F32),
                  jnp.where(row == 4, c_hi, jnp.where(row == 5, c_lo,
                  jnp.where(row == 6, 256.0 * c_hi, jnp.where(row == 7, 256.0 * c_lo, 0.0)))))).astype(BF16)
        for blk in range(t_len // tk):
            rows = slice(blk * tk, (blk + 1) * tk)
            ka_ref[blk, :hw, :] = k_ref[0, rows, :].astype(F32).T.astype(BF16)
            ka_ref[blk, hw:, :] = k_aug_t
        vb_ref[:, :hw] = v_ref[0].astype(BF16)
        vb_ref[:, hw:] = jnp.where(lane == 0, 1.0, 0.0).astype(BF16)
        i_rel = lax.broadcasted_iota(I32, (2 * tq, hw), 0) & (tq - 1)
        coef, ints, lane = aug_lanes((2 * tq, hw), i_rel, -1.0)
        qaug_ref[...] = jnp.where(lane < 4, coef, pltpu.roll(ints, 4, 1)).astype(BF16)
        i = lax.broadcasted_iota(I32, (2 * tq, tq), 0) & (tq - 1)
        j = lax.broadcasted_iota(I32, (2 * tq, tq), 1)
        fwd = jnp.maximum(j - i, 0).astype(F32)
        d_ref[...] = jnp.where((j >> 6) <= (i >> 6), -2.0 * c * fwd, NEG_INF)

    qa = jnp.concatenate([_split_maps(q_ref[...]), qaug_ref[...]], axis=1)
    m_ref[...] = jnp.full(m_ref.shape, -1e30, F32)

    def scores(kv):
        return jnp.dot(qa, ka_ref[kv], preferred_element_type=F32)

    def weighted_values(p_ref, kv):
        k0 = pl.multiple_of(kv * tk, tk)
        return jnp.dot(p_ref[...], vb_ref[pl.ds(k0, tk), :], preferred_element_type=F32)

    def softmax_step(s, off):
        m_old = m_ref[...]
        m_new = jnp.maximum(m_old, jnp.max(s, axis=-1, keepdims=True) - off)
        shift = m_new + off
        p = jnp.exp2(s - jnp.concatenate([shift] * (tk // hw), axis=1))
        alpha = jnp.exp2(m_old - m_new)
        m_ref[...] = m_new
        return p.astype(BF16), jnp.concatenate([alpha, alpha], axis=1)

    s_first = scores(0)
    p_diag, _ = softmax_step(scores(qi) + d_ref[...], jnp.zeros((1, 1), F32))
    p1_ref[...] = p_diag
    s0_ref[...] = s_first
    acc_ref[...] = jnp.zeros(acc_ref.shape, F32)

    def step(kv, s_cur, s_nxt, p_prev, p_cur):
        pv_prev = weighted_values(p_prev, jnp.where(kv == 0, qi, kv - 1))
        s_nxt[...] = scores(jnp.minimum(kv + 1, qi - 1))
        p, alpha = softmax_step(s_cur[...], c * (qi * tq - kv * tk).astype(F32))
        acc_ref[...] = alpha * (acc_ref[...] + pv_prev)
        p_cur[...] = p

    def pair(j, carry):
        step(2 * j, s0_ref, s1_ref, p1_ref, p0_ref)
        step(2 * j + 1, s1_ref, s0_ref, p0_ref, p1_ref)
        return carry

    lax.fori_loop(0, qi // 2, pair, 0)

    @pl.when(qi % 2 == 1)
    def _():
        step(qi - 1, s0_ref, s1_ref, p1_ref, p0_ref)
        acc_ref[...] += weighted_values(p0_ref, qi - 1)

    @pl.when(qi % 2 == 0)
    def _():
        acc_ref[...] += weighted_values(p1_ref, jnp.maximum(qi - 1, 0))

    lam = _lambda_from(dl_ref, lam_init)
    acc = acc_ref[...]
    o_ref[...] = _finish_heads(acc[:, :hw], 1.0 / acc[:, hw:hw + 1], lam, g_ref[...], tq, lam_init)


def _attn_prompt(slab, kb, vb, g_norm, dl, batch, t_len, lam_init):
    tq = min(512, t_len)
    tk = tq
    assert tq % CHUNK == 0 and CHUNK == 64
    nq = t_len // tq
    k3 = kb.reshape(batch, t_len, KB_W)
    v3 = vb.reshape(batch, t_len, VB_W)
    kern = functools.partial(_attn_prompt_kernel, tq=tq, tk=tk, lam_init=lam_init)
    hw = 2 * D_B
    return pl.pallas_call(
        kern,
        grid=(batch, H_B, nq),
        in_specs=[
            pl.BlockSpec((tq, hw), lambda b, h, q: (b * nq + q, SLAB_QB // hw + h)),
            pl.BlockSpec((1, t_len, hw), lambda b, h, q: (b, 0, h)),
            pl.BlockSpec((1, t_len, hw), lambda b, h, q: (b, 0, h)),
            pl.BlockSpec((1, hw), lambda b, h, q: (0, 0)),
            pl.BlockSpec((4, D_B), lambda b, h, q: (0, 0)),
        ],
        out_specs=pl.BlockSpec((tq, hw), lambda b, h, q: (b * nq + q, h)),
        out_shape=jax.ShapeDtypeStruct((batch * t_len, VB_W), BF16),
        scratch_shapes=[
            pltpu.VMEM((t_len // tk, 2 * hw, tk), BF16),
            pltpu.VMEM((t_len, 2 * hw), BF16),
            pltpu.VMEM((2 * tq, hw), BF16),
            pltpu.VMEM((2 * tq, tq), F32),
            pltpu.VMEM((2 * tq, hw), F32),
            pltpu.VMEM((2 * tq, 2 * hw), F32),
            pltpu.VMEM((2 * tq, tk), F32),
            pltpu.VMEM((2 * tq, tk), F32),
            pltpu.VMEM((2 * tq, tk), BF16),
            pltpu.VMEM((2 * tq, tk), BF16),
        ],
        compiler_params=_cparams(("parallel", "parallel", "arbitrary")),
        name="attn_prompt",
    )(slab, k3, v3, g_norm, dl)


def _attn_sample_kernel(q_ref, kct_ref, vc_ref, kn_ref, vn_ref, g_ref, dl_ref, o_ref, *, tq, past, lam_init):
    hw = 2 * D_B
    q_pos = past + lax.broadcasted_iota(I32, (2 * tq, 1), 0) % tq
    lam = _lambda_from(dl_ref, lam_init)

    def biased(s, k_pos, c):
        s = s - c * jnp.abs(q_pos - k_pos).astype(F32)
        return jnp.where((k_pos // CHUNK) <= (q_pos // CHUNK), s, NEG_INF)

    for h in range(H_B):
        cols = slice(h * hw, (h + 1) * hw)
        c = LOG2E * 2.0 ** -(h + 1)
        qq = _split_maps(q_ref[:, cols])
        sc = biased(jnp.dot(qq, kct_ref[0, h].astype(BF16), preferred_element_type=F32),
                    lax.broadcasted_iota(I32, (1, past), 1), c)
        sn = biased(lax.dot_general(qq, kn_ref[:, cols], (((1,), (1,)), ((), ())), preferred_element_type=F32),
                    past + lax.broadcasted_iota(I32, (1, tq), 1), c)
        m = jnp.maximum(jnp.max(sc, axis=-1, keepdims=True), jnp.max(sn, axis=-1, keepdims=True))
        pc = jnp.exp2(sc - m)
        pn = jnp.exp2(sn - m)
        l = jnp.sum(pc, axis=-1, keepdims=True) + jnp.sum(pn, axis=-1, keepdims=True)
        vc = vc_ref[0, pl.ds(h, past, stride=H_B), :].astype(BF16)
        acc = (jnp.dot(pc.astype(BF16), vc, preferred_element_type=F32)
               + jnp.dot(pn.astype(BF16), vn_ref[:, cols].astype(BF16), preferred_element_type=F32))
        o_ref[:, cols] = _finish_heads(acc, 1.0 / l, lam, g_ref[...], tq, lam_init)


def _attn_sample(slab, kb, vb, cache_kt, cache_v2, g_norm, dl, batch, t_len, lam_init):
    past = cache_kt.shape[3]
    hw = 2 * D_B
    kern = functools.partial(_attn_sample_kernel, tq=t_len, past=past, lam_init=lam_init)
    return pl.pallas_call(
        kern,
        grid=(batch,),
        in_specs=[
            pl.BlockSpec((t_len, QB_W), lambda b: (b, SLAB_QB // QB_W)),
            pl.BlockSpec((1, H_B, hw, past), lambda b: (b, 0, 0, 0)),
            pl.BlockSpec((1, past * H_B, hw), lambda b: (b, 0, 0)),
            pl.BlockSpec((t_len, KB_W), lambda b: (b, 0)),
            pl.BlockSpec((t_len, VB_W), lambda b: (b, 0)),
            pl.BlockSpec((1, hw), lambda b: (0, 0)),
            pl.BlockSpec((4, D_B), lambda b: (0, 0)),
        ],
        out_specs=pl.BlockSpec((t_len, VB_W), lambda b: (b, 0)),
        out_shape=jax.ShapeDtypeStruct((batch * t_len, VB_W), BF16),
        compiler_params=_cparams(("parallel",)),
        name="attn_sample",
    )(slab, cache_kt, cache_v2, kb, vb, g_norm, dl)


def _finish_kernel(x_ref, oa_ref, ra_ref, ob_ref, gt_ref, wpa_ref, wpb_ref, wo_ref, g_ref, b_ref,
                   wrh_ref, wrl_ref, br_ref, h_ref, hp_ref, idx_ref, tw_ref, cnt_ref, *, parts):
    i = pl.program_id(0)

    @pl.when(i == 0)
    def _():
        cnt_ref[...] = jnp.zeros(cnt_ref.shape, F32)

    pm = x_ref.shape[0] // parts
    for part in range(parts):
        r = slice(part * pm, (part + 1) * pm)
        ra = ra_ref[r, :].astype(F32)
        ua = ra * _sigmoid(ra) * oa_ref[r, :].astype(F32)
        ya = jnp.dot(ua.astype(BF16), wpa_ref[...], preferred_element_type=F32)
        yb = jnp.dot(ob_ref[r, :], wpb_ref[...], preferred_element_type=F32)
        gt = gt_ref[r, :].astype(F32)
        mixed = _sigmoid(gt[:, :D_MODEL]) * ya + _sigmoid(gt[:, D_MODEL:]) * yb
        mix = jnp.dot(mixed.astype(BF16), wo_ref[...], preferred_element_type=F32)
        hh = _layer_norm(DEEPNORM_ALPHA * x_ref[r, :] + mix, g_ref[...], b_ref[...])
        h_ref[r, :] = hh
        _store_rows(hp_ref.at[pl.ds(part * pm * ROW_TILES, pm * ROW_TILES)], _pack_bf16_pairs(hh))

        h_hi = hh.astype(BF16)
        h_lo = (hh - h_hi.astype(F32)).astype(BF16)
        lg = (jnp.dot(h_hi, wrh_ref[...], preferred_element_type=F32)
              + jnp.dot(h_lo, wrh_ref[...], preferred_element_type=F32)
              + jnp.dot(h_hi, wrl_ref[...], preferred_element_type=F32)) + br_ref[...]
        lane = lax.broadcasted_iota(I32, (pm, LANES), 1)
        lane_f = lane.astype(F32)
        work = lg
        vals, idxs = [], []
        cnt = jnp.zeros((pm, LANES), F32)
        for _ in range(TOP_K):
            mx = jnp.max(work, axis=-1, keepdims=True)
            ix = jnp.min(jnp.where(work == mx, lane_f, float(LANES)), axis=-1, keepdims=True)
            hit = lane_f == ix
            cnt = cnt + hit.astype(F32)
            work = jnp.where(hit, NEG_INF, work)
            vals.append(mx)
            idxs.append(ix)
        es = [jnp.exp(v - vals[0]) for v in vals]
        den = es[0] + es[1] + es[2] + es[3]
        idx_full = jnp.zeros((pm, LANES), F32)
        tw_full = jnp.zeros((pm, LANES), F32)
        for k in range(TOP_K):
            idx_full = jnp.where(lane == k, idxs[k], idx_full)
            tw_full = jnp.where(lane == k, es[k] / den, tw_full)
        idx_ref[:, r] = idx_full.T[:SUBLANES].astype(I32)
        tw_ref[r, :] = tw_full[:, :TOP_K]
        cnt_ref[...] += jnp.sum(cnt, axis=0, keepdims=True)


def _finish(x2, oa, slab, ob, p):
    n = x2.shape[0]
    tm = min(512, n)
    const = lambda i: (0, 0)
    return pl.pallas_call(
        functools.partial(_finish_kernel, parts=1),
        grid=(n // tm,),
        in_specs=[
            pl.BlockSpec((tm, D_MODEL), lambda i: (i, 0)),
            pl.BlockSpec((tm, VA_W), lambda i: (i, 0)),
            pl.BlockSpec((tm, RA_W), lambda i: (i, SLAB_RA // RA_W)),
            pl.BlockSpec((tm, VB_W), lambda i: (i, 0)),
            pl.BlockSpec((tm, GT_W), lambda i: (i, SLAB_GT // GT_W)),
            pl.BlockSpec((VA_W, D_MODEL), const),
            pl.BlockSpec((VB_W, D_MODEL), const),
            pl.BlockSpec((D_MODEL, D_MODEL), const),
            pl.BlockSpec((1, D_MODEL), const),
            pl.BlockSpec((1, D_MODEL), const),
            pl.BlockSpec((D_MODEL, LANES), const),
            pl.BlockSpec((D_MODEL, LANES), const),
            pl.BlockSpec((1, LANES), const),
        ],
        out_specs=[
            pl.BlockSpec((tm, D_MODEL), lambda i: (i, 0)),
            pl.BlockSpec((tm * ROW_TILES, LANES), lambda i: (i, 0)),
            pl.BlockSpec((SUBLANES, tm), lambda i: (0, i)),
            pl.BlockSpec((tm, TOP_K), lambda i: (i, 0)),
            pl.BlockSpec((1, LANES), const),
        ],
        out_shape=[
            jax.ShapeDtypeStruct((n, D_MODEL), F32),
            jax.ShapeDtypeStruct((n * ROW_TILES, LANES), U32),
            jax.ShapeDtypeStruct((SUBLANES, n), I32),
            jax.ShapeDtypeStruct((n, TOP_K), F32),
            jax.ShapeDtypeStruct((1, LANES), F32),
        ],
        compiler_params=_cparams(("arbitrary",)),
        name="finish",
    )(x2, oa, slab, ob, slab, p["w_pa"], p["w_pb"], p["w_o"], p["ln1_g"], p["ln1_b"],
      p["w_r_hi"], p["w_r_lo"], p["b_r"])


def _dest_kernel(idx_ref, ps_ref, ut_ref, dest_ref, carry_ref):
    i = pl.program_id(0)

    @pl.when(i == 0)
    def _():
        carry_ref[...] = jnp.zeros(carry_ref.shape, F32)

    tm = idx_ref.shape[1]
    expert = lax.broadcasted_iota(I32, (N_EXPERTS, tm), 0)
    idx = idx_ref[...]
    hits = [expert == idx[k:k + 1, :] for k in range(TOP_K)]
    cnt = jnp.zeros((N_EXPERTS, tm), F32)
    for hit in hits:
        cnt = cnt + hit.astype(F32)
    before = jnp.dot(cnt.astype(BF16), ut_ref[...], preferred_element_type=F32)
    base = before + carry_ref[:, 0:1] + ps_ref[:, 0:1]
    row = lax.broadcasted_iota(I32, (SUBLANES, tm), 0)
    dest = jnp.zeros((SUBLANES, tm), F32)
    for k in range(TOP_K):
        d = jnp.sum(jnp.where(hits[k], base, 0.0), axis=0, keepdims=True)
        dest = jnp.where(row == k, d, dest)
    dest_ref[...] = dest.astype(I32)
    carry_ref[...] += jnp.sum(cnt, axis=1, keepdims=True)


def _dest(idx_t, pstart):
    n = idx_t.shape[1]
    tm = min(512, n)
    utri = jnp.asarray(np.triu(np.ones((tm, tm), np.float32), 1), BF16)
    return pl.pallas_call(
        _dest_kernel,
        grid=(n // tm,),
        in_specs=[
            pl.BlockSpec((SUBLANES, tm), lambda i: (0, i)),
            pl.BlockSpec((N_EXPERTS, LANES), lambda i: (0, 0)),
            pl.BlockSpec((tm, tm), lambda i: (0, 0)),
        ],
        out_specs=pl.BlockSpec((SUBLANES, tm), lambda i: (0, i)),
        out_shape=jax.ShapeDtypeStruct((SUBLANES, n), I32),
        scratch_shapes=[pltpu.VMEM((N_EXPERTS, LANES), F32)],
        compiler_params=_cparams(("arbitrary",)),
        name="dest",
    )(idx_t, pstart, utri)


def _scatter_kernel(dest_ref, pad0_ref, padn_ref, hp_ref, rows_ref, zero_ref, sem, pad_sem):
    tm = hp_ref.shape[0] // ROW_TILES

    def row_copy(t, d):
        return pltpu.make_async_copy(hp_ref.at[pl.ds(pl.multiple_of(t * ROW_TILES, ROW_TILES), ROW_TILES)],
                                     rows_ref.at[pl.ds(pl.multiple_of(d * ROW_TILES, ROW_TILES), ROW_TILES)], sem)

    def issue(t, carry):
        for k in range(TOP_K):
            row_copy(t, dest_ref[k * tm + t]).start(priority=k % 2)
        return carry

    lax.fori_loop(0, tm, issue, 0, unroll=4)

    @pl.when(pl.program_id(0) == pl.num_programs(0) - 1)
    def _():
        zero_ref[...] = jnp.zeros(zero_ref.shape, U32)

        def pad_copy(d, p):
            return pltpu.make_async_copy(
                zero_ref.at[pl.ds(0, p * ROW_TILES)],
                rows_ref.at[pl.ds(pl.multiple_of(d * ROW_TILES, ROW_TILES), p * ROW_TILES)], pad_sem)

        def for_each_pad_run(fn):
            def per_expert(e, carry):
                d, n = pad0_ref[e], padn_ref[e]
                p = EXPERT_BM // 2
                while p:
                    @pl.when((n & p) != 0)
                    def _(d=d, p=p):
                        fn(d, p)
                    d = d + (n & p)
                    p //= 2
                return carry
            lax.fori_loop(0, N_EXPERTS, per_expert, 0)

        for_each_pad_run(lambda d, p: pad_copy(d, p).start())
        for_each_pad_run(lambda d, p: pad_copy(d, p).wait())

    for k in range(TOP_K):
        pltpu.make_async_copy(hp_ref, rows_ref.at[pl.ds(0, tm * ROW_TILES)], sem).wait()


def _scatter(dest_t, pad_start, pad_len, hp, n_rows):
    n = hp.shape[0] // ROW_TILES
    tm = min(512, n)
    smem = pl.BlockSpec(memory_space=pltpu.SMEM)
    return pl.pallas_call(
        _scatter_kernel,
        grid=(n // tm,),
        in_specs=[
            pl.BlockSpec((TOP_K * tm,), lambda i: (i,), memory_space=pltpu.SMEM),
            smem, smem,
            pl.BlockSpec((tm * ROW_TILES, LANES), lambda i: (i, 0)),
        ],
        out_specs=pl.BlockSpec(memory_space=pl.ANY),
        out_shape=jax.ShapeDtypeStruct((n_rows * ROW_TILES, LANES), U32),
        scratch_shapes=[pltpu.VMEM((EXPERT_BM // 2 * ROW_TILES, LANES), U32), pltpu.SemaphoreType.DMA(()),
                        pltpu.SemaphoreType.DMA(())],
        compiler_params=_cparams(("arbitrary",)),
        name="scatter",
    )(dest_t, pad_start, pad_len, hp)


def _expert_kernel(be_ref, nu_ref, rows_ref, wgu_ref, bgu_ref, wd_ref, bd_ref, out_ref, wgu_bf_ref, wd_bf_ref):
    i = pl.program_id(0)

    @pl.when((i == 0) | (be_ref[i] != be_ref[jnp.maximum(i - 1, 0)]))
    def _():
        wgu_bf_ref[...] = wgu_ref[0].astype(BF16)
        wd_bf_ref[...] = wd_ref[0].astype(BF16)

    @pl.when(i < nu_ref[0])
    def _():
        x = _unpack_bf16_pairs(_load_rows(rows_ref)).astype(BF16)
        gu = jnp.dot(x, wgu_bf_ref[...], preferred_element_type=F32) + bgu_ref[0]
        gate = jnp.minimum(gu[:, :D_FF], SWIGLU_LIMIT)
        up = jnp.clip(gu[:, D_FF:], -SWIGLU_LIMIT, SWIGLU_LIMIT)
        hidden = (up + 1.0) * (gate * _sigmoid(SWIGLU_ALPHA * gate))
        o = jnp.dot(hidden.astype(BF16), wd_bf_ref[...], preferred_element_type=F32) + bd_ref[0]
        _store_rows(out_ref, _pack_bf16_pairs(o))

    @pl.when(i >= nu_ref[0])
    def _():
        out_ref[...] = jnp.zeros(out_ref.shape, U32)


def _experts(rows, block_e, n_used, p):
    n_rows = rows.shape[0] // ROW_TILES
    bm = EXPERT_BM
    n_blocks = n_rows // bm
    grid_spec = pltpu.PrefetchScalarGridSpec(
        num_scalar_prefetch=2,
        grid=(n_blocks,),
        in_specs=[
            pl.BlockSpec((bm * ROW_TILES, LANES), lambda i, be, nu: (jnp.minimum(i, nu[0] - 1), 0)),
            pl.BlockSpec((1, D_MODEL, 2 * D_FF), lambda i, be, nu: (be[i], 0, 0)),
            pl.BlockSpec((1, 1, 2 * D_FF), lambda i, be, nu: (be[i], 0, 0)),
            pl.BlockSpec((1, D_FF, D_MODEL), lambda i, be, nu: (be[i], 0, 0)),
            pl.BlockSpec((1, 1, D_MODEL), lambda i, be, nu: (be[i], 0, 0)),
        ],
        out_specs=pl.BlockSpec((bm * ROW_TILES, LANES), lambda i, be, nu: (i, 0)),
        scratch_shapes=[pltpu.VMEM((D_MODEL, 2 * D_FF), BF16), pltpu.VMEM((D_FF, D_MODEL), BF16)],
    )
    return pl.pallas_call(
        _expert_kernel,
        grid_spec=grid_spec,
        out_shape=jax.ShapeDtypeStruct((n_rows * ROW_TILES, LANES), U32),
        compiler_params=_cparams(("arbitrary",)),
        name="experts",
    )(block_e, n_used, rows, p["w_gu"], p["b_gu"], p["w_down"], p["b_down"])


def _combine_kernel(dest_ref, tw_ref, h_ref, rows_ref, g_ref, b_ref, y_ref, buf_ref, sem):
    tm = h_ref.shape[0]

    def row_copy(t, k, d):
        return pltpu.make_async_copy(rows_ref.at[pl.ds(pl.multiple_of(d * ROW_TILES, ROW_TILES), ROW_TILES)],
                                     buf_ref.at[k, pl.ds(pl.multiple_of(t * ROW_TILES, ROW_TILES), ROW_TILES)], sem)

    def issue(t, carry):
        for k in range(TOP_K):
            row_copy(t, k, dest_ref[k * tm + t]).start(priority=k % 2)
        return carry

    lax.fori_loop(0, tm, issue, 0, unroll=4)
    for k in range(TOP_K):
        pltpu.make_async_copy(rows_ref.at[pl.ds(0, tm * ROW_TILES)], buf_ref.at[k], sem).wait()
    tw = tw_ref[...]
    f = jnp.zeros((tm, D_MODEL), F32)
    for k in range(TOP_K):
        f = f + tw[:, k:k + 1] * _unpack_bf16_pairs(_load_rows(buf_ref.at[k]))
    y_ref[...] = _layer_norm(DEEPNORM_ALPHA * h_ref[...] + f, g_ref[...], b_ref[...])


def _combine(dest_t, top_w, hh, out_rows, p):
    n = hh.shape[0]
    tm = min(512, n)
    const = lambda i: (0, 0)
    return pl.pallas_call(
        _combine_kernel,
        grid=(n // tm,),
        in_specs=[
            pl.BlockSpec((TOP_K * tm,), lambda i: (i,), memory_space=pltpu.SMEM),
            pl.BlockSpec((tm, TOP_K), lambda i: (i, 0)),
            pl.BlockSpec((tm, D_MODEL), lambda i: (i, 0)),
            pl.BlockSpec(memory_space=pl.ANY),
            pl.BlockSpec((1, D_MODEL), const),
            pl.BlockSpec((1, D_MODEL), const),
        ],
        out_specs=pl.BlockSpec((tm, D_MODEL), lambda i: (i, 0)),
        out_shape=jax.ShapeDtypeStruct((n, D_MODEL), F32),
        scratch_shapes=[pltpu.VMEM((TOP_K, tm * ROW_TILES, LANES), U32), pltpu.SemaphoreType.DMA(())],
        compiler_params=_cparams(("arbitrary",)),
        name="combine",
    )(dest_t, top_w, hh, out_rows, p["ln2_g"], p["ln2_b"])


def _moe(hh, hp, top_idx, top_w, counts, p):
    n = hh.shape[0]
    bm = EXPERT_BM
    n_rows = n * TOP_K + N_EXPERTS * bm
    cnt = counts[0, :N_EXPERTS].astype(I32)
    padded = (cnt + bm - 1) // bm * bm
    pend = jnp.cumsum(padded)
    pstart = pend - padded
    ps_col = jnp.broadcast_to(pstart.astype(F32)[:, None], (N_EXPERTS, LANES))
    block_row0 = jnp.arange(n_rows // bm, dtype=I32) * bm
    block_e = jnp.minimum(jnp.sum((pend[None, :] <= block_row0[:, None]).astype(I32), axis=1), N_EXPERTS - 1)
    n_used = (pend[-1:] // bm).astype(I32)
    tm = min(512, n)
    dest = _dest(top_idx, ps_col)[:TOP_K].reshape(TOP_K, n // tm, tm).transpose(1, 0, 2).reshape(-1)
    rows = _scatter(dest, pstart + cnt, padded - cnt, hp, n_rows)
    out_rows = _experts(rows, block_e, n_used, p)
    return _combine(dest, top_w, hh, out_rows, p)


def _prep_params(w_in, b_in, w_alpha, b_alpha, gla_norm_g, diff_norm_g, w_pa, w_pb, w_o, ln1_g, ln1_b,
                 w_router, b_router, w_gu, b_gu, w_down, b_down, ln2_g, ln2_b):
    o = IN_OFFS
    seg = lambda a, i: a[..., o[i]:o[i + 1]]
    order = (0, 1, 2, 3, 5, 8)
    w_slab = jnp.concatenate([seg(w_in, i) for i in order], axis=1).astype(BF16)
    b_slab = jnp.concatenate([seg(b_in, i) for i in order])[None, :]
    scale = np.ones((1, SLAB_W), np.float32)
    scale[0, SLAB_QA:SLAB_QA + QA_W] = DK_A ** -0.5
    scale[0, SLAB_QB:SLAB_QB + QB_W] = D_B ** -0.5 * LOG2E
    w_lr = jnp.zeros((D_MODEL, LANES), F32).at[:, :LR_W].set(seg(w_in, 4)).astype(BF16)
    b_lr = jnp.zeros((1, LANES), F32).at[0, :LR_W].set(seg(b_in, 4))
    w_al = jnp.zeros((LANES, QA_W), F32).at[:LR_W].set(w_alpha).astype(BF16)
    w_r = jnp.zeros((D_MODEL, LANES), F32).at[:, :N_EXPERTS].set(w_router)
    w_r_hi = w_r.astype(BF16)
    w_r_lo = (w_r - w_r_hi.astype(F32)).astype(BF16)
    b_r = jnp.full((1, LANES), NEG_INF, F32).at[0, :N_EXPERTS].set(b_router)
    return dict(
        w_slab=w_slab, b_slab=b_slab, s_slab=jnp.asarray(scale),
        w_lr=w_lr, b_lr=b_lr, w_alpha=w_al, b_alpha=b_alpha[None, :],
        w_k=seg(w_in, 6).astype(BF16), w_v=seg(w_in, 7).astype(BF16),
        b_k=seg(b_in, 6)[None, :], b_v=seg(b_in, 7)[None, :],
        gla_g=gla_norm_g[None, :], diff_g=diff_norm_g[None, :],
        w_pa=w_pa.astype(BF16), w_pb=w_pb.astype(BF16), w_o=w_o.astype(BF16),
        ln1_g=ln1_g[None, :], ln1_b=ln1_b[None, :],
        w_r_hi=w_r_hi, w_r_lo=w_r_lo, b_r=b_r,
        w_gu=w_gu, b_gu=b_gu[:, None, :], w_down=w_down, b_down=b_down[:, None, :],
        ln2_g=ln2_g[None, :], ln2_b=ln2_b[None, :],
    )


def _layer(x, p, dl, lam_init, s0, cache_k, cache_v):
    batch, t_len, _ = x.shape
    x2 = x.reshape(batch * t_len, D_MODEL)
    slab, loga, k_rows, kb, vb = _proj(x2, p)
    oa, s_new = _gla(slab, loga, s0, p["gla_g"], batch, t_len)
    if cache_k is None:
        ob = _attn_prompt(slab, kb, vb, p["diff_g"], dl, batch, t_len, lam_init)
    else:
        ob = _attn_sample(slab, kb, vb, cache_k, cache_v, p["diff_g"], dl, batch, t_len, lam_init)
    hh, hp, top_idx, top_w, counts = _finish(x2, oa, slab, ob, p)
    y = _moe(hh, hp, top_idx, top_w, counts, p)
    return (y.reshape(batch, t_len, D_MODEL), s_new,
            k_rows.reshape(batch, t_len, H_B, 2, D_B), vb.reshape(batch, t_len, H_B, 2 * D_B))


def kernel(x_prompt, x_sample, cache_k, cache_v, state_gla, w_in, b_in, w_alpha, b_alpha, gla_norm_g,
           diff_lambda, diff_norm_g, w_pa, w_pb, w_o, ln1_g, ln1_b, w_router, b_router, w_gu, b_gu,
           w_down, b_down, ln2_g, ln2_b):
    assert w_in.shape[0] == DEPTH == 1
    l = 0
    lam_init = 0.8 - 0.6 * math.exp(-0.3 * l)
    p = _prep_params(w_in[l], b_in[l], w_alpha[l], b_alpha[l], gla_norm_g[l], diff_norm_g[l], w_pa[l],
                     w_pb[l], w_o[l], ln1_g[l], ln1_b[l], w_router[l], b_router[l], w_gu[l], b_gu[l],
                     w_down[l], b_down[l], ln2_g[l], ln2_b[l])
    dl = diff_lambda[l]
    bp = x_prompt.shape[0]
    bs, ts = x_sample.shape[0], x_sample.shape[1]
    past = cache_k.shape[2]
    yp, s_p, k_p, v_p = _layer(x_prompt, p, dl, lam_init, jnp.zeros((bp, H_A, DK_A, DV_A), F32), None, None)
    ck = jnp.transpose(cache_k[l], (0, 2, 3, 4, 1)).reshape(bs, H_B, 2 * D_B, past)
    cv = cache_v[l].reshape(bs, past * H_B, 2 * D_B)
    ys, s_s, k_s, v_s = _layer(x_sample, p, dl, lam_init, state_gla[l], ck, cv)
    return (yp, ys, s_p[None], k_p[None], v_p[None], s_s[None], k_s[None], v_s[None])
```

```python
import functools
import math

import numpy as np
import jax
import jax.numpy as jnp
from jax import lax
from jax.experimental import pallas as pl
from jax.experimental.pallas import tpu as pltpu

F32 = jnp.float32
BF16 = jnp.bfloat16
U32 = jnp.uint32
I32 = jnp.int32

D_MODEL = 1024
CHUNK = 64
H_A = 4
DK_A = 128
DV_A = 256
GATE_RANK = 16
GATE_TAU = 16.0
H_B = 8
D_B = 64
N_EXPERTS = 32
TOP_K = 4
D_FF = D_MODEL
SWIGLU_LIMIT = 7.0
SWIGLU_ALPHA = 1.702
EPS = 1e-5
DEPTH = 1
DEEPNORM_ALPHA = (2.0 * DEPTH) ** 0.25

QA_W = H_A * DK_A
KA_W = H_A * DK_A
VA_W = H_A * DV_A
RA_W = H_A * DV_A
LR_W = GATE_RANK
QB_W = H_B * 2 * D_B
KB_W = H_B * 2 * D_B
VB_W = H_B * 2 * D_B
GT_W = 2 * D_MODEL
IN_SIZES = (QA_W, KA_W, VA_W, RA_W, LR_W, QB_W, KB_W, VB_W, GT_W)
IN_OFFS = tuple(int(v) for v in np.cumsum((0,) + IN_SIZES))

SLAB_QA, SLAB_KA, SLAB_VA, SLAB_RA, SLAB_QB, SLAB_GT = 0, 512, 1024, 2048, 3072, 4096
SLAB_W = 6144
LANES = 128
SUBLANES = 8
EXPERT_BM = 512
VMEM_LIMIT = 56 * 1024 * 1024
PROJ_VMEM_LIMIT = 61 * 1024 * 1024

NEG_INF = float("-inf")
LOG2E = math.log2(math.e)
CHUNK_BITS = CHUNK.bit_length() - 1
POS_BITS = 8
POS_BASE = 1 << POS_BITS


def _cparams(sem, vmem_limit=VMEM_LIMIT):
    return pltpu.CompilerParams(dimension_semantics=sem, vmem_limit_bytes=vmem_limit)


def _sigmoid(x):
    return 1.0 / (1.0 + jnp.exp(-x))


def _pack_bf16_pairs(x):
    n = x.shape[1] // 2
    xb = x.astype(BF16).astype(F32)
    lo = pltpu.bitcast(xb[:, :n], U32) >> 16
    hi = pltpu.bitcast(xb[:, n:], U32) & jnp.uint32(0xFFFF0000)
    return hi | lo


def _unpack_bf16_pairs(r):
    lo = pltpu.bitcast(r << 16, F32)
    hi = pltpu.bitcast(r & jnp.uint32(0xFFFF0000), F32)
    return jnp.concatenate([lo, hi], axis=1)


ROW_TILES = D_MODEL // 2 // LANES


def _load_rows(ref):
    m = ref.shape[0] // ROW_TILES
    return jnp.concatenate([ref[pl.ds(c, m, stride=ROW_TILES), :] for c in range(ROW_TILES)], axis=1)


def _store_rows(ref, val):
    m = ref.shape[0] // ROW_TILES
    for c in range(ROW_TILES):
        ref[pl.ds(c, m, stride=ROW_TILES), :] = val[:, c * LANES:(c + 1) * LANES]


def _layer_norm(y, g, b):
    mu = jnp.mean(y, axis=-1, keepdims=True)
    d = y - mu
    var = jnp.mean(d * d, axis=-1, keepdims=True)
    return d * lax.rsqrt(var + EPS) * g + b


def _proj_kernel(x_ref, w_ref, b_ref, s_ref, wlr_ref, blr_ref, wal_ref, bal_ref, wk_ref, wv_ref, bk_ref, bv_ref,
                 slab_ref, loga_ref, krows_ref, kb_ref, v_ref, *, tn):
    xb = x_ref[...].astype(BF16)
    k = jnp.dot(xb, wk_ref[...], preferred_element_type=F32) + bk_ref[...]
    kb_ref[...] = k.astype(BF16)
    for h in range(H_B):
        for j in range(2):
            c0 = (2 * h + j) * D_B
            krows_ref[:, h, j, :] = k[:, c0:c0 + D_B]
    v_ref[...] = jnp.dot(xb, wv_ref[...], preferred_element_type=F32) + bv_ref[...]
    lra = jnp.dot(xb, wlr_ref[...], preferred_element_type=F32) + blr_ref[...]
    z = jnp.dot(lra.astype(BF16), wal_ref[...], preferred_element_type=F32) + bal_ref[...]
    loga_ref[...] = (jnp.minimum(z, 0.0) - jnp.log(1.0 + jnp.exp(-jnp.abs(z)))) * (1.0 / GATE_TAU)
    for c0 in range(0, SLAB_W, tn):
        acc = jnp.dot(xb, w_ref[:, c0:c0 + tn], preferred_element_type=F32)
        slab_ref[:, c0:c0 + tn] = ((acc + b_ref[:, c0:c0 + tn]) * s_ref[:, c0:c0 + tn]).astype(BF16)


def _proj(x2, p):
    n = x2.shape[0]
    tm = min(512, n)
    const = lambda i: (0, 0)
    whole = lambda shape: pl.BlockSpec(shape, const, pipeline_mode=pl.Buffered(1))
    row = lambda w: pl.BlockSpec((tm, w), lambda i: (i, 0))
    return pl.pallas_call(
        functools.partial(_proj_kernel, tn=1024),
        grid=(n // tm,),
        in_specs=[
            row(D_MODEL),
            whole((D_MODEL, SLAB_W)), whole((1, SLAB_W)), whole((1, SLAB_W)),
            whole((D_MODEL, LANES)), whole((1, LANES)), whole((LANES, QA_W)), whole((1, QA_W)),
            whole((D_MODEL, KB_W)), whole((D_MODEL, VB_W)), whole((1, KB_W)), whole((1, VB_W)),
        ],
        out_specs=[
            row(SLAB_W), row(QA_W),
            pl.BlockSpec((tm, H_B, 2, D_B), lambda i: (i, 0, 0, 0)),
            row(KB_W), row(VB_W),
        ],
        out_shape=[
            jax.ShapeDtypeStruct((n, SLAB_W), BF16),
            jax.ShapeDtypeStruct((n, QA_W), F32),
            jax.ShapeDtypeStruct((n, H_B, 2, D_B), F32),
            jax.ShapeDtypeStruct((n, KB_W), BF16),
            jax.ShapeDtypeStruct((n, VB_W), F32),
        ],
        compiler_params=_cparams(("parallel",), PROJ_VMEM_LIMIT),
        name="proj",
    )(x2, p["w_slab"], p["b_slab"], p["s_slab"], p["w_lr"], p["b_lr"], p["w_alpha"], p["b_alpha"],
      p["w_k"], p["w_v"], p["b_k"], p["b_v"])


def _gla_tables(L):
    nl = int(math.log2(L))
    t = np.arange(L)
    D = np.zeros(((nl + 2) * L, L), np.float32)
    masks = np.zeros((nl + 1, L, L), np.float32)
    for l in range(nl):
        m = L >> (l + 1)
        grp = t // (2 * m)
        mid = grp * 2 * m + m - 1
        upper = (t % (2 * m)) >= m
        for r in range(L):
            if upper[r]:
                D[l * L + r, mid[r] + 1:r + 1] = 1.0
            else:
                D[l * L + r, r + 1:mid[r] + 1] = 1.0
        masks[l] = (upper[:, None] & ~upper[None, :] & (grp[:, None] == grp[None, :])).astype(np.float32)
    D[nl * L:(nl + 1) * L] = np.tril(np.ones((L, L), np.float32))
    D[(nl + 1) * L:] = np.triu(np.ones((L, L), np.float32), 1)
    masks[nl] = np.eye(L, dtype=np.float32)
    return jnp.asarray(D, BF16), jnp.asarray(masks, F32)


def _gla_kernel(q_ref, k_ref, v_ref, la_ref, s0_ref, g_ref, d_ref, m_ref, o_ref, s_ref, *, L, n_chunks):
    c = pl.program_id(1)
    nl = int(math.log2(L))

    @pl.when(c == 0)
    def _():
        s_ref[...] = s0_ref[...]

    dmat = d_ref[...]
    ones_col = jnp.ones((L, LANES), BF16)
    g = g_ref[...]
    nt = (((1,), (1,)), ((), ()))
    tn = (((0,), (0,)), ((), ()))

    def chunk(ci, carry):
        r0 = pl.multiple_of(ci * L, L)
        rows = pl.ds(r0, L)
        la_all = la_ref[rows, :]
        la_hi = la_all.astype(BF16)
        la_lo = (la_all - la_hi.astype(F32)).astype(BF16)
        x_all = jnp.exp(jnp.dot(dmat, la_hi, preferred_element_type=F32)
                        + jnp.dot(dmat, la_lo, preferred_element_type=F32))
        bl_all = (lax.dot_general(la_hi, ones_col, tn, preferred_element_type=F32)
                  + lax.dot_general(la_lo, ones_col, tn, preferred_element_type=F32))
        dec_all = jnp.exp(bl_all[:, 0:1])
        for h in range(H_A):
            kc = slice(h * DK_A, (h + 1) * DK_A)
            vc = slice(h * DV_A, (h + 1) * DV_A)
            q_bf = q_ref[rows, kc]
            k_bf = k_ref[rows, kc]
            q = q_bf.astype(F32)
            k = k_bf.astype(F32)
            v = v_ref[rows, vc]
            x = x_all[:, kc]
            a = m_ref[nl] * lax.dot_general(q_bf, k_bf, nt, preferred_element_type=F32)
            for l in range(nl):
                xl = x[l * L:(l + 1) * L]
                a = a + m_ref[l] * lax.dot_general((q * xl).astype(BF16), (k * xl).astype(BF16), nt,
                                                   preferred_element_type=F32)
            xb = x[nl * L:(nl + 1) * L]
            xs = x[(nl + 1) * L:]
            s_old = s_ref[0, h]
            o = (jnp.dot((q * xb).astype(BF16), s_old.astype(BF16), preferred_element_type=F32)
                 + jnp.dot(a.astype(BF16), v, preferred_element_type=F32))
            s_ref[0, h] = dec_all[kc] * s_old + lax.dot_general((k * xs).astype(BF16), v, tn, preferred_element_type=F32)
            ms = jnp.mean(o * o, axis=-1, keepdims=True)
            o_ref[rows, vc] = (o * lax.rsqrt(ms + EPS) * g).astype(BF16)
        return carry

    lax.fori_loop(0, n_chunks, chunk, 0, unroll=True)


def _gla(slab, loga, s0, g_norm, batch, t_len):
    L = min(CHUNK, t_len)
    tb = min(512, t_len)
    nb = t_len // tb
    dmat, masks = _gla_tables(L)
    kern = functools.partial(_gla_kernel, L=L, n_chunks=tb // L)
    row = lambda b, c: b * nb + c
    return pl.pallas_call(
        kern,
        grid=(batch, nb),
        in_specs=[
            pl.BlockSpec((tb, QA_W), lambda b, c: (row(b, c), SLAB_QA // QA_W)),
            pl.BlockSpec((tb, KA_W), lambda b, c: (row(b, c), SLAB_KA // KA_W)),
            pl.BlockSpec((tb, VA_W), lambda b, c: (row(b, c), SLAB_VA // VA_W)),
            pl.BlockSpec((tb, QA_W), lambda b, c: (row(b, c), 0)),
            pl.BlockSpec((1, H_A, DK_A, DV_A), lambda b, c: (b, 0, 0, 0)),
            pl.BlockSpec((1, DV_A), lambda b, c: (0, 0)),
            pl.BlockSpec(dmat.shape, lambda b, c: (0, 0)),
            pl.BlockSpec(masks.shape, lambda b, c: (0, 0, 0)),
        ],
        out_specs=[
            pl.BlockSpec((tb, VA_W), lambda b, c: (row(b, c), 0)),
            pl.BlockSpec((1, H_A, DK_A, DV_A), lambda b, c: (b, 0, 0, 0)),
        ],
        out_shape=[
            jax.ShapeDtypeStruct((batch * t_len, VA_W), BF16),
            jax.ShapeDtypeStruct((batch, H_A, DK_A, DV_A), F32),
        ],
        compiler_params=_cparams(("parallel", "arbitrary")),
        name="gla",
    )(slab, slab, slab, loga, s0, g_norm, dmat, masks)


def _lambda_from(dl_ref, lam_init):
    dl = dl_ref[...]
    a = jnp.sum(dl[0:1] * dl[1:2], axis=-1, keepdims=True)
    b = jnp.sum(dl[2:3] * dl[3:4], axis=-1, keepdims=True)
    return jnp.exp(a) - jnp.exp(b) + lam_init


def _split_maps(q):
    lane = lax.broadcasted_iota(I32, q.shape, 1)
    zero = jnp.zeros_like(q)
    return jnp.concatenate([jnp.where(lane < D_B, q, zero), jnp.where(lane >= D_B, q, zero)], axis=0)


def _alibi_coef(h):
    c = jnp.full((1, 1), LOG2E, F32) * jnp.exp2(-(h + 1).astype(F32))
    c_hi = c.astype(BF16).astype(F32)
    return c, c_hi, c - c_hi


def _finish_heads(acc, inv_l, lam, g, tq, lam_init):
    o = acc * inv_l
    out = o[:tq] - lam * o[tq:]
    ms = jnp.mean(out * out, axis=-1, keepdims=True)
    return (out * lax.rsqrt(ms + EPS) * g * (1.0 - lam_init)).astype(BF16)


def _attn_prompt_kernel(q_ref, k_ref, v_ref, g_ref, dl_ref, o_ref, ka_ref, vb_ref, qaug_ref, d_ref, m_ref,
                        acc_ref, s0_ref, s1_ref, p0_ref, p1_ref,
                        *, tq, tk, lam_init):
    h = pl.program_id(1)
    qi = pl.program_id(2)
    c, c_hi, c_lo = _alibi_coef(h)
    hw = 2 * D_B

    def aug_lanes(shape, pos, sign):
        lane = lax.broadcasted_iota(I32, shape, 1)
        r = (pos & (POS_BASE - 1)).astype(F32) * sign
        a = (pos >> POS_BITS).astype(F32) * sign
        coef = jnp.where(lane == 0, c_hi, jnp.where(lane == 1, c_lo,
                         jnp.where(lane == 2, POS_BASE * c_hi, jnp.where(lane == 3, POS_BASE * c_lo, 0.0))))
        ints = jnp.where(lane < 2, r, jnp.where(lane < 4, a, 0.0))
        return coef, ints, lane

    @pl.when(qi == 0)
    def _():
        t_len = vb_ref.shape[0]
        lane = lax.broadcasted_iota(I32, (t_len, hw), 1)
        row = lax.broadcasted_iota(I32, (hw, tk), 0)
        j_rel = lax.broadcasted_iota(I32, (hw, tk), 1)
        k_aug_t = jnp.where(row < 2, (j_rel & (POS_BASE - 1)).astype(F32),
                  jnp.where(row < 4, (j_rel >> POS_BITS).astype(F32),
                  jnp.where(row == 4, c_hi, jnp.where(row == 5, c_lo,
                  jnp.where(row == 6, POS_BASE * c_hi, jnp.where(row == 7, POS_BASE * c_lo, 0.0)))))).astype(BF16)
        for blk in range(t_len // tk):
            rows = slice(blk * tk, (blk + 1) * tk)
            ka_ref[blk, :hw, :] = k_ref[0, rows, :].astype(F32).T.astype(BF16)
            ka_ref[blk, hw:, :] = k_aug_t
        vb_ref[:, :hw] = v_ref[0].astype(BF16)
        vb_ref[:, hw:] = jnp.where(lane == 0, 1.0, 0.0).astype(BF16)
        i_rel = lax.broadcasted_iota(I32, (2 * tq, hw), 0) & (tq - 1)
        coef, ints, lane = aug_lanes((2 * tq, hw), i_rel, -1.0)
        qaug_ref[...] = jnp.where(lane < 4, coef, pltpu.roll(ints, 4, 1)).astype(BF16)
        i = lax.broadcasted_iota(I32, (2 * tq, tq), 0) & (tq - 1)
        j = lax.broadcasted_iota(I32, (2 * tq, tq), 1)
        fwd = jnp.maximum(j - i, 0).astype(F32)
        d_ref[...] = jnp.where((j >> CHUNK_BITS) <= (i >> CHUNK_BITS), -2.0 * c * fwd, NEG_INF)

    qa = jnp.concatenate([_split_maps(q_ref[...]), qaug_ref[...]], axis=1)
    m_ref[...] = jnp.full(m_ref.shape, -1e30, F32)

    def scores(kv):
        return jnp.dot(qa, ka_ref[kv], preferred_element_type=F32)

    def weighted_values(p_ref, kv):
        k0 = pl.multiple_of(kv * tk, tk)
        return jnp.dot(p_ref[...], vb_ref[pl.ds(k0, tk), :], preferred_element_type=F32)

    def softmax_step(s, off):
        m_old = m_ref[...]
        m_new = jnp.maximum(m_old, jnp.max(s, axis=-1, keepdims=True) - off)
        shift = m_new + off
        p = jnp.exp2(s - jnp.concatenate([shift] * (tk // hw), axis=1))
        alpha = jnp.exp2(m_old - m_new)
        m_ref[...] = m_new
        return p.astype(BF16), jnp.concatenate([alpha, alpha], axis=1)

    s_first = scores(0)
    p_diag, _ = softmax_step(scores(qi) + d_ref[...], jnp.zeros((1, 1), F32))
    p1_ref[...] = p_diag
    s0_ref[...] = s_first
    acc_ref[...] = jnp.zeros(acc_ref.shape, F32)

    def step(kv, s_cur, s_nxt, p_prev, p_cur):
        pv_prev = weighted_values(p_prev, jnp.where(kv == 0, qi, kv - 1))
        s_nxt[...] = scores(jnp.minimum(kv + 1, qi - 1))
        p, alpha = softmax_step(s_cur[...], c * (qi * tq - kv * tk).astype(F32))
        acc_ref[...] = alpha * (acc_ref[...] + pv_prev)
        p_cur[...] = p

    def pair(j, carry):
        step(2 * j, s0_ref, s1_ref, p1_ref, p0_ref)
        step(2 * j + 1, s1_ref, s0_ref, p0_ref, p1_ref)
        return carry

    lax.fori_loop(0, qi // 2, pair, 0)

    @pl.when(qi % 2 == 1)
    def _():
        step(qi - 1, s0_ref, s1_ref, p1_ref, p0_ref)
        acc_ref[...] += weighted_values(p0_ref, qi - 1)

    @pl.when(qi % 2 == 0)
    def _():
        acc_ref[...] += weighted_values(p1_ref, jnp.maximum(qi - 1, 0))

    lam = _lambda_from(dl_ref, lam_init)
    acc = acc_ref[...]
    o_ref[...] = _finish_heads(acc[:, :hw], 1.0 / acc[:, hw:hw + 1], lam, g_ref[...], tq, lam_init)


def _attn_prompt(slab, kb, vb, g_norm, dl, batch, t_len, lam_init):
    tq = min(512, t_len)
    tk = tq
    assert tq % CHUNK == 0 and CHUNK == 1 << CHUNK_BITS
    nq = t_len // tq
    k3 = kb.reshape(batch, t_len, KB_W)
    v3 = vb.reshape(batch, t_len, VB_W)
    kern = functools.partial(_attn_prompt_kernel, tq=tq, tk=tk, lam_init=lam_init)
    hw = 2 * D_B
    return pl.pallas_call(
        kern,
        grid=(batch, H_B, nq),
        in_specs=[
            pl.BlockSpec((tq, hw), lambda b, h, q: (b * nq + q, SLAB_QB // hw + h)),
            pl.BlockSpec((1, t_len, hw), lambda b, h, q: (b, 0, h)),
            pl.BlockSpec((1, t_len, hw), lambda b, h, q: (b, 0, h)),
            pl.BlockSpec((1, hw), lambda b, h, q: (0, 0)),
            pl.BlockSpec((4, D_B), lambda b, h, q: (0, 0)),
        ],
        out_specs=pl.BlockSpec((tq, hw), lambda b, h, q: (b * nq + q, h)),
        out_shape=jax.ShapeDtypeStruct((batch * t_len, VB_W), BF16),
        scratch_shapes=[
            pltpu.VMEM((t_len // tk, 2 * hw, tk), BF16),
            pltpu.VMEM((t_len, 2 * hw), BF16),
            pltpu.VMEM((2 * tq, hw), BF16),
            pltpu.VMEM((2 * tq, tq), F32),
            pltpu.VMEM((2 * tq, hw), F32),
            pltpu.VMEM((2 * tq, 2 * hw), F32),
            pltpu.VMEM((2 * tq, tk), F32),
            pltpu.VMEM((2 * tq, tk), F32),
            pltpu.VMEM((2 * tq, tk), BF16),
            pltpu.VMEM((2 * tq, tk), BF16),
        ],
        compiler_params=_cparams(("parallel", "parallel", "arbitrary")),
        name="attn_prompt",
    )(slab, k3, v3, g_norm, dl)


def _attn_sample_kernel(q_ref, kct_ref, vc_ref, kn_ref, vn_ref, g_ref, dl_ref, o_ref, *, tq, past, lam_init):
    hw = 2 * D_B
    q_pos = past + lax.broadcasted_iota(I32, (2 * tq, 1), 0) % tq
    lam = _lambda_from(dl_ref, lam_init)

    def biased(s, k_pos, c):
        s = s - c * jnp.abs(q_pos - k_pos).astype(F32)
        return jnp.where((k_pos // CHUNK) <= (q_pos // CHUNK), s, NEG_INF)

    for h in range(H_B):
        cols = slice(h * hw, (h + 1) * hw)
        c = LOG2E * 2.0 ** -(h + 1)
        qq = _split_maps(q_ref[:, cols])
        sc = biased(jnp.dot(qq, kct_ref[0, h].astype(BF16), preferred_element_type=F32),
                    lax.broadcasted_iota(I32, (1, past), 1), c)
        sn = biased(lax.dot_general(qq, kn_ref[:, cols], (((1,), (1,)), ((), ())), preferred_element_type=F32),
                    past + lax.broadcasted_iota(I32, (1, tq), 1), c)
        m = jnp.maximum(jnp.max(sc, axis=-1, keepdims=True), jnp.max(sn, axis=-1, keepdims=True))
        pc = jnp.exp2(sc - m)
        pn = jnp.exp2(sn - m)
        l = jnp.sum(pc, axis=-1, keepdims=True) + jnp.sum(pn, axis=-1, keepdims=True)
        vc = vc_ref[0, pl.ds(h, past, stride=H_B), :].astype(BF16)
        acc = (jnp.dot(pc.astype(BF16), vc, preferred_element_type=F32)
               + jnp.dot(pn.astype(BF16), vn_ref[:, cols].astype(BF16), preferred_element_type=F32))
        o_ref[:, cols] = _finish_heads(acc, 1.0 / l, lam, g_ref[...], tq, lam_init)


def _attn_sample(slab, kb, vb, cache_kt, cache_v2, g_norm, dl, batch, t_len, lam_init):
    past = cache_kt.shape[3]
    hw = 2 * D_B
    kern = functools.partial(_attn_sample_kernel, tq=t_len, past=past, lam_init=lam_init)
    return pl.pallas_call(
        kern,
        grid=(batch,),
        in_specs=[
            pl.BlockSpec((t_len, QB_W), lambda b: (b, SLAB_QB // QB_W)),
            pl.BlockSpec((1, H_B, hw, past), lambda b: (b, 0, 0, 0)),
            pl.BlockSpec((1, past * H_B, hw), lambda b: (b, 0, 0)),
            pl.BlockSpec((t_len, KB_W), lambda b: (b, 0)),
            pl.BlockSpec((t_len, VB_W), lambda b: (b, 0)),
            pl.BlockSpec((1, hw), lambda b: (0, 0)),
            pl.BlockSpec((4, D_B), lambda b: (0, 0)),
        ],
        out_specs=pl.BlockSpec((t_len, VB_W), lambda b: (b, 0)),
        out_shape=jax.ShapeDtypeStruct((batch * t_len, VB_W), BF16),
        compiler_params=_cparams(("parallel",)),
        name="attn_sample",
    )(slab, cache_kt, cache_v2, kb, vb, g_norm, dl)


def _finish_kernel(x_ref, oa_ref, ra_ref, ob_ref, gt_ref, wpa_ref, wpb_ref, wo_ref, g_ref, b_ref,
                   wrh_ref, wrl_ref, br_ref, h_ref, hp_ref, idx_ref, tw_ref, cnt_ref, *, parts):
    i = pl.program_id(0)

    @pl.when(i == 0)
    def _():
        cnt_ref[...] = jnp.zeros(cnt_ref.shape, F32)

    pm = x_ref.shape[0] // parts
    for part in range(parts):
        r = slice(part * pm, (part + 1) * pm)
        ra = ra_ref[r, :].astype(F32)
        ua = ra * _sigmoid(ra) * oa_ref[r, :].astype(F32)
        ya = jnp.dot(ua.astype(BF16), wpa_ref[...], preferred_element_type=F32)
        yb = jnp.dot(ob_ref[r, :], wpb_ref[...], preferred_element_type=F32)
        gt = gt_ref[r, :].astype(F32)
        mixed = _sigmoid(gt[:, :D_MODEL]) * ya + _sigmoid(gt[:, D_MODEL:]) * yb
        mix = jnp.dot(mixed.astype(BF16), wo_ref[...], preferred_element_type=F32)
        hh = _layer_norm(DEEPNORM_ALPHA * x_ref[r, :] + mix, g_ref[...], b_ref[...])
        h_ref[r, :] = hh
        _store_rows(hp_ref.at[pl.ds(part * pm * ROW_TILES, pm * ROW_TILES)], _pack_bf16_pairs(hh))

        h_hi = hh.astype(BF16)
        h_lo = (hh - h_hi.astype(F32)).astype(BF16)
        lg = (jnp.dot(h_hi, wrh_ref[...], preferred_element_type=F32)
              + jnp.dot(h_lo, wrh_ref[...], preferred_element_type=F32)
              + jnp.dot(h_hi, wrl_ref[...], preferred_element_type=F32)) + br_ref[...]
        lane = lax.broadcasted_iota(I32, (pm, LANES), 1)
        lane_f = lane.astype(F32)
        work = lg
        vals, idxs = [], []
        cnt = jnp.zeros((pm, LANES), F32)
        for _ in range(TOP_K):
            mx = jnp.max(work, axis=-1, keepdims=True)
            ix = jnp.min(jnp.where(work == mx, lane_f, float(LANES)), axis=-1, keepdims=True)
            hit = lane_f == ix
            cnt = cnt + hit.astype(F32)
            work = jnp.where(hit, NEG_INF, work)
            vals.append(mx)
            idxs.append(ix)
        es = [jnp.exp(v - vals[0]) for v in vals]
        den = es[0] + es[1] + es[2] + es[3]
        idx_full = jnp.zeros((pm, LANES), F32)
        tw_full = jnp.zeros((pm, LANES), F32)
        for k in range(TOP_K):
            idx_full = jnp.where(lane == k, idxs[k], idx_full)
            tw_full = jnp.where(lane == k, es[k] / den, tw_full)
        idx_ref[:, r] = idx_full.T[:SUBLANES].astype(I32)
        tw_ref[r, :] = tw_full[:, :TOP_K]
        cnt_ref[...] += jnp.sum(cnt, axis=0, keepdims=True)


def _finish(x2, oa, slab, ob, p):
    n = x2.shape[0]
    tm = min(512, n)
    const = lambda i: (0, 0)
    return pl.pallas_call(
        functools.partial(_finish_kernel, parts=1),
        grid=(n // tm,),
        in_specs=[
            pl.BlockSpec((tm, D_MODEL), lambda i: (i, 0)),
            pl.BlockSpec((tm, VA_W), lambda i: (i, 0)),
            pl.BlockSpec((tm, RA_W), lambda i: (i, SLAB_RA // RA_W)),
            pl.BlockSpec((tm, VB_W), lambda i: (i, 0)),
            pl.BlockSpec((tm, GT_W), lambda i: (i, SLAB_GT // GT_W)),
            pl.BlockSpec((VA_W, D_MODEL), const),
            pl.BlockSpec((VB_W, D_MODEL), const),
            pl.BlockSpec((D_MODEL, D_MODEL), const),
            pl.BlockSpec((1, D_MODEL), const),
            pl.BlockSpec((1, D_MODEL), const),
            pl.BlockSpec((D_MODEL, LANES), const),
            pl.BlockSpec((D_MODEL, LANES), const),
            pl.BlockSpec((1, LANES), const),
        ],
        out_specs=[
            pl.BlockSpec((tm, D_MODEL), lambda i: (i, 0)),
            pl.BlockSpec((tm * ROW_TILES, LANES), lambda i: (i, 0)),
            pl.BlockSpec((SUBLANES, tm), lambda i: (0, i)),
            pl.BlockSpec((tm, TOP_K), lambda i: (i, 0)),
            pl.BlockSpec((1, LANES), const),
        ],
        out_shape=[
            jax.ShapeDtypeStruct((n, D_MODEL), F32),
            jax.ShapeDtypeStruct((n * ROW_TILES, LANES), U32),
            jax.ShapeDtypeStruct((SUBLANES, n), I32),
            jax.ShapeDtypeStruct((n, TOP_K), F32),
            jax.ShapeDtypeStruct((1, LANES), F32),
        ],
        compiler_params=_cparams(("arbitrary",)),
        name="finish",
    )(x2, oa, slab, ob, slab, p["w_pa"], p["w_pb"], p["w_o"], p["ln1_g"], p["ln1_b"],
      p["w_r_hi"], p["w_r_lo"], p["b_r"])


def _dest_kernel(idx_ref, ps_ref, ut_ref, dest_ref, carry_ref):
    i = pl.program_id(0)

    @pl.when(i == 0)
    def _():
        carry_ref[...] = jnp.zeros(carry_ref.shape, F32)

    tm = idx_ref.shape[1]
    expert = lax.broadcasted_iota(I32, (N_EXPERTS, tm), 0)
    idx = idx_ref[...]
    hits = [expert == idx[k:k + 1, :] for k in range(TOP_K)]
    cnt = jnp.zeros((N_EXPERTS, tm), F32)
    for hit in hits:
        cnt = cnt + hit.astype(F32)
    before = jnp.dot(cnt.astype(BF16), ut_ref[...], preferred_element_type=F32)
    base = before + carry_ref[:, 0:1] + ps_ref[:, 0:1]
    row = lax.broadcasted_iota(I32, (SUBLANES, tm), 0)
    dest = jnp.zeros((SUBLANES, tm), F32)
    for k in range(TOP_K):
        d = jnp.sum(jnp.where(hits[k], base, 0.0), axis=0, keepdims=True)
        dest = jnp.where(row == k, d, dest)
    dest_ref[...] = dest.astype(I32)
    carry_ref[...] += jnp.sum(cnt, axis=1, keepdims=True)


def _dest(idx_t, pstart):
    n = idx_t.shape[1]
    tm = min(512, n)
    utri = jnp.asarray(np.triu(np.ones((tm, tm), np.float32), 1), BF16)
    return pl.pallas_call(
        _dest_kernel,
        grid=(n // tm,),
        in_specs=[
            pl.BlockSpec((SUBLANES, tm), lambda i: (0, i)),
            pl.BlockSpec((N_EXPERTS, LANES), lambda i: (0, 0)),
            pl.BlockSpec((tm, tm), lambda i: (0, 0)),
        ],
        out_specs=pl.BlockSpec((SUBLANES, tm), lambda i: (0, i)),
        out_shape=jax.ShapeDtypeStruct((SUBLANES, n), I32),
        scratch_shapes=[pltpu.VMEM((N_EXPERTS, LANES), F32)],
        compiler_params=_cparams(("arbitrary",)),
        name="dest",
    )(idx_t, pstart, utri)


def _scatter_kernel(dest_ref, pad0_ref, padn_ref, hp_ref, rows_ref, zero_ref, sem, pad_sem):
    tm = hp_ref.shape[0] // ROW_TILES

    def row_copy(t, d):
        return pltpu.make_async_copy(hp_ref.at[pl.ds(pl.multiple_of(t * ROW_TILES, ROW_TILES), ROW_TILES)],
                                     rows_ref.at[pl.ds(pl.multiple_of(d * ROW_TILES, ROW_TILES), ROW_TILES)], sem)

    def issue(t, carry):
        for k in range(TOP_K):
            row_copy(t, dest_ref[k * tm + t]).start(priority=k % 2)
        return carry

    lax.fori_loop(0, tm, issue, 0, unroll=4)

    @pl.when(pl.program_id(0) == pl.num_programs(0) - 1)
    def _():
        zero_ref[...] = jnp.zeros(zero_ref.shape, U32)

        def pad_copy(d, p):
            return pltpu.make_async_copy(
                zero_ref.at[pl.ds(0, p * ROW_TILES)],
                rows_ref.at[pl.ds(pl.multiple_of(d * ROW_TILES, ROW_TILES), p * ROW_TILES)], pad_sem)

        def for_each_pad_run(fn):
            def per_expert(e, carry):
                d, n = pad0_ref[e], padn_ref[e]
                p = EXPERT_BM // 2
                while p:
                    @pl.when((n & p) != 0)
                    def _(d=d, p=p):
                        fn(d, p)
                    d = d + (n & p)
                    p //= 2
                return carry
            lax.fori_loop(0, N_EXPERTS, per_expert, 0)

        for_each_pad_run(lambda d, p: pad_copy(d, p).start())
        for_each_pad_run(lambda d, p: pad_copy(d, p).wait())

    for k in range(TOP_K):
        pltpu.make_async_copy(hp_ref, rows_ref.at[pl.ds(0, tm * ROW_TILES)], sem).wait()


def _scatter(dest_t, pad_start, pad_len, hp, n_rows):
    n = hp.shape[0] // ROW_TILES
    tm = min(512, n)
    smem = pl.BlockSpec(memory_space=pltpu.SMEM)
    return pl.pallas_call(
        _scatter_kernel,
        grid=(n // tm,),
        in_specs=[
            pl.BlockSpec((TOP_K * tm,), lambda i: (i,), memory_space=pltpu.SMEM),
            smem, smem,
            pl.BlockSpec((tm * ROW_TILES, LANES), lambda i: (i, 0)),
        ],
        out_specs=pl.BlockSpec(memory_space=pl.ANY),
        out_shape=jax.ShapeDtypeStruct((n_rows * ROW_TILES, LANES), U32),
        scratch_shapes=[pltpu.VMEM((EXPERT_BM // 2 * ROW_TILES, LANES), U32), pltpu.SemaphoreType.DMA(()),
                        pltpu.SemaphoreType.DMA(())],
        compiler_params=_cparams(("arbitrary",)),
        name="scatter",
    )(dest_t, pad_start, pad_len, hp)


def _expert_kernel(be_ref, nu_ref, rows_ref, wgu_ref, bgu_ref, wd_ref, bd_ref, out_ref, wgu_bf_ref, wd_bf_ref):
    i = pl.program_id(0)

    @pl.when((i == 0) | (be_ref[i] != be_ref[jnp.maximum(i - 1, 0)]))
    def _():
        wgu_bf_ref[...] = wgu_ref[0].astype(BF16)
        wd_bf_ref[...] = wd_ref[0].astype(BF16)

    @pl.when(i < nu_ref[0])
    def _():
        x = _unpack_bf16_pairs(_load_rows(rows_ref)).astype(BF16)
        gu = jnp.dot(x, wgu_bf_ref[...], preferred_element_type=F32) + bgu_ref[0]
        gate = jnp.minimum(gu[:, :D_FF], SWIGLU_LIMIT)
        up = jnp.clip(gu[:, D_FF:], -SWIGLU_LIMIT, SWIGLU_LIMIT)
        hidden = (up + 1.0) * (gate * _sigmoid(SWIGLU_ALPHA * gate))
        o = jnp.dot(hidden.astype(BF16), wd_bf_ref[...], preferred_element_type=F32) + bd_ref[0]
        _store_rows(out_ref, _pack_bf16_pairs(o))

    @pl.when(i >= nu_ref[0])
    def _():
        out_ref[...] = jnp.zeros(out_ref.shape, U32)


def _experts(rows, block_e, n_used, p):
    n_rows = rows.shape[0] // ROW_TILES
    bm = EXPERT_BM
    n_blocks = n_rows // bm
    grid_spec = pltpu.PrefetchScalarGridSpec(
        num_scalar_prefetch=2,
        grid=(n_blocks,),
        in_specs=[
            pl.BlockSpec((bm * ROW_TILES, LANES), lambda i, be, nu: (jnp.minimum(i, nu[0] - 1), 0)),
            pl.BlockSpec((1, D_MODEL, 2 * D_FF), lambda i, be, nu: (be[i], 0, 0)),
            pl.BlockSpec((1, 1, 2 * D_FF), lambda i, be, nu: (be[i], 0, 0)),
            pl.BlockSpec((1, D_FF, D_MODEL), lambda i, be, nu: (be[i], 0, 0)),
            pl.BlockSpec((1, 1, D_MODEL), lambda i, be, nu: (be[i], 0, 0)),
        ],
        out_specs=pl.BlockSpec((bm * ROW_TILES, LANES), lambda i, be, nu: (i, 0)),
        scratch_shapes=[pltpu.VMEM((D_MODEL, 2 * D_FF), BF16), pltpu.VMEM((D_FF, D_MODEL), BF16)],
    )
    return pl.pallas_call(
        _expert_kernel,
        grid_spec=grid_spec,
        out_shape=jax.ShapeDtypeStruct((n_rows * ROW_TILES, LANES), U32),
        compiler_params=_cparams(("arbitrary",)),
        name="experts",
    )(block_e, n_used, rows, p["w_gu"], p["b_gu"], p["w_down"], p["b_down"])


def _combine_kernel(dest_ref, tw_ref, h_ref, rows_ref, g_ref, b_ref, y_ref, buf_ref, sem):
    tm = h_ref.shape[0]

    def row_copy(t, k, d):
        return pltpu.make_async_copy(rows_ref.at[pl.ds(pl.multiple_of(d * ROW_TILES, ROW_TILES), ROW_TILES)],
                                     buf_ref.at[k, pl.ds(pl.multiple_of(t * ROW_TILES, ROW_TILES), ROW_TILES)], sem)

    def issue(t, carry):
        for k in range(TOP_K):
            row_copy(t, k, dest_ref[k * tm + t]).start(priority=k % 2)
        return carry

    lax.fori_loop(0, tm, issue, 0, unroll=4)
    for k in range(TOP_K):
        pltpu.make_async_copy(rows_ref.at[pl.ds(0, tm * ROW_TILES)], buf_ref.at[k], sem).wait()
    tw = tw_ref[...]
    f = jnp.zeros((tm, D_MODEL), F32)
    for k in range(TOP_K):
        f = f + tw[:, k:k + 1] * _unpack_bf16_pairs(_load_rows(buf_ref.at[k]))
    y_ref[...] = _layer_norm(DEEPNORM_ALPHA * h_ref[...] + f, g_ref[...], b_ref[...])


def _combine(dest_t, top_w, hh, out_rows, p):
    n = hh.shape[0]
    tm = min(512, n)
    const = lambda i: (0, 0)
    return pl.pallas_call(
        _combine_kernel,
        grid=(n // tm,),
        in_specs=[
            pl.BlockSpec((TOP_K * tm,), lambda i: (i,), memory_space=pltpu.SMEM),
            pl.BlockSpec((tm, TOP_K), lambda i: (i, 0)),
            pl.BlockSpec((tm, D_MODEL), lambda i: (i, 0)),
            pl.BlockSpec(memory_space=pl.ANY),
            pl.BlockSpec((1, D_MODEL), const),
            pl.BlockSpec((1, D_MODEL), const),
        ],
        out_specs=pl.BlockSpec((tm, D_MODEL), lambda i: (i, 0)),
        out_shape=jax.ShapeDtypeStruct((n, D_MODEL), F32),
        scratch_shapes=[pltpu.VMEM((TOP_K, tm * ROW_TILES, LANES), U32), pltpu.SemaphoreType.DMA(())],
        compiler_params=_cparams(("arbitrary",)),
        name="combine",
    )(dest_t, top_w, hh, out_rows, p["ln2_g"], p["ln2_b"])


def _moe(hh, hp, top_idx, top_w, counts, p):
    n = hh.shape[0]
    bm = EXPERT_BM
    n_rows = n * TOP_K + N_EXPERTS * bm
    cnt = counts[0, :N_EXPERTS].astype(I32)
    padded = (cnt + bm - 1) // bm * bm
    pend = jnp.cumsum(padded)
    pstart = pend - padded
    ps_col = jnp.broadcast_to(pstart.astype(F32)[:, None], (N_EXPERTS, LANES))
    block_row0 = jnp.arange(n_rows // bm, dtype=I32) * bm
    block_e = jnp.minimum(jnp.sum((pend[None, :] <= block_row0[:, None]).astype(I32), axis=1), N_EXPERTS - 1)
    n_used = (pend[-1:] // bm).astype(I32)
    tm = min(512, n)
    dest = _dest(top_idx, ps_col)[:TOP_K].reshape(TOP_K, n // tm, tm).transpose(1, 0, 2).reshape(-1)
    rows = _scatter(dest, pstart + cnt, padded - cnt, hp, n_rows)
    out_rows = _experts(rows, block_e, n_used, p)
    return _combine(dest, top_w, hh, out_rows, p)


def _prep_params(w_in, b_in, w_alpha, b_alpha, gla_norm_g, diff_norm_g, w_pa, w_pb, w_o, ln1_g, ln1_b,
                 w_router, b_router, w_gu, b_gu, w_down, b_down, ln2_g, ln2_b):
    o = IN_OFFS
    seg = lambda a, i: a[..., o[i]:o[i + 1]]
    order = (0, 1, 2, 3, 5, 8)
    w_slab = jnp.concatenate([seg(w_in, i) for i in order], axis=1).astype(BF16)
    b_slab = jnp.concatenate([seg(b_in, i) for i in order])[None, :]
    scale = np.ones((1, SLAB_W), np.float32)
    scale[0, SLAB_QA:SLAB_QA + QA_W] = DK_A ** -0.5
    scale[0, SLAB_QB:SLAB_QB + QB_W] = D_B ** -0.5 * LOG2E
    w_lr = jnp.zeros((D_MODEL, LANES), F32).at[:, :LR_W].set(seg(w_in, 4)).astype(BF16)
    b_lr = jnp.zeros((1, LANES), F32).at[0, :LR_W].set(seg(b_in, 4))
    w_al = jnp.zeros((LANES, QA_W), F32).at[:LR_W].set(w_alpha).astype(BF16)
    w_r = jnp.zeros((D_MODEL, LANES), F32).at[:, :N_EXPERTS].set(w_router)
    w_r_hi = w_r.astype(BF16)
    w_r_lo = (w_r - w_r_hi.astype(F32)).astype(BF16)
    b_r = jnp.full((1, LANES), NEG_INF, F32).at[0, :N_EXPERTS].set(b_router)
    return dict(
        w_slab=w_slab, b_slab=b_slab, s_slab=jnp.asarray(scale),
        w_lr=w_lr, b_lr=b_lr, w_alpha=w_al, b_alpha=b_alpha[None, :],
        w_k=seg(w_in, 6).astype(BF16), w_v=seg(w_in, 7).astype(BF16),
        b_k=seg(b_in, 6)[None, :], b_v=seg(b_in, 7)[None, :],
        gla_g=gla_norm_g[None, :], diff_g=diff_norm_g[None, :],
        w_pa=w_pa.astype(BF16), w_pb=w_pb.astype(BF16), w_o=w_o.astype(BF16),
        ln1_g=ln1_g[None, :], ln1_b=ln1_b[None, :],
        w_r_hi=w_r_hi, w_r_lo=w_r_lo, b_r=b_r,
        w_gu=w_gu, b_gu=b_gu[:, None, :], w_down=w_down, b_down=b_down[:, None, :],
        ln2_g=ln2_g[None, :], ln2_b=ln2_b[None, :],
    )


def _layer(x, p, dl, lam_init, s0, cache_k, cache_v):
    batch, t_len, _ = x.shape
    x2 = x.reshape(batch * t_len, D_MODEL)
    slab, loga, k_rows, kb, vb = _proj(x2, p)
    oa, s_new = _gla(slab, loga, s0, p["gla_g"], batch, t_len)
    if cache_k is None:
        ob = _attn_prompt(slab, kb, vb, p["diff_g"], dl, batch, t_len, lam_init)
    else:
        ob = _attn_sample(slab, kb, vb, cache_k, cache_v, p["diff_g"], dl, batch, t_len, lam_init)
    hh, hp, top_idx, top_w, counts = _finish(x2, oa, slab, ob, p)
    y = _moe(hh, hp, top_idx, top_w, counts, p)
    return (y.reshape(batch, t_len, D_MODEL), s_new,
            k_rows.reshape(batch, t_len, H_B, 2, D_B), vb.reshape(batch, t_len, H_B, 2 * D_B))


def kernel(x_prompt, x_sample, cache_k, cache_v, state_gla, w_in, b_in, w_alpha, b_alpha, gla_norm_g,
           diff_lambda, diff_norm_g, w_pa, w_pb, w_o, ln1_g, ln1_b, w_router, b_router, w_gu, b_gu,
           w_down, b_down, ln2_g, ln2_b):
    assert w_in.shape[0] == DEPTH == 1
    l = 0
    lam_init = 0.8 - 0.6 * math.exp(-0.3 * l)
    p = _prep_params(w_in[l], b_in[l], w_alpha[l], b_alpha[l], gla_norm_g[l], diff_norm_g[l], w_pa[l],
                     w_pb[l], w_o[l], ln1_g[l], ln1_b[l], w_router[l], b_router[l], w_gu[l], b_gu[l],
                     w_down[l], b_down[l], ln2_g[l], ln2_b[l])
    dl = diff_lambda[l]
    bp = x_prompt.shape[0]
    bs, ts = x_sample.shape[0], x_sample.shape[1]
    past = cache_k.shape[2]
    yp, s_p, k_p, v_p = _layer(x_prompt, p, dl, lam_init, jnp.zeros((bp, H_A, DK_A, DV_A), F32), None, None)
    ck = jnp.transpose(cache_k[l], (0, 2, 3, 4, 1)).reshape(bs, H_B, 2 * D_B, past)
    cv = cache_v[l].reshape(bs, past * H_B, 2 * D_B)
    ys, s_s, k_s, v_s = _layer(x_sample, p, dl, lam_init, state_gla[l], ck, cv)
    return (yp, ys, s_p[None], k_p[None], v_p[None], s_s[None], k_s[None], v_s[None])
```

```python
import functools
import math

import numpy as np
import jax
import jax.numpy as jnp
from jax import lax
from jax.experimental import pallas as pl
from jax.experimental.pallas import tpu as pltpu

F32 = jnp.float32
BF16 = jnp.bfloat16
U32 = jnp.uint32
I32 = jnp.int32

D_MODEL = 1024
CHUNK = 64
H_A = 4
DK_A = 128
DV_A = 256
GATE_RANK = 16
GATE_TAU = 16.0
H_B = 8
D_B = 64
N_EXPERTS = 32
TOP_K = 4
D_FF = D_MODEL
SWIGLU_LIMIT = 7.0
SWIGLU_ALPHA = 1.702
EPS = 1e-5
DEPTH = 1
DEEPNORM_ALPHA = (2.0 * DEPTH) ** 0.25

QA_W = H_A * DK_A
KA_W = H_A * DK_A
VA_W = H_A * DV_A
RA_W = H_A * DV_A
LR_W = GATE_RANK
QB_W = H_B * 2 * D_B
KB_W = H_B * 2 * D_B
VB_W = H_B * 2 * D_B
GT_W = 2 * D_MODEL
IN_SIZES = (QA_W, KA_W, VA_W, RA_W, LR_W, QB_W, KB_W, VB_W, GT_W)
IN_OFFS = tuple(int(v) for v in np.cumsum((0,) + IN_SIZES))

SLAB_QA, SLAB_KA, SLAB_VA, SLAB_RA, SLAB_QB, SLAB_GT = 0, 512, 1024, 2048, 3072, 4096
SLAB_W = 6144
LANES = 128
SUBLANES = 8
EXPERT_BM = 512
VMEM_LIMIT = 56 * 1024 * 1024
PROJ_VMEM_LIMIT = 61 * 1024 * 1024

NEG_INF = float("-inf")
LOG2E = math.log2(math.e)
CHUNK_BITS = CHUNK.bit_length() - 1
POS_BITS = 8
POS_BASE = 1 << POS_BITS


def _cparams(sem, vmem_limit=VMEM_LIMIT):
    return pltpu.CompilerParams(dimension_semantics=sem, vmem_limit_bytes=vmem_limit)


def _sigmoid(x):
    return 1.0 / (1.0 + jnp.exp(-x))


def _pack_bf16_pairs(x):
    n = x.shape[1] // 2
    xb = x.astype(BF16).astype(F32)
    lo = pltpu.bitcast(xb[:, :n], U32) >> 16
    hi = pltpu.bitcast(xb[:, n:], U32) & jnp.uint32(0xFFFF0000)
    return hi | lo


def _unpack_bf16_pairs(r):
    lo = pltpu.bitcast(r << 16, F32)
    hi = pltpu.bitcast(r & jnp.uint32(0xFFFF0000), F32)
    return jnp.concatenate([lo, hi], axis=1)


ROW_TILES = D_MODEL // 2 // LANES


def _load_rows(ref):
    m = ref.shape[0] // ROW_TILES
    return jnp.concatenate([ref[pl.ds(c, m, stride=ROW_TILES), :] for c in range(ROW_TILES)], axis=1)


def _store_rows(ref, val):
    m = ref.shape[0] // ROW_TILES
    for c in range(ROW_TILES):
        ref[pl.ds(c, m, stride=ROW_TILES), :] = val[:, c * LANES:(c + 1) * LANES]


def _layer_norm(y, g, b):
    mu = jnp.mean(y, axis=-1, keepdims=True)
    d = y - mu
    var = jnp.mean(d * d, axis=-1, keepdims=True)
    return d * lax.rsqrt(var + EPS) * g + b


def _proj_kernel(x_ref, w_ref, b_ref, s_ref, wlr_ref, blr_ref, wal_ref, bal_ref, wk_ref, wv_ref, bk_ref, bv_ref,
                 slab_ref, loga_ref, krows_ref, kb_ref, v_ref, *, tn):
    xb = x_ref[...].astype(BF16)
    k = jnp.dot(xb, wk_ref[...], preferred_element_type=F32) + bk_ref[...]
    kb_ref[...] = k.astype(BF16)
    for h in range(H_B):
        for j in range(2):
            c0 = (2 * h + j) * D_B
            krows_ref[:, h, j, :] = k[:, c0:c0 + D_B]
    v_ref[...] = jnp.dot(xb, wv_ref[...], preferred_element_type=F32) + bv_ref[...]
    lra = jnp.dot(xb, wlr_ref[...], preferred_element_type=F32) + blr_ref[...]
    z = jnp.dot(lra.astype(BF16), wal_ref[...], preferred_element_type=F32) + bal_ref[...]
    loga_ref[...] = (jnp.minimum(z, 0.0) - jnp.log(1.0 + jnp.exp(-jnp.abs(z)))) * (1.0 / GATE_TAU)
    for c0 in range(0, SLAB_W, tn):
        acc = jnp.dot(xb, w_ref[:, c0:c0 + tn], preferred_element_type=F32)
        slab_ref[:, c0:c0 + tn] = ((acc + b_ref[:, c0:c0 + tn]) * s_ref[:, c0:c0 + tn]).astype(BF16)


def _proj(x2, p):
    n = x2.shape[0]
    tm = min(512, n)
    const = lambda i: (0, 0)
    whole = lambda shape: pl.BlockSpec(shape, const, pipeline_mode=pl.Buffered(1))
    row = lambda w: pl.BlockSpec((tm, w), lambda i: (i, 0))
    return pl.pallas_call(
        functools.partial(_proj_kernel, tn=1024),
        grid=(n // tm,),
        in_specs=[
            row(D_MODEL),
            whole((D_MODEL, SLAB_W)), whole((1, SLAB_W)), whole((1, SLAB_W)),
            whole((D_MODEL, LANES)), whole((1, LANES)), whole((LANES, QA_W)), whole((1, QA_W)),
            whole((D_MODEL, KB_W)), whole((D_MODEL, VB_W)), whole((1, KB_W)), whole((1, VB_W)),
        ],
        out_specs=[
            row(SLAB_W), row(QA_W),
            pl.BlockSpec((tm, H_B, 2, D_B), lambda i: (i, 0, 0, 0)),
            row(KB_W), row(VB_W),
        ],
        out_shape=[
            jax.ShapeDtypeStruct((n, SLAB_W), BF16),
            jax.ShapeDtypeStruct((n, QA_W), F32),
            jax.ShapeDtypeStruct((n, H_B, 2, D_B), F32),
            jax.ShapeDtypeStruct((n, KB_W), BF16),
            jax.ShapeDtypeStruct((n, VB_W), F32),
        ],
        compiler_params=_cparams(("parallel",), PROJ_VMEM_LIMIT),
        name="proj",
    )(x2, p["w_slab"], p["b_slab"], p["s_slab"], p["w_lr"], p["b_lr"], p["w_alpha"], p["b_alpha"],
      p["w_k"], p["w_v"], p["b_k"], p["b_v"])


def _gla_tables(L):
    nl = int(math.log2(L))
    t = np.arange(L)
    D = np.zeros(((nl + 2) * L, L), np.float32)
    masks = np.zeros((nl + 1, L, L), np.float32)
    for l in range(nl):
        m = L >> (l + 1)
        grp = t // (2 * m)
        mid = grp * 2 * m + m - 1
        upper = (t % (2 * m)) >= m
        for r in range(L):
            if upper[r]:
                D[l * L + r, mid[r] + 1:r + 1] = 1.0
            else:
                D[l * L + r, r + 1:mid[r] + 1] = 1.0
        masks[l] = (upper[:, None] & ~upper[None, :] & (grp[:, None] == grp[None, :])).astype(np.float32)
    D[nl * L:(nl + 1) * L] = np.tril(np.ones((L, L), np.float32))
    D[(nl + 1) * L:] = np.triu(np.ones((L, L), np.float32), 1)
    masks[nl] = np.eye(L, dtype=np.float32)
    return jnp.asarray(D, BF16), jnp.asarray(masks, F32)


def _gla_kernel(q_ref, k_ref, v_ref, la_ref, s0_ref, g_ref, d_ref, m_ref, o_ref, s_ref, *, L, n_chunks):
    c = pl.program_id(1)
    nl = int(math.log2(L))

    @pl.when(c == 0)
    def _():
        s_ref[...] = s0_ref[...]

    dmat = d_ref[...]
    ones_col = jnp.ones((L, LANES), BF16)
    g = g_ref[...]
    nt = (((1,), (1,)), ((), ()))
    tn = (((0,), (0,)), ((), ()))

    def chunk(ci, carry):
        r0 = pl.multiple_of(ci * L, L)
        rows = pl.ds(r0, L)
        la_all = la_ref[rows, :]
        la_hi = la_all.astype(BF16)
        la_lo = (la_all - la_hi.astype(F32)).astype(BF16)
        x_all = jnp.exp(jnp.dot(dmat, la_hi, preferred_element_type=F32)
                        + jnp.dot(dmat, la_lo, preferred_element_type=F32))
        bl_all = (lax.dot_general(la_hi, ones_col, tn, preferred_element_type=F32)
                  + lax.dot_general(la_lo, ones_col, tn, preferred_element_type=F32))
        dec_all = jnp.exp(bl_all[:, 0:1])
        for h in range(H_A):
            kc = slice(h * DK_A, (h + 1) * DK_A)
            vc = slice(h * DV_A, (h + 1) * DV_A)
            q_bf = q_ref[rows, kc]
            k_bf = k_ref[rows, kc]
            q = q_bf.astype(F32)
            k = k_bf.astype(F32)
            v = v_ref[rows, vc]
            x = x_all[:, kc]
            a = m_ref[nl] * lax.dot_general(q_bf, k_bf, nt, preferred_element_type=F32)
            for l in range(nl):
                xl = x[l * L:(l + 1) * L]
                a = a + m_ref[l] * lax.dot_general((q * xl).astype(BF16), (k * xl).astype(BF16), nt,
                                                   preferred_element_type=F32)
            xb = x[nl * L:(nl + 1) * L]
            xs = x[(nl + 1) * L:]
            s_old = s_ref[0, h]
            o = (jnp.dot((q * xb).astype(BF16), s_old.astype(BF16), preferred_element_type=F32)
                 + jnp.dot(a.astype(BF16), v, preferred_element_type=F32))
            s_ref[0, h] = dec_all[kc] * s_old + lax.dot_general((k * xs).astype(BF16), v, tn, preferred_element_type=F32)
            ms = jnp.mean(o * o, axis=-1, keepdims=True)
            o_ref[rows, vc] = (o * lax.rsqrt(ms + EPS) * g).astype(BF16)
        return carry

    lax.fori_loop(0, n_chunks, chunk, 0, unroll=True)


def _gla(slab, loga, s0, g_norm, batch, t_len):
    L = min(CHUNK, t_len)
    tb = min(512, t_len)
    nb = t_len // tb
    dmat, masks = _gla_tables(L)
    kern = functools.partial(_gla_kernel, L=L, n_chunks=tb // L)
    row = lambda b, c: b * nb + c
    return pl.pallas_call(
        kern,
        grid=(batch, nb),
        in_specs=[
            pl.BlockSpec((tb, QA_W), lambda b, c: (row(b, c), SLAB_QA // QA_W)),
            pl.BlockSpec((tb, KA_W), lambda b, c: (row(b, c), SLAB_KA // KA_W)),
            pl.BlockSpec((tb, VA_W), lambda b, c: (row(b, c), SLAB_VA // VA_W)),
            pl.BlockSpec((tb, QA_W), lambda b, c: (row(b, c), 0)),
            pl.BlockSpec((1, H_A, DK_A, DV_A), lambda b, c: (b, 0, 0, 0)),
            pl.BlockSpec((1, DV_A), lambda b, c: (0, 0)),
            pl.BlockSpec(dmat.shape, lambda b, c: (0, 0)),
            pl.BlockSpec(masks.shape, lambda b, c: (0, 0, 0)),
        ],
        out_specs=[
            pl.BlockSpec((tb, VA_W), lambda b, c: (row(b, c), 0)),
            pl.BlockSpec((1, H_A, DK_A, DV_A), lambda b, c: (b, 0, 0, 0)),
        ],
        out_shape=[
            jax.ShapeDtypeStruct((batch * t_len, VA_W), BF16),
            jax.ShapeDtypeStruct((batch, H_A, DK_A, DV_A), F32),
        ],
        compiler_params=_cparams(("parallel", "arbitrary")),
        name="gla",
    )(slab, slab, slab, loga, s0, g_norm, dmat, masks)


def _lambda_from(dl_ref, lam_init):
    dl = dl_ref[...]
    a = jnp.sum(dl[0:1] * dl[1:2], axis=-1, keepdims=True)
    b = jnp.sum(dl[2:3] * dl[3:4], axis=-1, keepdims=True)
    return jnp.exp(a) - jnp.exp(b) + lam_init


def _split_maps(q):
    lane = lax.broadcasted_iota(I32, q.shape, 1)
    zero = jnp.zeros_like(q)
    return jnp.concatenate([jnp.where(lane < D_B, q, zero), jnp.where(lane >= D_B, q, zero)], axis=0)


def _alibi_coef(h):
    c = jnp.full((1, 1), LOG2E, F32) * jnp.exp2(-(h + 1).astype(F32))
    c_hi = c.astype(BF16).astype(F32)
    return c, c_hi, c - c_hi


def _finish_heads(acc, inv_l, lam, g, tq, lam_init):
    o = acc * inv_l
    out = o[:tq] - lam * o[tq:]
    ms = jnp.mean(out * out, axis=-1, keepdims=True)
    return (out * lax.rsqrt(ms + EPS) * g * (1.0 - lam_init)).astype(BF16)


def _attn_prompt_kernel(q_ref, k_ref, v_ref, g_ref, dl_ref, o_ref, ka_ref, vb_ref, qaug_ref, d_ref, m_ref,
                        acc_ref, s0_ref, s1_ref, p0_ref, p1_ref,
                        *, tq, tk, lam_init):
    h = pl.program_id(1)
    qi = pl.program_id(2)
    c, c_hi, c_lo = _alibi_coef(h)
    hw = 2 * D_B

    def aug_lanes(shape, pos, sign):
        lane = lax.broadcasted_iota(I32, shape, 1)
        r = (pos & (POS_BASE - 1)).astype(F32) * sign
        a = (pos >> POS_BITS).astype(F32) * sign
        coef = jnp.where(lane == 0, c_hi, jnp.where(lane == 1, c_lo,
                         jnp.where(lane == 2, POS_BASE * c_hi, jnp.where(lane == 3, POS_BASE * c_lo, 0.0))))
        ints = jnp.where(lane < 2, r, jnp.where(lane < 4, a, 0.0))
        return coef, ints, lane

    @pl.when(qi == 0)
    def _():
        t_len = vb_ref.shape[0]
        lane = lax.broadcasted_iota(I32, (t_len, hw), 1)
        row = lax.broadcasted_iota(I32, (hw, tk), 0)
        j_rel = lax.broadcasted_iota(I32, (hw, tk), 1)
        k_aug_t = jnp.where(row < 2, (j_rel & (POS_BASE - 1)).astype(F32),
                  jnp.where(row < 4, (j_rel >> POS_BITS).astype(F32),
                  jnp.where(row == 4, c_hi, jnp.where(row == 5, c_lo,
                  jnp.where(row == 6, POS_BASE * c_hi, jnp.where(row == 7, POS_BASE * c_lo, 0.0)))))).astype(BF16)
        for blk in range(t_len // tk):
            rows = slice(blk * tk, (blk + 1) * tk)
            ka_ref[blk, :hw, :] = k_ref[0, rows, :].astype(F32).T.astype(BF16)
            ka_ref[blk, hw:, :] = k_aug_t
        vb_ref[:, :hw] = v_ref[0].astype(BF16)
        vb_ref[:, hw:] = jnp.where(lane == 0, 1.0, 0.0).astype(BF16)
        i_rel = lax.broadcasted_iota(I32, (2 * tq, hw), 0) & (tq - 1)
        coef, ints, lane = aug_lanes((2 * tq, hw), i_rel, -1.0)
        qaug_ref[...] = jnp.where(lane < 4, coef, pltpu.roll(ints, 4, 1)).astype(BF16)
        i = lax.broadcasted_iota(I32, (2 * tq, tq), 0) & (tq - 1)
        j = lax.broadcasted_iota(I32, (2 * tq, tq), 1)
        fwd = jnp.maximum(j - i, 0).astype(F32)
        d_ref[...] = jnp.where((j >> CHUNK_BITS) <= (i >> CHUNK_BITS), -2.0 * c * fwd, NEG_INF)

    qa = jnp.concatenate([_split_maps(q_ref[...]), qaug_ref[...]], axis=1)
    m_ref[...] = jnp.full(m_ref.shape, -1e30, F32)

    def scores(kv):
        return jnp.dot(qa, ka_ref[kv], preferred_element_type=F32)

    def weighted_values(p_ref, kv):
        k0 = pl.multiple_of(kv * tk, tk)
        return jnp.dot(p_ref[...], vb_ref[pl.ds(k0, tk), :], preferred_element_type=F32)

    def softmax_step(s, off):
        m_old = m_ref[...]
        m_new = jnp.maximum(m_old, jnp.max(s, axis=-1, keepdims=True) - off)
        shift = m_new + off
        p = jnp.exp2(s - jnp.concatenate([shift] * (tk // hw), axis=1))
        alpha = jnp.exp2(m_old - m_new)
        m_ref[...] = m_new
        return p.astype(BF16), jnp.concatenate([alpha, alpha], axis=1)

    s_first = scores(0)
    p_diag, _ = softmax_step(scores(qi) + d_ref[...], jnp.zeros((1, 1), F32))
    p1_ref[...] = p_diag
    s0_ref[...] = s_first
    acc_ref[...] = jnp.zeros(acc_ref.shape, F32)

    def step(kv, s_cur, s_nxt, p_prev, p_cur):
        pv_prev = weighted_values(p_prev, jnp.where(kv == 0, qi, kv - 1))
        s_nxt[...] = scores(jnp.minimum(kv + 1, qi - 1))
        p, alpha = softmax_step(s_cur[...], c * (qi * tq - kv * tk).astype(F32))
        acc_ref[...] = alpha * (acc_ref[...] + pv_prev)
        p_cur[...] = p

    def pair(j, carry):
        step(2 * j, s0_ref, s1_ref, p1_ref, p0_ref)
        step(2 * j + 1, s1_ref, s0_ref, p0_ref, p1_ref)
        return carry

    lax.fori_loop(0, qi // 2, pair, 0)

    @pl.when(qi % 2 == 1)
    def _():
        step(qi - 1, s0_ref, s1_ref, p1_ref, p0_ref)
        acc_ref[...] += weighted_values(p0_ref, qi - 1)

    @pl.when(qi % 2 == 0)
    def _():
        acc_ref[...] += weighted_values(p1_ref, jnp.maximum(qi - 1, 0))

    lam = _lambda_from(dl_ref, lam_init)
    acc = acc_ref[...]
    o_ref[...] = _finish_heads(acc[:, :hw], 1.0 / acc[:, hw:hw + 1], lam, g_ref[...], tq, lam_init)


def _attn_prompt(slab, kb, vb, g_norm, dl, batch, t_len, lam_init):
    tq = min(512, t_len)
    tk = tq
    assert tq % CHUNK == 0 and CHUNK == 1 << CHUNK_BITS
    nq = t_len // tq
    k3 = kb.reshape(batch, t_len, KB_W)
    v3 = vb.reshape(batch, t_len, VB_W)
    kern = functools.partial(_attn_prompt_kernel, tq=tq, tk=tk, lam_init=lam_init)
    hw = 2 * D_B
    return pl.pallas_call(
        kern,
        grid=(batch, H_B, nq),
        in_specs=[
            pl.BlockSpec((tq, hw), lambda b, h, q: (b * nq + q, SLAB_QB // hw + h)),
            pl.BlockSpec((1, t_len, hw), lambda b, h, q: (b, 0, h)),
            pl.BlockSpec((1, t_len, hw), lambda b, h, q: (b, 0, h)),
            pl.BlockSpec((1, hw), lambda b, h, q: (0, 0)),
            pl.BlockSpec((4, D_B), lambda b, h, q: (0, 0)),
        ],
        out_specs=pl.BlockSpec((tq, hw), lambda b, h, q: (b * nq + q, h)),
        out_shape=jax.ShapeDtypeStruct((batch * t_len, VB_W), BF16),
        scratch_shapes=[
            pltpu.VMEM((t_len // tk, 2 * hw, tk), BF16),
            pltpu.VMEM((t_len, 2 * hw), BF16),
            pltpu.VMEM((2 * tq, hw), BF16),
            pltpu.VMEM((2 * tq, tq), F32),
            pltpu.VMEM((2 * tq, hw), F32),
            pltpu.VMEM((2 * tq, 2 * hw), F32),
            pltpu.VMEM((2 * tq, tk), F32),
            pltpu.VMEM((2 * tq, tk), F32),
            pltpu.VMEM((2 * tq, tk), BF16),
            pltpu.VMEM((2 * tq, tk), BF16),
        ],
        compiler_params=_cparams(("parallel", "parallel", "arbitrary")),
        name="attn_prompt",
    )(slab, k3, v3, g_norm, dl)


def _attn_sample_kernel(q_ref, kct_ref, vc_ref, kn_ref, vn_ref, g_ref, dl_ref, o_ref, *, tq, past, lam_init):
    hw = 2 * D_B
    q_pos = past + lax.broadcasted_iota(I32, (2 * tq, 1), 0) % tq
    lam = _lambda_from(dl_ref, lam_init)

    def biased(s, k_pos, c):
        s = s - c * jnp.abs(q_pos - k_pos).astype(F32)
        return jnp.where((k_pos // CHUNK) <= (q_pos // CHUNK), s, NEG_INF)

    for h in range(H_B):
        cols = slice(h * hw, (h + 1) * hw)
        c = LOG2E * 2.0 ** -(h + 1)
        qq = _split_maps(q_ref[:, cols])
        sc = biased(jnp.dot(qq, kct_ref[0, h].astype(BF16), preferred_element_type=F32),
                    lax.broadcasted_iota(I32, (1, past), 1), c)
        sn = biased(lax.dot_general(qq, kn_ref[:, cols], (((1,), (1,)), ((), ())), preferred_element_type=F32),
                    past + lax.broadcasted_iota(I32, (1, tq), 1), c)
        m = jnp.maximum(jnp.max(sc, axis=-1, keepdims=True), jnp.max(sn, axis=-1, keepdims=True))
        pc = jnp.exp2(sc - m)
        pn = jnp.exp2(sn - m)
        l = jnp.sum(pc, axis=-1, keepdims=True) + jnp.sum(pn, axis=-1, keepdims=True)
        vc = vc_ref[0, pl.ds(h, past, stride=H_B), :].astype(BF16)
        acc = (jnp.dot(pc.astype(BF16), vc, preferred_element_type=F32)
               + jnp.dot(pn.astype(BF16), vn_ref[:, cols].astype(BF16), preferred_element_type=F32))
        o_ref[:, cols] = _finish_heads(acc, 1.0 / l, lam, g_ref[...], tq, lam_init)


def _attn_sample(slab, kb, vb, cache_kt, cache_v2, g_norm, dl, batch, t_len, lam_init):
    past = cache_kt.shape[3]
    hw = 2 * D_B
    kern = functools.partial(_attn_sample_kernel, tq=t_len, past=past, lam_init=lam_init)
    return pl.pallas_call(
        kern,
        grid=(batch,),
        in_specs=[
            pl.BlockSpec((t_len, QB_W), lambda b: (b, SLAB_QB // QB_W)),
            pl.BlockSpec((1, H_B, hw, past), lambda b: (b, 0, 0, 0)),
            pl.BlockSpec((1, past * H_B, hw), lambda b: (b, 0, 0)),
            pl.BlockSpec((t_len, KB_W), lambda b: (b, 0)),
            pl.BlockSpec((t_len, VB_W), lambda b: (b, 0)),
            pl.BlockSpec((1, hw), lambda b: (0, 0)),
            pl.BlockSpec((4, D_B), lambda b: (0, 0)),
        ],
        out_specs=pl.BlockSpec((t_len, VB_W), lambda b: (b, 0)),
        out_shape=jax.ShapeDtypeStruct((batch * t_len, VB_W), BF16),
        compiler_params=_cparams(("parallel",)),
        name="attn_sample",
    )(slab, cache_kt, cache_v2, kb, vb, g_norm, dl)


def _finish_kernel(x_ref, oa_ref, ra_ref, ob_ref, gt_ref, wpa_ref, wpb_ref, wo_ref, g_ref, b_ref,
                   wrh_ref, wrl_ref, br_ref, h_ref, hp_ref, idx_ref, tw_ref, cnt_ref, *, parts):
    i = pl.program_id(0)

    @pl.when(i == 0)
    def _():
        cnt_ref[...] = jnp.zeros(cnt_ref.shape, F32)

    pm = x_ref.shape[0] // parts
    for part in range(parts):
        r = slice(part * pm, (part + 1) * pm)
        ra = ra_ref[r, :].astype(F32)
        ua = ra * _sigmoid(ra) * oa_ref[r, :].astype(F32)
        ya = jnp.dot(ua.astype(BF16), wpa_ref[...], preferred_element_type=F32)
        yb = jnp.dot(ob_ref[r, :], wpb_ref[...], preferred_element_type=F32)
        gt = gt_ref[r, :].astype(F32)
        mixed = _sigmoid(gt[:, :D_MODEL]) * ya + _sigmoid(gt[:, D_MODEL:]) * yb
        mix = jnp.dot(mixed.astype(BF16), wo_ref[...], preferred_element_type=F32)
        hh = _layer_norm(DEEPNORM_ALPHA * x_ref[r, :] + mix, g_ref[...], b_ref[...])
        h_ref[r, :] = hh
        _store_rows(hp_ref.at[pl.ds(part * pm * ROW_TILES, pm * ROW_TILES)], _pack_bf16_pairs(hh))

        h_hi = hh.astype(BF16)
        h_lo = (hh - h_hi.astype(F32)).astype(BF16)
        lg = (jnp.dot(h_hi, wrh_ref[...], preferred_element_type=F32)
              + jnp.dot(h_lo, wrh_ref[...], preferred_element_type=F32)
              + jnp.dot(h_hi, wrl_ref[...], preferred_element_type=F32)) + br_ref[...]
        lane = lax.broadcasted_iota(I32, (pm, LANES), 1)
        lane_f = lane.astype(F32)
        work = lg
        vals, idxs = [], []
        cnt = jnp.zeros((pm, LANES), F32)
        for _ in range(TOP_K):
            mx = jnp.max(work, axis=-1, keepdims=True)
            ix = jnp.min(jnp.where(work == mx, lane_f, float(LANES)), axis=-1, keepdims=True)
            hit = lane_f == ix
            cnt = cnt + hit.astype(F32)
            work = jnp.where(hit, NEG_INF, work)
            vals.append(mx)
            idxs.append(ix)
        es = [jnp.exp(v - vals[0]) for v in vals]
        den = es[0] + es[1] + es[2] + es[3]
        idx_full = jnp.zeros((pm, LANES), F32)
        tw_full = jnp.zeros((pm, LANES), F32)
        for k in range(TOP_K):
            idx_full = jnp.where(lane == k, idxs[k], idx_full)
            tw_full = jnp.where(lane == k, es[k] / den, tw_full)
        idx_ref[:, r] = idx_full.T[:SUBLANES].astype(I32)
        tw_ref[r, :] = tw_full[:, :TOP_K]
        cnt_ref[...] += jnp.sum(cnt, axis=0, keepdims=True)


def _finish(x2, oa, slab, ob, p):
    n = x2.shape[0]
    tm = min(512, n)
    const = lambda i: (0, 0)
    return pl.pallas_call(
        functools.partial(_finish_kernel, parts=1),
        grid=(n // tm,),
        in_specs=[
            pl.BlockSpec((tm, D_MODEL), lambda i: (i, 0)),
            pl.BlockSpec((tm, VA_W), lambda i: (i, 0)),
            pl.BlockSpec((tm, RA_W), lambda i: (i, SLAB_RA // RA_W)),
            pl.BlockSpec((tm, VB_W), lambda i: (i, 0)),
            pl.BlockSpec((tm, GT_W), lambda i: (i, SLAB_GT // GT_W)),
            pl.BlockSpec((VA_W, D_MODEL), const),
            pl.BlockSpec((VB_W, D_MODEL), const),
            pl.BlockSpec((D_MODEL, D_MODEL), const),
            pl.BlockSpec((1, D_MODEL), const),
            pl.BlockSpec((1, D_MODEL), const),
            pl.BlockSpec((D_MODEL, LANES), const),
            pl.BlockSpec((D_MODEL, LANES), const),
            pl.BlockSpec((1, LANES), const),
        ],
        out_specs=[
            pl.BlockSpec((tm, D_MODEL), lambda i: (i, 0)),
            pl.BlockSpec((tm * ROW_TILES, LANES), lambda i: (i, 0)),
            pl.BlockSpec((SUBLANES, tm), lambda i: (0, i)),
            pl.BlockSpec((tm, TOP_K), lambda i: (i, 0)),
            pl.BlockSpec((1, LANES), const),
        ],
        out_shape=[
            jax.ShapeDtypeStruct((n, D_MODEL), F32),
            jax.ShapeDtypeStruct((n * ROW_TILES, LANES), U32),
            jax.ShapeDtypeStruct((SUBLANES, n), I32),
            jax.ShapeDtypeStruct((n, TOP_K), F32),
            jax.ShapeDtypeStruct((1, LANES), F32),
        ],
        compiler_params=_cparams(("arbitrary",)),
        name="finish",
    )(x2, oa, slab, ob, slab, p["w_pa"], p["w_pb"], p["w_o"], p["ln1_g"], p["ln1_b"],
      p["w_r_hi"], p["w_r_lo"], p["b_r"])


def _dest_kernel(idx_ref, ps_ref, ut_ref, dest_ref, carry_ref):
    i = pl.program_id(0)

    @pl.when(i == 0)
    def _():
        carry_ref[...] = jnp.zeros(carry_ref.shape, F32)

    tm = idx_ref.shape[1]
    expert = lax.broadcasted_iota(I32, (N_EXPERTS, tm), 0)
    idx = idx_ref[...]
    hits = [expert == idx[k:k + 1, :] for k in range(TOP_K)]
    cnt = jnp.zeros((N_EXPERTS, tm), F32)
    for hit in hits:
        cnt = cnt + hit.astype(F32)
    before = jnp.dot(cnt.astype(BF16), ut_ref[...], preferred_element_type=F32)
    base = before + carry_ref[:, 0:1] + ps_ref[:, 0:1]
    row = lax.broadcasted_iota(I32, (SUBLANES, tm), 0)
    dest = jnp.zeros((SUBLANES, tm), F32)
    for k in range(TOP_K):
        d = jnp.sum(jnp.where(hits[k], base, 0.0), axis=0, keepdims=True)
        dest = jnp.where(row == k, d, dest)
    dest_ref[...] = dest.astype(I32)
    carry_ref[...] += jnp.sum(cnt, axis=1, keepdims=True)


def _dest(idx_t, pstart):
    n = idx_t.shape[1]
    tm = min(512, n)
    utri = jnp.asarray(np.triu(np.ones((tm, tm), np.float32), 1), BF16)
    return pl.pallas_call(
        _dest_kernel,
        grid=(n // tm,),
        in_specs=[
            pl.BlockSpec((SUBLANES, tm), lambda i: (0, i)),
            pl.BlockSpec((N_EXPERTS, LANES), lambda i: (0, 0)),
            pl.BlockSpec((tm, tm), lambda i: (0, 0)),
        ],
        out_specs=pl.BlockSpec((SUBLANES, tm), lambda i: (0, i)),
        out_shape=jax.ShapeDtypeStruct((SUBLANES, n), I32),
        scratch_shapes=[pltpu.VMEM((N_EXPERTS, LANES), F32)],
        compiler_params=_cparams(("arbitrary",)),
        name="dest",
    )(idx_t, pstart, utri)


def _scatter_kernel(dest_ref, pad0_ref, padn_ref, hp_ref, rows_ref, zero_ref, sem, pad_sem):
    tm = hp_ref.shape[0] // ROW_TILES

    def row_copy(t, d):
        return pltpu.make_async_copy(hp_ref.at[pl.ds(pl.multiple_of(t * ROW_TILES, ROW_TILES), ROW_TILES)],
                                     rows_ref.at[pl.ds(pl.multiple_of(d * ROW_TILES, ROW_TILES), ROW_TILES)], sem)

    def issue(t, carry):
        for k in range(TOP_K):
            row_copy(t, dest_ref[k * tm + t]).start(priority=k % 2)
        return carry

    lax.fori_loop(0, tm, issue, 0, unroll=4)

    @pl.when(pl.program_id(0) == pl.num_programs(0) - 1)
    def _():
        zero_ref[...] = jnp.zeros(zero_ref.shape, U32)

        def pad_copy(d, p):
            return pltpu.make_async_copy(
                zero_ref.at[pl.ds(0, p * ROW_TILES)],
                rows_ref.at[pl.ds(pl.multiple_of(d * ROW_TILES, ROW_TILES), p * ROW_TILES)], pad_sem)

        def for_each_pad_run(fn):
            def per_expert(e, carry):
                d, n = pad0_ref[e], padn_ref[e]
                p = EXPERT_BM // 2
                while p:
                    @pl.when((n & p) != 0)
                    def _(d=d, p=p):
                        fn(d, p)
                    d = d + (n & p)
                    p //= 2
                return carry
            lax.fori_loop(0, N_EXPERTS, per_expert, 0)

        for_each_pad_run(lambda d, p: pad_copy(d, p).start())
        for_each_pad_run(lambda d, p: pad_copy(d, p).wait())

    for k in range(TOP_K):
        pltpu.make_async_copy(hp_ref, rows_ref.at[pl.ds(0, tm * ROW_TILES)], sem).wait()


def _scatter(dest_t, pad_start, pad_len, hp, n_rows):
    n = hp.shape[0] // ROW_TILES
    tm = min(512, n)
    smem = pl.BlockSpec(memory_space=pltpu.SMEM)
    return pl.pallas_call(
        _scatter_kernel,
        grid=(n // tm,),
        in_specs=[
            pl.BlockSpec((TOP_K * tm,), lambda i: (i,), memory_space=pltpu.SMEM),
            smem, smem,
            pl.BlockSpec((tm * ROW_TILES, LANES), lambda i: (i, 0)),
        ],
        out_specs=pl.BlockSpec(memory_space=pl.ANY),
        out_shape=jax.ShapeDtypeStruct((n_rows * ROW_TILES, LANES), U32),
        scratch_shapes=[pltpu.VMEM((EXPERT_BM // 2 * ROW_TILES, LANES), U32), pltpu.SemaphoreType.DMA(()),
                        pltpu.SemaphoreType.DMA(())],
        compiler_params=_cparams(("arbitrary",)),
        name="scatter",
    )(dest_t, pad_start, pad_len, hp)


def _expert_kernel(be_ref, nu_ref, rows_ref, wgu_ref, bgu_ref, wd_ref, bd_ref, out_ref, wgu_bf_ref, wd_bf_ref):
    i = pl.program_id(0)

    @pl.when((i == 0) | (be_ref[i] != be_ref[jnp.maximum(i - 1, 0)]))
    def _():
        wgu_bf_ref[...] = wgu_ref[0].astype(BF16)
        wd_bf_ref[...] = wd_ref[0].astype(BF16)

    @pl.when(i < nu_ref[0])
    def _():
        x = _unpack_bf16_pairs(_load_rows(rows_ref)).astype(BF16)
        gu = jnp.dot(x, wgu_bf_ref[...], preferred_element_type=F32) + bgu_ref[0]
        gate = jnp.minimum(gu[:, :D_FF], SWIGLU_LIMIT)
        up = jnp.clip(gu[:, D_FF:], -SWIGLU_LIMIT, SWIGLU_LIMIT)
        hidden = (up + 1.0) * (gate * _sigmoid(SWIGLU_ALPHA * gate))
        o = jnp.dot(hidden.astype(BF16), wd_bf_ref[...], preferred_element_type=F32) + bd_ref[0]
        _store_rows(out_ref, _pack_bf16_pairs(o))

    @pl.when(i >= nu_ref[0])
    def _():
        out_ref[...] = jnp.zeros(out_ref.shape, U32)


def _experts(rows, block_e, n_used, p):
    n_rows = rows.shape[0] // ROW_TILES
    bm = EXPERT_BM
    n_blocks = n_rows // bm
    grid_spec = pltpu.PrefetchScalarGridSpec(
        num_scalar_prefetch=2,
        grid=(n_blocks,),
        in_specs=[
            pl.BlockSpec((bm * ROW_TILES, LANES), lambda i, be, nu: (jnp.minimum(i, nu[0] - 1), 0)),
            pl.BlockSpec((1, D_MODEL, 2 * D_FF), lambda i, be, nu: (be[i], 0, 0)),
            pl.BlockSpec((1, 1, 2 * D_FF), lambda i, be, nu: (be[i], 0, 0)),
            pl.BlockSpec((1, D_FF, D_MODEL), lambda i, be, nu: (be[i], 0, 0)),
            pl.BlockSpec((1, 1, D_MODEL), lambda i, be, nu: (be[i], 0, 0)),
        ],
        out_specs=pl.BlockSpec((bm * ROW_TILES, LANES), lambda i, be, nu: (i, 0)),
        scratch_shapes=[pltpu.VMEM((D_MODEL, 2 * D_FF), BF16), pltpu.VMEM((D_FF, D_MODEL), BF16)],
    )
    return pl.pallas_call(
        _expert_kernel,
        grid_spec=grid_spec,
        out_shape=jax.ShapeDtypeStruct((n_rows * ROW_TILES, LANES), U32),
        compiler_params=_cparams(("arbitrary",)),
        name="experts",
    )(block_e, n_used, rows, p["w_gu"], p["b_gu"], p["w_down"], p["b_down"])


def _combine_kernel(dest_ref, tw_ref, h_ref, rows_ref, g_ref, b_ref, y_ref, buf_ref, sems):
    tm = h_ref.shape[0]
    group = 4

    def row_copy(t, k, d):
        return pltpu.make_async_copy(rows_ref.at[pl.ds(pl.multiple_of(d * ROW_TILES, ROW_TILES), ROW_TILES)],
                                     buf_ref.at[k, pl.ds(pl.multiple_of(t * ROW_TILES, ROW_TILES), ROW_TILES)],
                                     sems.at[k])

    for k in range(TOP_K):
        def issue(g, carry, k=k):
            for j in range(group):
                t = g * group + j
                row_copy(t, k, dest_ref[k * tm + t]).start(priority=j % 2)
            return carry
        lax.fori_loop(0, tm // group, issue, 0)

    tw = tw_ref[...]
    f = jnp.zeros((tm, D_MODEL), F32)
    for k in range(TOP_K):
        pltpu.make_async_copy(rows_ref.at[pl.ds(0, tm * ROW_TILES)], buf_ref.at[k], sems.at[k]).wait()
        f = f + tw[:, k:k + 1] * _unpack_bf16_pairs(_load_rows(buf_ref.at[k]))
    y_ref[...] = _layer_norm(DEEPNORM_ALPHA * h_ref[...] + f, g_ref[...], b_ref[...])


def _combine(dest_t, top_w, hh, out_rows, p):
    n = hh.shape[0]
    tm = min(512, n)
    const = lambda i: (0, 0)
    return pl.pallas_call(
        _combine_kernel,
        grid=(n // tm,),
        in_specs=[
            pl.BlockSpec((TOP_K * tm,), lambda i: (i,), memory_space=pltpu.SMEM),
            pl.BlockSpec((tm, TOP_K), lambda i: (i, 0)),
            pl.BlockSpec((tm, D_MODEL), lambda i: (i, 0)),
            pl.BlockSpec(memory_space=pl.ANY),
            pl.BlockSpec((1, D_MODEL), const),
            pl.BlockSpec((1, D_MODEL), const),
        ],
        out_specs=pl.BlockSpec((tm, D_MODEL), lambda i: (i, 0)),
        out_shape=jax.ShapeDtypeStruct((n, D_MODEL), F32),
        scratch_shapes=[pltpu.VMEM((TOP_K, tm * ROW_TILES, LANES), U32), pltpu.SemaphoreType.DMA((TOP_K,))],
        compiler_params=_cparams(("arbitrary",)),
        name="combine",
    )(dest_t, top_w, hh, out_rows, p["ln2_g"], p["ln2_b"])


def _moe(hh, hp, top_idx, top_w, counts, p):
    n = hh.shape[0]
    bm = EXPERT_BM
    n_rows = n * TOP_K + N_EXPERTS * bm
    cnt = counts[0, :N_EXPERTS].astype(I32)
    padded = (cnt + bm - 1) // bm * bm
    pend = jnp.cumsum(padded)
    pstart = pend - padded
    ps_col = jnp.broadcast_to(pstart.astype(F32)[:, None], (N_EXPERTS, LANES))
    block_row0 = jnp.arange(n_rows // bm, dtype=I32) * bm
    block_e = jnp.minimum(jnp.sum((pend[None, :] <= block_row0[:, None]).astype(I32), axis=1), N_EXPERTS - 1)
    n_used = (pend[-1:] // bm).astype(I32)
    tm = min(512, n)
    dest = _dest(top_idx, ps_col)[:TOP_K].reshape(TOP_K, n // tm, tm).transpose(1, 0, 2).reshape(-1)
    rows = _scatter(dest, pstart + cnt, padded - cnt, hp, n_rows)
    out_rows = _experts(rows, block_e, n_used, p)
    return _combine(dest, top_w, hh, out_rows, p)


def _prep_params(w_in, b_in, w_alpha, b_alpha, gla_norm_g, diff_norm_g, w_pa, w_pb, w_o, ln1_g, ln1_b,
                 w_router, b_router, w_gu, b_gu, w_down, b_down, ln2_g, ln2_b):
    o = IN_OFFS
    seg = lambda a, i: a[..., o[i]:o[i + 1]]
    order = (0, 1, 2, 3, 5, 8)
    w_slab = jnp.concatenate([seg(w_in, i) for i in order], axis=1).astype(BF16)
    b_slab = jnp.concatenate([seg(b_in, i) for i in order])[None, :]
    scale = np.ones((1, SLAB_W), np.float32)
    scale[0, SLAB_QA:SLAB_QA + QA_W] = DK_A ** -0.5
    scale[0, SLAB_QB:SLAB_QB + QB_W] = D_B ** -0.5 * LOG2E
    w_lr = jnp.zeros((D_MODEL, LANES), F32).at[:, :LR_W].set(seg(w_in, 4)).astype(BF16)
    b_lr = jnp.zeros((1, LANES), F32).at[0, :LR_W].set(seg(b_in, 4))
    w_al = jnp.zeros((LANES, QA_W), F32).at[:LR_W].set(w_alpha).astype(BF16)
    w_r = jnp.zeros((D_MODEL, LANES), F32).at[:, :N_EXPERTS].set(w_router)
    w_r_hi = w_r.astype(BF16)
    w_r_lo = (w_r - w_r_hi.astype(F32)).astype(BF16)
    b_r = jnp.full((1, LANES), NEG_INF, F32).at[0, :N_EXPERTS].set(b_router)
    return dict(
        w_slab=w_slab, b_slab=b_slab, s_slab=jnp.asarray(scale),
        w_lr=w_lr, b_lr=b_lr, w_alpha=w_al, b_alpha=b_alpha[None, :],
        w_k=seg(w_in, 6).astype(BF16), w_v=seg(w_in, 7).astype(BF16),
        b_k=seg(b_in, 6)[None, :], b_v=seg(b_in, 7)[None, :],
        gla_g=gla_norm_g[None, :], diff_g=diff_norm_g[None, :],
        w_pa=w_pa.astype(BF16), w_pb=w_pb.astype(BF16), w_o=w_o.astype(BF16),
        ln1_g=ln1_g[None, :], ln1_b=ln1_b[None, :],
        w_r_hi=w_r_hi, w_r_lo=w_r_lo, b_r=b_r,
        w_gu=w_gu, b_gu=b_gu[:, None, :], w_down=w_down, b_down=b_down[:, None, :],
        ln2_g=ln2_g[None, :], ln2_b=ln2_b[None, :],
    )


def _layer(x, p, dl, lam_init, s0, cache_k, cache_v):
    batch, t_len, _ = x.shape
    x2 = x.reshape(batch * t_len, D_MODEL)
    slab, loga, k_rows, kb, vb = _proj(x2, p)
    oa, s_new = _gla(slab, loga, s0, p["gla_g"], batch, t_len)
    if cache_k is None:
        ob = _attn_prompt(slab, kb, vb, p["diff_g"], dl, batch, t_len, lam_init)
    else:
        ob = _attn_sample(slab, kb, vb, cache_k, cache_v, p["diff_g"], dl, batch, t_len, lam_init)
    hh, hp, top_idx, top_w, counts = _finish(x2, oa, slab, ob, p)
    y = _moe(hh, hp, top_idx, top_w, counts, p)
    return (y.reshape(batch, t_len, D_MODEL), s_new,
            k_rows.reshape(batch, t_len, H_B, 2, D_B), vb.reshape(batch, t_len, H_B, 2 * D_B))


def kernel(x_prompt, x_sample, cache_k, cache_v, state_gla, w_in, b_in, w_alpha, b_alpha, gla_norm_g,
           diff_lambda, diff_norm_g, w_pa, w_pb, w_o, ln1_g, ln1_b, w_router, b_router, w_gu, b_gu,
           w_down, b_down, ln2_g, ln2_b):
    assert w_in.shape[0] == DEPTH == 1
    l = 0
    lam_init = 0.8 - 0.6 * math.exp(-0.3 * l)
    p = _prep_params(w_in[l], b_in[l], w_alpha[l], b_alpha[l], gla_norm_g[l], diff_norm_g[l], w_pa[l],
                     w_pb[l], w_o[l], ln1_g[l], ln1_b[l], w_router[l], b_router[l], w_gu[l], b_gu[l],
                     w_down[l], b_down[l], ln2_g[l], ln2_b[l])
    dl = diff_lambda[l]
    bp = x_prompt.shape[0]
    bs, ts = x_sample.shape[0], x_sample.shape[1]
    past = cache_k.shape[2]
    yp, s_p, k_p, v_p = _layer(x_prompt, p, dl, lam_init, jnp.zeros((bp, H_A, DK_A, DV_A), F32), None, None)
    ck = jnp.transpose(cache_k[l], (0, 2, 3, 4, 1)).reshape(bs, H_B, 2 * D_B, past)
    cv = cache_v[l].reshape(bs, past * H_B, 2 * D_B)
    ys, s_s, k_s, v_s = _layer(x_sample, p, dl, lam_init, state_gla[l], ck, cv)
    return (yp, ys, s_p[None], k_p[None], v_p[None], s_s[None], k_s[None], v_s[None])
```

```python
import functools
import math

import numpy as np
import jax
import jax.numpy as jnp
from jax import lax
from jax.experimental import pallas as pl
from jax.experimental.pallas import tpu as pltpu

F32 = jnp.float32
BF16 = jnp.bfloat16
U32 = jnp.uint32
I32 = jnp.int32

D_MODEL = 1024
CHUNK = 64
H_A = 4
DK_A = 128
DV_A = 256
GATE_RANK = 16
GATE_TAU = 16.0
H_B = 8
D_B = 64
N_EXPERTS = 32
TOP_K = 4
D_FF = D_MODEL
SWIGLU_LIMIT = 7.0
SWIGLU_ALPHA = 1.702
EPS = 1e-5
DEPTH = 1
DEEPNORM_ALPHA = (2.0 * DEPTH) ** 0.25

QA_W = H_A * DK_A
KA_W = H_A * DK_A
VA_W = H_A * DV_A
RA_W = H_A * DV_A
LR_W = GATE_RANK
QB_W = H_B * 2 * D_B
KB_W = H_B * 2 * D_B
VB_W = H_B * 2 * D_B
GT_W = 2 * D_MODEL
IN_SIZES = (QA_W, KA_W, VA_W, RA_W, LR_W, QB_W, KB_W, VB_W, GT_W)
IN_OFFS = tuple(int(v) for v in np.cumsum((0,) + IN_SIZES))

SLAB_QA, SLAB_KA, SLAB_VA, SLAB_RA, SLAB_QB, SLAB_GT = 0, 512, 1024, 2048, 3072, 4096
SLAB_W = 6144
LANES = 128
SUBLANES = 8
EXPERT_BM = 512
VMEM_LIMIT = 56 * 1024 * 1024
PROJ_VMEM_LIMIT = 61 * 1024 * 1024

NEG_INF = float("-inf")
LOG2E = math.log2(math.e)
CHUNK_BITS = CHUNK.bit_length() - 1
POS_BITS = 8
POS_BASE = 1 << POS_BITS


def _cparams(sem, vmem_limit=VMEM_LIMIT):
    return pltpu.CompilerParams(dimension_semantics=sem, vmem_limit_bytes=vmem_limit)


def _sigmoid(x):
    return 1.0 / (1.0 + jnp.exp(-x))


def _pack_bf16_pairs(x):
    n = x.shape[1] // 2
    xb = x.astype(BF16).astype(F32)
    lo = pltpu.bitcast(xb[:, :n], U32) >> 16
    hi = pltpu.bitcast(xb[:, n:], U32) & jnp.uint32(0xFFFF0000)
    return hi | lo


def _unpack_bf16_pairs(r):
    lo = pltpu.bitcast(r << 16, F32)
    hi = pltpu.bitcast(r & jnp.uint32(0xFFFF0000), F32)
    return jnp.concatenate([lo, hi], axis=1)


ROW_TILES = D_MODEL // 2 // LANES


def _load_rows(ref):
    m = ref.shape[0] // ROW_TILES
    return jnp.concatenate([ref[pl.ds(c, m, stride=ROW_TILES), :] for c in range(ROW_TILES)], axis=1)


def _store_rows(ref, val):
    m = ref.shape[0] // ROW_TILES
    for c in range(ROW_TILES):
        ref[pl.ds(c, m, stride=ROW_TILES), :] = val[:, c * LANES:(c + 1) * LANES]


def _layer_norm(y, g, b):
    mu = jnp.mean(y, axis=-1, keepdims=True)
    d = y - mu
    var = jnp.mean(d * d, axis=-1, keepdims=True)
    return d * lax.rsqrt(var + EPS) * g + b


def _proj_kernel(x_ref, w_ref, b_ref, s_ref, wlr_ref, blr_ref, wal_ref, bal_ref, wk_ref, wv_ref, bk_ref, bv_ref,
                 slab_ref, loga_ref, krows_ref, kb_ref, v_ref, *, tn):
    xb = x_ref[...].astype(BF16)
    k = jnp.dot(xb, wk_ref[...], preferred_element_type=F32) + bk_ref[...]
    kb_ref[...] = k.astype(BF16)
    for h in range(H_B):
        for j in range(2):
            c0 = (2 * h + j) * D_B
            krows_ref[:, h, j, :] = k[:, c0:c0 + D_B]
    v_ref[...] = jnp.dot(xb, wv_ref[...], preferred_element_type=F32) + bv_ref[...]
    lra = jnp.dot(xb, wlr_ref[...], preferred_element_type=F32) + blr_ref[...]
    z = jnp.dot(lra.astype(BF16), wal_ref[...], preferred_element_type=F32) + bal_ref[...]
    loga_ref[...] = (jnp.minimum(z, 0.0) - jnp.log(1.0 + jnp.exp(-jnp.abs(z)))) * (1.0 / GATE_TAU)
    for c0 in range(0, SLAB_W, tn):
        acc = jnp.dot(xb, w_ref[:, c0:c0 + tn], preferred_element_type=F32)
        slab_ref[:, c0:c0 + tn] = ((acc + b_ref[:, c0:c0 + tn]) * s_ref[:, c0:c0 + tn]).astype(BF16)


def _proj(x2, p):
    n = x2.shape[0]
    tm = min(512, n)
    const = lambda i: (0, 0)
    whole = lambda shape: pl.BlockSpec(shape, const, pipeline_mode=pl.Buffered(1))
    row = lambda w: pl.BlockSpec((tm, w), lambda i: (i, 0))
    return pl.pallas_call(
        functools.partial(_proj_kernel, tn=1024),
        grid=(n // tm,),
        in_specs=[
            row(D_MODEL),
            whole((D_MODEL, SLAB_W)), whole((1, SLAB_W)), whole((1, SLAB_W)),
            whole((D_MODEL, LANES)), whole((1, LANES)), whole((LANES, QA_W)), whole((1, QA_W)),
            whole((D_MODEL, KB_W)), whole((D_MODEL, VB_W)), whole((1, KB_W)), whole((1, VB_W)),
        ],
        out_specs=[
            row(SLAB_W), row(QA_W),
            pl.BlockSpec((tm, H_B, 2, D_B), lambda i: (i, 0, 0, 0)),
            row(KB_W), row(VB_W),
        ],
        out_shape=[
            jax.ShapeDtypeStruct((n, SLAB_W), BF16),
            jax.ShapeDtypeStruct((n, QA_W), F32),
            jax.ShapeDtypeStruct((n, H_B, 2, D_B), F32),
            jax.ShapeDtypeStruct((n, KB_W), BF16),
            jax.ShapeDtypeStruct((n, VB_W), F32),
        ],
        compiler_params=_cparams(("parallel",), PROJ_VMEM_LIMIT),
        name="proj",
    )(x2, p["w_slab"], p["b_slab"], p["s_slab"], p["w_lr"], p["b_lr"], p["w_alpha"], p["b_alpha"],
      p["w_k"], p["w_v"], p["b_k"], p["b_v"])


def _gla_tables(L):
    nl = int(math.log2(L))
    t = np.arange(L)
    D = np.zeros(((nl + 2) * L, L), np.float32)
    masks = np.zeros((nl + 1, L, L), np.float32)
    for l in range(nl):
        m = L >> (l + 1)
        grp = t // (2 * m)
        mid = grp * 2 * m + m - 1
        upper = (t % (2 * m)) >= m
        for r in range(L):
            if upper[r]:
                D[l * L + r, mid[r] + 1:r + 1] = 1.0
            else:
                D[l * L + r, r + 1:mid[r] + 1] = 1.0
        masks[l] = (upper[:, None] & ~upper[None, :] & (grp[:, None] == grp[None, :])).astype(np.float32)
    D[nl * L:(nl + 1) * L] = np.tril(np.ones((L, L), np.float32))
    D[(nl + 1) * L:] = np.triu(np.ones((L, L), np.float32), 1)
    masks[nl] = np.eye(L, dtype=np.float32)
    return jnp.asarray(D, BF16), jnp.asarray(masks, F32)


def _gla_kernel(q_ref, k_ref, v_ref, la_ref, s0_ref, g_ref, d_ref, m_ref, o_ref, s_ref, *, L, n_chunks):
    c = pl.program_id(1)
    nl = int(math.log2(L))

    @pl.when(c == 0)
    def _():
        s_ref[...] = s0_ref[...]

    dmat = d_ref[...]
    ones_col = jnp.ones((L, LANES), BF16)
    g = g_ref[...]
    nt = (((1,), (1,)), ((), ()))
    tn = (((0,), (0,)), ((), ()))

    def chunk(ci, carry):
        r0 = pl.multiple_of(ci * L, L)
        rows = pl.ds(r0, L)
        la_all = la_ref[rows, :]
        la_hi = la_all.astype(BF16)
        la_lo = (la_all - la_hi.astype(F32)).astype(BF16)
        x_all = jnp.exp(jnp.dot(dmat, la_hi, preferred_element_type=F32)
                        + jnp.dot(dmat, la_lo, preferred_element_type=F32))
        bl_all = (lax.dot_general(la_hi, ones_col, tn, preferred_element_type=F32)
                  + lax.dot_general(la_lo, ones_col, tn, preferred_element_type=F32))
        dec_all = jnp.exp(bl_all[:, 0:1])
        for h in range(H_A):
            kc = slice(h * DK_A, (h + 1) * DK_A)
            vc = slice(h * DV_A, (h + 1) * DV_A)
            q_bf = q_ref[rows, kc]
            k_bf = k_ref[rows, kc]
            q = q_bf.astype(F32)
            k = k_bf.astype(F32)
            v = v_ref[rows, vc]
            x = x_all[:, kc]
            a = m_ref[nl] * lax.dot_general(q_bf, k_bf, nt, preferred_element_type=F32)
            for l in range(nl):
                xl = x[l * L:(l + 1) * L]
                a = a + m_ref[l] * lax.dot_general((q * xl).astype(BF16), (k * xl).astype(BF16), nt,
                                                   preferred_element_type=F32)
            xb = x[nl * L:(nl + 1) * L]
            xs = x[(nl + 1) * L:]
            s_old = s_ref[0, h]
            o = (jnp.dot((q * xb).astype(BF16), s_old.astype(BF16), preferred_element_type=F32)
                 + jnp.dot(a.astype(BF16), v, preferred_element_type=F32))
            s_ref[0, h] = dec_all[kc] * s_old + lax.dot_general((k * xs).astype(BF16), v, tn, preferred_element_type=F32)
            ms = jnp.mean(o * o, axis=-1, keepdims=True)
            o_ref[rows, vc] = (o * lax.rsqrt(ms + EPS) * g).astype(BF16)
        return carry

    lax.fori_loop(0, n_chunks, chunk, 0, unroll=True)


def _gla(slab, loga, s0, g_norm, batch, t_len):
    L = min(CHUNK, t_len)
    tb = min(512, t_len)
    nb = t_len // tb
    dmat, masks = _gla_tables(L)
    kern = functools.partial(_gla_kernel, L=L, n_chunks=tb // L)
    row = lambda b, c: b * nb + c
    return pl.pallas_call(
        kern,
        grid=(batch, nb),
        in_specs=[
            pl.BlockSpec((tb, QA_W), lambda b, c: (row(b, c), SLAB_QA // QA_W)),
            pl.BlockSpec((tb, KA_W), lambda b, c: (row(b, c), SLAB_KA // KA_W)),
            pl.BlockSpec((tb, VA_W), lambda b, c: (row(b, c), SLAB_VA // VA_W)),
            pl.BlockSpec((tb, QA_W), lambda b, c: (row(b, c), 0)),
            pl.BlockSpec((1, H_A, DK_A, DV_A), lambda b, c: (b, 0, 0, 0)),
            pl.BlockSpec((1, DV_A), lambda b, c: (0, 0)),
            pl.BlockSpec(dmat.shape, lambda b, c: (0, 0)),
            pl.BlockSpec(masks.shape, lambda b, c: (0, 0, 0)),
        ],
        out_specs=[
            pl.BlockSpec((tb, VA_W), lambda b, c: (row(b, c), 0)),
            pl.BlockSpec((1, H_A, DK_A, DV_A), lambda b, c: (b, 0, 0, 0)),
        ],
        out_shape=[
            jax.ShapeDtypeStruct((batch * t_len, VA_W), BF16),
            jax.ShapeDtypeStruct((batch, H_A, DK_A, DV_A), F32),
        ],
        compiler_params=_cparams(("parallel", "arbitrary")),
        name="gla",
    )(slab, slab, slab, loga, s0, g_norm, dmat, masks)


def _lambda_from(dl_ref, lam_init):
    dl = dl_ref[...]
    a = jnp.sum(dl[0:1] * dl[1:2], axis=-1, keepdims=True)
    b = jnp.sum(dl[2:3] * dl[3:4], axis=-1, keepdims=True)
    return jnp.exp(a) - jnp.exp(b) + lam_init


def _split_maps(q):
    lane = lax.broadcasted_iota(I32, q.shape, 1)
    zero = jnp.zeros_like(q)
    return jnp.concatenate([jnp.where(lane < D_B, q, zero), jnp.where(lane >= D_B, q, zero)], axis=0)


def _alibi_coef(h):
    c = jnp.full((1, 1), LOG2E, F32) * jnp.exp2(-(h + 1).astype(F32))
    c_hi = c.astype(BF16).astype(F32)
    return c, c_hi, c - c_hi


def _finish_heads(acc, inv_l, lam, g, tq, lam_init):
    o = acc * inv_l
    out = o[:tq] - lam * o[tq:]
    ms = jnp.mean(out * out, axis=-1, keepdims=True)
    return (out * lax.rsqrt(ms + EPS) * g * (1.0 - lam_init)).astype(BF16)


def _attn_prompt_kernel(q_ref, k_ref, v_ref, g_ref, dl_ref, o_ref, ka_ref, vb_ref, qaug_ref, d_ref, m_ref,
                        acc_ref, s0_ref, s1_ref, p0_ref, p1_ref,
                        *, tq, tk, lam_init):
    h = pl.program_id(1)
    qi = pl.program_id(2)
    c, c_hi, c_lo = _alibi_coef(h)
    hw = 2 * D_B

    def aug_lanes(shape, pos, sign):
        lane = lax.broadcasted_iota(I32, shape, 1)
        r = (pos & (POS_BASE - 1)).astype(F32) * sign
        a = (pos >> POS_BITS).astype(F32) * sign
        coef = jnp.where(lane == 0, c_hi, jnp.where(lane == 1, c_lo,
                         jnp.where(lane == 2, POS_BASE * c_hi, jnp.where(lane == 3, POS_BASE * c_lo, 0.0))))
        ints = jnp.where(lane < 2, r, jnp.where(lane < 4, a, 0.0))
        return coef, ints, lane

    @pl.when(qi == 0)
    def _():
        t_len = vb_ref.shape[0]
        lane = lax.broadcasted_iota(I32, (t_len, hw), 1)
        row = lax.broadcasted_iota(I32, (hw, tk), 0)
        j_rel = lax.broadcasted_iota(I32, (hw, tk), 1)
        k_aug_t = jnp.where(row < 2, (j_rel & (POS_BASE - 1)).astype(F32),
                  jnp.where(row < 4, (j_rel >> POS_BITS).astype(F32),
                  jnp.where(row == 4, c_hi, jnp.where(row == 5, c_lo,
                  jnp.where(row == 6, POS_BASE * c_hi, jnp.where(row == 7, POS_BASE * c_lo, 0.0)))))).astype(BF16)
        for blk in range(t_len // tk):
            rows = slice(blk * tk, (blk + 1) * tk)
            ka_ref[blk, :hw, :] = k_ref[0, rows, :].astype(F32).T.astype(BF16)
            ka_ref[blk, hw:, :] = k_aug_t
        vb_ref[:, :hw] = v_ref[0].astype(BF16)
        vb_ref[:, hw:] = jnp.where(lane == 0, 1.0, 0.0).astype(BF16)
        i_rel = lax.broadcasted_iota(I32, (2 * tq, hw), 0) & (tq - 1)
        coef, ints, lane = aug_lanes((2 * tq, hw), i_rel, -1.0)
        qaug_ref[...] = jnp.where(lane < 4, coef, pltpu.roll(ints, 4, 1)).astype(BF16)
        i = lax.broadcasted_iota(I32, (2 * tq, tq), 0) & (tq - 1)
        j = lax.broadcasted_iota(I32, (2 * tq, tq), 1)
        fwd = jnp.maximum(j - i, 0).astype(F32)
        d_ref[...] = jnp.where((j >> CHUNK_BITS) <= (i >> CHUNK_BITS), -2.0 * c * fwd, NEG_INF)

    qa = jnp.concatenate([_split_maps(q_ref[...]), qaug_ref[...]], axis=1)
    m_ref[...] = jnp.full(m_ref.shape, -1e30, F32)

    def scores(kv):
        return jnp.dot(qa, ka_ref[kv], preferred_element_type=F32)

    def weighted_values(p_ref, kv):
        k0 = pl.multiple_of(kv * tk, tk)
        return jnp.dot(p_ref[...], vb_ref[pl.ds(k0, tk), :], preferred_element_type=F32)

    def softmax_step(s, off):
        m_old = m_ref[...]
        m_new = jnp.maximum(m_old, jnp.max(s, axis=-1, keepdims=True) - off)
        shift = m_new + off
        p = jnp.exp2(s - jnp.concatenate([shift] * (tk // hw), axis=1))
        alpha = jnp.exp2(m_old - m_new)
        m_ref[...] = m_new
        return p.astype(BF16), jnp.concatenate([alpha, alpha], axis=1)

    s_first = scores(0)
    p_diag, _ = softmax_step(scores(qi) + d_ref[...], jnp.zeros((1, 1), F32))
    p1_ref[...] = p_diag
    s0_ref[...] = s_first
    acc_ref[...] = jnp.zeros(acc_ref.shape, F32)

    def step(kv, s_cur, s_nxt, p_prev, p_cur):
        pv_prev = weighted_values(p_prev, jnp.where(kv == 0, qi, kv - 1))
        s_nxt[...] = scores(jnp.minimum(kv + 1, qi - 1))
        p, alpha = softmax_step(s_cur[...], c * (qi * tq - kv * tk).astype(F32))
        acc_ref[...] = alpha * (acc_ref[...] + pv_prev)
        p_cur[...] = p

    def pair(j, carry):
        step(2 * j, s0_ref, s1_ref, p1_ref, p0_ref)
        step(2 * j + 1, s1_ref, s0_ref, p0_ref, p1_ref)
        return carry

    lax.fori_loop(0, qi // 2, pair, 0)

    @pl.when(qi % 2 == 1)
    def _():
        step(qi - 1, s0_ref, s1_ref, p1_ref, p0_ref)
        acc_ref[...] += weighted_values(p0_ref, qi - 1)

    @pl.when(qi % 2 == 0)
    def _():
        acc_ref[...] += weighted_values(p1_ref, jnp.maximum(qi - 1, 0))

    lam = _lambda_from(dl_ref, lam_init)
    acc = acc_ref[...]
    o_ref[...] = _finish_heads(acc[:, :hw], 1.0 / acc[:, hw:hw + 1], lam, g_ref[...], tq, lam_init)


def _attn_prompt(slab, kb, vb, g_norm, dl, batch, t_len, lam_init):
    tq = min(512, t_len)
    tk = tq
    assert tq % CHUNK == 0 and CHUNK == 1 << CHUNK_BITS
    nq = t_len // tq
    k3 = kb.reshape(batch, t_len, KB_W)
    v3 = vb.reshape(batch, t_len, VB_W)
    kern = functools.partial(_attn_prompt_kernel, tq=tq, tk=tk, lam_init=lam_init)
    hw = 2 * D_B
    return pl.pallas_call(
        kern,
        grid=(batch, H_B, nq),
        in_specs=[
            pl.BlockSpec((tq, hw), lambda b, h, q: (b * nq + q, SLAB_QB // hw + h)),
            pl.BlockSpec((1, t_len, hw), lambda b, h, q: (b, 0, h)),
            pl.BlockSpec((1, t_len, hw), lambda b, h, q: (b, 0, h)),
            pl.BlockSpec((1, hw), lambda b, h, q: (0, 0)),
            pl.BlockSpec((4, D_B), lambda b, h, q: (0, 0)),
        ],
        out_specs=pl.BlockSpec((tq, hw), lambda b, h, q: (b * nq + q, h)),
        out_shape=jax.ShapeDtypeStruct((batch * t_len, VB_W), BF16),
        scratch_shapes=[
            pltpu.VMEM((t_len // tk, 2 * hw, tk), BF16),
            pltpu.VMEM((t_len, 2 * hw), BF16),
            pltpu.VMEM((2 * tq, hw), BF16),
            pltpu.VMEM((2 * tq, tq), F32),
            pltpu.VMEM((2 * tq, hw), F32),
            pltpu.VMEM((2 * tq, 2 * hw), F32),
            pltpu.VMEM((2 * tq, tk), F32),
            pltpu.VMEM((2 * tq, tk), F32),
            pltpu.VMEM((2 * tq, tk), BF16),
            pltpu.VMEM((2 * tq, tk), BF16),
        ],
        compiler_params=_cparams(("parallel", "parallel", "arbitrary")),
        name="attn_prompt",
    )(slab, k3, v3, g_norm, dl)


def _attn_sample_kernel(q_ref, kct_ref, vc_ref, kn_ref, vn_ref, g_ref, dl_ref, o_ref, *, tq, past, lam_init):
    hw = 2 * D_B
    q_pos = past + lax.broadcasted_iota(I32, (2 * tq, 1), 0) % tq
    lam = _lambda_from(dl_ref, lam_init)

    def biased(s, k_pos, c):
        s = s - c * jnp.abs(q_pos - k_pos).astype(F32)
        return jnp.where((k_pos // CHUNK) <= (q_pos // CHUNK), s, NEG_INF)

    for h in range(H_B):
        cols = slice(h * hw, (h + 1) * hw)
        c = LOG2E * 2.0 ** -(h + 1)
        qq = _split_maps(q_ref[:, cols])
        sc = biased(jnp.dot(qq, kct_ref[0, h].astype(BF16), preferred_element_type=F32),
                    lax.broadcasted_iota(I32, (1, past), 1), c)
        sn = biased(lax.dot_general(qq, kn_ref[:, cols], (((1,), (1,)), ((), ())), preferred_element_type=F32),
                    past + lax.broadcasted_iota(I32, (1, tq), 1), c)
        m = jnp.maximum(jnp.max(sc, axis=-1, keepdims=True), jnp.max(sn, axis=-1, keepdims=True))
        pc = jnp.exp2(sc - m)
        pn = jnp.exp2(sn - m)
        l = jnp.sum(pc, axis=-1, keepdims=True) + jnp.sum(pn, axis=-1, keepdims=True)
        vc = vc_ref[0, pl.ds(h, past, stride=H_B), :].astype(BF16)
        acc = (jnp.dot(pc.astype(BF16), vc, preferred_element_type=F32)
               + jnp.dot(pn.astype(BF16), vn_ref[:, cols].astype(BF16), preferred_element_type=F32))
        o_ref[:, cols] = _finish_heads(acc, 1.0 / l, lam, g_ref[...], tq, lam_init)


def _attn_sample(slab, kb, vb, cache_kt, cache_v2, g_norm, dl, batch, t_len, lam_init):
    past = cache_kt.shape[3]
    hw = 2 * D_B
    kern = functools.partial(_attn_sample_kernel, tq=t_len, past=past, lam_init=lam_init)
    return pl.pallas_call(
        kern,
        grid=(batch,),
        in_specs=[
            pl.BlockSpec((t_len, QB_W), lambda b: (b, SLAB_QB // QB_W)),
            pl.BlockSpec((1, H_B, hw, past), lambda b: (b, 0, 0, 0)),
            pl.BlockSpec((1, past * H_B, hw), lambda b: (b, 0, 0)),
            pl.BlockSpec((t_len, KB_W), lambda b: (b, 0)),
            pl.BlockSpec((t_len, VB_W), lambda b: (b, 0)),
            pl.BlockSpec((1, hw), lambda b: (0, 0)),
            pl.BlockSpec((4, D_B), lambda b: (0, 0)),
        ],
        out_specs=pl.BlockSpec((t_len, VB_W), lambda b: (b, 0)),
        out_shape=jax.ShapeDtypeStruct((batch * t_len, VB_W), BF16),
        compiler_params=_cparams(("parallel",)),
        name="attn_sample",
    )(slab, cache_kt, cache_v2, kb, vb, g_norm, dl)


def _finish_kernel(x_ref, oa_ref, ra_ref, ob_ref, gt_ref, wpa_ref, wpb_ref, wo_ref, g_ref, b_ref,
                   wrh_ref, wrl_ref, br_ref, h_ref, hp_ref, idx_ref, tw_ref, cnt_ref, *, parts):
    i = pl.program_id(0)

    @pl.when(i == 0)
    def _():
        cnt_ref[...] = jnp.zeros(cnt_ref.shape, F32)

    pm = x_ref.shape[0] // parts
    for part in range(parts):
        r = slice(part * pm, (part + 1) * pm)
        ra = ra_ref[r, :].astype(F32)
        ua = ra * _sigmoid(ra) * oa_ref[r, :].astype(F32)
        ya = jnp.dot(ua.astype(BF16), wpa_ref[...], preferred_element_type=F32)
        yb = jnp.dot(ob_ref[r, :], wpb_ref[...], preferred_element_type=F32)
        gt = gt_ref[r, :].astype(F32)
        mixed = _sigmoid(gt[:, :D_MODEL]) * ya + _sigmoid(gt[:, D_MODEL:]) * yb
        mix = jnp.dot(mixed.astype(BF16), wo_ref[...], preferred_element_type=F32)
        hh = _layer_norm(DEEPNORM_ALPHA * x_ref[r, :] + mix, g_ref[...], b_ref[...])
        h_ref[r, :] = hh
        _store_rows(hp_ref.at[pl.ds(part * pm * ROW_TILES, pm * ROW_TILES)], _pack_bf16_pairs(hh))

        h_hi = hh.astype(BF16)
        h_lo = (hh - h_hi.astype(F32)).astype(BF16)
        lg = (jnp.dot(h_hi, wrh_ref[...], preferred_element_type=F32)
              + jnp.dot(h_lo, wrh_ref[...], preferred_element_type=F32)
              + jnp.dot(h_hi, wrl_ref[...], preferred_element_type=F32)) + br_ref[...]
        lane = lax.broadcasted_iota(I32, (pm, LANES), 1)
        lane_f = lane.astype(F32)
        work = lg
        vals, idxs = [], []
        cnt = jnp.zeros((pm, LANES), F32)
        for _ in range(TOP_K):
            mx = jnp.max(work, axis=-1, keepdims=True)
            ix = jnp.min(jnp.where(work == mx, lane_f, float(LANES)), axis=-1, keepdims=True)
            hit = lane_f == ix
            cnt = cnt + hit.astype(F32)
            work = jnp.where(hit, NEG_INF, work)
            vals.append(mx)
            idxs.append(ix)
        es = [jnp.exp(v - vals[0]) for v in vals]
        den = es[0] + es[1] + es[2] + es[3]
        idx_full = jnp.zeros((pm, LANES), F32)
        tw_full = jnp.zeros((pm, LANES), F32)
        for k in range(TOP_K):
            idx_full = jnp.where(lane == k, idxs[k], idx_full)
            tw_full = jnp.where(lane == k, es[k] / den, tw_full)
        idx_ref[:, r] = idx_full.T[:SUBLANES].astype(I32)
        tw_ref[r, :] = tw_full[:, :TOP_K]
        cnt_ref[...] += jnp.sum(cnt, axis=0, keepdims=True)


def _finish(x2, oa, slab, ob, p):
    n = x2.shape[0]
    tm = min(512, n)
    const = lambda i: (0, 0)
    return pl.pallas_call(
        functools.partial(_finish_kernel, parts=1),
        grid=(n // tm,),
        in_specs=[
            pl.BlockSpec((tm, D_MODEL), lambda i: (i, 0)),
            pl.BlockSpec((tm, VA_W), lambda i: (i, 0)),
            pl.BlockSpec((tm, RA_W), lambda i: (i, SLAB_RA // RA_W)),
            pl.BlockSpec((tm, VB_W), lambda i: (i, 0)),
            pl.BlockSpec((tm, GT_W), lambda i: (i, SLAB_GT // GT_W)),
            pl.BlockSpec((VA_W, D_MODEL), const),
            pl.BlockSpec((VB_W, D_MODEL), const),
            pl.BlockSpec((D_MODEL, D_MODEL), const),
            pl.BlockSpec((1, D_MODEL), const),
            pl.BlockSpec((1, D_MODEL), const),
            pl.BlockSpec((D_MODEL, LANES), const),
            pl.BlockSpec((D_MODEL, LANES), const),
            pl.BlockSpec((1, LANES), const),
        ],
        out_specs=[
            pl.BlockSpec((tm, D_MODEL), lambda i: (i, 0)),
            pl.BlockSpec((tm * ROW_TILES, LANES), lambda i: (i, 0)),
            pl.BlockSpec((SUBLANES, tm), lambda i: (0, i)),
            pl.BlockSpec((tm, TOP_K), lambda i: (i, 0)),
            pl.BlockSpec((1, LANES), const),
        ],
        out_shape=[
            jax.ShapeDtypeStruct((n, D_MODEL), F32),
            jax.ShapeDtypeStruct((n * ROW_TILES, LANES), U32),
            jax.ShapeDtypeStruct((SUBLANES, n), I32),
            jax.ShapeDtypeStruct((n, TOP_K), F32),
            jax.ShapeDtypeStruct((1, LANES), F32),
        ],
        compiler_params=_cparams(("arbitrary",)),
        name="finish",
    )(x2, oa, slab, ob, slab, p["w_pa"], p["w_pb"], p["w_o"], p["ln1_g"], p["ln1_b"],
      p["w_r_hi"], p["w_r_lo"], p["b_r"])


def _dest_kernel(idx_ref, ps_ref, ut_ref, dest_ref, carry_ref):
    i = pl.program_id(0)

    @pl.when(i == 0)
    def _():
        carry_ref[...] = jnp.zeros(carry_ref.shape, F32)

    tm = idx_ref.shape[1]
    expert = lax.broadcasted_iota(I32, (N_EXPERTS, tm), 0)
    idx = idx_ref[...]
    hits = [expert == idx[k:k + 1, :] for k in range(TOP_K)]
    cnt = jnp.zeros((N_EXPERTS, tm), F32)
    for hit in hits:
        cnt = cnt + hit.astype(F32)
    before = jnp.dot(cnt.astype(BF16), ut_ref[...], preferred_element_type=F32)
    base = before + carry_ref[:, 0:1] + ps_ref[:, 0:1]
    row = lax.broadcasted_iota(I32, (SUBLANES, tm), 0)
    dest = jnp.zeros((SUBLANES, tm), F32)
    for k in range(TOP_K):
        d = jnp.sum(jnp.where(hits[k], base, 0.0), axis=0, keepdims=True)
        dest = jnp.where(row == k, d, dest)
    dest_ref[...] = dest.astype(I32)
    carry_ref[...] += jnp.sum(cnt, axis=1, keepdims=True)


def _dest(idx_t, pstart):
    n = idx_t.shape[1]
    tm = min(512, n)
    utri = jnp.asarray(np.triu(np.ones((tm, tm), np.float32), 1), BF16)
    return pl.pallas_call(
        _dest_kernel,
        grid=(n // tm,),
        in_specs=[
            pl.BlockSpec((SUBLANES, tm), lambda i: (0, i)),
            pl.BlockSpec((N_EXPERTS, LANES), lambda i: (0, 0)),
            pl.BlockSpec((tm, tm), lambda i: (0, 0)),
        ],
        out_specs=pl.BlockSpec((SUBLANES, tm), lambda i: (0, i)),
        out_shape=jax.ShapeDtypeStruct((SUBLANES, n), I32),
        scratch_shapes=[pltpu.VMEM((N_EXPERTS, LANES), F32)],
        compiler_params=_cparams(("arbitrary",)),
        name="dest",
    )(idx_t, pstart, utri)


def _scatter_kernel(dest_ref, pad0_ref, padn_ref, hp_ref, rows_ref, zero_ref, sem, pad_sem):
    tm = hp_ref.shape[0] // ROW_TILES

    def row_copy(t, d):
        return pltpu.make_async_copy(hp_ref.at[pl.ds(pl.multiple_of(t * ROW_TILES, ROW_TILES), ROW_TILES)],
                                     rows_ref.at[pl.ds(pl.multiple_of(d * ROW_TILES, ROW_TILES), ROW_TILES)], sem)

    def issue(t, carry):
        for k in range(TOP_K):
            row_copy(t, dest_ref[k * tm + t]).start(priority=k % 2)
        return carry

    lax.fori_loop(0, tm, issue, 0, unroll=4)

    @pl.when(pl.program_id(0) == pl.num_programs(0) - 1)
    def _():
        zero_ref[...] = jnp.zeros(zero_ref.shape, U32)

        def pad_copy(d, p):
            return pltpu.make_async_copy(
                zero_ref.at[pl.ds(0, p * ROW_TILES)],
                rows_ref.at[pl.ds(pl.multiple_of(d * ROW_TILES, ROW_TILES), p * ROW_TILES)], pad_sem)

        def for_each_pad_run(fn):
            def per_expert(e, carry):
                d, n = pad0_ref[e], padn_ref[e]
                p = EXPERT_BM // 2
                while p:
                    @pl.when((n & p) != 0)
                    def _(d=d, p=p):
                        fn(d, p)
                    d = d + (n & p)
                    p //= 2
                return carry
            lax.fori_loop(0, N_EXPERTS, per_expert, 0)

        for_each_pad_run(lambda d, p: pad_copy(d, p).start())
        for_each_pad_run(lambda d, p: pad_copy(d, p).wait())

    for k in range(TOP_K):
        pltpu.make_async_copy(hp_ref, rows_ref.at[pl.ds(0, tm * ROW_TILES)], sem).wait()


def _scatter(dest_t, pad_start, pad_len, hp, n_rows):
    n = hp.shape[0] // ROW_TILES
    tm = min(512, n)
    smem = pl.BlockSpec(memory_space=pltpu.SMEM)
    return pl.pallas_call(
        _scatter_kernel,
        grid=(n // tm,),
        in_specs=[
            pl.BlockSpec((TOP_K * tm,), lambda i: (i,), memory_space=pltpu.SMEM),
            smem, smem,
            pl.BlockSpec((tm * ROW_TILES, LANES), lambda i: (i, 0)),
        ],
        out_specs=pl.BlockSpec(memory_space=pl.ANY),
        out_shape=jax.ShapeDtypeStruct((n_rows * ROW_TILES, LANES), U32),
        scratch_shapes=[pltpu.VMEM((EXPERT_BM // 2 * ROW_TILES, LANES), U32), pltpu.SemaphoreType.DMA(()),
                        pltpu.SemaphoreType.DMA(())],
        compiler_params=_cparams(("arbitrary",)),
        name="scatter",
    )(dest_t, pad_start, pad_len, hp)


def _expert_kernel(be_ref, nu_ref, rows_ref, wgu_ref, bgu_ref, wd_ref, bd_ref, out_ref, wgu_bf_ref, wd_bf_ref):
    i = pl.program_id(0)

    @pl.when((i == 0) | (be_ref[i] != be_ref[jnp.maximum(i - 1, 0)]))
    def _():
        wgu_bf_ref[...] = wgu_ref[0].astype(BF16)
        wd_bf_ref[...] = wd_ref[0].astype(BF16)

    @pl.when(i < nu_ref[0])
    def _():
        x = _unpack_bf16_pairs(_load_rows(rows_ref)).astype(BF16)
        o = bd_ref[0]
        half = D_FF // 2
        for c0 in (0, half):
            gate = jnp.dot(x, wgu_bf_ref[:, c0:c0 + half], preferred_element_type=F32) + bgu_ref[0, :, c0:c0 + half]
            up = (jnp.dot(x, wgu_bf_ref[:, D_FF + c0:D_FF + c0 + half], preferred_element_type=F32)
                  + bgu_ref[0, :, D_FF + c0:D_FF + c0 + half])
            gate = jnp.minimum(gate, SWIGLU_LIMIT)
            up = jnp.clip(up, -SWIGLU_LIMIT, SWIGLU_LIMIT)
            hidden = (up + 1.0) * (gate * _sigmoid(SWIGLU_ALPHA * gate))
            o = o + jnp.dot(hidden.astype(BF16), wd_bf_ref[c0:c0 + half, :], preferred_element_type=F32)
        _store_rows(out_ref, _pack_bf16_pairs(o))

    @pl.when(i >= nu_ref[0])
    def _():
        out_ref[...] = jnp.zeros(out_ref.shape, U32)


def _experts(rows, block_e, n_used, p):
    n_rows = rows.shape[0] // ROW_TILES
    bm = EXPERT_BM
    n_blocks = n_rows // bm
    grid_spec = pltpu.PrefetchScalarGridSpec(
        num_scalar_prefetch=2,
        grid=(n_blocks,),
        in_specs=[
            pl.BlockSpec((bm * ROW_TILES, LANES), lambda i, be, nu: (jnp.minimum(i, nu[0] - 1), 0)),
            pl.BlockSpec((1, D_MODEL, 2 * D_FF), lambda i, be, nu: (be[i], 0, 0)),
            pl.BlockSpec((1, 1, 2 * D_FF), lambda i, be, nu: (be[i], 0, 0)),
            pl.BlockSpec((1, D_FF, D_MODEL), lambda i, be, nu: (be[i], 0, 0)),
            pl.BlockSpec((1, 1, D_MODEL), lambda i, be, nu: (be[i], 0, 0)),
        ],
        out_specs=pl.BlockSpec((bm * ROW_TILES, LANES), lambda i, be, nu: (i, 0)),
        scratch_shapes=[pltpu.VMEM((D_MODEL, 2 * D_FF), BF16), pltpu.VMEM((D_FF, D_MODEL), BF16)],
    )
    return pl.pallas_call(
        _expert_kernel,
        grid_spec=grid_spec,
        out_shape=jax.ShapeDtypeStruct((n_rows * ROW_TILES, LANES), U32),
        compiler_params=_cparams(("arbitrary",)),
        name="experts",
    )(block_e, n_used, rows, p["w_gu"], p["b_gu"], p["w_down"], p["b_down"])


def _combine_kernel(dest_ref, tw_ref, h_ref, rows_ref, g_ref, b_ref, y_ref, buf_ref, sem):
    tm = h_ref.shape[0]

    def row_copy(t, k, d):
        return pltpu.make_async_copy(rows_ref.at[pl.ds(pl.multiple_of(d * ROW_TILES, ROW_TILES), ROW_TILES)],
                                     buf_ref.at[k, pl.ds(pl.multiple_of(t * ROW_TILES, ROW_TILES), ROW_TILES)], sem)

    def issue(t, carry):
        for k in range(TOP_K):
            row_copy(t, k, dest_ref[k * tm + t]).start(priority=k % 2)
        return carry

    lax.fori_loop(0, tm, issue, 0, unroll=4)
    for k in range(TOP_K):
        pltpu.make_async_copy(rows_ref.at[pl.ds(0, tm * ROW_TILES)], buf_ref.at[k], sem).wait()
    tw = tw_ref[...]
    f = jnp.zeros((tm, D_MODEL), F32)
    for k in range(TOP_K):
        f = f + tw[:, k:k + 1] * _unpack_bf16_pairs(_load_rows(buf_ref.at[k]))
    y_ref[...] = _layer_norm(DEEPNORM_ALPHA * h_ref[...] + f, g_ref[...], b_ref[...])


def _combine(dest_t, top_w, hh, out_rows, p):
    n = hh.shape[0]
    tm = min(512, n)
    const = lambda i: (0, 0)
    return pl.pallas_call(
        _combine_kernel,
        grid=(n // tm,),
        in_specs=[
            pl.BlockSpec((TOP_K * tm,), lambda i: (i,), memory_space=pltpu.SMEM),
            pl.BlockSpec((tm, TOP_K), lambda i: (i, 0)),
            pl.BlockSpec((tm, D_MODEL), lambda i: (i, 0)),
            pl.BlockSpec(memory_space=pl.ANY),
            pl.BlockSpec((1, D_MODEL), const),
            pl.BlockSpec((1, D_MODEL), const),
        ],
        out_specs=pl.BlockSpec((tm, D_MODEL), lambda i: (i, 0)),
        out_shape=jax.ShapeDtypeStruct((n, D_MODEL), F32),
        scratch_shapes=[pltpu.VMEM((TOP_K, tm * ROW_TILES, LANES), U32), pltpu.SemaphoreType.DMA(())],
        compiler_params=_cparams(("arbitrary",)),
        name="combine",
    )(dest_t, top_w, hh, out_rows, p["ln2_g"], p["ln2_b"])


def _moe(hh, hp, top_idx, top_w, counts, p):
    n = hh.shape[0]
    bm = EXPERT_BM
    n_rows = n * TOP_K + N_EXPERTS * bm
    cnt = counts[0, :N_EXPERTS].astype(I32)
    padded = (cnt + bm - 1) // bm * bm
    pend = jnp.cumsum(padded)
    pstart = pend - padded
    ps_col = jnp.broadcast_to(pstart.astype(F32)[:, None], (N_EXPERTS, LANES))
    block_row0 = jnp.arange(n_rows // bm, dtype=I32) * bm
    block_e = jnp.minimum(jnp.sum((pend[None, :] <= block_row0[:, None]).astype(I32), axis=1), N_EXPERTS - 1)
    n_used = (pend[-1:] // bm).astype(I32)
    tm = min(512, n)
    dest = _dest(top_idx, ps_col)[:TOP_K].reshape(TOP_K, n // tm, tm).transpose(1, 0, 2).reshape(-1)
    rows = _scatter(dest, pstart + cnt, padded - cnt, hp, n_rows)
    out_rows = _experts(rows, block_e, n_used, p)
    return _combine(dest, top_w, hh, out_rows, p)


def _prep_params(w_in, b_in, w_alpha, b_alpha, gla_norm_g, diff_norm_g, w_pa, w_pb, w_o, ln1_g, ln1_b,
                 w_router, b_router, w_gu, b_gu, w_down, b_down, ln2_g, ln2_b):
    o = IN_OFFS
    seg = lambda a, i: a[..., o[i]:o[i + 1]]
    order = (0, 1, 2, 3, 5, 8)
    w_slab = jnp.concatenate([seg(w_in, i) for i in order], axis=1).astype(BF16)
    b_slab = jnp.concatenate([seg(b_in, i) for i in order])[None, :]
    scale = np.ones((1, SLAB_W), np.float32)
    scale[0, SLAB_QA:SLAB_QA + QA_W] = DK_A ** -0.5
    scale[0, SLAB_QB:SLAB_QB + QB_W] = D_B ** -0.5 * LOG2E
    w_lr = jnp.zeros((D_MODEL, LANES), F32).at[:, :LR_W].set(seg(w_in, 4)).astype(BF16)
    b_lr = jnp.zeros((1, LANES), F32).at[0, :LR_W].set(seg(b_in, 4))
    w_al = jnp.zeros((LANES, QA_W), F32).at[:LR_W].set(w_alpha).astype(BF16)
    w_r = jnp.zeros((D_MODEL, LANES), F32).at[:, :N_EXPERTS].set(w_router)
    w_r_hi = w_r.astype(BF16)
    w_r_lo = (w_r - w_r_hi.astype(F32)).astype(BF16)
    b_r = jnp.full((1, LANES), NEG_INF, F32).at[0, :N_EXPERTS].set(b_router)
    return dict(
        w_slab=w_slab, b_slab=b_slab, s_slab=jnp.asarray(scale),
        w_lr=w_lr, b_lr=b_lr, w_alpha=w_al, b_alpha=b_alpha[None, :],
        w_k=seg(w_in, 6).astype(BF16), w_v=seg(w_in, 7).astype(BF16),
        b_k=seg(b_in, 6)[None, :], b_v=seg(b_in, 7)[None, :],
        gla_g=gla_norm_g[None, :], diff_g=diff_norm_g[None, :],
        w_pa=w_pa.astype(BF16), w_pb=w_pb.astype(BF16), w_o=w_o.astype(BF16),
        ln1_g=ln1_g[None, :], ln1_b=ln1_b[None, :],
        w_r_hi=w_r_hi, w_r_lo=w_r_lo, b_r=b_r,
        w_gu=w_gu, b_gu=b_gu[:, None, :], w_down=w_down, b_down=b_down[:, None, :],
        ln2_g=ln2_g[None, :], ln2_b=ln2_b[None, :],
    )


def _layer(x, p, dl, lam_init, s0, cache_k, cache_v):
    batch, t_len, _ = x.shape
    x2 = x.reshape(batch * t_len, D_MODEL)
    slab, loga, k_rows, kb, vb = _proj(x2, p)
    oa, s_new = _gla(slab, loga, s0, p["gla_g"], batch, t_len)
    if cache_k is None:
        ob = _attn_prompt(slab, kb, vb, p["diff_g"], dl, batch, t_len, lam_init)
    else:
        ob = _attn_sample(slab, kb, vb, cache_k, cache_v, p["diff_g"], dl, batch, t_len, lam_init)
    hh, hp, top_idx, top_w, counts = _finish(x2, oa, slab, ob, p)
    y = _moe(hh, hp, top_idx, top_w, counts, p)
    return (y.reshape(batch, t_len, D_MODEL), s_new,
            k_rows.reshape(batch, t_len, H_B, 2, D_B), vb.reshape(batch, t_len, H_B, 2 * D_B))


def kernel(x_prompt, x_sample, cache_k, cache_v, state_gla, w_in, b_in, w_alpha, b_alpha, gla_norm_g,
           diff_lambda, diff_norm_g, w_pa, w_pb, w_o, ln1_g, ln1_b, w_router, b_router, w_gu, b_gu,
           w_down, b_down, ln2_g, ln2_b):
    assert w_in.shape[0] == DEPTH == 1
    l = 0
    lam_init = 0.8 - 0.6 * math.exp(-0.3 * l)
    p = _prep_params(w_in[l], b_in[l], w_alpha[l], b_alpha[l], gla_norm_g[l], diff_norm_g[l], w_pa[l],
                     w_pb[l], w_o[l], ln1_g[l], ln1_b[l], w_router[l], b_router[l], w_gu[l], b_gu[l],
                     w_down[l], b_down[l], ln2_g[l], ln2_b[l])
    dl = diff_lambda[l]
    bp = x_prompt.shape[0]
    bs, ts = x_sample.shape[0], x_sample.shape[1]
    past = cache_k.shape[2]
    yp, s_p, k_p, v_p = _layer(x_prompt, p, dl, lam_init, jnp.zeros((bp, H_A, DK_A, DV_A), F32), None, None)
    ck = jnp.transpose(cache_k[l], (0, 2, 3, 4, 1)).reshape(bs, H_B, 2 * D_B, past)
    cv = cache_v[l].reshape(bs, past * H_B, 2 * D_B)
    ys, s_s, k_s, v_s = _layer(x_sample, p, dl, lam_init, state_gla[l], ck, cv)
    return (yp, ys, s_p[None], k_p[None], v_p[None], s_s[None], k_s[None], v_s[None])
```
